```python
import math
import jax, jax.numpy as jnp
from jax import lax
import numpy as np

D_MODEL = 1024
BATCH = 8
SEQ = 16384
DEPTH = 2

HEAD_DIM = 64
GRID_W = 64
EPS = 1e-6

ATT_HEADS = 8
ATT_KV_HEADS = 2
ATT_WIDTH = ATT_HEADS * HEAD_DIM
KV_WIDTH = ATT_KV_HEADS * HEAD_DIM
Q_BLOCK = 128
ROPE_THETA = 10000.0
ROPE_AXIS_DIM = HEAD_DIM // 2

CONV_GROUPS = 4
CONV_WIDTH = CONV_GROUPS * HEAD_DIM
CONV_KERNEL = 31

SG_HEADS = 4
SG_WIDTH = SG_HEADS * HEAD_DIM
SG_CHUNK = 128

D_MIX = ATT_WIDTH + CONV_WIDTH + SG_WIDTH

IN_SPLIT_SIZES = (
    ATT_WIDTH,
    KV_WIDTH,
    KV_WIDTH,
    ATT_WIDTH,
    2 * CONV_WIDTH,
    CONV_WIDTH,
    SG_WIDTH,
    SG_WIDTH,
    SG_WIDTH,
)
D_IN = sum(IN_SPLIT_SIZES)

kernel_name = "hybrid_parallel_conv_gqa_sgu_encoder"


def _split_points():
    pts, acc = [], 0
    for s in IN_SPLIT_SIZES[:-1]:
        acc += s
        pts.append(acc)
    return pts


def rms_norm(x, g):
    xf = x.astype(jnp.float32)
    y = xf * lax.rsqrt(jnp.mean(xf * xf, axis=-1, keepdims=True) + EPS) * g.astype(jnp.float32)
    return y.astype(x.dtype)


def layer_norm(x, g, b):
    xf = x.astype(jnp.float32)
    mu = jnp.mean(xf, axis=-1, keepdims=True)
    xc = xf - mu
    var = jnp.mean(xc * xc, axis=-1, keepdims=True)
    y = xc * lax.rsqrt(var + EPS) * g.astype(jnp.float32) + b.astype(jnp.float32)
    return y.astype(x.dtype)


def rope_1d(x, pos):
    d = x.shape[-1]
    half = d // 2
    inv_freq = ROPE_THETA ** (-jnp.arange(half, dtype=jnp.float32) / half)
    ang = pos[:, None] * inv_freq[None, :]
    cos = jnp.cos(ang)[:, None, :]
    sin = jnp.sin(ang)[:, None, :]
    xf = x.astype(jnp.float32)
    x1, x2 = xf[..., :half], xf[..., half:]
    out = jnp.concatenate([x1 * cos - x2 * sin, x2 * cos + x1 * sin], axis=-1)
    return out.astype(x.dtype)


def axial_rope(x, row, col):
    return jnp.concatenate([rope_1d(x[..., :ROPE_AXIS_DIM], row),
                            rope_1d(x[..., ROPE_AXIS_DIM:], col)], axis=-1)


def attention_group(q, k, v):
    B, S = q.shape[0], q.shape[1]
    G = ATT_HEADS // ATT_KV_HEADS
    nblk = S // Q_BLOCK
    qb = q.reshape(B, nblk, Q_BLOCK, ATT_KV_HEADS, G, HEAD_DIM).transpose(1, 0, 3, 4, 2, 5)
    kt = k.transpose(0, 2, 1, 3)
    vt = v.transpose(0, 2, 1, 3)
    scale = HEAD_DIM ** -0.5

    def one_block(qi):
        s = jnp.einsum('bkgqd,bksd->bkgqs', qi, kt, preferred_element_type=jnp.float32) * scale
        p = jax.nn.softmax(s, axis=-1)
        return jnp.einsum('bkgqs,bksd->bkgqd', p.astype(vt.dtype), vt)

    o = lax.map(one_block, qb)
    return o.transpose(1, 0, 4, 2, 3, 5).reshape(B, S, ATT_WIDTH)


def conv_group(a, dw_w, dw_b, ln_g, ln_b):
    h = a[..., :CONV_WIDTH] * jax.nn.sigmoid(a[..., CONV_WIDTH:])
    pad = CONV_KERNEL // 2
    h = lax.conv_general_dilated(
        h, dw_w[:, None, :].astype(h.dtype), window_strides=(1,), padding=[(pad, pad)],
        dimension_numbers=('NWC', 'WIO', 'NWC'), feature_group_count=CONV_WIDTH) + dw_b
    h = layer_norm(h, ln_g, ln_b)
    return jax.nn.silu(h)


def spatial_gating_group(u, v, ln_g, ln_b, w_s, b_s):
    B, S = u.shape[0], u.shape[1]
    u = jax.nn.gelu(u, approximate=False)
    v = layer_norm(jax.nn.gelu(v, approximate=False), ln_g, ln_b)
    n = S // SG_CHUNK
    vc = v.reshape(B, n, SG_CHUNK, SG_HEADS, HEAD_DIM)
    mixed = jnp.einsum('hpq,bnqhd->bnphd', w_s, vc) + b_s.T[None, None, :, :, None]
    return u * mixed.reshape(B, S, SG_WIDTH)


def _fwd_setup_inputs(seed: int = 0) -> dict:
    key = jax.random.key(seed)
    ks = jax.random.split(key, 16)
    f32 = jnp.float32
    x = jax.random.normal(ks[0], (BATCH, SEQ, D_MODEL), f32)
    pre_norm = 1.0 + 0.05 * jax.random.normal(ks[1], (DEPTH, D_MODEL), f32)
    post_norm = 1.0 + 0.05 * jax.random.normal(ks[2], (DEPTH, D_MODEL), f32)
    w_in = jax.random.normal(ks[3], (DEPTH, D_MODEL, D_IN), f32) * D_MODEL ** -0.5
    w_out = jax.random.normal(ks[4], (DEPTH, D_MIX, D_MODEL), f32) * D_MIX ** -0.5
    q_norm = 1.0 + 0.05 * jax.random.normal(ks[5], (DEPTH, HEAD_DIM), f32)
    k_norm = 1.0 + 0.05 * jax.random.normal(ks[6], (DEPTH, HEAD_DIM), f32)
    conv_dw = jax.random.normal(ks[7], (DEPTH, CONV_KERNEL, CONV_WIDTH), f32) * CONV_KERNEL ** -0.5
    conv_dw_b = 0.02 * jax.random.normal(ks[8], (DEPTH, CONV_WIDTH), f32)
    conv_ln_g = 1.0 + 0.05 * jax.random.normal(ks[9], (DEPTH, CONV_WIDTH), f32)
    conv_ln_b = 0.02 * jax.random.normal(ks[10], (DEPTH, CONV_WIDTH), f32)
    sg_ln_g = 1.0 + 0.05 * jax.random.normal(ks[11], (DEPTH, SG_WIDTH), f32)
    sg_ln_b = 0.02 * jax.random.normal(ks[12], (DEPTH, SG_WIDTH), f32)
    sg_w = jax.random.normal(ks[13], (DEPTH, SG_HEADS, SG_CHUNK, SG_CHUNK), f32) * SG_CHUNK ** -0.5
    sg_b = 1.0 + 0.1 * jax.random.normal(ks[14], (DEPTH, SG_HEADS, SG_CHUNK), f32)
    return {"x": x, "pre_norm": pre_norm, "post_norm": post_norm, "w_in": w_in, "w_out": w_out,
            "q_norm": q_norm, "k_norm": k_norm, "conv_dw": conv_dw, "conv_dw_b": conv_dw_b,
            "conv_ln_g": conv_ln_g, "conv_ln_b": conv_ln_b, "sg_ln_g": sg_ln_g, "sg_ln_b": sg_ln_b,
            "sg_w": sg_w, "sg_b": sg_b}


def _fwd_reference(x, pre_norm, post_norm, w_in, w_out, q_norm, k_norm, conv_dw, conv_dw_b,
              conv_ln_g, conv_ln_b, sg_ln_g, sg_ln_b, sg_w, sg_b):
    B, S = x.shape[0], x.shape[1]
    rows = S // GRID_W
    row = jnp.repeat(jnp.arange(rows, dtype=jnp.int32), GRID_W).astype(jnp.float32)
    col = jnp.tile(jnp.arange(GRID_W, dtype=jnp.int32), rows).astype(jnp.float32)
    split_pts = _split_points()

    for l in range(DEPTH):
        h = rms_norm(x, pre_norm[l])
        proj = jnp.einsum('bsd,de->bse', h, w_in[l])
        q, k, v, g_att, a_conv, g_conv, u_sg, v_sg, g_sg = jnp.split(proj, split_pts, axis=-1)

        q = axial_rope(rms_norm(q.reshape(B, S, ATT_HEADS, HEAD_DIM), q_norm[l]), row, col)
        k = axial_rope(rms_norm(k.reshape(B, S, ATT_KV_HEADS, HEAD_DIM), k_norm[l]), row, col)
        v = v.reshape(B, S, ATT_KV_HEADS, HEAD_DIM)
        att = attention_group(q, k, v) * jax.nn.silu(g_att)

        cnv = conv_group(a_conv, conv_dw[l], conv_dw_b[l], conv_ln_g[l], conv_ln_b[l]) * jax.nn.silu(g_conv)

        sgu = spatial_gating_group(u_sg, v_sg, sg_ln_g[l], sg_ln_b[l], sg_w[l], sg_b[l]) * jax.nn.silu(g_sg)

        mix = jnp.einsum('bse,ed->bsd', jnp.concatenate([att, cnv, sgu], axis=-1), w_out[l])
        x = x + rms_norm(mix, post_norm[l])
    return x


import jax as _jax
import jax.numpy as _jnp

TWIN_FORMAT = 'train_step'
FWD_PARAMS = ['x', 'pre_norm', 'post_norm', 'w_in', 'w_out', 'q_norm', 'k_norm', 'conv_dw', 'conv_dw_b', 'conv_ln_g', 'conv_ln_b', 'sg_ln_g', 'sg_ln_b', 'sg_w', 'sg_b']
TWIN_WEIGHTS = ['pre_norm', 'post_norm', 'w_in', 'w_out', 'q_norm', 'k_norm', 'conv_dw', 'conv_dw_b', 'conv_ln_g', 'conv_ln_b', 'sg_ln_g', 'sg_ln_b', 'sg_w', 'sg_b']
TWIN_DIFF_INPUT = 'x'
TWIN_INPUTS = ['x', 'pre_norm', 'post_norm', 'w_in', 'w_out', 'q_norm', 'k_norm', 'conv_dw', 'conv_dw_b', 'conv_ln_g', 'conv_ln_b', 'sg_ln_g', 'sg_ln_b', 'sg_w', 'sg_b', 'loss_target', 'm_pre_norm', 'm_post_norm', 'm_w_in', 'm_w_out', 'm_q_norm', 'm_k_norm', 'm_conv_dw', 'm_conv_dw_b', 'm_conv_ln_g', 'm_conv_ln_b', 'm_sg_ln_g', 'm_sg_ln_b', 'm_sg_w', 'm_sg_b', 'v_pre_norm', 'v_post_norm', 'v_w_in', 'v_w_out', 'v_q_norm', 'v_k_norm', 'v_conv_dw', 'v_conv_dw_b', 'v_conv_ln_g', 'v_conv_ln_b', 'v_sg_ln_g', 'v_sg_ln_b', 'v_sg_w', 'v_sg_b']
TWIN_OUTPUTS = ['loss', 'grad_x', 'grad_pre_norm', 'grad_post_norm', 'grad_w_in', 'grad_w_out', 'grad_q_norm', 'grad_k_norm', 'grad_conv_dw', 'grad_conv_dw_b', 'grad_conv_ln_g', 'grad_conv_ln_b', 'grad_sg_ln_g', 'grad_sg_ln_b', 'grad_sg_w', 'grad_sg_b', 'delta_pre_norm', 'delta_post_norm', 'delta_w_in', 'delta_w_out', 'delta_q_norm', 'delta_k_norm', 'delta_conv_dw', 'delta_conv_dw_b', 'delta_conv_ln_g', 'delta_conv_ln_b', 'delta_sg_ln_g', 'delta_sg_ln_b', 'delta_sg_w', 'delta_sg_b', 'new_m_pre_norm', 'new_m_post_norm', 'new_m_w_in', 'new_m_w_out', 'new_m_q_norm', 'new_m_k_norm', 'new_m_conv_dw', 'new_m_conv_dw_b', 'new_m_conv_ln_g', 'new_m_conv_ln_b', 'new_m_sg_ln_g', 'new_m_sg_ln_b', 'new_m_sg_w', 'new_m_sg_b', 'new_v_pre_norm', 'new_v_post_norm', 'new_v_w_in', 'new_v_w_out', 'new_v_q_norm', 'new_v_k_norm', 'new_v_conv_dw', 'new_v_conv_dw_b', 'new_v_conv_ln_g', 'new_v_conv_ln_b', 'new_v_sg_ln_g', 'new_v_sg_ln_b', 'new_v_sg_w', 'new_v_sg_b']
TWIN_LEAF_KINDS = {'loss': 'loss', 'grad_x': 'grad_x', 'grad_pre_norm': 'grad_w', 'grad_post_norm': 'grad_w', 'grad_w_in': 'grad_w', 'grad_w_out': 'grad_w', 'grad_q_norm': 'grad_w', 'grad_k_norm': 'grad_w', 'grad_conv_dw': 'grad_w', 'grad_conv_dw_b': 'grad_w', 'grad_conv_ln_g': 'grad_w', 'grad_conv_ln_b': 'grad_w', 'grad_sg_ln_g': 'grad_w', 'grad_sg_ln_b': 'grad_w', 'grad_sg_w': 'grad_w', 'grad_sg_b': 'grad_w', 'delta_pre_norm': 'delta_w', 'delta_post_norm': 'delta_w', 'delta_w_in': 'delta_w', 'delta_w_out': 'delta_w', 'delta_q_norm': 'delta_w', 'delta_k_norm': 'delta_w', 'delta_conv_dw': 'delta_w', 'delta_conv_dw_b': 'delta_w', 'delta_conv_ln_g': 'delta_w', 'delta_conv_ln_b': 'delta_w', 'delta_sg_ln_g': 'delta_w', 'delta_sg_ln_b': 'delta_w', 'delta_sg_w': 'delta_w', 'delta_sg_b': 'delta_w', 'new_m_pre_norm': 'new_m', 'new_m_post_norm': 'new_m', 'new_m_w_in': 'new_m', 'new_m_w_out': 'new_m', 'new_m_q_norm': 'new_m', 'new_m_k_norm': 'new_m', 'new_m_conv_dw': 'new_m', 'new_m_conv_dw_b': 'new_m', 'new_m_conv_ln_g': 'new_m', 'new_m_conv_ln_b': 'new_m', 'new_m_sg_ln_g': 'new_m', 'new_m_sg_ln_b': 'new_m', 'new_m_sg_w': 'new_m', 'new_m_sg_b': 'new_m', 'new_v_pre_norm': 'new_v', 'new_v_post_norm': 'new_v', 'new_v_w_in': 'new_v', 'new_v_w_out': 'new_v', 'new_v_q_norm': 'new_v', 'new_v_k_norm': 'new_v', 'new_v_conv_dw': 'new_v', 'new_v_conv_dw_b': 'new_v', 'new_v_conv_ln_g': 'new_v', 'new_v_conv_ln_b': 'new_v', 'new_v_sg_ln_g': 'new_v', 'new_v_sg_ln_b': 'new_v', 'new_v_sg_w': 'new_v', 'new_v_sg_b': 'new_v'}


def _forward(args):
    return _fwd_reference(*[args[k] for k in FWD_PARAMS])


def _output_shape():
    def fwd():
        inp = _fwd_setup_inputs(0)
        return _fwd_reference(*[inp[k] for k in FWD_PARAMS])
    out = _jax.eval_shape(fwd)
    return out.shape, out.dtype

N_MICROBATCH = 1
ADAM_LR = 0.001
ADAM_B1 = 0.9
ADAM_B2 = 0.999
ADAM_EPS = 1e-08
ADAM_WD = 0.01
ADAM_STEP = 10
PER_EXAMPLE_BATCH_AXIS = {'x': 0, 'loss_target': 0}
SHARED_INPUTS = []
_WEIGHT_DTYPES = {'pre_norm': _jnp.float32, 'post_norm': _jnp.float32, 'w_in': _jnp.float32, 'w_out': _jnp.float32, 'q_norm': _jnp.float32, 'k_norm': _jnp.float32, 'conv_dw': _jnp.float32, 'conv_dw_b': _jnp.float32, 'conv_ln_g': _jnp.float32, 'conv_ln_b': _jnp.float32, 'sg_ln_g': _jnp.float32, 'sg_ln_b': _jnp.float32, 'sg_w': _jnp.float32, 'sg_b': _jnp.float32}
MOMENT_SCALE = {'pre_norm': 1.601612e+00, 'post_norm': 1.283733e+02, 'w_in': 9.657732e-01, 'w_out': 1.784795e+00, 'q_norm': 4.749473e-01, 'k_norm': 4.735369e-01, 'conv_dw': 1.074113e+00, 'conv_dw_b': 7.843295e+00, 'conv_ln_g': 2.731457e+00, 'conv_ln_b': 4.364513e+00, 'sg_ln_g': 1.110424e+00, 'sg_ln_b': 9.637739e-01, 'sg_w': 6.830318e-01, 'sg_b': 6.687281e-01}


def _to_microbatches(a, axis):
    t = _jnp.moveaxis(a, axis, 0)
    t = t.reshape((N_MICROBATCH, t.shape[0] // N_MICROBATCH) + t.shape[1:])
    return _jnp.moveaxis(t, 1, axis + 1)


def setup_inputs(seed: int = 0) -> dict:
    inp = _fwd_setup_inputs(seed)
    key = _jax.random.fold_in(_jax.random.key(seed), 7919)
    shape, _ = _output_shape()
    out = dict(inp)
    out["loss_target"] = _jax.random.normal(_jax.random.fold_in(key, 0), shape, _jnp.float32)
    for i, name in enumerate(TWIN_WEIGHTS):
        w = inp[name].astype(_jnp.float32)
        if MOMENT_SCALE is None:
            s = _jnp.sqrt(_jnp.mean(_jnp.square(w)) + 1e-30)
        else:
            s = MOMENT_SCALE[name]
        km, kv = _jax.random.split(_jax.random.fold_in(key, i + 1))
        out[name] = w
        out["m_" + name] = s * _jax.random.normal(km, w.shape, _jnp.float32)
        out["v_" + name] = (s * s) * _jax.random.uniform(kv, w.shape, _jnp.float32, 0.5, 1.5)
    if N_MICROBATCH > 1:
        for name, axis in PER_EXAMPLE_BATCH_AXIS.items():
            out[name] = _to_microbatches(out[name], axis)
    return {'x': out['x'], 'pre_norm': out['pre_norm'], 'post_norm': out['post_norm'], 'w_in': out['w_in'], 'w_out': out['w_out'], 'q_norm': out['q_norm'], 'k_norm': out['k_norm'], 'conv_dw': out['conv_dw'], 'conv_dw_b': out['conv_dw_b'], 'conv_ln_g': out['conv_ln_g'], 'conv_ln_b': out['conv_ln_b'], 'sg_ln_g': out['sg_ln_g'], 'sg_ln_b': out['sg_ln_b'], 'sg_w': out['sg_w'], 'sg_b': out['sg_b'], 'loss_target': out['loss_target'], 'm_pre_norm': out['m_pre_norm'], 'm_post_norm': out['m_post_norm'], 'm_w_in': out['m_w_in'], 'm_w_out': out['m_w_out'], 'm_q_norm': out['m_q_norm'], 'm_k_norm': out['m_k_norm'], 'm_conv_dw': out['m_conv_dw'], 'm_conv_dw_b': out['m_conv_dw_b'], 'm_conv_ln_g': out['m_conv_ln_g'], 'm_conv_ln_b': out['m_conv_ln_b'], 'm_sg_ln_g': out['m_sg_ln_g'], 'm_sg_ln_b': out['m_sg_ln_b'], 'm_sg_w': out['m_sg_w'], 'm_sg_b': out['m_sg_b'], 'v_pre_norm': out['v_pre_norm'], 'v_post_norm': out['v_post_norm'], 'v_w_in': out['v_w_in'], 'v_w_out': out['v_w_out'], 'v_q_norm': out['v_q_norm'], 'v_k_norm': out['v_k_norm'], 'v_conv_dw': out['v_conv_dw'], 'v_conv_dw_b': out['v_conv_dw_b'], 'v_conv_ln_g': out['v_conv_ln_g'], 'v_conv_ln_b': out['v_conv_ln_b'], 'v_sg_ln_g': out['v_sg_ln_g'], 'v_sg_ln_b': out['v_sg_ln_b'], 'v_sg_w': out['v_sg_w'], 'v_sg_b': out['v_sg_b']}


def _loss(weights, diff, rest, loss_target):
    with _jax.named_scope("forward"):
        args = {**rest, TWIN_DIFF_INPUT: diff, **{k: w.astype(_WEIGHT_DTYPES[k]) for k, w in weights.items()}}
        y = _forward(args)
    with _jax.named_scope("loss_head"):
        err = _jnp.square(y.astype(_jnp.float32) - loss_target)
        return 0.5 * _jnp.sum(_jnp.mean(err, axis=-1)) if err.ndim else 0.5 * err


def _adamw(w, g, m, v):
    m = ADAM_B1 * m + (1.0 - ADAM_B1) * g
    v = ADAM_B2 * v + (1.0 - ADAM_B2) * _jnp.square(g)
    m_hat = m / (1.0 - ADAM_B1 ** ADAM_STEP)
    v_hat = v / (1.0 - ADAM_B2 ** ADAM_STEP)
    delta = -ADAM_LR * (m_hat / (_jnp.sqrt(v_hat) + ADAM_EPS) + ADAM_WD * w)
    return delta, m, v


def reference(x, pre_norm, post_norm, w_in, w_out, q_norm, k_norm, conv_dw, conv_dw_b, conv_ln_g, conv_ln_b, sg_ln_g, sg_ln_b, sg_w, sg_b, loss_target, m_pre_norm, m_post_norm, m_w_in, m_w_out, m_q_norm, m_k_norm, m_conv_dw, m_conv_dw_b, m_conv_ln_g, m_conv_ln_b, m_sg_ln_g, m_sg_ln_b, m_sg_w, m_sg_b, v_pre_norm, v_post_norm, v_w_in, v_w_out, v_q_norm, v_k_norm, v_conv_dw, v_conv_dw_b, v_conv_ln_g, v_conv_ln_b, v_sg_ln_g, v_sg_ln_b, v_sg_w, v_sg_b):
    given = dict(x=x, pre_norm=pre_norm, post_norm=post_norm, w_in=w_in, w_out=w_out, q_norm=q_norm, k_norm=k_norm, conv_dw=conv_dw, conv_dw_b=conv_dw_b, conv_ln_g=conv_ln_g, conv_ln_b=conv_ln_b, sg_ln_g=sg_ln_g, sg_ln_b=sg_ln_b, sg_w=sg_w, sg_b=sg_b, loss_target=loss_target, m_pre_norm=m_pre_norm, m_post_norm=m_post_norm, m_w_in=m_w_in, m_w_out=m_w_out, m_q_norm=m_q_norm, m_k_norm=m_k_norm, m_conv_dw=m_conv_dw, m_conv_dw_b=m_conv_dw_b, m_conv_ln_g=m_conv_ln_g, m_conv_ln_b=m_conv_ln_b, m_sg_ln_g=m_sg_ln_g, m_sg_ln_b=m_sg_ln_b, m_sg_w=m_sg_w, m_sg_b=m_sg_b, v_pre_norm=v_pre_norm, v_post_norm=v_post_norm, v_w_in=v_w_in, v_w_out=v_w_out, v_q_norm=v_q_norm, v_k_norm=v_k_norm, v_conv_dw=v_conv_dw, v_conv_dw_b=v_conv_dw_b, v_conv_ln_g=v_conv_ln_g, v_conv_ln_b=v_conv_ln_b, v_sg_ln_g=v_sg_ln_g, v_sg_ln_b=v_sg_ln_b, v_sg_w=v_sg_w, v_sg_b=v_sg_b)
    weights = {n: given[n] for n in TWIN_WEIGHTS}
    shared = {n: given[n] for n in SHARED_INPUTS}
    per_example = {n: given[n] for n in ['x']}
    grad_fn = _jax.value_and_grad(_loss, argnums=(0, 1))

    def one_microbatch(ex, loss_target):
        ex = dict(ex)
        diff = ex.pop(TWIN_DIFF_INPUT)
        return grad_fn(weights, diff, {**shared, **ex}, loss_target)

    if N_MICROBATCH == 1:
        loss, (grad_w, grad_x) = one_microbatch(per_example, given["loss_target"])
    else:
        def body(carry, xs):
            loss_sum, grad_sum = carry
            l_k, (gw_k, gx_k) = one_microbatch(xs[0], xs[1])
            with _jax.named_scope("update"):
                return (loss_sum + l_k, _jax.tree.map(_jnp.add, grad_sum, gw_k)), gx_k

        init = (_jnp.zeros((), _jnp.float32), _jax.tree.map(_jnp.zeros_like, weights))
        (loss, grad_w), grad_x = _jax.lax.scan(body, init, (per_example, given["loss_target"]))
    with _jax.named_scope("update"):
        delta_w, new_m, new_v = {}, {}, {}
        for n in TWIN_WEIGHTS:
            delta_w[n], new_m[n], new_v[n] = _adamw(weights[n], grad_w[n], given["m_" + n], given["v_" + n])
    return (loss, grad_x, *[grad_w[n] for n in TWIN_WEIGHTS], *[delta_w[n] for n in TWIN_WEIGHTS],
            *[new_m[n] for n in TWIN_WEIGHTS], *[new_v[n] for n in TWIN_WEIGHTS])
```

```python
import math

import jax
import jax.numpy as jnp
from jax import lax
from jax.experimental import pallas as pl
from jax.experimental.pallas import tpu as pltpu

F32, BF16 = jnp.float32, jnp.bfloat16

N_DEV = 8
MESH_AXES = ("x", "y", "c")
EPS = 1e-6
D_MODEL = 1024
HEAD_DIM = 64
ATT_HEADS, KV_HEADS = 8, 2
N_QK = ATT_HEADS + KV_HEADS
GROUP = ATT_HEADS // KV_HEADS
ATT_W, KV_W, CONV_W, SG_W = 512, 128, 256, 256
CONV_K, CONV_PAD, HALO = 31, 15, 16
SG_HEADS, SG_CHUNK = 4, 128
D_IN = 2816
GRID_W = 64
ROPE_THETA = 10000.0
Q_SCALE = HEAD_DIM ** -0.5
LANES = 128

COL_GATT, COL_A, COL_B, COL_GCONV, COL_U, COL_VSG, COL_GSG = 3, 5, 6, 7, 8, 9, 10

ADAM_LR, ADAM_B1, ADAM_B2, ADAM_EPS, ADAM_WD, ADAM_STEP = 0.001, 0.9, 0.999, 1e-08, 0.01, 10

T_ROW = 256
T_PREP = 1024
T_GROUP = 512
BQ, BK = 512, 512
VMEM_MB = 56


def _pcall(body, *, name, grid, in_specs, out_specs, out_shape, scratch=(), sem=None, vmem_mb=None):
    params = {}
    if sem is not None:
        params["dimension_semantics"] = sem
    if vmem_mb is not None:
        params["vmem_limit_bytes"] = vmem_mb << 20
    return pl.pallas_call(body, name=name, grid=grid, in_specs=in_specs, out_specs=out_specs, out_shape=out_shape,
                          scratch_shapes=list(scratch), compiler_params=pltpu.CompilerParams(**params))


def _dot(a, b):
    return jnp.dot(a, b, preferred_element_type=F32)


def _sigmoid(x):
    return 1.0 / (1.0 + jnp.exp(-x))


def _silu_and_grad(x):
    s = _sigmoid(x)
    return x * s, s * (1.0 + x * (1.0 - s))


def _gelu_and_grad(x):
    cdf = 0.5 * (1.0 + lax.erf(x * (1.0 / math.sqrt(2.0))))
    pdf = jnp.exp(-0.5 * x * x) * (1.0 / math.sqrt(2.0 * math.pi))
    return x * cdf, cdf + x * pdf


def _split_dot(y, mat):
    hi = y.astype(BF16)
    lo = (y - hi.astype(F32)).astype(BF16)
    return _dot(hi, mat) + _dot(lo, mat)


def _row_tile(rows, cap):
    best = 8
    for t in range(8, min(rows, cap) + 1, 8):
        if rows % t == 0:
            best = t
    return best


def _exchange(scatter, gather, name):
    n_s = len(scatter)
    arrs = list(scatter) + list(gather)
    n = len(arrs)
    flips = [(fx, fy, fc) for fx in (0, 1) for fy in (0, 1) for fc in (0, 1)][1:]
    n_peer = len(flips)

    def body(*refs):
        ins, outs = refs[:n], refs[n:2 * n]
        send_sems, recv_sems, local_sems = refs[2 * n:]
        pos = tuple(lax.axis_index(a) for a in MESH_AXES)

        def peer(flip):
            return tuple((1 - p) if f else p for p, f in zip(pos, flip))

        def slot(p):
            return 4 * p[0] + 2 * p[1] + p[2]

        def src(a, p):
            return ins[a].at[slot(p)] if a < n_s else ins[a]

        def remote(a, k, src_ref, dst_slot, to):
            return pltpu.make_async_remote_copy(
                src_ref=src_ref, dst_ref=outs[a].at[dst_slot], send_sem=send_sems.at[a * n_peer + k],
                recv_sem=recv_sems.at[a * n_peer + k], device_id=to, device_id_type=pl.DeviceIdType.MESH)

        local = [pltpu.make_async_copy(src(a, pos), outs[a].at[slot(pos)], local_sems.at[a]) for a in range(n)]
        for cp in local:
            cp.start()
        sends = [remote(a, k, src(a, peer(f)), slot(pos), peer(f)) for a in range(n) for k, f in enumerate(flips)]
        for cp in sends:
            cp.start()
        for a in range(n):
            for k, f in enumerate(flips):
                remote(a, k, src(a, peer(f)), slot(peer(f)), peer(f)).wait_recv()
        for cp in sends:
            cp.wait_send()
        for cp in local:
            cp.wait()

    out_shape = [jax.ShapeDtypeStruct((N_DEV,) + (a.shape[1:] if i < n_s else a.shape), a.dtype)
                 for i, a in enumerate(arrs)]
    any_spec = pl.BlockSpec(memory_space=pl.ANY)
    return pl.pallas_call(
        body, name=name, out_shape=out_shape, in_specs=[any_spec] * n, out_specs=[any_spec] * n,
        scratch_shapes=[pltpu.SemaphoreType.DMA((n * n_peer,)), pltpu.SemaphoreType.DMA((n * n_peer,)),
                        pltpu.SemaphoreType.DMA((n,))],
    )(*arrs)


def _sum_adamw(slots, w, m, v, name):
    rows = w.shape[0]
    tr = _row_tile(rows, 1024)
    c1 = 1.0 - ADAM_B1 ** ADAM_STEP
    c2 = 1.0 - ADAM_B2 ** ADAM_STEP

    def body(s_ref, w_ref, m_ref, v_ref, g_out, d_out, m_out, v_out):
        g = s_ref[0]
        for d in range(1, N_DEV):
            g = g + s_ref[d]
        m_new = ADAM_B1 * m_ref[...] + (1.0 - ADAM_B1) * g
        v_new = ADAM_B2 * v_ref[...] + (1.0 - ADAM_B2) * (g * g)
        m_hat = m_new / c1
        v_hat = v_new / c2
        g_out[...] = g
        d_out[...] = -ADAM_LR * (m_hat / (jnp.sqrt(v_hat) + ADAM_EPS) + ADAM_WD * w_ref[...])
        m_out[...] = m_new
        v_out[...] = v_new

    flat = pl.BlockSpec((tr, LANES), lambda i: (i, 0))
    return _pcall(
        body, name=name, grid=(rows // tr,),
        in_specs=[pl.BlockSpec((N_DEV, tr, LANES), lambda i: (0, i, 0)), flat, flat, flat],
        out_specs=[flat] * 4, out_shape=[jax.ShapeDtypeStruct((rows, LANES), F32)] * 4,
        sem=("parallel",), vmem_mb=VMEM_MB)(slots, w, m, v)


def _proj_fwd(x, gain, w, name):
    seq = x.shape[0]

    def body(x_ref, g_ref, w_ref, proj_ref, hb_ref):
        xf = x_ref[...]
        r = lax.rsqrt(jnp.mean(xf * xf, axis=-1, keepdims=True) + EPS)
        h = (xf * r * g_ref[...]).astype(BF16)
        hb_ref[...] = h
        proj_ref[...] = _dot(h, w_ref[...])

    return _pcall(
        body, name=name, grid=(seq // T_ROW,),
        in_specs=[pl.BlockSpec((T_ROW, D_MODEL), lambda i: (i, 0)), pl.BlockSpec((1, D_MODEL), lambda i: (0, 0)),
                  pl.BlockSpec((D_MODEL, D_IN), lambda i: (0, 0))],
        out_specs=[pl.BlockSpec((T_ROW, D_IN), lambda i: (i, 0)), pl.BlockSpec((T_ROW, D_MODEL), lambda i: (i, 0))],
        out_shape=[jax.ShapeDtypeStruct((seq, D_IN), F32), jax.ShapeDtypeStruct((seq, D_MODEL), BF16)],
        sem=("parallel",), vmem_mb=VMEM_MB)(x, gain, w)


def _proj_bwd(dproj, w_t, x, gain, dxo, name):
    seq = x.shape[0]

    def body(dp_ref, w_ref, x_ref, g_ref, dxo_ref, dx_ref, dg_ref):
        dh = _dot(dp_ref[...], w_ref[...])
        xf = x_ref[...]
        r = lax.rsqrt(jnp.mean(xf * xf, axis=-1, keepdims=True) + EPS)
        n = xf * r
        dn = dh * g_ref[...]
        dx_ref[...] = dxo_ref[...] + r * (dn - n * jnp.mean(dn * n, axis=-1, keepdims=True))

        @pl.when(pl.program_id(0) == 0)
        def _():
            dg_ref[...] = jnp.zeros_like(dg_ref)

        dg_ref[...] += jnp.sum(dh * n, axis=0, keepdims=True)

    row = pl.BlockSpec((T_ROW, D_MODEL), lambda i: (i, 0))
    vec = pl.BlockSpec((1, D_MODEL), lambda i: (0, 0))
    return _pcall(
        body, name=name, grid=(seq // T_ROW,),
        in_specs=[pl.BlockSpec((T_ROW, D_IN), lambda i: (i, 0)), pl.BlockSpec((D_IN, D_MODEL), lambda i: (0, 0)),
                  row, vec, row],
        out_specs=[row, vec],
        out_shape=[jax.ShapeDtypeStruct((seq, D_MODEL), F32), jax.ShapeDtypeStruct((1, D_MODEL), F32)],
        sem=("arbitrary",), vmem_mb=VMEM_MB)(dproj, w_t, x, gain, dxo)


def _matmul_acc(a_t, b, tn, name):
    m, seq = a_t.shape
    n = b.shape[1]
    ts = min(512, seq)

    def body(a_ref, b_ref, o_ref):
        @pl.when(pl.program_id(1) == 0)
        def _():
            o_ref[...] = jnp.zeros_like(o_ref)

        o_ref[...] += _dot(a_ref[...], b_ref[...])

    return _pcall(
        body, name=name, grid=(n // tn, seq // ts),
        in_specs=[pl.BlockSpec((m, ts), lambda j, k: (0, k)), pl.BlockSpec((ts, tn), lambda j, k: (k, j))],
        out_specs=pl.BlockSpec((m, tn), lambda j, k: (0, j)), out_shape=jax.ShapeDtypeStruct((m, n), F32),
        sem=("parallel", "arbitrary"), vmem_mb=VMEM_MB)(a_t, b)


def _rope_tables(seq):
    t = jnp.arange(seq, dtype=jnp.int32)
    row = (t // GRID_W).astype(F32)
    col = (t % GRID_W).astype(F32)
    half = HEAD_DIM // 4
    inv_freq = ROPE_THETA ** (-jnp.arange(half, dtype=F32) / half)
    ang_r = row[:, None] * inv_freq[None, :]
    ang_c = col[:, None] * inv_freq[None, :]
    cos = jnp.concatenate([jnp.cos(ang_r), jnp.cos(ang_r), jnp.cos(ang_c), jnp.cos(ang_c)], axis=-1)
    sin = jnp.concatenate([-jnp.sin(ang_r), jnp.sin(ang_r), -jnp.sin(ang_c), jnp.sin(ang_c)], axis=-1)
    j = jnp.arange(HEAD_DIM)
    partner = jnp.where((j % (2 * half)) < half, j + half, j - half)
    perm = (j[:, None] == partner[None, :]).astype(BF16)
    return cos, sin, perm


def _qk_specs(seq):
    t = min(T_PREP, seq)
    blk = pl.BlockSpec((None, t, HEAD_DIM), lambda h, i: (h, i, 0))
    vec = pl.BlockSpec((None, 1, HEAD_DIM), lambda h, i: (h, 0, 0))
    tab = pl.BlockSpec((t, HEAD_DIM), lambda h, i: (i, 0))
    mat = pl.BlockSpec((HEAD_DIM, HEAD_DIM), lambda h, i: (0, 0))
    return t, blk, vec, tab, mat


def _qk_prep_fwd(raw, gains, post, cos, sin, perm, name):
    seq = raw.shape[1]
    t, blk, vec, tab, mat = _qk_specs(seq)

    def body(x_ref, g_ref, p_ref, c_ref, s_ref, m_ref, o_ref):
        xf = x_ref[...]
        r = lax.rsqrt(jnp.mean(xf * xf, axis=-1, keepdims=True) + EPS)
        y = xf * r * g_ref[...]
        z = y * c_ref[...] + _split_dot(y, m_ref[...]) * s_ref[...]
        o_ref[...] = (z * p_ref[...]).astype(BF16)

    return _pcall(body, name=name, grid=(N_QK, seq // t), in_specs=[blk, vec, vec, tab, tab, mat], out_specs=blk,
                  out_shape=jax.ShapeDtypeStruct(raw.shape, BF16), sem=("parallel", "parallel"))(
                      raw, gains, post, cos, sin, perm)


def _qk_prep_bwd(raw, dout, gains, post, cos, sin, perm, name):
    seq = raw.shape[1]
    t, blk, vec, tab, mat = _qk_specs(seq)

    def body(x_ref, d_ref, g_ref, p_ref, c_ref, s_ref, m_ref, dx_ref, dg_ref):
        xf = x_ref[...]
        r = lax.rsqrt(jnp.mean(xf * xf, axis=-1, keepdims=True) + EPS)
        n = xf * r
        dz = d_ref[...] * p_ref[...]
        dy = dz * c_ref[...] + _split_dot(dz * s_ref[...], m_ref[...])
        dn = dy * g_ref[...]
        dx_ref[...] = r * (dn - n * jnp.mean(dn * n, axis=-1, keepdims=True))

        @pl.when(pl.program_id(1) == 0)
        def _():
            dg_ref[...] = jnp.zeros_like(dg_ref)

        dg_ref[...] += jnp.sum(dy * n, axis=0, keepdims=True)

    return _pcall(body, name=name, grid=(N_QK, seq // t), in_specs=[blk, blk, vec, vec, tab, tab, mat],
                  out_specs=[blk, vec],
                  out_shape=[jax.ShapeDtypeStruct(raw.shape, F32), jax.ShapeDtypeStruct((N_QK, 1, HEAD_DIM), F32)],
                  sem=("parallel", "arbitrary"))(raw, dout, gains, post, cos, sin, perm)


def _attn_fwd(qs, k_t, v, name):
    seq = qs.shape[1]
    nk, bk = k_t.shape[1], k_t.shape[3]
    bq = min(BQ, seq)

    def body(q_ref, kt_ref, v_ref, o_ref, lse_ref):
        q = q_ref[...]

        def step(j, carry):
            m, l, acc = carry
            s = _dot(q, kt_ref[j])
            m_new = jnp.maximum(m, jnp.max(s, axis=-1, keepdims=True))
            alpha = jnp.exp(m - m_new)
            p = jnp.exp(s - m_new)
            l = alpha * l + jnp.sum(p, axis=-1, keepdims=True)
            acc = alpha * acc + _dot(p.astype(BF16), v_ref[j])
            return m_new, l, acc

        init = (jnp.full((bq, 1), -jnp.inf, F32), jnp.zeros((bq, 1), F32), jnp.zeros((bq, HEAD_DIM), F32))
        m, l, acc = lax.fori_loop(0, nk, step, init)
        o_ref[...] = acc / l
        lse_ref[...] = jnp.broadcast_to(m + jnp.log(l), (bq, LANES))

    return _pcall(
        body, name=name, grid=(ATT_HEADS, seq // bq),
        in_specs=[pl.BlockSpec((None, bq, HEAD_DIM), lambda h, i: (h, i, 0)),
                  pl.BlockSpec((None, nk, HEAD_DIM, bk), lambda h, i: (h // GROUP, 0, 0, 0)),
                  pl.BlockSpec((None, nk, bk, HEAD_DIM), lambda h, i: (h // GROUP, 0, 0, 0))],
        out_specs=[pl.BlockSpec((None, bq, HEAD_DIM), lambda h, i: (h, i, 0)),
                   pl.BlockSpec((None, bq, LANES), lambda h, i: (h, i, 0))],
        out_shape=[jax.ShapeDtypeStruct((ATT_HEADS, seq, HEAD_DIM), F32),
                   jax.ShapeDtypeStruct((ATT_HEADS, seq, LANES), F32)],
        sem=("parallel", "parallel"), vmem_mb=VMEM_MB)(qs, k_t, v)


def _attn_bwd(qs, qs_t, do, do_t, o, lse, k, k_t, v_t, name):
    seq = qs.shape[1]
    nk, bk = k.shape[1], k.shape[2]
    bq = min(BQ, seq)
    nq = seq // bq

    def body(q_ref, qt_ref, do_ref, dot_ref, o_ref, lse_ref, k_ref, kt_ref, vt_ref, dq_ref, dkt_ref, dvt_ref):
        @pl.when((pl.program_id(1) == 0) & (pl.program_id(2) == 0))
        def _():
            dkt_ref[...] = jnp.zeros_like(dkt_ref)
            dvt_ref[...] = jnp.zeros_like(dvt_ref)

        q, q_t, do_t_b = q_ref[...], qt_ref[...], dot_ref[...]
        do_f = do_ref[...]
        do_b = do_f.astype(BF16)
        delta = jnp.sum(do_f * o_ref[...], axis=-1, keepdims=True)
        lse_col = jnp.max(lse_ref[...], axis=-1, keepdims=True)

        def step(j, dq):
            s = _dot(q, kt_ref[j])
            p = jnp.exp(s - lse_col)
            dp = _dot(do_b, vt_ref[j])
            ds = (p * (dp - delta)).astype(BF16)
            dvt_ref[j] += _dot(do_t_b, p.astype(BF16))
            dkt_ref[j] += _dot(q_t, ds)
            return dq + _dot(ds, k_ref[j])

        dq_ref[...] = lax.fori_loop(0, nk, step, jnp.zeros((bq, HEAD_DIM), F32))

    head = lambda g, hh, i: g * GROUP + hh
    row = pl.BlockSpec((None, bq, HEAD_DIM), lambda g, hh, i: (head(g, hh, i), i, 0))
    col = pl.BlockSpec((None, HEAD_DIM, bq), lambda g, hh, i: (head(g, hh, i), 0, i))
    kv_rows = pl.BlockSpec((None, nk, bk, HEAD_DIM), lambda g, hh, i: (g, 0, 0, 0))
    kv_cols = pl.BlockSpec((None, nk, HEAD_DIM, bk), lambda g, hh, i: (g, 0, 0, 0))
    return _pcall(
        body, name=name, grid=(KV_HEADS, GROUP, nq),
        in_specs=[row, col, row, col, row, pl.BlockSpec((None, bq, LANES), lambda g, hh, i: (head(g, hh, i), i, 0)),
                  kv_rows, kv_cols, kv_cols],
        out_specs=[row, kv_cols, kv_cols],
        out_shape=[jax.ShapeDtypeStruct((ATT_HEADS, seq, HEAD_DIM), F32),
                   jax.ShapeDtypeStruct((KV_HEADS, nk, HEAD_DIM, bk), F32),
                   jax.ShapeDtypeStruct((KV_HEADS, nk, HEAD_DIM, bk), F32)],
        sem=("parallel", "arbitrary", "arbitrary"), vmem_mb=VMEM_MB)(qs, qs_t, do, do_t, o, lse, k, k_t, v_t)


def _halo_specs(t, col, n_tiles):
    per = t // HALO
    last = n_tiles * per - 1
    before = pl.BlockSpec((HALO, CONV_W), lambda i: (jnp.maximum(i * per - 1, 0), col))
    after = pl.BlockSpec((HALO, CONV_W), lambda i: (jnp.minimum((i + 1) * per, last), col))
    return before, after


def _glu(a, b):
    return a * _sigmoid(b)


def _conv_taps(ext_ref, w_ref, t, flip):
    acc = jnp.zeros((t, CONV_W), F32)
    for k in range(CONV_K):
        off = (HALO + CONV_PAD - k) if flip else (HALO - CONV_PAD + k)
        acc = acc + w_ref[k:k + 1, :] * ext_ref[pl.ds(off, t), :]
    return acc


def _fill_ext(ext_ref, before, tile, after, t, i, n_tiles):
    ext_ref[pl.ds(0, HALO), :] = jnp.where(i > 0, before, 0.0)
    ext_ref[pl.ds(HALO, t), :] = tile
    ext_ref[pl.ds(HALO + t, HALO), :] = jnp.where(i < n_tiles - 1, after, 0.0)


def _conv_fwd(proj, w, bias, ln_g, ln_b, name):
    seq = proj.shape[0]
    t = min(T_GROUP, seq)
    n_tiles = seq // t

    def body(a_ref, b_ref, ap_ref, bp_ref, an_ref, bn_ref, gate_ref, w_ref, bias_ref, g_ref, beta_ref, o_ref, ext_ref):
        i = pl.program_id(0)
        _fill_ext(ext_ref, _glu(ap_ref[...], bp_ref[...]), _glu(a_ref[...], b_ref[...]),
                  _glu(an_ref[...], bn_ref[...]), t, i, n_tiles)
        y = _conv_taps(ext_ref, w_ref, t, False) + bias_ref[...]
        mu = jnp.mean(y, axis=-1, keepdims=True)
        yc = y - mu
        rs = lax.rsqrt(jnp.mean(yc * yc, axis=-1, keepdims=True) + EPS)
        z = yc * rs * g_ref[...] + beta_ref[...]
        o_ref[...] = _silu_and_grad(z)[0] * _silu_and_grad(gate_ref[...])[0]

    tile = lambda c: pl.BlockSpec((t, CONV_W), lambda i: (i, c))
    ab, aa = _halo_specs(t, COL_A, n_tiles)
    bb, ba = _halo_specs(t, COL_B, n_tiles)
    vec = pl.BlockSpec((1, CONV_W), lambda i: (0, 0))
    return _pcall(
        body, name=name, grid=(n_tiles,),
        in_specs=[tile(COL_A), tile(COL_B), ab, bb, aa, ba, tile(COL_GCONV),
                  pl.BlockSpec((CONV_K, CONV_W), lambda i: (0, 0)), vec, vec, vec],
        out_specs=pl.BlockSpec((t, CONV_W), lambda i: (i, 0)), out_shape=jax.ShapeDtypeStruct((seq, CONV_W), F32),
        scratch=[pltpu.VMEM((t + 2 * HALO, CONV_W), F32)], sem=("parallel",))(
            proj, proj, proj, proj, proj, proj, proj, w, bias, ln_g, ln_b)


def _conv_bwd_a(proj, dcnv, w, bias, ln_g, ln_b, name):
    seq = proj.shape[0]
    t = min(T_GROUP, seq)
    n_tiles = seq // t

    def body(a_ref, b_ref, ap_ref, bp_ref, an_ref, bn_ref, gate_ref, d_ref, w_ref, bias_ref, g_ref, beta_ref,
             dy_ref, dgate_ref, dw_ref, dbias_ref, dg_ref, dbeta_ref, ext_ref):
        i = pl.program_id(0)
        _fill_ext(ext_ref, _glu(ap_ref[...], bp_ref[...]), _glu(a_ref[...], b_ref[...]),
                  _glu(an_ref[...], bn_ref[...]), t, i, n_tiles)
        y = _conv_taps(ext_ref, w_ref, t, False) + bias_ref[...]
        mu = jnp.mean(y, axis=-1, keepdims=True)
        yc = y - mu
        rs = lax.rsqrt(jnp.mean(yc * yc, axis=-1, keepdims=True) + EPS)
        n = yc * rs
        z = n * g_ref[...] + beta_ref[...]
        act, dact = _silu_and_grad(z)
        gate, dgate = _silu_and_grad(gate_ref[...])
        d = d_ref[...]
        dgate_ref[...] = d * act * dgate
        dz = d * gate * dact
        dn = dz * g_ref[...]
        dy = rs * (dn - jnp.mean(dn, axis=-1, keepdims=True) - n * jnp.mean(dn * n, axis=-1, keepdims=True))
        dy_ref[...] = dy

        @pl.when(i == 0)
        def _():
            dw_ref[...] = jnp.zeros_like(dw_ref)
            dbias_ref[...] = jnp.zeros_like(dbias_ref)
            dg_ref[...] = jnp.zeros_like(dg_ref)
            dbeta_ref[...] = jnp.zeros_like(dbeta_ref)

        dg_ref[...] += jnp.sum(dz * n, axis=0, keepdims=True)
        dbeta_ref[...] += jnp.sum(dz, axis=0, keepdims=True)
        dbias_ref[...] += jnp.sum(dy, axis=0, keepdims=True)
        for k in range(CONV_K):
            dw_ref[k:k + 1, :] += jnp.sum(dy * ext_ref[pl.ds(HALO - CONV_PAD + k, t), :], axis=0, keepdims=True)

    tile = lambda c: pl.BlockSpec((t, CONV_W), lambda i: (i, c))
    own = pl.BlockSpec((t, CONV_W), lambda i: (i, 0))
    ab, aa = _halo_specs(t, COL_A, n_tiles)
    bb, ba = _halo_specs(t, COL_B, n_tiles)
    vec = pl.BlockSpec((1, CONV_W), lambda i: (0, 0))
    taps = pl.BlockSpec((CONV_K, CONV_W), lambda i: (0, 0))
    vshape = jax.ShapeDtypeStruct((1, CONV_W), F32)
    return _pcall(
        body, name=name, grid=(n_tiles,),
        in_specs=[tile(COL_A), tile(COL_B), ab, bb, aa, ba, tile(COL_GCONV), own, taps, vec, vec, vec],
        out_specs=[own, own, taps, vec, vec, vec],
        out_shape=[jax.ShapeDtypeStruct((seq, CONV_W), F32), jax.ShapeDtypeStruct((seq, CONV_W), F32),
                   jax.ShapeDtypeStruct((CONV_K, CONV_W), F32), vshape, vshape, vshape],
        scratch=[pltpu.VMEM((t + 2 * HALO, CONV_W), F32)], sem=("arbitrary",))(
            proj, proj, proj, proj, proj, proj, proj, dcnv, w, bias, ln_g, ln_b)


def _conv_bwd_b(proj, dy, w, name):
    seq = proj.shape[0]
    t = min(T_GROUP, seq)
    n_tiles = seq // t

    def body(a_ref, b_ref, dy_ref, dyp_ref, dyn_ref, w_ref, da_ref, db_ref, ext_ref):
        i = pl.program_id(0)
        _fill_ext(ext_ref, dyp_ref[...], dy_ref[...], dyn_ref[...], t, i, n_tiles)
        dh = _conv_taps(ext_ref, w_ref, t, True)
        sig = _sigmoid(b_ref[...])
        da_ref[...] = dh * sig
        db_ref[...] = dh * a_ref[...] * sig * (1.0 - sig)

    tile = lambda c: pl.BlockSpec((t, CONV_W), lambda i: (i, c))
    own = pl.BlockSpec((t, CONV_W), lambda i: (i, 0))
    before, after = _halo_specs(t, 0, n_tiles)
    return _pcall(
        body, name=name, grid=(n_tiles,),
        in_specs=[tile(COL_A), tile(COL_B), own, before, after, pl.BlockSpec((CONV_K, CONV_W), lambda i: (0, 0))],
        out_specs=[own, own], out_shape=[jax.ShapeDtypeStruct((seq, CONV_W), F32)] * 2,
        scratch=[pltpu.VMEM((t + 2 * HALO, CONV_W), F32)], sem=("parallel",))(proj, proj, dy, dy, dy, w)


def _head_masks():
    lane_head = lax.broadcasted_iota(jnp.int32, (SG_CHUNK, SG_W), 1) // HEAD_DIM
    return [lane_head == h for h in range(SG_HEADS)]


def _sg_mix(mats_ref, rhs, masks):
    out = jnp.zeros((SG_CHUNK, SG_W), F32)
    for h in range(SG_HEADS):
        out = out + jnp.where(masks[h], _dot(mats_ref[h], rhs), 0.0)
    return out


def _sg_specs(seq):
    t = min(T_GROUP, seq)
    tile = lambda c: pl.BlockSpec((t, SG_W), lambda i: (i, c))
    own = pl.BlockSpec((t, SG_W), lambda i: (i, 0))
    vec = pl.BlockSpec((1, SG_W), lambda i: (0, 0))
    mats = pl.BlockSpec((SG_HEADS, SG_CHUNK, SG_CHUNK), lambda i: (0, 0, 0))
    full = pl.BlockSpec((SG_CHUNK, SG_W), lambda i: (0, 0))
    return t, tile, own, vec, mats, full


def _sg_fwd(proj, ln_g, ln_b, w_b, bias_full, name):
    seq = proj.shape[0]
    t, tile, own, vec, mats, full = _sg_specs(seq)

    def body(u_ref, v_ref, gate_ref, g_ref, beta_ref, w_ref, bias_ref, o_ref):
        masks = _head_masks()
        for c in range(t // SG_CHUNK):
            rows = pl.ds(c * SG_CHUNK, SG_CHUNK)
            vg = _gelu_and_grad(v_ref[rows, :])[0]
            mu = jnp.mean(vg, axis=-1, keepdims=True)
            vc = vg - mu
            rs = lax.rsqrt(jnp.mean(vc * vc, axis=-1, keepdims=True) + EPS)
            vln = vc * rs * g_ref[...] + beta_ref[...]
            mixed = _sg_mix(w_ref, vln.astype(BF16), masks) + bias_ref[...]
            o_ref[rows, :] = _gelu_and_grad(u_ref[rows, :])[0] * mixed * _silu_and_grad(gate_ref[rows, :])[0]

    return _pcall(body, name=name, grid=(seq // t,),
                  in_specs=[tile(COL_U), tile(COL_VSG), tile(COL_GSG), vec, vec, mats, full], out_specs=own,
                  out_shape=jax.ShapeDtypeStruct((seq, SG_W), F32), sem=("parallel",))(
                      proj, proj, proj, ln_g, ln_b, w_b, bias_full)


def _sg_bwd(proj, dsg, ln_g, ln_b, w_b, w_t_b, bias_full, fold, name):
    seq = proj.shape[0]
    t, tile, own, vec, mats, full = _sg_specs(seq)
    n_tiles = seq // t

    def body(u_ref, v_ref, gate_ref, d_ref, g_ref, beta_ref, w_ref, wt_ref, bias_ref, fold_ref,
             du_ref, dv_ref, dgate_ref, dg_ref, dbeta_ref, dw_ref, db_ref, dbias_acc):
        i = pl.program_id(0)

        @pl.when(i == 0)
        def _():
            dg_ref[...] = jnp.zeros_like(dg_ref)
            dbeta_ref[...] = jnp.zeros_like(dbeta_ref)
            dw_ref[...] = jnp.zeros_like(dw_ref)
            dbias_acc[...] = jnp.zeros_like(dbias_acc)

        masks = _head_masks()
        for c in range(t // SG_CHUNK):
            rows = pl.ds(c * SG_CHUNK, SG_CHUNK)
            ug, dug = _gelu_and_grad(u_ref[rows, :])
            vg, dvg = _gelu_and_grad(v_ref[rows, :])
            mu = jnp.mean(vg, axis=-1, keepdims=True)
            vc = vg - mu
            rs = lax.rsqrt(jnp.mean(vc * vc, axis=-1, keepdims=True) + EPS)
            vn = vc * rs
            vln_b = (vn * g_ref[...] + beta_ref[...]).astype(BF16)
            mixed = _sg_mix(w_ref, vln_b, masks) + bias_ref[...]
            gate, dgate = _silu_and_grad(gate_ref[rows, :])
            d = d_ref[rows, :]
            dgate_ref[rows, :] = d * ug * mixed * dgate
            du_ref[rows, :] = d * mixed * gate * dug
            dmixed = d * ug * gate
            dbias_acc[...] += dmixed
            dmixed_b = dmixed.astype(BF16)
            for h in range(SG_HEADS):
                dm_h = jnp.where(masks[h], dmixed_b, jnp.zeros_like(dmixed_b))
                dw_ref[h] += lax.dot_general(dm_h, vln_b, (((1,), (1,)), ((), ())), preferred_element_type=F32)
            dvln = _sg_mix(wt_ref, dmixed_b, masks)
            dg_ref[...] += jnp.sum(dvln * vn, axis=0, keepdims=True)
            dbeta_ref[...] += jnp.sum(dvln, axis=0, keepdims=True)
            dvn = dvln * g_ref[...]
            dvgelu = rs * (dvn - jnp.mean(dvn, axis=-1, keepdims=True) - vn * jnp.mean(dvn * vn, axis=-1, keepdims=True))
            dv_ref[rows, :] = dvgelu * dvg

        @pl.when(i == n_tiles - 1)
        def _():
            db_ref[...] = _split_dot(dbias_acc[...], fold_ref[...])

    sq = pl.BlockSpec((SG_CHUNK, SG_CHUNK), lambda i: (0, 0))
    vshape = jax.ShapeDtypeStruct((1, SG_W), F32)
    return _pcall(
        body, name=name, grid=(n_tiles,),
        in_specs=[tile(COL_U), tile(COL_VSG), tile(COL_GSG), own, vec, vec, mats, mats, full,
                  pl.BlockSpec((SG_W, SG_CHUNK), lambda i: (0, 0))],
        out_specs=[own, own, own, vec, vec, mats, sq],
        out_shape=[jax.ShapeDtypeStruct((seq, SG_W), F32)] * 3 + [
            vshape, vshape, jax.ShapeDtypeStruct((SG_HEADS, SG_CHUNK, SG_CHUNK), F32),
            jax.ShapeDtypeStruct((SG_CHUNK, SG_CHUNK), F32)],
        scratch=[pltpu.VMEM((SG_CHUNK, SG_W), F32)], sem=("arbitrary",))(
            proj, proj, proj, dsg, ln_g, ln_b, w_b, w_t_b, bias_full, fold)


def _out_fwd(att, proj, cnv, sgu, x, w, gain, name):
    seq = x.shape[0]

    def body(att_ref, g0_ref, g1_ref, cnv_ref, sgu_ref, x_ref, w_ref, gain_ref, xo_ref, mix_ref, cat_ref):
        gate = jnp.concatenate([_silu_and_grad(g0_ref[...])[0], _silu_and_grad(g1_ref[...])[0]], axis=-1)
        cat_ref[:, 0:ATT_W] = (att_ref[...] * gate).astype(BF16)
        cat_ref[:, ATT_W:ATT_W + CONV_W] = cnv_ref[...].astype(BF16)
        cat_ref[:, ATT_W + CONV_W:] = sgu_ref[...].astype(BF16)
        mix = _dot(cat_ref[...], w_ref[...])
        mix_ref[...] = mix
        r = lax.rsqrt(jnp.mean(mix * mix, axis=-1, keepdims=True) + EPS)
        xo_ref[...] = x_ref[...] + mix * r * gain_ref[...]

    row = lambda w_: pl.BlockSpec((T_ROW, w_), lambda i: (i, 0))
    gate_blk = lambda c: pl.BlockSpec((T_ROW, 256), lambda i: (i, c))
    return _pcall(
        body, name=name, grid=(seq // T_ROW,),
        in_specs=[row(ATT_W), gate_blk(COL_GATT), gate_blk(COL_GATT + 1), row(CONV_W), row(SG_W), row(D_MODEL),
                  pl.BlockSpec((D_MODEL, D_MODEL), lambda i: (0, 0)), pl.BlockSpec((1, D_MODEL), lambda i: (0, 0))],
        out_specs=[row(D_MODEL), row(D_MODEL), row(D_MODEL)],
        out_shape=[jax.ShapeDtypeStruct((seq, D_MODEL), F32), jax.ShapeDtypeStruct((seq, D_MODEL), F32),
                   jax.ShapeDtypeStruct((seq, D_MODEL), BF16)],
        sem=("parallel",), vmem_mb=VMEM_MB)(att, proj, proj, cnv, sgu, x, w, gain)


def _out_bwd(dxo, mix, gain, w_t, att, proj, name):
    seq = dxo.shape[0]

    def body(dxo_ref, mix_ref, gain_ref, w_ref, att_ref, g0_ref, g1_ref,
             dmix_ref, datt_ref, dgatt_ref, dcnv_ref, dsgu_ref, dgain_ref):
        mix = mix_ref[...]
        r = lax.rsqrt(jnp.mean(mix * mix, axis=-1, keepdims=True) + EPS)
        n = mix * r
        dout = dxo_ref[...]
        dn = dout * gain_ref[...]
        dmix = (r * (dn - n * jnp.mean(dn * n, axis=-1, keepdims=True))).astype(BF16)
        dmix_ref[...] = dmix

        @pl.when(pl.program_id(0) == 0)
        def _():
            dgain_ref[...] = jnp.zeros_like(dgain_ref)

        dgain_ref[...] += jnp.sum(dout * n, axis=0, keepdims=True)
        dcat = _dot(dmix, w_ref[...])
        g0, dg0 = _silu_and_grad(g0_ref[...])
        g1, dg1 = _silu_and_grad(g1_ref[...])
        gate = jnp.concatenate([g0, g1], axis=-1)
        dgate = jnp.concatenate([dg0, dg1], axis=-1)
        dca = dcat[:, 0:ATT_W]
        datt_ref[...] = dca * gate
        dgatt_ref[...] = dca * att_ref[...] * dgate
        dcnv_ref[...] = dcat[:, ATT_W:ATT_W + CONV_W]
        dsgu_ref[...] = dcat[:, ATT_W + CONV_W:]

    row = lambda w_: pl.BlockSpec((T_ROW, w_), lambda i: (i, 0))
    gate_blk = lambda c: pl.BlockSpec((T_ROW, 256), lambda i: (i, c))
    vec = pl.BlockSpec((1, D_MODEL), lambda i: (0, 0))
    return _pcall(
        body, name=name, grid=(seq // T_ROW,),
        in_specs=[row(D_MODEL), row(D_MODEL), vec, pl.BlockSpec((D_MODEL, D_MODEL), lambda i: (0, 0)), row(ATT_W),
                  gate_blk(COL_GATT), gate_blk(COL_GATT + 1)],
        out_specs=[row(D_MODEL), row(ATT_W), row(ATT_W), row(CONV_W), row(SG_W), vec],
        out_shape=[jax.ShapeDtypeStruct((seq, D_MODEL), BF16), jax.ShapeDtypeStruct((seq, ATT_W), F32),
                   jax.ShapeDtypeStruct((seq, ATT_W), F32), jax.ShapeDtypeStruct((seq, CONV_W), F32),
                   jax.ShapeDtypeStruct((seq, SG_W), F32), jax.ShapeDtypeStruct((1, D_MODEL), F32)],
        sem=("arbitrary",), vmem_mb=VMEM_MB)(dxo, mix, gain, w_t, att, proj, proj)


def _loss_head(y, target, name):
    seq = y.shape[0]
    t = min(T_GROUP, seq)

    def body(y_ref, t_ref, sse_ref, dy_ref):
        err = y_ref[...] - t_ref[...]
        dy_ref[...] = err * (1.0 / D_MODEL)

        @pl.when(pl.program_id(0) == 0)
        def _():
            sse_ref[...] = jnp.zeros_like(sse_ref)

        part = jnp.sum(jnp.sum(err * err, axis=0, keepdims=True), axis=-1, keepdims=True)
        sse_ref[...] += jnp.broadcast_to(part, (1, LANES))

    row = pl.BlockSpec((t, D_MODEL), lambda i: (i, 0))
    return _pcall(body, name=name, grid=(seq // t,), in_specs=[row, row],
                  out_specs=[pl.BlockSpec((1, LANES), lambda i: (0, 0)), row],
                  out_shape=[jax.ShapeDtypeStruct((1, LANES), F32), jax.ShapeDtypeStruct((seq, D_MODEL), F32)],
                  sem=("arbitrary",))(y, target)


def _to_heads(a, heads):
    return a.reshape(a.shape[0], heads, HEAD_DIM).transpose(1, 0, 2)


def _from_heads(a):
    return a.transpose(1, 0, 2).reshape(a.shape[1], a.shape[0] * HEAD_DIM)


def _row_blocks(a, bk):
    return a.reshape(a.shape[0], a.shape[1] // bk, bk, HEAD_DIM)


def _col_blocks(a, bk):
    return _row_blocks(a, bk).transpose(0, 1, 3, 2)


def _from_col_blocks(a):
    return a.transpose(0, 1, 3, 2).reshape(a.shape[0], a.shape[1] * a.shape[3], HEAD_DIM)


def _flat_rows(a, rows):
    flat = a.reshape(-1)
    return jnp.pad(flat, (0, rows * LANES - flat.shape[0])).reshape(rows, LANES)


SHARD_ROWS = {"w_in": 2 * D_MODEL * (D_IN // N_DEV) // LANES, "w_out": 2 * (D_MODEL // N_DEV) * D_MODEL // LANES,
              "conv_dw": 16}
REPL_SHAPES = [("pre_norm", (2, D_MODEL)), ("post_norm", (2, D_MODEL)), ("q_norm", (2, HEAD_DIM)),
               ("k_norm", (2, HEAD_DIM)), ("conv_dw_b", (2, CONV_W)), ("conv_ln_g", (2, CONV_W)),
               ("conv_ln_b", (2, CONV_W)), ("sg_ln_g", (2, SG_W)), ("sg_ln_b", (2, SG_W)),
               ("sg_w", (2, SG_HEADS, SG_CHUNK, SG_CHUNK)), ("sg_b", (2, SG_HEADS, SG_CHUNK))]
REPL_ROWS = 1088
WEIGHT_ORDER = ["pre_norm", "post_norm", "w_in", "w_out", "q_norm", "k_norm", "conv_dw", "conv_dw_b", "conv_ln_g",
                "conv_ln_b", "sg_ln_g", "sg_ln_b", "sg_w", "sg_b"]


def _pack_shard(parts):
    return jnp.concatenate([_flat_rows(parts[k], SHARD_ROWS[k]) for k in ("w_in", "w_out", "conv_dw")], axis=0)


def _unpack_shard(flat, shapes):
    out, at = {}, 0
    for k in ("w_in", "w_out", "conv_dw"):
        size = math.prod(shapes[k])
        out[k] = flat[at:at + SHARD_ROWS[k]].reshape(-1)[:size].reshape(shapes[k])
        at += SHARD_ROWS[k]
    return out


def _pack_repl(parts):
    flat = jnp.concatenate([parts[k].reshape(-1) for k, _ in REPL_SHAPES])
    return jnp.pad(flat, (0, REPL_ROWS * LANES - flat.shape[0])).reshape(REPL_ROWS, LANES)


def _unpack_repl(flat):
    out, at, flat = {}, 0, flat.reshape(-1)
    for k, shape in REPL_SHAPES:
        size = math.prod(shape)
        out[k] = flat[at:at + size].reshape(shape)
        at += size
    return out


def kernel(x, pre_norm, post_norm, w_in, w_out, q_norm, k_norm, conv_dw, conv_dw_b, conv_ln_g, conv_ln_b, sg_ln_g, sg_ln_b, sg_w, sg_b, loss_target, m_pre_norm, m_post_norm, m_w_in, m_w_out, m_q_norm, m_k_norm, m_conv_dw, m_conv_dw_b, m_conv_ln_g, m_conv_ln_b, m_sg_ln_g, m_sg_ln_b, m_sg_w, m_sg_b, v_pre_norm, v_post_norm, v_w_in, v_w_out, v_q_norm, v_k_norm, v_conv_dw, v_conv_dw_b, v_conv_ln_g, v_conv_ln_b, v_sg_ln_g, v_sg_ln_b, v_sg_w, v_sg_b):
    weights = dict(pre_norm=pre_norm, post_norm=post_norm, w_in=w_in, w_out=w_out, q_norm=q_norm, k_norm=k_norm,
                   conv_dw=conv_dw, conv_dw_b=conv_dw_b, conv_ln_g=conv_ln_g, conv_ln_b=conv_ln_b, sg_ln_g=sg_ln_g,
                   sg_ln_b=sg_ln_b, sg_w=sg_w, sg_b=sg_b)
    mom_m = dict(pre_norm=m_pre_norm, post_norm=m_post_norm, w_in=m_w_in, w_out=m_w_out, q_norm=m_q_norm,
                 k_norm=m_k_norm, conv_dw=m_conv_dw, conv_dw_b=m_conv_dw_b, conv_ln_g=m_conv_ln_g,
                 conv_ln_b=m_conv_ln_b, sg_ln_g=m_sg_ln_g, sg_ln_b=m_sg_ln_b, sg_w=m_sg_w, sg_b=m_sg_b)
    mom_v = dict(pre_norm=v_pre_norm, post_norm=v_post_norm, w_in=v_w_in, w_out=v_w_out, q_norm=v_q_norm,
                 k_norm=v_k_norm, conv_dw=v_conv_dw, conv_dw_b=v_conv_dw_b, conv_ln_g=v_conv_ln_g,
                 conv_ln_b=v_conv_ln_b, sg_ln_g=v_sg_ln_g, sg_ln_b=v_sg_ln_b, sg_w=v_sg_w, sg_b=v_sg_b)
    depth = pre_norm.shape[0]
    seq = x.shape[1]
    bk = min(BK, seq)
    x0 = x.reshape(seq, D_MODEL)
    target = loss_target.reshape(seq, D_MODEL)

    w_in_all, w_out_all, dw_all = _exchange(
        [], [w_in.astype(BF16), w_out.astype(BF16), jnp.pad(conv_dw, ((0, 0), (0, 1), (0, 0)))], "gather_weights")
    w_in_full = w_in_all.transpose(1, 2, 0, 3).reshape(depth, D_MODEL, D_IN)
    w_out_full = w_out_all.transpose(1, 0, 2, 3).reshape(depth, D_MODEL, D_MODEL)
    dw_full = dw_all[:, :, :CONV_K, :].transpose(1, 2, 0, 3).reshape(depth, CONV_K, CONV_W)

    cos, sin, perm = _rope_tables(seq)
    post = jnp.concatenate([jnp.full((ATT_HEADS, 1, HEAD_DIM), Q_SCALE, F32), jnp.ones((KV_HEADS, 1, HEAD_DIM), F32)])
    lane = jnp.arange(SG_W)
    fold = (lane[:, None] // HEAD_DIM == jnp.arange(SG_CHUNK)[None, :]).astype(BF16)

    def layer_consts(l):
        gains = jnp.concatenate([jnp.broadcast_to(q_norm[l], (ATT_HEADS, 1, HEAD_DIM)),
                                 jnp.broadcast_to(k_norm[l], (KV_HEADS, 1, HEAD_DIM))])
        return dict(
            gains=gains, sg_w_b=sg_w[l].astype(BF16), sg_wt_b=sg_w[l].transpose(0, 2, 1).astype(BF16),
            sg_bias=jnp.repeat(sg_b[l].T, HEAD_DIM, axis=1),
            vec=lambda a: a[l].reshape(1, -1))

    saved = []
    xc = x0
    for l in range(depth):
        c = layer_consts(l)
        proj, hb = _proj_fwd(xc, c["vec"](pre_norm), w_in_full[l], f"proj_fwd_{l}")
        raw = jnp.concatenate([_to_heads(proj[:, :ATT_W], ATT_HEADS),
                               _to_heads(proj[:, ATT_W:ATT_W + KV_W], KV_HEADS)], axis=0)
        qk = _qk_prep_fwd(raw, c["gains"], post, cos, sin, perm, f"qk_prep_fwd_{l}")
        qs, kr = qk[:ATT_HEADS], qk[ATT_HEADS:]
        vh = _to_heads(proj[:, ATT_W + KV_W:ATT_W + 2 * KV_W], KV_HEADS).astype(BF16)
        k_rows, k_cols = _row_blocks(kr, bk), _col_blocks(kr, bk)
        v_rows, v_cols = _row_blocks(vh, bk), _col_blocks(vh, bk)
        o, lse = _attn_fwd(qs, k_cols, v_rows, f"attn_fwd_{l}")
        att = _from_heads(o)
        cnv = _conv_fwd(proj, dw_full[l], c["vec"](conv_dw_b), c["vec"](conv_ln_g), c["vec"](conv_ln_b),
                        f"conv_fwd_{l}")
        sgu = _sg_fwd(proj, c["vec"](sg_ln_g), c["vec"](sg_ln_b), c["sg_w_b"], c["sg_bias"], f"sg_fwd_{l}")
        x_new, mix, cat_b = _out_fwd(att, proj, cnv, sgu, xc, w_out_full[l], c["vec"](post_norm), f"out_fwd_{l}")
        saved.append(dict(x=xc, proj=proj, hb=hb, raw=raw, qs=qs, k_rows=k_rows, k_cols=k_cols, v_cols=v_cols,
                          o=o, lse=lse, att=att, mix=mix, cat_b=cat_b))
        xc = x_new

    sse, dx = _loss_head(xc, target, "loss_head")
    loss = lax.psum(sse[0, 0] * (0.5 / D_MODEL), MESH_AXES)

    grads = {k: [None] * depth for k in WEIGHT_ORDER}
    for l in reversed(range(depth)):
        c, s = layer_consts(l), saved[l]
        dmix_b, datt, dgatt, dcnv, dsgu, g_post = _out_bwd(
            dx, s["mix"], c["vec"](post_norm), w_out_full[l].T, s["att"], s["proj"], f"out_bwd_{l}")
        grads["post_norm"][l] = g_post.reshape(-1)
        grads["w_out"][l] = _matmul_acc(s["cat_b"].T, dmix_b, D_MODEL, f"grad_w_out_{l}")
        do = _to_heads(datt, ATT_HEADS)
        dqs, dkt, dvt = _attn_bwd(s["qs"], s["qs"].transpose(0, 2, 1), do, do.astype(BF16).transpose(0, 2, 1),
                                  s["o"], s["lse"], s["k_rows"], s["k_cols"], s["v_cols"], f"attn_bwd_{l}")
        d_qk = jnp.concatenate([dqs, _from_col_blocks(dkt)], axis=0)
        d_raw, g_gain = _qk_prep_bwd(s["raw"], d_qk, c["gains"], post, cos, sin, perm, f"qk_prep_bwd_{l}")
        grads["q_norm"][l] = jnp.sum(g_gain[:ATT_HEADS, 0], axis=0)
        grads["k_norm"][l] = jnp.sum(g_gain[ATT_HEADS:, 0], axis=0)
        dy_conv, dg_conv, g_dw, g_dwb, g_clg, g_clb = _conv_bwd_a(
            s["proj"], dcnv, dw_full[l], c["vec"](conv_dw_b), c["vec"](conv_ln_g), c["vec"](conv_ln_b),
            f"conv_bwd_a_{l}")
        da, db = _conv_bwd_b(s["proj"], dy_conv, dw_full[l], f"conv_bwd_b_{l}")
        grads["conv_dw"][l], grads["conv_dw_b"][l] = g_dw, g_dwb.reshape(-1)
        grads["conv_ln_g"][l], grads["conv_ln_b"][l] = g_clg.reshape(-1), g_clb.reshape(-1)
        du, dv_sg, dg_sg, g_slg, g_slb, g_sw, g_sb = _sg_bwd(
            s["proj"], dsgu, c["vec"](sg_ln_g), c["vec"](sg_ln_b), c["sg_w_b"], c["sg_wt_b"], c["sg_bias"], fold,
            f"sg_bwd_{l}")
        grads["sg_ln_g"][l], grads["sg_ln_b"][l] = g_slg.reshape(-1), g_slb.reshape(-1)
        grads["sg_w"][l], grads["sg_b"][l] = g_sw, g_sb[:, :SG_HEADS].T
        dproj = jnp.concatenate(
            [_from_heads(d_raw[:ATT_HEADS]), _from_heads(d_raw[ATT_HEADS:]), _from_heads(_from_col_blocks(dvt)),
             dgatt, da, db, dg_conv, du, dv_sg, dg_sg], axis=-1).astype(BF16)
        grads["w_in"][l] = _matmul_acc(s["hb"].T, dproj, D_IN // 2, f"grad_w_in_{l}")
        dx, g_pre = _proj_bwd(dproj, w_in_full[l].T, s["x"], c["vec"](pre_norm), dx, f"proj_bwd_{l}")
        grads["pre_norm"][l] = g_pre.reshape(-1)
    grad_x = dx.reshape(x.shape)
    grads = {k: jnp.stack(v) for k, v in grads.items()}

    shard_blocks = dict(
        w_in=grads["w_in"].reshape(depth, D_MODEL, N_DEV, D_IN // N_DEV).transpose(2, 0, 1, 3),
        w_out=grads["w_out"].reshape(depth, N_DEV, D_MODEL // N_DEV, D_MODEL).transpose(1, 0, 2, 3),
        conv_dw=grads["conv_dw"].reshape(depth, CONV_K, N_DEV, CONV_W // N_DEV).transpose(2, 0, 1, 3))
    scatter_src = jnp.stack([_pack_shard({k: a[d] for k, a in shard_blocks.items()}) for d in range(N_DEV)])
    shard_slots, repl_slots = _exchange([scatter_src], [_pack_repl(grads)], "exchange_grads")

    shard_shapes = {k: weights[k].shape for k in SHARD_ROWS}
    gs, ds_, ms, vs = _sum_adamw(shard_slots, _pack_shard(weights), _pack_shard(mom_m), _pack_shard(mom_v),
                                 "adamw_sharded")
    gr, dr, mr, vr = _sum_adamw(repl_slots, _pack_repl(weights), _pack_repl(mom_m), _pack_repl(mom_v),
                                "adamw_replicated")
    results = []
    for shard_flat, repl_flat in ((gs, gr), (ds_, dr), (ms, mr), (vs, vr)):
        parts = {**_unpack_shard(shard_flat, shard_shapes), **_unpack_repl(repl_flat)}
        results.append([parts[k] for k in WEIGHT_ORDER])
    return (loss, grad_x, *results[0], *results[1], *results[2], *results[3])
```

```python
import math

import jax
import jax.numpy as jnp
from jax import lax
from jax.experimental import pallas as pl
from jax.experimental.pallas import tpu as pltpu

F32, BF16 = jnp.float32, jnp.bfloat16

N_DEV = 8
MESH_AXES = ("x", "y", "c")
EPS = 1e-6
D_MODEL = 1024
HEAD_DIM = 64
ATT_HEADS, KV_HEADS = 8, 2
N_QK = ATT_HEADS + KV_HEADS
GROUP = ATT_HEADS // KV_HEADS
ATT_W, KV_W, CONV_W, SG_W = 512, 128, 256, 256
CONV_K, CONV_PAD, HALO = 31, 15, 16
SG_HEADS, SG_CHUNK = 4, 128
D_IN = 2816
GRID_W = 64
ROPE_THETA = 10000.0
Q_SCALE = HEAD_DIM ** -0.5
LANES = 128

COL_GATT, COL_A, COL_B, COL_GCONV, COL_U, COL_VSG, COL_GSG = 3, 5, 6, 7, 8, 9, 10

ADAM_LR, ADAM_B1, ADAM_B2, ADAM_EPS, ADAM_WD, ADAM_STEP = 0.001, 0.9, 0.999, 1e-08, 0.01, 10

T_ROW = 256
T_PREP = 1024
T_GROUP = 512
BQ, BK = 512, 512
BK_FWD = 1024
VMEM_MB = 56
ATTN_BWD_VMEM_MB = 58


def _pcall(body, *, name, grid, in_specs, out_specs, out_shape, scratch=(), sem=None, vmem_mb=None):
    params = {}
    if sem is not None:
        params["dimension_semantics"] = sem
    if vmem_mb is not None:
        params["vmem_limit_bytes"] = vmem_mb << 20
    return pl.pallas_call(body, name=name, grid=grid, in_specs=in_specs, out_specs=out_specs, out_shape=out_shape,
                          scratch_shapes=list(scratch), compiler_params=pltpu.CompilerParams(**params))


def _dot(a, b):
    return jnp.dot(a, b, preferred_element_type=F32)


def _sigmoid(x):
    return 1.0 / (1.0 + jnp.exp(-x))


def _silu_and_grad(x):
    s = _sigmoid(x)
    return x * s, s * (1.0 + x * (1.0 - s))


def _gelu_and_grad(x):
    cdf = 0.5 * (1.0 + lax.erf(x * (1.0 / math.sqrt(2.0))))
    pdf = jnp.exp(-0.5 * x * x) * (1.0 / math.sqrt(2.0 * math.pi))
    return x * cdf, cdf + x * pdf


def _split_dot(y, mat):
    hi = y.astype(BF16)
    lo = (y - hi.astype(F32)).astype(BF16)
    return _dot(hi, mat) + _dot(lo, mat)


def _row_tile(rows, cap):
    best = 8
    for t in range(8, min(rows, cap) + 1, 8):
        if rows % t == 0:
            best = t
    return best


def _exchange(scatter, gather, name):
    n_s = len(scatter)
    arrs = list(scatter) + list(gather)
    n = len(arrs)
    flips = [(fx, fy, fc) for fx in (0, 1) for fy in (0, 1) for fc in (0, 1)][1:]
    n_peer = len(flips)

    def body(*refs):
        ins, outs = refs[:n], refs[n:2 * n]
        send_sems, recv_sems, local_sems = refs[2 * n:]
        pos = tuple(lax.axis_index(a) for a in MESH_AXES)

        def peer(flip):
            return tuple((1 - p) if f else p for p, f in zip(pos, flip))

        def slot(p):
            return 4 * p[0] + 2 * p[1] + p[2]

        def src(a, p):
            return ins[a].at[slot(p)] if a < n_s else ins[a]

        def remote(a, k, src_ref, dst_slot, to):
            return pltpu.make_async_remote_copy(
                src_ref=src_ref, dst_ref=outs[a].at[dst_slot], send_sem=send_sems.at[a * n_peer + k],
                recv_sem=recv_sems.at[a * n_peer + k], device_id=to, device_id_type=pl.DeviceIdType.MESH)

        local = [pltpu.make_async_copy(src(a, pos), outs[a].at[slot(pos)], local_sems.at[a]) for a in range(n)]
        for cp in local:
            cp.start()
        sends = [remote(a, k, src(a, peer(f)), slot(pos), peer(f)) for a in range(n) for k, f in enumerate(flips)]
        for cp in sends:
            cp.start()
        for a in range(n):
            for k, f in enumerate(flips):
                remote(a, k, src(a, peer(f)), slot(peer(f)), peer(f)).wait_recv()
        for cp in sends:
            cp.wait_send()
        for cp in local:
            cp.wait()

    out_shape = [jax.ShapeDtypeStruct((N_DEV,) + (a.shape[1:] if i < n_s else a.shape), a.dtype)
                 for i, a in enumerate(arrs)]
    any_spec = pl.BlockSpec(memory_space=pl.ANY)
    return pl.pallas_call(
        body, name=name, out_shape=out_shape, in_specs=[any_spec] * n, out_specs=[any_spec] * n,
        scratch_shapes=[pltpu.SemaphoreType.DMA((n * n_peer,)), pltpu.SemaphoreType.DMA((n * n_peer,)),
                        pltpu.SemaphoreType.DMA((n,))],
    )(*arrs)


def _sum_adamw(slots, w, m, v, name):
    rows = w.shape[0]
    tr = _row_tile(rows, 1024)
    c1 = 1.0 - ADAM_B1 ** ADAM_STEP
    c2 = 1.0 - ADAM_B2 ** ADAM_STEP

    def body(s_ref, w_ref, m_ref, v_ref, g_out, d_out, m_out, v_out):
        g = s_ref[0]
        for d in range(1, N_DEV):
            g = g + s_ref[d]
        m_new = ADAM_B1 * m_ref[...] + (1.0 - ADAM_B1) * g
        v_new = ADAM_B2 * v_ref[...] + (1.0 - ADAM_B2) * (g * g)
        m_hat = m_new / c1
        v_hat = v_new / c2
        g_out[...] = g
        d_out[...] = -ADAM_LR * (m_hat / (jnp.sqrt(v_hat) + ADAM_EPS) + ADAM_WD * w_ref[...])
        m_out[...] = m_new
        v_out[...] = v_new

    flat = pl.BlockSpec((tr, LANES), lambda i: (i, 0))
    return _pcall(
        body, name=name, grid=(rows // tr,),
        in_specs=[pl.BlockSpec((N_DEV, tr, LANES), lambda i: (0, i, 0)), flat, flat, flat],
        out_specs=[flat] * 4, out_shape=[jax.ShapeDtypeStruct((rows, LANES), F32)] * 4,
        sem=("parallel",), vmem_mb=VMEM_MB)(slots, w, m, v)


def _proj_fwd(x, gain, w, name):
    seq = x.shape[0]

    def body(x_ref, g_ref, w_ref, proj_ref, hb_ref):
        xf = x_ref[...]
        r = lax.rsqrt(jnp.mean(xf * xf, axis=-1, keepdims=True) + EPS)
        h = (xf * r * g_ref[...]).astype(BF16)
        hb_ref[...] = h
        proj_ref[...] = _dot(h, w_ref[...])

    return _pcall(
        body, name=name, grid=(seq // T_ROW,),
        in_specs=[pl.BlockSpec((T_ROW, D_MODEL), lambda i: (i, 0)), pl.BlockSpec((1, D_MODEL), lambda i: (0, 0)),
                  pl.BlockSpec((D_MODEL, D_IN), lambda i: (0, 0))],
        out_specs=[pl.BlockSpec((T_ROW, D_IN), lambda i: (i, 0)), pl.BlockSpec((T_ROW, D_MODEL), lambda i: (i, 0))],
        out_shape=[jax.ShapeDtypeStruct((seq, D_IN), F32), jax.ShapeDtypeStruct((seq, D_MODEL), BF16)],
        sem=("parallel",), vmem_mb=VMEM_MB)(x, gain, w)


def _proj_bwd(dproj, w_t, x, gain, dxo, name):
    seq = x.shape[0]

    def body(dp_ref, w_ref, x_ref, g_ref, dxo_ref, dx_ref, dg_ref):
        dh = _dot(dp_ref[...], w_ref[...])
        xf = x_ref[...]
        r = lax.rsqrt(jnp.mean(xf * xf, axis=-1, keepdims=True) + EPS)
        n = xf * r
        dn = dh * g_ref[...]
        dx_ref[...] = dxo_ref[...] + r * (dn - n * jnp.mean(dn * n, axis=-1, keepdims=True))

        @pl.when(pl.program_id(0) == 0)
        def _():
            dg_ref[...] = jnp.zeros_like(dg_ref)

        dg_ref[...] += jnp.sum(dh * n, axis=0, keepdims=True)

    row = pl.BlockSpec((T_ROW, D_MODEL), lambda i: (i, 0))
    vec = pl.BlockSpec((1, D_MODEL), lambda i: (0, 0))
    return _pcall(
        body, name=name, grid=(seq // T_ROW,),
        in_specs=[pl.BlockSpec((T_ROW, D_IN), lambda i: (i, 0)), pl.BlockSpec((D_IN, D_MODEL), lambda i: (0, 0)),
                  row, vec, row],
        out_specs=[row, vec],
        out_shape=[jax.ShapeDtypeStruct((seq, D_MODEL), F32), jax.ShapeDtypeStruct((1, D_MODEL), F32)],
        sem=("arbitrary",), vmem_mb=VMEM_MB)(dproj, w_t, x, gain, dxo)


def _matmul_acc(a_t, b, tn, name):
    m, seq = a_t.shape
    n = b.shape[1]
    ts = min(512, seq)

    def body(a_ref, b_ref, o_ref):
        @pl.when(pl.program_id(1) == 0)
        def _():
            o_ref[...] = jnp.zeros_like(o_ref)

        o_ref[...] += _dot(a_ref[...], b_ref[...])

    return _pcall(
        body, name=name, grid=(n // tn, seq // ts),
        in_specs=[pl.BlockSpec((m, ts), lambda j, k: (0, k)), pl.BlockSpec((ts, tn), lambda j, k: (k, j))],
        out_specs=pl.BlockSpec((m, tn), lambda j, k: (0, j)), out_shape=jax.ShapeDtypeStruct((m, n), F32),
        sem=("parallel", "arbitrary"), vmem_mb=VMEM_MB)(a_t, b)


def _rope_tables(seq):
    t = jnp.arange(seq, dtype=jnp.int32)
    row = (t // GRID_W).astype(F32)
    col = (t % GRID_W).astype(F32)
    half = HEAD_DIM // 4
    inv_freq = ROPE_THETA ** (-jnp.arange(half, dtype=F32) / half)
    ang_r = row[:, None] * inv_freq[None, :]
    ang_c = col[:, None] * inv_freq[None, :]
    cos = jnp.concatenate([jnp.cos(ang_r), jnp.cos(ang_r), jnp.cos(ang_c), jnp.cos(ang_c)], axis=-1)
    sin = jnp.concatenate([-jnp.sin(ang_r), jnp.sin(ang_r), -jnp.sin(ang_c), jnp.sin(ang_c)], axis=-1)
    j = jnp.arange(HEAD_DIM)
    partner = jnp.where((j % (2 * half)) < half, j + half, j - half)
    perm = (j[:, None] == partner[None, :]).astype(BF16)
    return cos, sin, perm


def _qk_specs(seq):
    t = min(T_PREP, seq)
    blk = pl.BlockSpec((None, t, HEAD_DIM), lambda h, i: (h, i, 0))
    vec = pl.BlockSpec((None, 1, HEAD_DIM), lambda h, i: (h, 0, 0))
    tab = pl.BlockSpec((t, HEAD_DIM), lambda h, i: (i, 0))
    mat = pl.BlockSpec((HEAD_DIM, HEAD_DIM), lambda h, i: (0, 0))
    return t, blk, vec, tab, mat


def _qk_prep_fwd(raw, gains, post, cos, sin, perm, name):
    seq = raw.shape[1]
    t, blk, vec, tab, mat = _qk_specs(seq)

    def body(x_ref, g_ref, p_ref, c_ref, s_ref, m_ref, o_ref):
        xf = x_ref[...]
        r = lax.rsqrt(jnp.mean(xf * xf, axis=-1, keepdims=True) + EPS)
        y = xf * r * g_ref[...]
        z = y * c_ref[...] + _split_dot(y, m_ref[...]) * s_ref[...]
        o_ref[...] = (z * p_ref[...]).astype(BF16)

    return _pcall(body, name=name, grid=(N_QK, seq // t), in_specs=[blk, vec, vec, tab, tab, mat], out_specs=blk,
                  out_shape=jax.ShapeDtypeStruct(raw.shape, BF16), sem=("parallel", "parallel"))(
                      raw, gains, post, cos, sin, perm)


def _qk_prep_bwd(raw, dout, gains, post, cos, sin, perm, name):
    seq = raw.shape[1]
    t, blk, vec, tab, mat = _qk_specs(seq)

    def body(x_ref, d_ref, g_ref, p_ref, c_ref, s_ref, m_ref, dx_ref, dg_ref):
        xf = x_ref[...]
        r = lax.rsqrt(jnp.mean(xf * xf, axis=-1, keepdims=True) + EPS)
        n = xf * r
        dz = d_ref[...] * p_ref[...]
        dy = dz * c_ref[...] + _split_dot(dz * s_ref[...], m_ref[...])
        dn = dy * g_ref[...]
        dx_ref[...] = r * (dn - n * jnp.mean(dn * n, axis=-1, keepdims=True))

        @pl.when(pl.program_id(1) == 0)
        def _():
            dg_ref[...] = jnp.zeros_like(dg_ref)

        dg_ref[...] += jnp.sum(dy * n, axis=0, keepdims=True)

    return _pcall(body, name=name, grid=(N_QK, seq // t), in_specs=[blk, blk, vec, vec, tab, tab, mat],
                  out_specs=[blk, vec],
                  out_shape=[jax.ShapeDtypeStruct(raw.shape, F32), jax.ShapeDtypeStruct((N_QK, 1, HEAD_DIM), F32)],
                  sem=("parallel", "arbitrary"))(raw, dout, gains, post, cos, sin, perm)


V_ROWS = HEAD_DIM + 8


def _attn_fwd(qs_t, k, v_t, name):
    seq = qs_t.shape[2]
    nk, bk = k.shape[1], k.shape[2]
    bq = min(BQ, seq)

    def body(qt_ref, k_ref, vt_ref, ot_ref, lse_ref):
        q_t = [qt_ref[h] for h in range(GROUP)]

        def step(j, carry):
            kj, vtj = k_ref[j], vt_ref[j]
            scores = [_dot(kj, q_t[h]) for h in range(GROUP)]
            out = []
            for h in range(GROUP):
                m, acc = carry[h]
                m_new = jnp.maximum(m, jnp.max(scores[h], axis=0, keepdims=True))
                p = jnp.exp(scores[h] - m_new).astype(BF16)
                out.append((m_new, jnp.exp(m - m_new) * acc + _dot(vtj, p)))
            return tuple(out)

        init = tuple((jnp.full((1, bq), -jnp.inf, F32), jnp.zeros((V_ROWS, bq), F32)) for _ in range(GROUP))
        res = lax.fori_loop(0, nk, step, init)
        for h in range(GROUP):
            m, acc = res[h]
            l = acc[HEAD_DIM:HEAD_DIM + 1, :]
            ot_ref[h] = acc[:HEAD_DIM, :] / l
            lse_ref[h] = m + jnp.log(l)

    return _pcall(
        body, name=name, grid=(KV_HEADS, seq // bq),
        in_specs=[pl.BlockSpec((GROUP, HEAD_DIM, bq), lambda g, i: (g, 0, i)),
                  pl.BlockSpec((None, nk, bk, HEAD_DIM), lambda g, i: (g, 0, 0, 0)),
                  pl.BlockSpec((None, nk, V_ROWS, bk), lambda g, i: (g, 0, 0, 0))],
        out_specs=[pl.BlockSpec((GROUP, HEAD_DIM, bq), lambda g, i: (g, 0, i)),
                   pl.BlockSpec((GROUP, 1, bq), lambda g, i: (g, 0, i))],
        out_shape=[jax.ShapeDtypeStruct((ATT_HEADS, HEAD_DIM, seq), F32),
                   jax.ShapeDtypeStruct((ATT_HEADS, 1, seq), F32)],
        sem=("parallel", "parallel"), vmem_mb=VMEM_MB)(qs_t, k, v_t)


def _attn_bwd(qs, qs_t, do, do_t, o, lse, k, k_t, v_t, name):
    seq = qs.shape[1]
    nk, bk = k.shape[1], k.shape[2]
    bq = min(BQ, seq)
    nq = seq // bq

    def body(q_ref, qt_ref, do_ref, dot_ref, o_ref, lse_ref, k_ref, kt_ref, vt_ref, dq_ref, dkt_ref, dvt_ref):
        @pl.when(pl.program_id(1) == 0)
        def _():
            dkt_ref[...] = jnp.zeros_like(dkt_ref)
            dvt_ref[...] = jnp.zeros_like(dvt_ref)

        heads = range(GROUP)
        q, q_t, do_t_b = [q_ref[h] for h in heads], [qt_ref[h] for h in heads], [dot_ref[h] for h in heads]
        do_f = [do_ref[h] for h in heads]
        do_b = [d.astype(BF16) for d in do_f]
        delta = [jnp.sum(do_f[h] * o_ref[h], axis=-1, keepdims=True) for h in heads]
        lse_col = [jnp.max(lse_ref[h], axis=-1, keepdims=True) for h in heads]

        def step(j, dq):
            ktj, vtj, kj = kt_ref[j], vt_ref[j], k_ref[j]
            dvt = jnp.zeros((HEAD_DIM, bk), F32)
            dkt = jnp.zeros((HEAD_DIM, bk), F32)
            new = []
            for h in heads:
                p = jnp.exp(_dot(q[h], ktj) - lse_col[h])
                ds = (p * (_dot(do_b[h], vtj) - delta[h])).astype(BF16)
                dvt = dvt + _dot(do_t_b[h], p.astype(BF16))
                dkt = dkt + _dot(q_t[h], ds)
                new.append(dq[h] + _dot(ds, kj))
            dvt_ref[j] += dvt
            dkt_ref[j] += dkt
            return tuple(new)

        res = lax.fori_loop(0, nk, step, tuple(jnp.zeros((bq, HEAD_DIM), F32) for _ in heads))
        for h in heads:
            dq_ref[h] = res[h]

    row = pl.BlockSpec((GROUP, bq, HEAD_DIM), lambda g, i: (g, i, 0))
    col = pl.BlockSpec((GROUP, HEAD_DIM, bq), lambda g, i: (g, 0, i))
    kv_rows = pl.BlockSpec((None, nk, bk, HEAD_DIM), lambda g, i: (g, 0, 0, 0))
    kv_cols = pl.BlockSpec((None, nk, HEAD_DIM, bk), lambda g, i: (g, 0, 0, 0))
    return _pcall(
        body, name=name, grid=(KV_HEADS, nq),
        in_specs=[row, col, row, col, row, pl.BlockSpec((GROUP, bq, LANES), lambda g, i: (g, i, 0)),
                  kv_rows, kv_cols, kv_cols],
        out_specs=[row, kv_cols, kv_cols],
        out_shape=[jax.ShapeDtypeStruct((ATT_HEADS, seq, HEAD_DIM), F32),
                   jax.ShapeDtypeStruct((KV_HEADS, nk, HEAD_DIM, bk), F32),
                   jax.ShapeDtypeStruct((KV_HEADS, nk, HEAD_DIM, bk), F32)],
        sem=("parallel", "arbitrary"), vmem_mb=ATTN_BWD_VMEM_MB)(qs, qs_t, do, do_t, o, lse, k, k_t, v_t)


def _halo_specs(t, col, n_tiles):
    per = t // HALO
    last = n_tiles * per - 1
    before = pl.BlockSpec((HALO, CONV_W), lambda i: (jnp.maximum(i * per - 1, 0), col))
    after = pl.BlockSpec((HALO, CONV_W), lambda i: (jnp.minimum((i + 1) * per, last), col))
    return before, after


def _glu(a, b):
    return a * _sigmoid(b)


def _conv_taps(ext_ref, w_ref, t, flip):
    acc = jnp.zeros((t, CONV_W), F32)
    for k in range(CONV_K):
        off = (HALO + CONV_PAD - k) if flip else (HALO - CONV_PAD + k)
        acc = acc + w_ref[k:k + 1, :] * ext_ref[pl.ds(off, t), :]
    return acc


def _fill_ext(ext_ref, before, tile, after, t, i, n_tiles):
    ext_ref[pl.ds(0, HALO), :] = jnp.where(i > 0, before, 0.0)
    ext_ref[pl.ds(HALO, t), :] = tile
    ext_ref[pl.ds(HALO + t, HALO), :] = jnp.where(i < n_tiles - 1, after, 0.0)


def _conv_fwd(proj, w, bias, ln_g, ln_b, name):
    seq = proj.shape[0]
    t = min(T_GROUP, seq)
    n_tiles = seq // t

    def body(a_ref, b_ref, ap_ref, bp_ref, an_ref, bn_ref, gate_ref, w_ref, bias_ref, g_ref, beta_ref, o_ref, ext_ref):
        i = pl.program_id(0)
        _fill_ext(ext_ref, _glu(ap_ref[...], bp_ref[...]), _glu(a_ref[...], b_ref[...]),
                  _glu(an_ref[...], bn_ref[...]), t, i, n_tiles)
        y = _conv_taps(ext_ref, w_ref, t, False) + bias_ref[...]
        mu = jnp.mean(y, axis=-1, keepdims=True)
        yc = y - mu
        rs = lax.rsqrt(jnp.mean(yc * yc, axis=-1, keepdims=True) + EPS)
        z = yc * rs * g_ref[...] + beta_ref[...]
        o_ref[...] = _silu_and_grad(z)[0] * _silu_and_grad(gate_ref[...])[0]

    tile = lambda c: pl.BlockSpec((t, CONV_W), lambda i: (i, c))
    ab, aa = _halo_specs(t, COL_A, n_tiles)
    bb, ba = _halo_specs(t, COL_B, n_tiles)
    vec = pl.BlockSpec((1, CONV_W), lambda i: (0, 0))
    return _pcall(
        body, name=name, grid=(n_tiles,),
        in_specs=[tile(COL_A), tile(COL_B), ab, bb, aa, ba, tile(COL_GCONV),
                  pl.BlockSpec((CONV_K, CONV_W), lambda i: (0, 0)), vec, vec, vec],
        out_specs=pl.BlockSpec((t, CONV_W), lambda i: (i, 0)), out_shape=jax.ShapeDtypeStruct((seq, CONV_W), F32),
        scratch=[pltpu.VMEM((t + 2 * HALO, CONV_W), F32)], sem=("parallel",))(
            proj, proj, proj, proj, proj, proj, proj, w, bias, ln_g, ln_b)


def _conv_bwd_a(proj, dcnv, w, bias, ln_g, ln_b, name):
    seq = proj.shape[0]
    t = min(T_GROUP, seq)
    n_tiles = seq // t

    def body(a_ref, b_ref, ap_ref, bp_ref, an_ref, bn_ref, gate_ref, d_ref, w_ref, bias_ref, g_ref, beta_ref,
             dy_ref, dgate_ref, dw_ref, dbias_ref, dg_ref, dbeta_ref, ext_ref):
        i = pl.program_id(0)
        _fill_ext(ext_ref, _glu(ap_ref[...], bp_ref[...]), _glu(a_ref[...], b_ref[...]),
                  _glu(an_ref[...], bn_ref[...]), t, i, n_tiles)
        y = _conv_taps(ext_ref, w_ref, t, False) + bias_ref[...]
        mu = jnp.mean(y, axis=-1, keepdims=True)
        yc = y - mu
        rs = lax.rsqrt(jnp.mean(yc * yc, axis=-1, keepdims=True) + EPS)
        n = yc * rs
        z = n * g_ref[...] + beta_ref[...]
        act, dact = _silu_and_grad(z)
        gate, dgate = _silu_and_grad(gate_ref[...])
        d = d_ref[...]
        dgate_ref[...] = d * act * dgate
        dz = d * gate * dact
        dn = dz * g_ref[...]
        dy = rs * (dn - jnp.mean(dn, axis=-1, keepdims=True) - n * jnp.mean(dn * n, axis=-1, keepdims=True))
        dy_ref[...] = dy

        @pl.when(i == 0)
        def _():
            dw_ref[...] = jnp.zeros_like(dw_ref)
            dbias_ref[...] = jnp.zeros_like(dbias_ref)
            dg_ref[...] = jnp.zeros_like(dg_ref)
            dbeta_ref[...] = jnp.zeros_like(dbeta_ref)

        dg_ref[...] += jnp.sum(dz * n, axis=0, keepdims=True)
        dbeta_ref[...] += jnp.sum(dz, axis=0, keepdims=True)
        dbias_ref[...] += jnp.sum(dy, axis=0, keepdims=True)
        for k in range(CONV_K):
            dw_ref[k:k + 1, :] += jnp.sum(dy * ext_ref[pl.ds(HALO - CONV_PAD + k, t), :], axis=0, keepdims=True)

    tile = lambda c: pl.BlockSpec((t, CONV_W), lambda i: (i, c))
    own = pl.BlockSpec((t, CONV_W), lambda i: (i, 0))
    ab, aa = _halo_specs(t, COL_A, n_tiles)
    bb, ba = _halo_specs(t, COL_B, n_tiles)
    vec = pl.BlockSpec((1, CONV_W), lambda i: (0, 0))
    taps = pl.BlockSpec((CONV_K, CONV_W), lambda i: (0, 0))
    vshape = jax.ShapeDtypeStruct((1, CONV_W), F32)
    return _pcall(
        body, name=name, grid=(n_tiles,),
        in_specs=[tile(COL_A), tile(COL_B), ab, bb, aa, ba, tile(COL_GCONV), own, taps, vec, vec, vec],
        out_specs=[own, own, taps, vec, vec, vec],
        out_shape=[jax.ShapeDtypeStruct((seq, CONV_W), F32), jax.ShapeDtypeStruct((seq, CONV_W), F32),
                   jax.ShapeDtypeStruct((CONV_K, CONV_W), F32), vshape, vshape, vshape],
        scratch=[pltpu.VMEM((t + 2 * HALO, CONV_W), F32)], sem=("arbitrary",))(
            proj, proj, proj, proj, proj, proj, proj, dcnv, w, bias, ln_g, ln_b)


def _conv_bwd_b(proj, dy, w, name):
    seq = proj.shape[0]
    t = min(T_GROUP, seq)
    n_tiles = seq // t

    def body(a_ref, b_ref, dy_ref, dyp_ref, dyn_ref, w_ref, da_ref, db_ref, ext_ref):
        i = pl.program_id(0)
        _fill_ext(ext_ref, dyp_ref[...], dy_ref[...], dyn_ref[...], t, i, n_tiles)
        dh = _conv_taps(ext_ref, w_ref, t, True)
        sig = _sigmoid(b_ref[...])
        da_ref[...] = dh * sig
        db_ref[...] = dh * a_ref[...] * sig * (1.0 - sig)

    tile = lambda c: pl.BlockSpec((t, CONV_W), lambda i: (i, c))
    own = pl.BlockSpec((t, CONV_W), lambda i: (i, 0))
    before, after = _halo_specs(t, 0, n_tiles)
    return _pcall(
        body, name=name, grid=(n_tiles,),
        in_specs=[tile(COL_A), tile(COL_B), own, before, after, pl.BlockSpec((CONV_K, CONV_W), lambda i: (0, 0))],
        out_specs=[own, own], out_shape=[jax.ShapeDtypeStruct((seq, CONV_W), F32)] * 2,
        scratch=[pltpu.VMEM((t + 2 * HALO, CONV_W), F32)], sem=("parallel",))(proj, proj, dy, dy, dy, w)


def _head_masks():
    lane_head = lax.broadcasted_iota(jnp.int32, (SG_CHUNK, SG_W), 1) // HEAD_DIM
    return [lane_head == h for h in range(SG_HEADS)]


def _sg_mix(mats_ref, rhs, masks):
    out = jnp.zeros((SG_CHUNK, SG_W), F32)
    for h in range(SG_HEADS):
        out = out + jnp.where(masks[h], _dot(mats_ref[h], rhs), 0.0)
    return out


def _sg_specs(seq):
    t = min(T_GROUP, seq)
    tile = lambda c: pl.BlockSpec((t, SG_W), lambda i: (i, c))
    own = pl.BlockSpec((t, SG_W), lambda i: (i, 0))
    vec = pl.BlockSpec((1, SG_W), lambda i: (0, 0))
    mats = pl.BlockSpec((SG_HEADS, SG_CHUNK, SG_CHUNK), lambda i: (0, 0, 0))
    full = pl.BlockSpec((SG_CHUNK, SG_W), lambda i: (0, 0))
    return t, tile, own, vec, mats, full


def _sg_fwd(proj, ln_g, ln_b, w_b, bias_full, name):
    seq = proj.shape[0]
    t, tile, own, vec, mats, full = _sg_specs(seq)

    def body(u_ref, v_ref, gate_ref, g_ref, beta_ref, w_ref, bias_ref, o_ref):
        masks = _head_masks()
        for c in range(t // SG_CHUNK):
            rows = pl.ds(c * SG_CHUNK, SG_CHUNK)
            vg = _gelu_and_grad(v_ref[rows, :])[0]
            mu = jnp.mean(vg, axis=-1, keepdims=True)
            vc = vg - mu
            rs = lax.rsqrt(jnp.mean(vc * vc, axis=-1, keepdims=True) + EPS)
            vln = vc * rs * g_ref[...] + beta_ref[...]
            mixed = _sg_mix(w_ref, vln.astype(BF16), masks) + bias_ref[...]
            o_ref[rows, :] = _gelu_and_grad(u_ref[rows, :])[0] * mixed * _silu_and_grad(gate_ref[rows, :])[0]

    return _pcall(body, name=name, grid=(seq // t,),
                  in_specs=[tile(COL_U), tile(COL_VSG), tile(COL_GSG), vec, vec, mats, full], out_specs=own,
                  out_shape=jax.ShapeDtypeStruct((seq, SG_W), F32), sem=("parallel",))(
                      proj, proj, proj, ln_g, ln_b, w_b, bias_full)


def _sg_bwd(proj, dsg, ln_g, ln_b, w_b, w_t_b, bias_full, fold, name):
    seq = proj.shape[0]
    t, tile, own, vec, mats, full = _sg_specs(seq)
    n_tiles = seq // t

    def body(u_ref, v_ref, gate_ref, d_ref, g_ref, beta_ref, w_ref, wt_ref, bias_ref, fold_ref,
             du_ref, dv_ref, dgate_ref, dg_ref, dbeta_ref, dw_ref, db_ref, dbias_acc):
        i = pl.program_id(0)

        @pl.when(i == 0)
        def _():
            dg_ref[...] = jnp.zeros_like(dg_ref)
            dbeta_ref[...] = jnp.zeros_like(dbeta_ref)
            dw_ref[...] = jnp.zeros_like(dw_ref)
            dbias_acc[...] = jnp.zeros_like(dbias_acc)

        masks = _head_masks()
        for c in range(t // SG_CHUNK):
            rows = pl.ds(c * SG_CHUNK, SG_CHUNK)
            ug, dug = _gelu_and_grad(u_ref[rows, :])
            vg, dvg = _gelu_and_grad(v_ref[rows, :])
            mu = jnp.mean(vg, axis=-1, keepdims=True)
            vc = vg - mu
            rs = lax.rsqrt(jnp.mean(vc * vc, axis=-1, keepdims=True) + EPS)
            vn = vc * rs
            vln_b = (vn * g_ref[...] + beta_ref[...]).astype(BF16)
            mixed = _sg_mix(w_ref, vln_b, masks) + bias_ref[...]
            gate, dgate = _silu_and_grad(gate_ref[rows, :])
            d = d_ref[rows, :]
            dgate_ref[rows, :] = d * ug * mixed * dgate
            du_ref[rows, :] = d * mixed * gate * dug
            dmixed = d * ug * gate
            dbias_acc[...] += dmixed
            dmixed_b = dmixed.astype(BF16)
            for h in range(SG_HEADS):
                dm_h = jnp.where(masks[h], dmixed_b, jnp.zeros_like(dmixed_b))
                dw_ref[h] += lax.dot_general(dm_h, vln_b, (((1,), (1,)), ((), ())), preferred_element_type=F32)
            dvln = _sg_mix(wt_ref, dmixed_b, masks)
            dg_ref[...] += jnp.sum(dvln * vn, axis=0, keepdims=True)
            dbeta_ref[...] += jnp.sum(dvln, axis=0, keepdims=True)
            dvn = dvln * g_ref[...]
            dvgelu = rs * (dvn - jnp.mean(dvn, axis=-1, keepdims=True) - vn * jnp.mean(dvn * vn, axis=-1, keepdims=True))
            dv_ref[rows, :] = dvgelu * dvg

        @pl.when(i == n_tiles - 1)
        def _():
            db_ref[...] = _split_dot(dbias_acc[...], fold_ref[...])

    sq = pl.BlockSpec((SG_CHUNK, SG_CHUNK), lambda i: (0, 0))
    vshape = jax.ShapeDtypeStruct((1, SG_W), F32)
    return _pcall(
        body, name=name, grid=(n_tiles,),
        in_specs=[tile(COL_U), tile(COL_VSG), tile(COL_GSG), own, vec, vec, mats, mats, full,
                  pl.BlockSpec((SG_W, SG_CHUNK), lambda i: (0, 0))],
        out_specs=[own, own, own, vec, vec, mats, sq],
        out_shape=[jax.ShapeDtypeStruct((seq, SG_W), F32)] * 3 + [
            vshape, vshape, jax.ShapeDtypeStruct((SG_HEADS, SG_CHUNK, SG_CHUNK), F32),
            jax.ShapeDtypeStruct((SG_CHUNK, SG_CHUNK), F32)],
        scratch=[pltpu.VMEM((SG_CHUNK, SG_W), F32)], sem=("arbitrary",))(
            proj, proj, proj, dsg, ln_g, ln_b, w_b, w_t_b, bias_full, fold)


def _out_fwd(att, proj, cnv, sgu, x, w, gain, name):
    seq = x.shape[0]

    def body(att_ref, g0_ref, g1_ref, cnv_ref, sgu_ref, x_ref, w_ref, gain_ref, xo_ref, mix_ref, cat_ref):
        gate = jnp.concatenate([_silu_and_grad(g0_ref[...])[0], _silu_and_grad(g1_ref[...])[0]], axis=-1)
        cat_ref[:, 0:ATT_W] = (att_ref[...] * gate).astype(BF16)
        cat_ref[:, ATT_W:ATT_W + CONV_W] = cnv_ref[...].astype(BF16)
        cat_ref[:, ATT_W + CONV_W:] = sgu_ref[...].astype(BF16)
        mix = _dot(cat_ref[...], w_ref[...])
        mix_ref[...] = mix
        r = lax.rsqrt(jnp.mean(mix * mix, axis=-1, keepdims=True) + EPS)
        xo_ref[...] = x_ref[...] + mix * r * gain_ref[...]

    row = lambda w_: pl.BlockSpec((T_ROW, w_), lambda i: (i, 0))
    gate_blk = lambda c: pl.BlockSpec((T_ROW, 256), lambda i: (i, c))
    return _pcall(
        body, name=name, grid=(seq // T_ROW,),
        in_specs=[row(ATT_W), gate_blk(COL_GATT), gate_blk(COL_GATT + 1), row(CONV_W), row(SG_W), row(D_MODEL),
                  pl.BlockSpec((D_MODEL, D_MODEL), lambda i: (0, 0)), pl.BlockSpec((1, D_MODEL), lambda i: (0, 0))],
        out_specs=[row(D_MODEL), row(D_MODEL), row(D_MODEL)],
        out_shape=[jax.ShapeDtypeStruct((seq, D_MODEL), F32), jax.ShapeDtypeStruct((seq, D_MODEL), F32),
                   jax.ShapeDtypeStruct((seq, D_MODEL), BF16)],
        sem=("parallel",), vmem_mb=VMEM_MB)(att, proj, proj, cnv, sgu, x, w, gain)


def _out_bwd(dxo, mix, gain, w_t, att, proj, name):
    seq = dxo.shape[0]

    def body(dxo_ref, mix_ref, gain_ref, w_ref, att_ref, g0_ref, g1_ref,
             dmix_ref, datt_ref, dgatt_ref, dcnv_ref, dsgu_ref, dgain_ref):
        mix = mix_ref[...]
        r = lax.rsqrt(jnp.mean(mix * mix, axis=-1, keepdims=True) + EPS)
        n = mix * r
        dout = dxo_ref[...]
        dn = dout * gain_ref[...]
        dmix = (r * (dn - n * jnp.mean(dn * n, axis=-1, keepdims=True))).astype(BF16)
        dmix_ref[...] = dmix

        @pl.when(pl.program_id(0) == 0)
        def _():
            dgain_ref[...] = jnp.zeros_like(dgain_ref)

        dgain_ref[...] += jnp.sum(dout * n, axis=0, keepdims=True)
        dcat = _dot(dmix, w_ref[...])
        g0, dg0 = _silu_and_grad(g0_ref[...])
        g1, dg1 = _silu_and_grad(g1_ref[...])
        gate = jnp.concatenate([g0, g1], axis=-1)
        dgate = jnp.concatenate([dg0, dg1], axis=-1)
        dca = dcat[:, 0:ATT_W]
        datt_ref[...] = dca * gate
        dgatt_ref[...] = dca * att_ref[...] * dgate
        dcnv_ref[...] = dcat[:, ATT_W:ATT_W + CONV_W]
        dsgu_ref[...] = dcat[:, ATT_W + CONV_W:]

    row = lambda w_: pl.BlockSpec((T_ROW, w_), lambda i: (i, 0))
    gate_blk = lambda c: pl.BlockSpec((T_ROW, 256), lambda i: (i, c))
    vec = pl.BlockSpec((1, D_MODEL), lambda i: (0, 0))
    return _pcall(
        body, name=name, grid=(seq // T_ROW,),
        in_specs=[row(D_MODEL), row(D_MODEL), vec, pl.BlockSpec((D_MODEL, D_MODEL), lambda i: (0, 0)), row(ATT_W),
                  gate_blk(COL_GATT), gate_blk(COL_GATT + 1)],
        out_specs=[row(D_MODEL), row(ATT_W), row(ATT_W), row(CONV_W), row(SG_W), vec],
        out_shape=[jax.ShapeDtypeStruct((seq, D_MODEL), BF16), jax.ShapeDtypeStruct((seq, ATT_W), F32),
                   jax.ShapeDtypeStruct((seq, ATT_W), F32), jax.ShapeDtypeStruct((seq, CONV_W), F32),
                   jax.ShapeDtypeStruct((seq, SG_W), F32), jax.ShapeDtypeStruct((1, D_MODEL), F32)],
        sem=("arbitrary",), vmem_mb=VMEM_MB)(dxo, mix, gain, w_t, att, proj, proj)


def _loss_head(y, target, name):
    seq = y.shape[0]
    t = min(T_GROUP, seq)

    def body(y_ref, t_ref, sse_ref, dy_ref):
        err = y_ref[...] - t_ref[...]
        dy_ref[...] = err * (1.0 / D_MODEL)

        @pl.when(pl.program_id(0) == 0)
        def _():
            sse_ref[...] = jnp.zeros_like(sse_ref)

        part = jnp.sum(jnp.sum(err * err, axis=0, keepdims=True), axis=-1, keepdims=True)
        sse_ref[...] += jnp.broadcast_to(part, (1, LANES))

    row = pl.BlockSpec((t, D_MODEL), lambda i: (i, 0))
    return _pcall(body, name=name, grid=(seq // t,), in_specs=[row, row],
                  out_specs=[pl.BlockSpec((1, LANES), lambda i: (0, 0)), row],
                  out_shape=[jax.ShapeDtypeStruct((1, LANES), F32), jax.ShapeDtypeStruct((seq, D_MODEL), F32)],
                  sem=("arbitrary",))(y, target)


def _to_heads(a, heads):
    return a.reshape(a.shape[0], heads, HEAD_DIM).transpose(1, 0, 2)


def _from_heads(a):
    return a.transpose(1, 0, 2).reshape(a.shape[1], a.shape[0] * HEAD_DIM)


def _row_blocks(a, bk):
    return a.reshape(a.shape[0], a.shape[1] // bk, bk, HEAD_DIM)


def _col_blocks(a, bk):
    return _row_blocks(a, bk).transpose(0, 1, 3, 2)


def _from_col_blocks(a):
    return a.transpose(0, 1, 3, 2).reshape(a.shape[0], a.shape[1] * a.shape[3], HEAD_DIM)


def _flat_rows(a, rows):
    flat = a.reshape(-1)
    return jnp.pad(flat, (0, rows * LANES - flat.shape[0])).reshape(rows, LANES)


SHARD_ROWS = {"w_in": 2 * D_MODEL * (D_IN // N_DEV) // LANES, "w_out": 2 * (D_MODEL // N_DEV) * D_MODEL // LANES,
              "conv_dw": 16}
REPL_SHAPES = [("pre_norm", (2, D_MODEL)), ("post_norm", (2, D_MODEL)), ("q_norm", (2, HEAD_DIM)),
               ("k_norm", (2, HEAD_DIM)), ("conv_dw_b", (2, CONV_W)), ("conv_ln_g", (2, CONV_W)),
               ("conv_ln_b", (2, CONV_W)), ("sg_ln_g", (2, SG_W)), ("sg_ln_b", (2, SG_W)),
               ("sg_w", (2, SG_HEADS, SG_CHUNK, SG_CHUNK)), ("sg_b", (2, SG_HEADS, SG_CHUNK))]
REPL_ROWS = 1088
WEIGHT_ORDER = ["pre_norm", "post_norm", "w_in", "w_out", "q_norm", "k_norm", "conv_dw", "conv_dw_b", "conv_ln_g",
                "conv_ln_b", "sg_ln_g", "sg_ln_b", "sg_w", "sg_b"]


def _pack_shard(parts):
    return jnp.concatenate([_flat_rows(parts[k], SHARD_ROWS[k]) for k in ("w_in", "w_out", "conv_dw")], axis=0)


def _unpack_shard(flat, shapes):
    out, at = {}, 0
    for k in ("w_in", "w_out", "conv_dw"):
        size = math.prod(shapes[k])
        out[k] = flat[at:at + SHARD_ROWS[k]].reshape(-1)[:size].reshape(shapes[k])
        at += SHARD_ROWS[k]
    return out


def _pack_repl(parts):
    flat = jnp.concatenate([parts[k].reshape(-1) for k, _ in REPL_SHAPES])
    return jnp.pad(flat, (0, REPL_ROWS * LANES - flat.shape[0])).reshape(REPL_ROWS, LANES)


def _unpack_repl(flat):
    out, at, flat = {}, 0, flat.reshape(-1)
    for k, shape in REPL_SHAPES:
        size = math.prod(shape)
        out[k] = flat[at:at + size].reshape(shape)
        at += size
    return out


def kernel(x, pre_norm, post_norm, w_in, w_out, q_norm, k_norm, conv_dw, conv_dw_b, conv_ln_g, conv_ln_b, sg_ln_g, sg_ln_b, sg_w, sg_b, loss_target, m_pre_norm, m_post_norm, m_w_in, m_w_out, m_q_norm, m_k_norm, m_conv_dw, m_conv_dw_b, m_conv_ln_g, m_conv_ln_b, m_sg_ln_g, m_sg_ln_b, m_sg_w, m_sg_b, v_pre_norm, v_post_norm, v_w_in, v_w_out, v_q_norm, v_k_norm, v_conv_dw, v_conv_dw_b, v_conv_ln_g, v_conv_ln_b, v_sg_ln_g, v_sg_ln_b, v_sg_w, v_sg_b):
    weights = dict(pre_norm=pre_norm, post_norm=post_norm, w_in=w_in, w_out=w_out, q_norm=q_norm, k_norm=k_norm,
                   conv_dw=conv_dw, conv_dw_b=conv_dw_b, conv_ln_g=conv_ln_g, conv_ln_b=conv_ln_b, sg_ln_g=sg_ln_g,
                   sg_ln_b=sg_ln_b, sg_w=sg_w, sg_b=sg_b)
    mom_m = dict(pre_norm=m_pre_norm, post_norm=m_post_norm, w_in=m_w_in, w_out=m_w_out, q_norm=m_q_norm,
                 k_norm=m_k_norm, conv_dw=m_conv_dw, conv_dw_b=m_conv_dw_b, conv_ln_g=m_conv_ln_g,
                 conv_ln_b=m_conv_ln_b, sg_ln_g=m_sg_ln_g, sg_ln_b=m_sg_ln_b, sg_w=m_sg_w, sg_b=m_sg_b)
    mom_v = dict(pre_norm=v_pre_norm, post_norm=v_post_norm, w_in=v_w_in, w_out=v_w_out, q_norm=v_q_norm,
                 k_norm=v_k_norm, conv_dw=v_conv_dw, conv_dw_b=v_conv_dw_b, conv_ln_g=v_conv_ln_g,
                 conv_ln_b=v_conv_ln_b, sg_ln_g=v_sg_ln_g, sg_ln_b=v_sg_ln_b, sg_w=v_sg_w, sg_b=v_sg_b)
    depth = pre_norm.shape[0]
    seq = x.shape[1]
    bk, bk_fwd = min(BK, seq), min(BK_FWD, seq)
    x0 = x.reshape(seq, D_MODEL)
    target = loss_target.reshape(seq, D_MODEL)

    w_in_all, w_out_all, dw_all = _exchange(
        [], [w_in.astype(BF16), w_out.astype(BF16), jnp.pad(conv_dw, ((0, 0), (0, 1), (0, 0)))], "gather_weights")
    w_in_full = w_in_all.transpose(1, 2, 0, 3).reshape(depth, D_MODEL, D_IN)
    w_out_full = w_out_all.transpose(1, 0, 2, 3).reshape(depth, D_MODEL, D_MODEL)
    dw_full = dw_all[:, :, :CONV_K, :].transpose(1, 2, 0, 3).reshape(depth, CONV_K, CONV_W)

    cos, sin, perm = _rope_tables(seq)
    post = jnp.concatenate([jnp.full((ATT_HEADS, 1, HEAD_DIM), Q_SCALE, F32), jnp.ones((KV_HEADS, 1, HEAD_DIM), F32)])
    lane = jnp.arange(SG_W)
    fold = (lane[:, None] // HEAD_DIM == jnp.arange(SG_CHUNK)[None, :]).astype(BF16)

    def layer_consts(l):
        gains = jnp.concatenate([jnp.broadcast_to(q_norm[l], (ATT_HEADS, 1, HEAD_DIM)),
                                 jnp.broadcast_to(k_norm[l], (KV_HEADS, 1, HEAD_DIM))])
        return dict(
            gains=gains, sg_w_b=sg_w[l].astype(BF16), sg_wt_b=sg_w[l].transpose(0, 2, 1).astype(BF16),
            sg_bias=jnp.repeat(sg_b[l].T, HEAD_DIM, axis=1),
            vec=lambda a: a[l].reshape(1, -1))

    saved = []
    xc = x0
    for l in range(depth):
        c = layer_consts(l)
        proj, hb = _proj_fwd(xc, c["vec"](pre_norm), w_in_full[l], f"proj_fwd_{l}")
        raw = jnp.concatenate([_to_heads(proj[:, :ATT_W], ATT_HEADS),
                               _to_heads(proj[:, ATT_W:ATT_W + KV_W], KV_HEADS)], axis=0)
        qk = _qk_prep_fwd(raw, c["gains"], post, cos, sin, perm, f"qk_prep_fwd_{l}")
        qs, kr = qk[:ATT_HEADS], qk[ATT_HEADS:]
        qs_t = qs.transpose(0, 2, 1)
        vh = _to_heads(proj[:, ATT_W + KV_W:ATT_W + 2 * KV_W], KV_HEADS).astype(BF16)
        k_rows, k_cols, v_cols = _row_blocks(kr, bk), _col_blocks(kr, bk), _col_blocks(vh, bk)
        v_ext = _col_blocks(vh, bk_fwd)
        v_ext = jnp.concatenate([v_ext, jnp.ones_like(v_ext[:, :, :1]), jnp.zeros_like(v_ext[:, :, :7])], axis=2)
        o_t, lse = _attn_fwd(qs_t, _row_blocks(kr, bk_fwd), v_ext, f"attn_fwd_{l}")
        att = o_t.reshape(ATT_W, seq).T
        o = o_t.transpose(0, 2, 1)
        lse = jnp.broadcast_to(lse.reshape(ATT_HEADS, seq, 1), (ATT_HEADS, seq, LANES))
        cnv = _conv_fwd(proj, dw_full[l], c["vec"](conv_dw_b), c["vec"](conv_ln_g), c["vec"](conv_ln_b),
                        f"conv_fwd_{l}")
        sgu = _sg_fwd(proj, c["vec"](sg_ln_g), c["vec"](sg_ln_b), c["sg_w_b"], c["sg_bias"], f"sg_fwd_{l}")
        x_new, mix, cat_b = _out_fwd(att, proj, cnv, sgu, xc, w_out_full[l], c["vec"](post_norm), f"out_fwd_{l}")
        saved.append(dict(x=xc, proj=proj, hb=hb, raw=raw, qs=qs, qs_t=qs_t, k_rows=k_rows, k_cols=k_cols, v_cols=v_cols,
                          o=o, lse=lse, att=att, mix=mix, cat_b=cat_b))
        xc = x_new

    sse, dx = _loss_head(xc, target, "loss_head")
    loss = lax.psum(sse[0, 0] * (0.5 / D_MODEL), MESH_AXES)

    grads = {k: [None] * depth for k in WEIGHT_ORDER}
    for l in reversed(range(depth)):
        c, s = layer_consts(l), saved[l]
        dmix_b, datt, dgatt, dcnv, dsgu, g_post = _out_bwd(
            dx, s["mix"], c["vec"](post_norm), w_out_full[l].T, s["att"], s["proj"], f"out_bwd_{l}")
        grads["post_norm"][l] = g_post.reshape(-1)
        grads["w_out"][l] = _matmul_acc(s["cat_b"].T, dmix_b, D_MODEL, f"grad_w_out_{l}")
        do = _to_heads(datt, ATT_HEADS)
        dqs, dkt, dvt = _attn_bwd(s["qs"], s["qs_t"], do, do.astype(BF16).transpose(0, 2, 1),
                                  s["o"], s["lse"], s["k_rows"], s["k_cols"], s["v_cols"], f"attn_bwd_{l}")
        d_qk = jnp.concatenate([dqs, _from_col_blocks(dkt)], axis=0)
        d_raw, g_gain = _qk_prep_bwd(s["raw"], d_qk, c["gains"], post, cos, sin, perm, f"qk_prep_bwd_{l}")
        grads["q_norm"][l] = jnp.sum(g_gain[:ATT_HEADS, 0], axis=0)
        grads["k_norm"][l] = jnp.sum(g_gain[ATT_HEADS:, 0], axis=0)
        dy_conv, dg_conv, g_dw, g_dwb, g_clg, g_clb = _conv_bwd_a(
            s["proj"], dcnv, dw_full[l], c["vec"](conv_dw_b), c["vec"](conv_ln_g), c["vec"](conv_ln_b),
            f"conv_bwd_a_{l}")
        da, db = _conv_bwd_b(s["proj"], dy_conv, dw_full[l], f"conv_bwd_b_{l}")
        grads["conv_dw"][l], grads["conv_dw_b"][l] = g_dw, g_dwb.reshape(-1)
        grads["conv_ln_g"][l], grads["conv_ln_b"][l] = g_clg.reshape(-1), g_clb.reshape(-1)
        du, dv_sg, dg_sg, g_slg, g_slb, g_sw, g_sb = _sg_bwd(
            s["proj"], dsgu, c["vec"](sg_ln_g), c["vec"](sg_ln_b), c["sg_w_b"], c["sg_wt_b"], c["sg_bias"], fold,
            f"sg_bwd_{l}")
        grads["sg_ln_g"][l], grads["sg_ln_b"][l] = g_slg.reshape(-1), g_slb.reshape(-1)
        grads["sg_w"][l], grads["sg_b"][l] = g_sw, g_sb[:, :SG_HEADS].T
        dproj = jnp.concatenate(
            [_from_heads(d_raw[:ATT_HEADS]), _from_heads(d_raw[ATT_HEADS:]), _from_heads(_from_col_blocks(dvt)),
             dgatt, da, db, dg_conv, du, dv_sg, dg_sg], axis=-1).astype(BF16)
        grads["w_in"][l] = _matmul_acc(s["hb"].T, dproj, D_IN // 2, f"grad_w_in_{l}")
        dx, g_pre = _proj_bwd(dproj, w_in_full[l].T, s["x"], c["vec"](pre_norm), dx, f"proj_bwd_{l}")
        grads["pre_norm"][l] = g_pre.reshape(-1)
    grad_x = dx.reshape(x.shape)
    grads = {k: jnp.stack(v) for k, v in grads.items()}

    shard_blocks = dict(
        w_in=grads["w_in"].reshape(depth, D_MODEL, N_DEV, D_IN // N_DEV).transpose(2, 0, 1, 3),
        w_out=grads["w_out"].reshape(depth, N_DEV, D_MODEL // N_DEV, D_MODEL).transpose(1, 0, 2, 3),
        conv_dw=grads["conv_dw"].reshape(depth, CONV_K, N_DEV, CONV_W // N_DEV).transpose(2, 0, 1, 3))
    scatter_src = jnp.stack([_pack_shard({k: a[d] for k, a in shard_blocks.items()}) for d in range(N_DEV)])
    shard_slots, repl_slots = _exchange([scatter_src], [_pack_repl(grads)], "exchange_grads")

    shard_shapes = {k: weights[k].shape for k in SHARD_ROWS}
    gs, ds_, ms, vs = _sum_adamw(shard_slots, _pack_shard(weights), _pack_shard(mom_m), _pack_shard(mom_v),
                                 "adamw_sharded")
    gr, dr, mr, vr = _sum_adamw(repl_slots, _pack_repl(weights), _pack_repl(mom_m), _pack_repl(mom_v),
                                "adamw_replicated")
    results = []
    for shard_flat, repl_flat in ((gs, gr), (ds_, dr), (ms, mr), (vs, vr)):
        parts = {**_unpack_shard(shard_flat, shard_shapes), **_unpack_repl(repl_flat)}
        results.append([parts[k] for k in WEIGHT_ORDER])
    return (loss, grad_x, *results[0], *results[1], *results[2], *results[3])
```

```python
import math

import jax
import jax.numpy as jnp
from jax import lax
from jax.experimental import pallas as pl
from jax.experimental.pallas import tpu as pltpu

F32, BF16 = jnp.float32, jnp.bfloat16

N_DEV = 8
MESH_AXES = ("x", "y", "c")
EPS = 1e-6
D_MODEL = 1024
HEAD_DIM = 64
ATT_HEADS, KV_HEADS = 8, 2
N_QK = ATT_HEADS + KV_HEADS
GROUP = ATT_HEADS // KV_HEADS
ATT_W, KV_W, CONV_W, SG_W = 512, 128, 256, 256
CONV_K, CONV_PAD, HALO = 31, 15, 16
SG_HEADS, SG_CHUNK = 4, 128
D_IN = 2816
GRID_W = 64
ROPE_THETA = 10000.0
LOG2E, LN2 = math.log2(math.e), math.log(2.0)
Q_SCALE = HEAD_DIM ** -0.5 * LOG2E
LANES = 128

COL_GATT, COL_A, COL_B, COL_GCONV, COL_U, COL_VSG, COL_GSG = 3, 5, 6, 7, 8, 9, 10

ADAM_LR, ADAM_B1, ADAM_B2, ADAM_EPS, ADAM_WD, ADAM_STEP = 0.001, 0.9, 0.999, 1e-08, 0.01, 10

T_ROW = 256
T_PREP = 1024
T_GROUP = 512
BQ, BK = 512, 512
HEADS_PER_STEP = 2
FWD_UNROLL, BWD_UNROLL = 8, 4
VMEM_MB = 56
ATTN_BWD_VMEM_MB = 58


def _pcall(body, *, name, grid, in_specs, out_specs, out_shape, scratch=(), sem=None, vmem_mb=None):
    params = {}
    if sem is not None:
        params["dimension_semantics"] = sem
    if vmem_mb is not None:
        params["vmem_limit_bytes"] = vmem_mb << 20
    return pl.pallas_call(body, name=name, grid=grid, in_specs=in_specs, out_specs=out_specs, out_shape=out_shape,
                          scratch_shapes=list(scratch), compiler_params=pltpu.CompilerParams(**params))


def _dot(a, b):
    return jnp.dot(a, b, preferred_element_type=F32)


def _sigmoid(x):
    return 1.0 / (1.0 + jnp.exp(-x))


def _silu_and_grad(x):
    s = _sigmoid(x)
    return x * s, s * (1.0 + x * (1.0 - s))


def _gelu_and_grad(x):
    cdf = 0.5 * (1.0 + lax.erf(x * (1.0 / math.sqrt(2.0))))
    pdf = jnp.exp(-0.5 * x * x) * (1.0 / math.sqrt(2.0 * math.pi))
    return x * cdf, cdf + x * pdf


def _split_dot(y, mat):
    hi = y.astype(BF16)
    lo = (y - hi.astype(F32)).astype(BF16)
    return _dot(hi, mat) + _dot(lo, mat)


def _row_tile(rows, cap):
    best = 8
    for t in range(8, min(rows, cap) + 1, 8):
        if rows % t == 0:
            best = t
    return best


def _exchange(scatter, gather, name):
    n_s = len(scatter)
    arrs = list(scatter) + list(gather)
    n = len(arrs)
    flips = [(fx, fy, fc) for fx in (0, 1) for fy in (0, 1) for fc in (0, 1)][1:]
    n_peer = len(flips)

    def body(*refs):
        ins, outs = refs[:n], refs[n:2 * n]
        send_sems, recv_sems, local_sems = refs[2 * n:]
        pos = tuple(lax.axis_index(a) for a in MESH_AXES)

        def peer(flip):
            return tuple((1 - p) if f else p for p, f in zip(pos, flip))

        def slot(p):
            return 4 * p[0] + 2 * p[1] + p[2]

        def src(a, p):
            return ins[a].at[slot(p)] if a < n_s else ins[a]

        def remote(a, k, src_ref, dst_slot, to):
            return pltpu.make_async_remote_copy(
                src_ref=src_ref, dst_ref=outs[a].at[dst_slot], send_sem=send_sems.at[a * n_peer + k],
                recv_sem=recv_sems.at[a * n_peer + k], device_id=to, device_id_type=pl.DeviceIdType.MESH)

        local = [pltpu.make_async_copy(src(a, pos), outs[a].at[slot(pos)], local_sems.at[a]) for a in range(n)]
        for cp in local:
            cp.start()
        sends = [remote(a, k, src(a, peer(f)), slot(pos), peer(f)) for a in range(n) for k, f in enumerate(flips)]
        for cp in sends:
            cp.start()
        for a in range(n):
            for k, f in enumerate(flips):
                remote(a, k, src(a, peer(f)), slot(peer(f)), peer(f)).wait_recv()
        for cp in sends:
            cp.wait_send()
        for cp in local:
            cp.wait()

    out_shape = [jax.ShapeDtypeStruct((N_DEV,) + (a.shape[1:] if i < n_s else a.shape), a.dtype)
                 for i, a in enumerate(arrs)]
    any_spec = pl.BlockSpec(memory_space=pl.ANY)
    return pl.pallas_call(
        body, name=name, out_shape=out_shape, in_specs=[any_spec] * n, out_specs=[any_spec] * n,
        scratch_shapes=[pltpu.SemaphoreType.DMA((n * n_peer,)), pltpu.SemaphoreType.DMA((n * n_peer,)),
                        pltpu.SemaphoreType.DMA((n,))],
    )(*arrs)


def _sum_adamw(slots, w, m, v, name):
    rows = w.shape[0]
    tr = _row_tile(rows, 1024)
    c1 = 1.0 - ADAM_B1 ** ADAM_STEP
    c2 = 1.0 - ADAM_B2 ** ADAM_STEP

    def body(s_ref, w_ref, m_ref, v_ref, g_out, d_out, m_out, v_out):
        g = s_ref[0]
        for d in range(1, N_DEV):
            g = g + s_ref[d]
        m_new = ADAM_B1 * m_ref[...] + (1.0 - ADAM_B1) * g
        v_new = ADAM_B2 * v_ref[...] + (1.0 - ADAM_B2) * (g * g)
        m_hat = m_new / c1
        v_hat = v_new / c2
        g_out[...] = g
        d_out[...] = -ADAM_LR * (m_hat / (jnp.sqrt(v_hat) + ADAM_EPS) + ADAM_WD * w_ref[...])
        m_out[...] = m_new
        v_out[...] = v_new

    flat = pl.BlockSpec((tr, LANES), lambda i: (i, 0))
    return _pcall(
        body, name=name, grid=(rows // tr,),
        in_specs=[pl.BlockSpec((N_DEV, tr, LANES), lambda i: (0, i, 0)), flat, flat, flat],
        out_specs=[flat] * 4, out_shape=[jax.ShapeDtypeStruct((rows, LANES), F32)] * 4,
        sem=("parallel",), vmem_mb=VMEM_MB)(slots, w, m, v)


def _proj_fwd(x, gain, w, name):
    seq = x.shape[0]

    def body(x_ref, g_ref, w_ref, proj_ref, hb_ref):
        xf = x_ref[...]
        r = lax.rsqrt(jnp.mean(xf * xf, axis=-1, keepdims=True) + EPS)
        h = (xf * r * g_ref[...]).astype(BF16)
        hb_ref[...] = h
        proj_ref[...] = _dot(h, w_ref[...])

    return _pcall(
        body, name=name, grid=(seq // T_ROW,),
        in_specs=[pl.BlockSpec((T_ROW, D_MODEL), lambda i: (i, 0)), pl.BlockSpec((1, D_MODEL), lambda i: (0, 0)),
                  pl.BlockSpec((D_MODEL, D_IN), lambda i: (0, 0))],
        out_specs=[pl.BlockSpec((T_ROW, D_IN), lambda i: (i, 0)), pl.BlockSpec((T_ROW, D_MODEL), lambda i: (i, 0))],
        out_shape=[jax.ShapeDtypeStruct((seq, D_IN), F32), jax.ShapeDtypeStruct((seq, D_MODEL), BF16)],
        sem=("parallel",), vmem_mb=VMEM_MB)(x, gain, w)


def _proj_bwd(dproj, w_t, x, gain, dxo, name):
    seq = x.shape[0]

    def body(dp_ref, w_ref, x_ref, g_ref, dxo_ref, dx_ref, dg_ref):
        dh = _dot(dp_ref[...], w_ref[...])
        xf = x_ref[...]
        r = lax.rsqrt(jnp.mean(xf * xf, axis=-1, keepdims=True) + EPS)
        n = xf * r
        dn = dh * g_ref[...]
        dx_ref[...] = dxo_ref[...] + r * (dn - n * jnp.mean(dn * n, axis=-1, keepdims=True))

        @pl.when(pl.program_id(0) == 0)
        def _():
            dg_ref[...] = jnp.zeros_like(dg_ref)

        dg_ref[...] += jnp.sum(dh * n, axis=0, keepdims=True)

    row = pl.BlockSpec((T_ROW, D_MODEL), lambda i: (i, 0))
    vec = pl.BlockSpec((1, D_MODEL), lambda i: (0, 0))
    return _pcall(
        body, name=name, grid=(seq // T_ROW,),
        in_specs=[pl.BlockSpec((T_ROW, D_IN), lambda i: (i, 0)), pl.BlockSpec((D_IN, D_MODEL), lambda i: (0, 0)),
                  row, vec, row],
        out_specs=[row, vec],
        out_shape=[jax.ShapeDtypeStruct((seq, D_MODEL), F32), jax.ShapeDtypeStruct((1, D_MODEL), F32)],
        sem=("arbitrary",), vmem_mb=VMEM_MB)(dproj, w_t, x, gain, dxo)


def _matmul_acc(a_t, b, tn, name):
    m, seq = a_t.shape
    n = b.shape[1]
    ts = min(512, seq)

    def body(a_ref, b_ref, o_ref):
        @pl.when(pl.program_id(1) == 0)
        def _():
            o_ref[...] = jnp.zeros_like(o_ref)

        o_ref[...] += _dot(a_ref[...], b_ref[...])

    return _pcall(
        body, name=name, grid=(n // tn, seq // ts),
        in_specs=[pl.BlockSpec((m, ts), lambda j, k: (0, k)), pl.BlockSpec((ts, tn), lambda j, k: (k, j))],
        out_specs=pl.BlockSpec((m, tn), lambda j, k: (0, j)), out_shape=jax.ShapeDtypeStruct((m, n), F32),
        sem=("parallel", "arbitrary"), vmem_mb=VMEM_MB)(a_t, b)


def _rope_tables(seq):
    t = jnp.arange(seq, dtype=jnp.int32)
    row = (t // GRID_W).astype(F32)
    col = (t % GRID_W).astype(F32)
    half = HEAD_DIM // 4
    inv_freq = ROPE_THETA ** (-jnp.arange(half, dtype=F32) / half)
    ang_r = row[:, None] * inv_freq[None, :]
    ang_c = col[:, None] * inv_freq[None, :]
    cos = jnp.concatenate([jnp.cos(ang_r), jnp.cos(ang_r), jnp.cos(ang_c), jnp.cos(ang_c)], axis=-1)
    sin = jnp.concatenate([-jnp.sin(ang_r), jnp.sin(ang_r), -jnp.sin(ang_c), jnp.sin(ang_c)], axis=-1)
    j = jnp.arange(HEAD_DIM)
    partner = jnp.where((j % (2 * half)) < half, j + half, j - half)
    perm = (j[:, None] == partner[None, :]).astype(BF16)
    return cos, sin, perm


def _qk_specs(seq):
    t = min(T_PREP, seq)
    blk = pl.BlockSpec((None, t, HEAD_DIM), lambda h, i: (h, i, 0))
    vec = pl.BlockSpec((None, 1, HEAD_DIM), lambda h, i: (h, 0, 0))
    tab = pl.BlockSpec((t, HEAD_DIM), lambda h, i: (i, 0))
    mat = pl.BlockSpec((HEAD_DIM, HEAD_DIM), lambda h, i: (0, 0))
    return t, blk, vec, tab, mat


def _qk_prep_fwd(raw, gains, post, cos, sin, perm, name):
    seq = raw.shape[1]
    t, blk, vec, tab, mat = _qk_specs(seq)

    def body(x_ref, g_ref, p_ref, c_ref, s_ref, m_ref, o_ref):
        xf = x_ref[...]
        r = lax.rsqrt(jnp.mean(xf * xf, axis=-1, keepdims=True) + EPS)
        y = xf * r * g_ref[...]
        z = y * c_ref[...] + _split_dot(y, m_ref[...]) * s_ref[...]
        o_ref[...] = (z * p_ref[...]).astype(BF16)

    return _pcall(body, name=name, grid=(raw.shape[0], seq // t), in_specs=[blk, vec, vec, tab, tab, mat], out_specs=blk,
                  out_shape=jax.ShapeDtypeStruct(raw.shape, BF16), sem=("parallel", "parallel"))(
                      raw, gains, post, cos, sin, perm)


def _qk_prep_bwd(raw, dout, gains, post, cos, sin, perm, name):
    seq = raw.shape[1]
    t, blk, vec, tab, mat = _qk_specs(seq)

    def body(x_ref, d_ref, g_ref, p_ref, c_ref, s_ref, m_ref, dx_ref, dg_ref):
        xf = x_ref[...]
        r = lax.rsqrt(jnp.mean(xf * xf, axis=-1, keepdims=True) + EPS)
        n = xf * r
        dz = d_ref[...] * p_ref[...]
        dy = dz * c_ref[...] + _split_dot(dz * s_ref[...], m_ref[...])
        dn = dy * g_ref[...]
        dx_ref[...] = r * (dn - n * jnp.mean(dn * n, axis=-1, keepdims=True))

        @pl.when(pl.program_id(1) == 0)
        def _():
            dg_ref[...] = jnp.zeros_like(dg_ref)

        dg_ref[...] += jnp.sum(dy * n, axis=0, keepdims=True)

    return _pcall(body, name=name, grid=(raw.shape[0], seq // t), in_specs=[blk, blk, vec, vec, tab, tab, mat],
                  out_specs=[blk, vec],
                  out_shape=[jax.ShapeDtypeStruct(raw.shape, F32),
                             jax.ShapeDtypeStruct((raw.shape[0], 1, HEAD_DIM), F32)],
                  sem=("parallel", "arbitrary"))(raw, dout, gains, post, cos, sin, perm)


V_ROWS = HEAD_DIM + 8


def _unroll(nk, cap):
    u = 1
    while u * 2 <= cap and nk % (u * 2) == 0:
        u *= 2
    return u


def _attn_fwd(qs_t, k, v_t, name):
    seq = qs_t.shape[2]
    nk, bk = k.shape[1], k.shape[2]
    bq = min(BQ, seq)
    unroll = _unroll(nk, FWD_UNROLL)
    heads = range(HEADS_PER_STEP)

    def body(qt_ref, k_ref, vt_ref, ot_ref, lse_ref, s_scr):
        q_t = [qt_ref[h] for h in heads]

        def scores(j, slot):
            kj = k_ref[j]
            top = []
            for h in heads:
                s = _dot(kj, q_t[h])
                s_scr[slot, h] = s
                top.append(jnp.max(s, axis=0, keepdims=True))
            return tuple(top)

        def accumulate(j, slot, state, top):
            vtj = vt_ref[j]
            out = []
            for h in heads:
                m, acc = state[h]
                m_new = jnp.maximum(m, top[h])
                p = jnp.exp2(s_scr[slot, h] - m_new).astype(BF16)
                out.append((m_new, jnp.exp2(m - m_new) * acc + _dot(vtj, p)))
            return tuple(out)

        def step(t, carry):
            state, top = carry
            for u in range(unroll):
                nxt = unroll * t + u + 1
                top_next = scores(jnp.minimum(nxt, nk - 1) if u == unroll - 1 else nxt, (u + 1) % 2)
                state = accumulate(unroll * t + u, u % 2, state, top)
                top = top_next
            return state, top

        init = tuple((jnp.full((1, bq), -jnp.inf, F32), jnp.zeros((V_ROWS, bq), F32)) for _ in heads)
        state, _ = lax.fori_loop(0, nk // unroll, step, (init, scores(0, 0)))
        for h in heads:
            m, acc = state[h]
            l = acc[HEAD_DIM:HEAD_DIM + 1, :]
            ot_ref[h] = acc[:HEAD_DIM, :] / l
            lse_ref[h] = m + jnp.log2(l)

    kv_of = lambda g: g * HEADS_PER_STEP // GROUP
    return _pcall(
        body, name=name, grid=(ATT_HEADS // HEADS_PER_STEP, seq // bq),
        in_specs=[pl.BlockSpec((HEADS_PER_STEP, HEAD_DIM, bq), lambda g, i: (g, 0, i)),
                  pl.BlockSpec((None, nk, bk, HEAD_DIM), lambda g, i: (kv_of(g), 0, 0, 0)),
                  pl.BlockSpec((None, nk, V_ROWS, bk), lambda g, i: (kv_of(g), 0, 0, 0))],
        out_specs=[pl.BlockSpec((HEADS_PER_STEP, HEAD_DIM, bq), lambda g, i: (g, 0, i)),
                   pl.BlockSpec((HEADS_PER_STEP, 1, bq), lambda g, i: (g, 0, i))],
        out_shape=[jax.ShapeDtypeStruct((ATT_HEADS, HEAD_DIM, seq), F32),
                   jax.ShapeDtypeStruct((ATT_HEADS, 1, seq), F32)],
        scratch=[pltpu.VMEM((2, HEADS_PER_STEP, bk, bq), F32)],
        sem=("parallel", "parallel"), vmem_mb=VMEM_MB)(qs_t, k, v_t)


def _attn_bwd(qs, qs_t, do, do_t, o, lse, k, k_t, v_t, name):
    seq = qs.shape[1]
    nk, bk = k.shape[1], k.shape[2]
    bq = min(BQ, seq)
    nq = seq // bq

    unroll = _unroll(nk, BWD_UNROLL)
    heads = range(HEADS_PER_STEP)
    pairs = GROUP // HEADS_PER_STEP

    def body(q_ref, qt_ref, do_ref, dot_ref, o_ref, lse_ref, k_ref, kt_ref, vt_ref, dq_ref, dkt_ref, dvt_ref,
             s_scr, dp_scr):
        @pl.when((pl.program_id(1) == 0) & (pl.program_id(2) == 0))
        def _():
            dkt_ref[...] = jnp.zeros_like(dkt_ref)
            dvt_ref[...] = jnp.zeros_like(dvt_ref)

        q, q_t, do_t_b = [q_ref[h] for h in heads], [qt_ref[h] for h in heads], [dot_ref[h] for h in heads]
        do_l = [do_ref[h] * LN2 for h in heads]
        do_b = [d.astype(BF16) for d in do_l]
        delta = [jnp.sum(do_l[h] * o_ref[h], axis=-1, keepdims=True) for h in heads]
        lse_col = [jnp.max(lse_ref[h], axis=-1, keepdims=True) for h in heads]

        def products(j, slot):
            ktj, vtj = kt_ref[j], vt_ref[j]
            for h in heads:
                s_scr[slot, h] = _dot(q[h], ktj)
                dp_scr[slot, h] = _dot(do_b[h], vtj)

        def gradients(j, slot, dq):
            kj = k_ref[j]
            dvt = jnp.zeros((HEAD_DIM, bk), F32)
            dkt = jnp.zeros((HEAD_DIM, bk), F32)
            new = []
            for h in heads:
                p = jnp.exp2(s_scr[slot, h] - lse_col[h])
                ds = (p * (dp_scr[slot, h] - delta[h])).astype(BF16)
                dvt = dvt + _dot(do_t_b[h], p.astype(BF16))
                dkt = dkt + _dot(q_t[h], ds)
                new.append(dq[h] + _dot(ds, kj))
            dvt_ref[j] += dvt
            dkt_ref[j] += dkt
            return tuple(new)

        def step(t, dq):
            for u in range(unroll):
                nxt = unroll * t + u + 1
                products(jnp.minimum(nxt, nk - 1) if u == unroll - 1 else nxt, (u + 1) % 2)
                dq = gradients(unroll * t + u, u % 2, dq)
            return dq

        products(0, 0)
        res = lax.fori_loop(0, nk // unroll, step, tuple(jnp.zeros((bq, HEAD_DIM), F32) for _ in heads))
        for h in heads:
            dq_ref[h] = res[h]

    first = lambda g, hh: g * pairs + hh
    row = pl.BlockSpec((HEADS_PER_STEP, bq, HEAD_DIM), lambda g, hh, i: (first(g, hh), i, 0))
    col = pl.BlockSpec((HEADS_PER_STEP, HEAD_DIM, bq), lambda g, hh, i: (first(g, hh), 0, i))
    kv_rows = pl.BlockSpec((None, nk, bk, HEAD_DIM), lambda g, hh, i: (g, 0, 0, 0))
    kv_cols = pl.BlockSpec((None, nk, HEAD_DIM, bk), lambda g, hh, i: (g, 0, 0, 0))
    return _pcall(
        body, name=name, grid=(KV_HEADS, pairs, nq),
        in_specs=[row, col, row, col, row,
                  pl.BlockSpec((HEADS_PER_STEP, bq, LANES), lambda g, hh, i: (first(g, hh), i, 0)),
                  kv_rows, kv_cols, kv_cols],
        out_specs=[row, kv_cols, kv_cols],
        out_shape=[jax.ShapeDtypeStruct((ATT_HEADS, seq, HEAD_DIM), F32),
                   jax.ShapeDtypeStruct((KV_HEADS, nk, HEAD_DIM, bk), F32),
                   jax.ShapeDtypeStruct((KV_HEADS, nk, HEAD_DIM, bk), F32)],
        scratch=[pltpu.VMEM((2, HEADS_PER_STEP, bq, bk), F32), pltpu.VMEM((2, HEADS_PER_STEP, bq, bk), F32)],
        sem=("parallel", "arbitrary", "arbitrary"), vmem_mb=ATTN_BWD_VMEM_MB)(
            qs, qs_t, do, do_t, o, lse, k, k_t, v_t)


def _halo_specs(t, col, n_tiles):
    per = t // HALO
    last = n_tiles * per - 1
    before = pl.BlockSpec((HALO, CONV_W), lambda i: (jnp.maximum(i * per - 1, 0), col))
    after = pl.BlockSpec((HALO, CONV_W), lambda i: (jnp.minimum((i + 1) * per, last), col))
    return before, after


def _glu(a, b):
    return a * _sigmoid(b)


def _conv_taps(ext_ref, w_ref, t, flip):
    acc = jnp.zeros((t, CONV_W), F32)
    for k in range(CONV_K):
        off = (HALO + CONV_PAD - k) if flip else (HALO - CONV_PAD + k)
        acc = acc + w_ref[k:k + 1, :] * ext_ref[pl.ds(off, t), :]
    return acc


def _fill_ext(ext_ref, before, tile, after, t, i, n_tiles):
    ext_ref[pl.ds(0, HALO), :] = jnp.where(i > 0, before, 0.0)
    ext_ref[pl.ds(HALO, t), :] = tile
    ext_ref[pl.ds(HALO + t, HALO), :] = jnp.where(i < n_tiles - 1, after, 0.0)


def _conv_fwd(proj, w, bias, ln_g, ln_b, name):
    seq = proj.shape[0]
    t = min(T_GROUP, seq)
    n_tiles = seq // t

    def body(a_ref, b_ref, ap_ref, bp_ref, an_ref, bn_ref, gate_ref, w_ref, bias_ref, g_ref, beta_ref, o_ref, ext_ref):
        i = pl.program_id(0)
        _fill_ext(ext_ref, _glu(ap_ref[...], bp_ref[...]), _glu(a_ref[...], b_ref[...]),
                  _glu(an_ref[...], bn_ref[...]), t, i, n_tiles)
        y = _conv_taps(ext_ref, w_ref, t, False) + bias_ref[...]
        mu = jnp.mean(y, axis=-1, keepdims=True)
        yc = y - mu
        rs = lax.rsqrt(jnp.mean(yc * yc, axis=-1, keepdims=True) + EPS)
        z = yc * rs * g_ref[...] + beta_ref[...]
        o_ref[...] = _silu_and_grad(z)[0] * _silu_and_grad(gate_ref[...])[0]

    tile = lambda c: pl.BlockSpec((t, CONV_W), lambda i: (i, c))
    ab, aa = _halo_specs(t, COL_A, n_tiles)
    bb, ba = _halo_specs(t, COL_B, n_tiles)
    vec = pl.BlockSpec((1, CONV_W), lambda i: (0, 0))
    return _pcall(
        body, name=name, grid=(n_tiles,),
        in_specs=[tile(COL_A), tile(COL_B), ab, bb, aa, ba, tile(COL_GCONV),
                  pl.BlockSpec((CONV_K, CONV_W), lambda i: (0, 0)), vec, vec, vec],
        out_specs=pl.BlockSpec((t, CONV_W), lambda i: (i, 0)), out_shape=jax.ShapeDtypeStruct((seq, CONV_W), F32),
        scratch=[pltpu.VMEM((t + 2 * HALO, CONV_W), F32)], sem=("parallel",))(
            proj, proj, proj, proj, proj, proj, proj, w, bias, ln_g, ln_b)


def _conv_bwd_a(proj, dcnv, w, bias, ln_g, ln_b, name):
    seq = proj.shape[0]
    t = min(T_GROUP, seq)
    n_tiles = seq // t

    def body(a_ref, b_ref, ap_ref, bp_ref, an_ref, bn_ref, gate_ref, d_ref, w_ref, bias_ref, g_ref, beta_ref,
             dy_ref, dgate_ref, dw_ref, dbias_ref, dg_ref, dbeta_ref, ext_ref):
        i = pl.program_id(0)
        _fill_ext(ext_ref, _glu(ap_ref[...], bp_ref[...]), _glu(a_ref[...], b_ref[...]),
                  _glu(an_ref[...], bn_ref[...]), t, i, n_tiles)
        y = _conv_taps(ext_ref, w_ref, t, False) + bias_ref[...]
        mu = jnp.mean(y, axis=-1, keepdims=True)
        yc = y - mu
        rs = lax.rsqrt(jnp.mean(yc * yc, axis=-1, keepdims=True) + EPS)
        n = yc * rs
        z = n * g_ref[...] + beta_ref[...]
        act, dact = _silu_and_grad(z)
        gate, dgate = _silu_and_grad(gate_ref[...])
        d = d_ref[...]
        dgate_ref[...] = d * act * dgate
        dz = d * gate * dact
        dn = dz * g_ref[...]
        dy = rs * (dn - jnp.mean(dn, axis=-1, keepdims=True) - n * jnp.mean(dn * n, axis=-1, keepdims=True))
        dy_ref[...] = dy

        @pl.when(i == 0)
        def _():
            dw_ref[...] = jnp.zeros_like(dw_ref)
            dbias_ref[...] = jnp.zeros_like(dbias_ref)
            dg_ref[...] = jnp.zeros_like(dg_ref)
            dbeta_ref[...] = jnp.zeros_like(dbeta_ref)

        dg_ref[...] += jnp.sum(dz * n, axis=0, keepdims=True)
        dbeta_ref[...] += jnp.sum(dz, axis=0, keepdims=True)
        dbias_ref[...] += jnp.sum(dy, axis=0, keepdims=True)
        for k in range(CONV_K):
            dw_ref[k:k + 1, :] += jnp.sum(dy * ext_ref[pl.ds(HALO - CONV_PAD + k, t), :], axis=0, keepdims=True)

    tile = lambda c: pl.BlockSpec((t, CONV_W), lambda i: (i, c))
    own = pl.BlockSpec((t, CONV_W), lambda i: (i, 0))
    ab, aa = _halo_specs(t, COL_A, n_tiles)
    bb, ba = _halo_specs(t, COL_B, n_tiles)
    vec = pl.BlockSpec((1, CONV_W), lambda i: (0, 0))
    taps = pl.BlockSpec((CONV_K, CONV_W), lambda i: (0, 0))
    vshape = jax.ShapeDtypeStruct((1, CONV_W), F32)
    return _pcall(
        body, name=name, grid=(n_tiles,),
        in_specs=[tile(COL_A), tile(COL_B), ab, bb, aa, ba, tile(COL_GCONV), own, taps, vec, vec, vec],
        out_specs=[own, own, taps, vec, vec, vec],
        out_shape=[jax.ShapeDtypeStruct((seq, CONV_W), F32), jax.ShapeDtypeStruct((seq, CONV_W), F32),
                   jax.ShapeDtypeStruct((CONV_K, CONV_W), F32), vshape, vshape, vshape],
        scratch=[pltpu.VMEM((t + 2 * HALO, CONV_W), F32)], sem=("arbitrary",))(
            proj, proj, proj, proj, proj, proj, proj, dcnv, w, bias, ln_g, ln_b)


def _conv_bwd_b(proj, dy, w, name):
    seq = proj.shape[0]
    t = min(T_GROUP, seq)
    n_tiles = seq // t

    def body(a_ref, b_ref, dy_ref, dyp_ref, dyn_ref, w_ref, da_ref, db_ref, ext_ref):
        i = pl.program_id(0)
        _fill_ext(ext_ref, dyp_ref[...], dy_ref[...], dyn_ref[...], t, i, n_tiles)
        dh = _conv_taps(ext_ref, w_ref, t, True)
        sig = _sigmoid(b_ref[...])
        da_ref[...] = dh * sig
        db_ref[...] = dh * a_ref[...] * sig * (1.0 - sig)

    tile = lambda c: pl.BlockSpec((t, CONV_W), lambda i: (i, c))
    own = pl.BlockSpec((t, CONV_W), lambda i: (i, 0))
    before, after = _halo_specs(t, 0, n_tiles)
    return _pcall(
        body, name=name, grid=(n_tiles,),
        in_specs=[tile(COL_A), tile(COL_B), own, before, after, pl.BlockSpec((CONV_K, CONV_W), lambda i: (0, 0))],
        out_specs=[own, own], out_shape=[jax.ShapeDtypeStruct((seq, CONV_W), F32)] * 2,
        scratch=[pltpu.VMEM((t + 2 * HALO, CONV_W), F32)], sem=("parallel",))(proj, proj, dy, dy, dy, w)


def _head_masks():
    lane_head = lax.broadcasted_iota(jnp.int32, (SG_CHUNK, SG_W), 1) // HEAD_DIM
    return [lane_head == h for h in range(SG_HEADS)]


def _sg_mix(mats_ref, rhs, masks):
    out = jnp.zeros((SG_CHUNK, SG_W), F32)
    for h in range(SG_HEADS):
        out = out + jnp.where(masks[h], _dot(mats_ref[h], rhs), 0.0)
    return out


def _sg_specs(seq):
    t = min(T_GROUP, seq)
    tile = lambda c: pl.BlockSpec((t, SG_W), lambda i: (i, c))
    own = pl.BlockSpec((t, SG_W), lambda i: (i, 0))
    vec = pl.BlockSpec((1, SG_W), lambda i: (0, 0))
    mats = pl.BlockSpec((SG_HEADS, SG_CHUNK, SG_CHUNK), lambda i: (0, 0, 0))
    full = pl.BlockSpec((SG_CHUNK, SG_W), lambda i: (0, 0))
    return t, tile, own, vec, mats, full


def _sg_fwd(proj, ln_g, ln_b, w_b, bias_full, name):
    seq = proj.shape[0]
    t, tile, own, vec, mats, full = _sg_specs(seq)

    def body(u_ref, v_ref, gate_ref, g_ref, beta_ref, w_ref, bias_ref, o_ref):
        masks = _head_masks()
        for c in range(t // SG_CHUNK):
            rows = pl.ds(c * SG_CHUNK, SG_CHUNK)
            vg = _gelu_and_grad(v_ref[rows, :])[0]
            mu = jnp.mean(vg, axis=-1, keepdims=True)
            vc = vg - mu
            rs = lax.rsqrt(jnp.mean(vc * vc, axis=-1, keepdims=True) + EPS)
            vln = vc * rs * g_ref[...] + beta_ref[...]
            mixed = _sg_mix(w_ref, vln.astype(BF16), masks) + bias_ref[...]
            o_ref[rows, :] = _gelu_and_grad(u_ref[rows, :])[0] * mixed * _silu_and_grad(gate_ref[rows, :])[0]

    return _pcall(body, name=name, grid=(seq // t,),
                  in_specs=[tile(COL_U), tile(COL_VSG), tile(COL_GSG), vec, vec, mats, full], out_specs=own,
                  out_shape=jax.ShapeDtypeStruct((seq, SG_W), F32), sem=("parallel",))(
                      proj, proj, proj, ln_g, ln_b, w_b, bias_full)


def _sg_bwd(proj, dsg, ln_g, ln_b, w_b, w_t_b, bias_full, fold, name):
    seq = proj.shape[0]
    t, tile, own, vec, mats, full = _sg_specs(seq)
    n_tiles = seq // t

    def body(u_ref, v_ref, gate_ref, d_ref, g_ref, beta_ref, w_ref, wt_ref, bias_ref, fold_ref,
             du_ref, dv_ref, dgate_ref, dg_ref, dbeta_ref, dw_ref, db_ref, dbias_acc):
        i = pl.program_id(0)

        @pl.when(i == 0)
        def _():
            dg_ref[...] = jnp.zeros_like(dg_ref)
            dbeta_ref[...] = jnp.zeros_like(dbeta_ref)
            dw_ref[...] = jnp.zeros_like(dw_ref)
            dbias_acc[...] = jnp.zeros_like(dbias_acc)

        masks = _head_masks()
        for c in range(t // SG_CHUNK):
            rows = pl.ds(c * SG_CHUNK, SG_CHUNK)
            ug, dug = _gelu_and_grad(u_ref[rows, :])
            vg, dvg = _gelu_and_grad(v_ref[rows, :])
            mu = jnp.mean(vg, axis=-1, keepdims=True)
            vc = vg - mu
            rs = lax.rsqrt(jnp.mean(vc * vc, axis=-1, keepdims=True) + EPS)
            vn = vc * rs
            vln_b = (vn * g_ref[...] + beta_ref[...]).astype(BF16)
            mixed = _sg_mix(w_ref, vln_b, masks) + bias_ref[...]
            gate, dgate = _silu_and_grad(gate_ref[rows, :])
            d = d_ref[rows, :]
            dgate_ref[rows, :] = d * ug * mixed * dgate
            du_ref[rows, :] = d * mixed * gate * dug
            dmixed = d * ug * gate
            dbias_acc[...] += dmixed
            dmixed_b = dmixed.astype(BF16)
            for h in range(SG_HEADS):
                dm_h = jnp.where(masks[h], dmixed_b, jnp.zeros_like(dmixed_b))
                dw_ref[h] += lax.dot_general(dm_h, vln_b, (((1,), (1,)), ((), ())), preferred_element_type=F32)
            dvln = _sg_mix(wt_ref, dmixed_b, masks)
            dg_ref[...] += jnp.sum(dvln * vn, axis=0, keepdims=True)
            dbeta_ref[...] += jnp.sum(dvln, axis=0, keepdims=True)
            dvn = dvln * g_ref[...]
            dvgelu = rs * (dvn - jnp.mean(dvn, axis=-1, keepdims=True) - vn * jnp.mean(dvn * vn, axis=-1, keepdims=True))
            dv_ref[rows, :] = dvgelu * dvg

        @pl.when(i == n_tiles - 1)
        def _():
            db_ref[...] = _split_dot(dbias_acc[...], fold_ref[...])

    sq = pl.BlockSpec((SG_CHUNK, SG_CHUNK), lambda i: (0, 0))
    vshape = jax.ShapeDtypeStruct((1, SG_W), F32)
    return _pcall(
        body, name=name, grid=(n_tiles,),
        in_specs=[tile(COL_U), tile(COL_VSG), tile(COL_GSG), own, vec, vec, mats, mats, full,
                  pl.BlockSpec((SG_W, SG_CHUNK), lambda i: (0, 0))],
        out_specs=[own, own, own, vec, vec, mats, sq],
        out_shape=[jax.ShapeDtypeStruct((seq, SG_W), F32)] * 3 + [
            vshape, vshape, jax.ShapeDtypeStruct((SG_HEADS, SG_CHUNK, SG_CHUNK), F32),
            jax.ShapeDtypeStruct((SG_CHUNK, SG_CHUNK), F32)],
        scratch=[pltpu.VMEM((SG_CHUNK, SG_W), F32)], sem=("arbitrary",))(
            proj, proj, proj, dsg, ln_g, ln_b, w_b, w_t_b, bias_full, fold)


def _out_fwd(att, proj, cnv, sgu, x, w, gain, name):
    seq = x.shape[0]

    def body(att_ref, g0_ref, g1_ref, cnv_ref, sgu_ref, x_ref, w_ref, gain_ref, xo_ref, mix_ref, cat_ref):
        gate = jnp.concatenate([_silu_and_grad(g0_ref[...])[0], _silu_and_grad(g1_ref[...])[0]], axis=-1)
        cat_ref[:, 0:ATT_W] = (att_ref[...] * gate).astype(BF16)
        cat_ref[:, ATT_W:ATT_W + CONV_W] = cnv_ref[...].astype(BF16)
        cat_ref[:, ATT_W + CONV_W:] = sgu_ref[...].astype(BF16)
        mix = _dot(cat_ref[...], w_ref[...])
        mix_ref[...] = mix
        r = lax.rsqrt(jnp.mean(mix * mix, axis=-1, keepdims=True) + EPS)
        xo_ref[...] = x_ref[...] + mix * r * gain_ref[...]

    row = lambda w_: pl.BlockSpec((T_ROW, w_), lambda i: (i, 0))
    gate_blk = lambda c: pl.BlockSpec((T_ROW, 256), lambda i: (i, c))
    return _pcall(
        body, name=name, grid=(seq // T_ROW,),
        in_specs=[row(ATT_W), gate_blk(COL_GATT), gate_blk(COL_GATT + 1), row(CONV_W), row(SG_W), row(D_MODEL),
                  pl.BlockSpec((D_MODEL, D_MODEL), lambda i: (0, 0)), pl.BlockSpec((1, D_MODEL), lambda i: (0, 0))],
        out_specs=[row(D_MODEL), row(D_MODEL), row(D_MODEL)],
        out_shape=[jax.ShapeDtypeStruct((seq, D_MODEL), F32), jax.ShapeDtypeStruct((seq, D_MODEL), F32),
                   jax.ShapeDtypeStruct((seq, D_MODEL), BF16)],
        sem=("parallel",), vmem_mb=VMEM_MB)(att, proj, proj, cnv, sgu, x, w, gain)


def _out_bwd(dxo, mix, gain, w_t, att, proj, name):
    seq = dxo.shape[0]

    def body(dxo_ref, mix_ref, gain_ref, w_ref, att_ref, g0_ref, g1_ref,
             dmix_ref, datt_ref, dgatt_ref, dcnv_ref, dsgu_ref, dgain_ref):
        mix = mix_ref[...]
        r = lax.rsqrt(jnp.mean(mix * mix, axis=-1, keepdims=True) + EPS)
        n = mix * r
        dout = dxo_ref[...]
        dn = dout * gain_ref[...]
        dmix = (r * (dn - n * jnp.mean(dn * n, axis=-1, keepdims=True))).astype(BF16)
        dmix_ref[...] = dmix

        @pl.when(pl.program_id(0) == 0)
        def _():
            dgain_ref[...] = jnp.zeros_like(dgain_ref)

        dgain_ref[...] += jnp.sum(dout * n, axis=0, keepdims=True)
        dcat = _dot(dmix, w_ref[...])
        g0, dg0 = _silu_and_grad(g0_ref[...])
        g1, dg1 = _silu_and_grad(g1_ref[...])
        gate = jnp.concatenate([g0, g1], axis=-1)
        dgate = jnp.concatenate([dg0, dg1], axis=-1)
        dca = dcat[:, 0:ATT_W]
        datt_ref[...] = dca * gate
        dgatt_ref[...] = dca * att_ref[...] * dgate
        dcnv_ref[...] = dcat[:, ATT_W:ATT_W + CONV_W]
        dsgu_ref[...] = dcat[:, ATT_W + CONV_W:]

    row = lambda w_: pl.BlockSpec((T_ROW, w_), lambda i: (i, 0))
    gate_blk = lambda c: pl.BlockSpec((T_ROW, 256), lambda i: (i, c))
    vec = pl.BlockSpec((1, D_MODEL), lambda i: (0, 0))
    return _pcall(
        body, name=name, grid=(seq // T_ROW,),
        in_specs=[row(D_MODEL), row(D_MODEL), vec, pl.BlockSpec((D_MODEL, D_MODEL), lambda i: (0, 0)), row(ATT_W),
                  gate_blk(COL_GATT), gate_blk(COL_GATT + 1)],
        out_specs=[row(D_MODEL), row(ATT_W), row(ATT_W), row(CONV_W), row(SG_W), vec],
        out_shape=[jax.ShapeDtypeStruct((seq, D_MODEL), BF16), jax.ShapeDtypeStruct((seq, ATT_W), F32),
                   jax.ShapeDtypeStruct((seq, ATT_W), F32), jax.ShapeDtypeStruct((seq, CONV_W), F32),
                   jax.ShapeDtypeStruct((seq, SG_W), F32), jax.ShapeDtypeStruct((1, D_MODEL), F32)],
        sem=("arbitrary",), vmem_mb=VMEM_MB)(dxo, mix, gain, w_t, att, proj, proj)


def _loss_head(y, target, name):
    seq = y.shape[0]
    t = min(T_GROUP, seq)

    def body(y_ref, t_ref, sse_ref, dy_ref):
        err = y_ref[...] - t_ref[...]
        dy_ref[...] = err * (1.0 / D_MODEL)

        @pl.when(pl.program_id(0) == 0)
        def _():
            sse_ref[...] = jnp.zeros_like(sse_ref)

        part = jnp.sum(jnp.sum(err * err, axis=0, keepdims=True), axis=-1, keepdims=True)
        sse_ref[...] += jnp.broadcast_to(part, (1, LANES))

    row = pl.BlockSpec((t, D_MODEL), lambda i: (i, 0))
    return _pcall(body, name=name, grid=(seq // t,), in_specs=[row, row],
                  out_specs=[pl.BlockSpec((1, LANES), lambda i: (0, 0)), row],
                  out_shape=[jax.ShapeDtypeStruct((1, LANES), F32), jax.ShapeDtypeStruct((seq, D_MODEL), F32)],
                  sem=("arbitrary",))(y, target)


def _to_heads(a, heads):
    return a.reshape(a.shape[0], heads, HEAD_DIM).transpose(1, 0, 2)


def _from_heads(a):
    return a.transpose(1, 0, 2).reshape(a.shape[1], a.shape[0] * HEAD_DIM)


def _row_blocks(a, bk):
    return a.reshape(a.shape[0], a.shape[1] // bk, bk, HEAD_DIM)


def _col_blocks(a, bk):
    return _row_blocks(a, bk).transpose(0, 1, 3, 2)


def _from_col_blocks(a):
    return a.transpose(0, 1, 3, 2).reshape(a.shape[0], a.shape[1] * a.shape[3], HEAD_DIM)


def _flat_rows(a, rows):
    flat = a.reshape(-1)
    return jnp.pad(flat, (0, rows * LANES - flat.shape[0])).reshape(rows, LANES)


SHARD_ROWS = {"w_in": 2 * D_MODEL * (D_IN // N_DEV) // LANES, "w_out": 2 * (D_MODEL // N_DEV) * D_MODEL // LANES,
              "conv_dw": 16}
REPL_SHAPES = [("pre_norm", (2, D_MODEL)), ("post_norm", (2, D_MODEL)), ("q_norm", (2, HEAD_DIM)),
               ("k_norm", (2, HEAD_DIM)), ("conv_dw_b", (2, CONV_W)), ("conv_ln_g", (2, CONV_W)),
               ("conv_ln_b", (2, CONV_W)), ("sg_ln_g", (2, SG_W)), ("sg_ln_b", (2, SG_W)),
               ("sg_w", (2, SG_HEADS, SG_CHUNK, SG_CHUNK)), ("sg_b", (2, SG_HEADS, SG_CHUNK))]
REPL_ROWS = 1088
WEIGHT_ORDER = ["pre_norm", "post_norm", "w_in", "w_out", "q_norm", "k_norm", "conv_dw", "conv_dw_b", "conv_ln_g",
                "conv_ln_b", "sg_ln_g", "sg_ln_b", "sg_w", "sg_b"]


def _pack_shard(parts):
    return jnp.concatenate([_flat_rows(parts[k], SHARD_ROWS[k]) for k in ("w_in", "w_out", "conv_dw")], axis=0)


def _unpack_shard(flat, shapes):
    out, at = {}, 0
    for k in ("w_in", "w_out", "conv_dw"):
        size = math.prod(shapes[k])
        out[k] = flat[at:at + SHARD_ROWS[k]].reshape(-1)[:size].reshape(shapes[k])
        at += SHARD_ROWS[k]
    return out


REPL_USED = sum(math.prod(shape) for _, shape in REPL_SHAPES)


def _pack_repl(parts, extra=None):
    tail = [] if extra is None else [extra.reshape(1)]
    flat = jnp.concatenate([parts[k].reshape(-1) for k, _ in REPL_SHAPES] + tail)
    return jnp.pad(flat, (0, REPL_ROWS * LANES - flat.shape[0])).reshape(REPL_ROWS, LANES)


def _unpack_repl(flat):
    out, at, flat = {}, 0, flat.reshape(-1)
    for k, shape in REPL_SHAPES:
        size = math.prod(shape)
        out[k] = flat[at:at + size].reshape(shape)
        at += size
    return out


def kernel(x, pre_norm, post_norm, w_in, w_out, q_norm, k_norm, conv_dw, conv_dw_b, conv_ln_g, conv_ln_b, sg_ln_g, sg_ln_b, sg_w, sg_b, loss_target, m_pre_norm, m_post_norm, m_w_in, m_w_out, m_q_norm, m_k_norm, m_conv_dw, m_conv_dw_b, m_conv_ln_g, m_conv_ln_b, m_sg_ln_g, m_sg_ln_b, m_sg_w, m_sg_b, v_pre_norm, v_post_norm, v_w_in, v_w_out, v_q_norm, v_k_norm, v_conv_dw, v_conv_dw_b, v_conv_ln_g, v_conv_ln_b, v_sg_ln_g, v_sg_ln_b, v_sg_w, v_sg_b):
    weights = dict(pre_norm=pre_norm, post_norm=post_norm, w_in=w_in, w_out=w_out, q_norm=q_norm, k_norm=k_norm,
                   conv_dw=conv_dw, conv_dw_b=conv_dw_b, conv_ln_g=conv_ln_g, conv_ln_b=conv_ln_b, sg_ln_g=sg_ln_g,
                   sg_ln_b=sg_ln_b, sg_w=sg_w, sg_b=sg_b)
    mom_m = dict(pre_norm=m_pre_norm, post_norm=m_post_norm, w_in=m_w_in, w_out=m_w_out, q_norm=m_q_norm,
                 k_norm=m_k_norm, conv_dw=m_conv_dw, conv_dw_b=m_conv_dw_b, conv_ln_g=m_conv_ln_g,
                 conv_ln_b=m_conv_ln_b, sg_ln_g=m_sg_ln_g, sg_ln_b=m_sg_ln_b, sg_w=m_sg_w, sg_b=m_sg_b)
    mom_v = dict(pre_norm=v_pre_norm, post_norm=v_post_norm, w_in=v_w_in, w_out=v_w_out, q_norm=v_q_norm,
                 k_norm=v_k_norm, conv_dw=v_conv_dw, conv_dw_b=v_conv_dw_b, conv_ln_g=v_conv_ln_g,
                 conv_ln_b=v_conv_ln_b, sg_ln_g=v_sg_ln_g, sg_ln_b=v_sg_ln_b, sg_w=v_sg_w, sg_b=v_sg_b)
    depth = pre_norm.shape[0]
    seq = x.shape[1]
    bk = min(BK, seq)
    x0 = x.reshape(seq, D_MODEL)
    target = loss_target.reshape(seq, D_MODEL)

    w_in_all, w_out_all, dw_all = _exchange(
        [], [w_in.astype(BF16), w_out.astype(BF16), jnp.pad(conv_dw, ((0, 0), (0, 1), (0, 0)))], "gather_weights")
    w_in_full = w_in_all.transpose(1, 2, 0, 3).reshape(depth, D_MODEL, D_IN)
    w_out_full = w_out_all.transpose(1, 0, 2, 3).reshape(depth, D_MODEL, D_MODEL)
    dw_full = dw_all[:, :, :CONV_K, :].transpose(1, 2, 0, 3).reshape(depth, CONV_K, CONV_W)

    cos, sin, perm = _rope_tables(seq)
    q_post = jnp.full((ATT_HEADS, 1, HEAD_DIM), Q_SCALE, F32)
    k_post = jnp.ones((KV_HEADS, 1, HEAD_DIM), F32)
    lane = jnp.arange(SG_W)
    fold = (lane[:, None] // HEAD_DIM == jnp.arange(SG_CHUNK)[None, :]).astype(BF16)

    def layer_consts(l):
        return dict(
            q_gain=jnp.broadcast_to(q_norm[l], (ATT_HEADS, 1, HEAD_DIM)),
            k_gain=jnp.broadcast_to(k_norm[l], (KV_HEADS, 1, HEAD_DIM)), sg_w_b=sg_w[l].astype(BF16), sg_wt_b=sg_w[l].transpose(0, 2, 1).astype(BF16),
            sg_bias=jnp.repeat(sg_b[l].T, HEAD_DIM, axis=1),
            vec=lambda a: a[l].reshape(1, -1))

    saved = []
    xc = x0
    for l in range(depth):
        c = layer_consts(l)
        proj, hb = _proj_fwd(xc, c["vec"](pre_norm), w_in_full[l], f"proj_fwd_{l}")
        raw_q = _to_heads(proj[:, :ATT_W], ATT_HEADS)
        raw_k = _to_heads(proj[:, ATT_W:ATT_W + KV_W], KV_HEADS)
        qs = _qk_prep_fwd(raw_q, c["q_gain"], q_post, cos, sin, perm, f"q_prep_fwd_{l}")
        kr = _qk_prep_fwd(raw_k, c["k_gain"], k_post, cos, sin, perm, f"k_prep_fwd_{l}")
        qs_t = qs.transpose(0, 2, 1)
        vh = _to_heads(proj[:, ATT_W + KV_W:ATT_W + 2 * KV_W], KV_HEADS).astype(BF16)
        k_rows, k_cols, v_cols = _row_blocks(kr, bk), _col_blocks(kr, bk), _col_blocks(vh, bk)
        v_ext = jnp.concatenate([v_cols, jnp.ones_like(v_cols[:, :, :1]), jnp.zeros_like(v_cols[:, :, :7])], axis=2)
        o_t, lse = _attn_fwd(qs_t, k_rows, v_ext, f"attn_fwd_{l}")
        att = o_t.reshape(ATT_W, seq).T
        o = o_t.transpose(0, 2, 1)
        lse = jnp.broadcast_to(lse.reshape(ATT_HEADS, seq, 1), (ATT_HEADS, seq, LANES))
        cnv = _conv_fwd(proj, dw_full[l], c["vec"](conv_dw_b), c["vec"](conv_ln_g), c["vec"](conv_ln_b),
                        f"conv_fwd_{l}")
        sgu = _sg_fwd(proj, c["vec"](sg_ln_g), c["vec"](sg_ln_b), c["sg_w_b"], c["sg_bias"], f"sg_fwd_{l}")
        x_new, mix, cat_b = _out_fwd(att, proj, cnv, sgu, xc, w_out_full[l], c["vec"](post_norm), f"out_fwd_{l}")
        saved.append(dict(x=xc, proj=proj, hb=hb, raw_q=raw_q, raw_k=raw_k, qs=qs, qs_t=qs_t, k_rows=k_rows,
                          k_cols=k_cols, v_cols=v_cols, o=o, lse=lse, att=att, mix=mix, cat_b=cat_b))
        xc = x_new

    sse, dx = _loss_head(xc, target, "loss_head")

    grads = {k: [None] * depth for k in WEIGHT_ORDER}
    for l in reversed(range(depth)):
        c, s = layer_consts(l), saved[l]
        dmix_b, datt, dgatt, dcnv, dsgu, g_post = _out_bwd(
            dx, s["mix"], c["vec"](post_norm), w_out_full[l].T, s["att"], s["proj"], f"out_bwd_{l}")
        grads["post_norm"][l] = g_post.reshape(-1)
        grads["w_out"][l] = _matmul_acc(s["cat_b"].T, dmix_b, D_MODEL, f"grad_w_out_{l}")
        do = _to_heads(datt, ATT_HEADS)
        dqs, dkt, dvt = _attn_bwd(s["qs"], s["qs_t"], do, do.astype(BF16).transpose(0, 2, 1),
                                  s["o"], s["lse"], s["k_rows"], s["k_cols"], s["v_cols"], f"attn_bwd_{l}")
        d_raw_q, g_qgain = _qk_prep_bwd(s["raw_q"], dqs, c["q_gain"], q_post, cos, sin, perm, f"q_prep_bwd_{l}")
        d_raw_k, g_kgain = _qk_prep_bwd(s["raw_k"], _from_col_blocks(dkt), c["k_gain"], k_post, cos, sin, perm,
                                        f"k_prep_bwd_{l}")
        grads["q_norm"][l] = jnp.sum(g_qgain[:, 0], axis=0)
        grads["k_norm"][l] = jnp.sum(g_kgain[:, 0], axis=0)
        dy_conv, dg_conv, g_dw, g_dwb, g_clg, g_clb = _conv_bwd_a(
            s["proj"], dcnv, dw_full[l], c["vec"](conv_dw_b), c["vec"](conv_ln_g), c["vec"](conv_ln_b),
            f"conv_bwd_a_{l}")
        da, db = _conv_bwd_b(s["proj"], dy_conv, dw_full[l], f"conv_bwd_b_{l}")
        grads["conv_dw"][l], grads["conv_dw_b"][l] = g_dw, g_dwb.reshape(-1)
        grads["conv_ln_g"][l], grads["conv_ln_b"][l] = g_clg.reshape(-1), g_clb.reshape(-1)
        du, dv_sg, dg_sg, g_slg, g_slb, g_sw, g_sb = _sg_bwd(
            s["proj"], dsgu, c["vec"](sg_ln_g), c["vec"](sg_ln_b), c["sg_w_b"], c["sg_wt_b"], c["sg_bias"], fold,
            f"sg_bwd_{l}")
        grads["sg_ln_g"][l], grads["sg_ln_b"][l] = g_slg.reshape(-1), g_slb.reshape(-1)
        grads["sg_w"][l], grads["sg_b"][l] = g_sw, g_sb[:, :SG_HEADS].T
        dproj = jnp.concatenate(
            [_from_heads(d_raw_q), _from_heads(d_raw_k), _from_heads(_from_col_blocks(dvt)),
             dgatt, da, db, dg_conv, du, dv_sg, dg_sg], axis=-1).astype(BF16)
        grads["w_in"][l] = _matmul_acc(s["hb"].T, dproj, D_IN // 2, f"grad_w_in_{l}")
        dx, g_pre = _proj_bwd(dproj, w_in_full[l].T, s["x"], c["vec"](pre_norm), dx, f"proj_bwd_{l}")
        grads["pre_norm"][l] = g_pre.reshape(-1)
    grad_x = dx.reshape(x.shape)
    grads = {k: jnp.stack(v) for k, v in grads.items()}

    shard_blocks = dict(
        w_in=grads["w_in"].reshape(depth, D_MODEL, N_DEV, D_IN // N_DEV).transpose(2, 0, 1, 3),
        w_out=grads["w_out"].reshape(depth, N_DEV, D_MODEL // N_DEV, D_MODEL).transpose(1, 0, 2, 3),
        conv_dw=grads["conv_dw"].reshape(depth, CONV_K, N_DEV, CONV_W // N_DEV).transpose(2, 0, 1, 3))
    scatter_src = jnp.stack([_pack_shard({k: a[d] for k, a in shard_blocks.items()}) for d in range(N_DEV)])
    shard_slots, repl_slots = _exchange([scatter_src], [_pack_repl(grads, sse[0, 0])], "exchange_grads")

    shard_shapes = {k: weights[k].shape for k in SHARD_ROWS}
    gs, ds_, ms, vs = _sum_adamw(shard_slots, _pack_shard(weights), _pack_shard(mom_m), _pack_shard(mom_v),
                                 "adamw_sharded")
    gr, dr, mr, vr = _sum_adamw(repl_slots, _pack_repl(weights), _pack_repl(mom_m), _pack_repl(mom_v),
                                "adamw_replicated")
    loss = gr.reshape(-1)[REPL_USED] * (0.5 / D_MODEL)
    results = []
    for shard_flat, repl_flat in ((gs, gr), (ds_, dr), (ms, mr), (vs, vr)):
        parts = {**_unpack_shard(shard_flat, shard_shapes), **_unpack_repl(repl_flat)}
        results.append([parts[k] for k in WEIGHT_ORDER])
    return (loss, grad_x, *results[0], *results[1], *results[2], *results[3])
```

```python
import math

import jax
import jax.numpy as jnp
from jax import lax
from jax.experimental import pallas as pl
from jax.experimental.pallas import tpu as pltpu

F32, BF16 = jnp.float32, jnp.bfloat16

N_DEV = 8
MESH_AXES = ("x", "y", "c")
EPS = 1e-6
D_MODEL = 1024
HEAD_DIM = 64
ATT_HEADS, KV_HEADS = 8, 2
N_QK = ATT_HEADS + KV_HEADS
GROUP = ATT_HEADS // KV_HEADS
ATT_W, KV_W, CONV_W, SG_W = 512, 128, 256, 256
CONV_K, CONV_PAD, HALO = 31, 15, 16
SG_HEADS, SG_CHUNK = 4, 128
D_IN = 2816
GRID_W = 64
ROPE_THETA = 10000.0
LOG2E, LN2 = math.log2(math.e), math.log(2.0)
Q_SCALE = HEAD_DIM ** -0.5 * LOG2E
LANES = 128

COL_GATT, COL_A, COL_B, COL_GCONV, COL_U, COL_VSG, COL_GSG = 3, 5, 6, 7, 8, 9, 10

ADAM_LR, ADAM_B1, ADAM_B2, ADAM_EPS, ADAM_WD, ADAM_STEP = 0.001, 0.9, 0.999, 1e-08, 0.01, 10

T_ROW = 256
T_PREP = 1024
T_GROUP = 512
BQ, BK = 512, 512
HEADS_PER_STEP = 2
FWD_UNROLL, BWD_UNROLL = 8, 4
VMEM_MB = 56
ATTN_BWD_VMEM_MB = 58


def _pcall(body, *, name, grid, in_specs, out_specs, out_shape, scratch=(), sem=None, vmem_mb=None):
    params = {}
    if sem is not None:
        params["dimension_semantics"] = sem
    if vmem_mb is not None:
        params["vmem_limit_bytes"] = vmem_mb << 20
    return pl.pallas_call(body, name=name, grid=grid, in_specs=in_specs, out_specs=out_specs, out_shape=out_shape,
                          scratch_shapes=list(scratch), compiler_params=pltpu.CompilerParams(**params))


def _dot(a, b):
    return jnp.dot(a, b, preferred_element_type=F32)


def _sigmoid(x):
    return 1.0 / (1.0 + jnp.exp(-x))


def _silu_and_grad(x):
    s = _sigmoid(x)
    return x * s, s * (1.0 + x * (1.0 - s))


def _gelu_and_grad(x):
    cdf = 0.5 * (1.0 + lax.erf(x * (1.0 / math.sqrt(2.0))))
    pdf = jnp.exp(-0.5 * x * x) * (1.0 / math.sqrt(2.0 * math.pi))
    return x * cdf, cdf + x * pdf


def _split_dot(y, mat):
    hi = y.astype(BF16)
    lo = (y - hi.astype(F32)).astype(BF16)
    return _dot(hi, mat) + _dot(lo, mat)


def _row_tile(rows, cap):
    best = 8
    for t in range(8, min(rows, cap) + 1, 8):
        if rows % t == 0:
            best = t
    return best


def _exchange(scatter, gather, name):
    n_s = len(scatter)
    arrs = list(scatter) + list(gather)
    n = len(arrs)
    flips = [(fx, fy, fc) for fx in (0, 1) for fy in (0, 1) for fc in (0, 1)][1:]
    n_peer = len(flips)

    def body(*refs):
        ins, outs = refs[:n], refs[n:2 * n]
        send_sems, recv_sems, local_sems = refs[2 * n:]
        pos = tuple(lax.axis_index(a) for a in MESH_AXES)

        def peer(flip):
            return tuple((1 - p) if f else p for p, f in zip(pos, flip))

        def slot(p):
            return 4 * p[0] + 2 * p[1] + p[2]

        def src(a, p):
            return ins[a].at[slot(p)] if a < n_s else ins[a]

        def remote(a, k, src_ref, dst_slot, to):
            return pltpu.make_async_remote_copy(
                src_ref=src_ref, dst_ref=outs[a].at[dst_slot], send_sem=send_sems.at[a * n_peer + k],
                recv_sem=recv_sems.at[a * n_peer + k], device_id=to, device_id_type=pl.DeviceIdType.MESH)

        local = [pltpu.make_async_copy(src(a, pos), outs[a].at[slot(pos)], local_sems.at[a]) for a in range(n)]
        for cp in local:
            cp.start()
        sends = [remote(a, k, src(a, peer(f)), slot(pos), peer(f)) for a in range(n) for k, f in enumerate(flips)]
        for cp in sends:
            cp.start()
        for a in range(n):
            for k, f in enumerate(flips):
                remote(a, k, src(a, peer(f)), slot(peer(f)), peer(f)).wait_recv()
        for cp in sends:
            cp.wait_send()
        for cp in local:
            cp.wait()

    out_shape = [jax.ShapeDtypeStruct((N_DEV,) + (a.shape[1:] if i < n_s else a.shape), a.dtype)
                 for i, a in enumerate(arrs)]
    any_spec = pl.BlockSpec(memory_space=pl.ANY)
    return pl.pallas_call(
        body, name=name, out_shape=out_shape, in_specs=[any_spec] * n, out_specs=[any_spec] * n,
        scratch_shapes=[pltpu.SemaphoreType.DMA((n * n_peer,)), pltpu.SemaphoreType.DMA((n * n_peer,)),
                        pltpu.SemaphoreType.DMA((n,))],
    )(*arrs)


def _sum_adamw(slots, w, m, v, name):
    rows = w.shape[0]
    tr = _row_tile(rows, 1024)
    c1 = 1.0 - ADAM_B1 ** ADAM_STEP
    c2 = 1.0 - ADAM_B2 ** ADAM_STEP

    def body(s_ref, w_ref, m_ref, v_ref, g_out, d_out, m_out, v_out):
        g = s_ref[0].astype(F32)
        for d in range(1, N_DEV):
            g = g + s_ref[d].astype(F32)
        m_new = ADAM_B1 * m_ref[...] + (1.0 - ADAM_B1) * g
        v_new = ADAM_B2 * v_ref[...] + (1.0 - ADAM_B2) * (g * g)
        m_hat = m_new / c1
        v_hat = v_new / c2
        g_out[...] = g
        d_out[...] = -ADAM_LR * (m_hat / (jnp.sqrt(v_hat) + ADAM_EPS) + ADAM_WD * w_ref[...])
        m_out[...] = m_new
        v_out[...] = v_new

    flat = pl.BlockSpec((tr, LANES), lambda i: (i, 0))
    return _pcall(
        body, name=name, grid=(rows // tr,),
        in_specs=[pl.BlockSpec((N_DEV, tr, LANES), lambda i: (0, i, 0)), flat, flat, flat],
        out_specs=[flat] * 4, out_shape=[jax.ShapeDtypeStruct((rows, LANES), F32)] * 4,
        sem=("parallel",), vmem_mb=VMEM_MB)(slots, w, m, v)


def _proj_fwd(x, gain, w, name):
    seq = x.shape[0]

    def body(x_ref, g_ref, w_ref, proj_ref, hb_ref):
        xf = x_ref[...]
        r = lax.rsqrt(jnp.mean(xf * xf, axis=-1, keepdims=True) + EPS)
        h = (xf * r * g_ref[...]).astype(BF16)
        hb_ref[...] = h
        proj_ref[...] = _dot(h, w_ref[...])

    return _pcall(
        body, name=name, grid=(seq // T_ROW,),
        in_specs=[pl.BlockSpec((T_ROW, D_MODEL), lambda i: (i, 0)), pl.BlockSpec((1, D_MODEL), lambda i: (0, 0)),
                  pl.BlockSpec((D_MODEL, D_IN), lambda i: (0, 0))],
        out_specs=[pl.BlockSpec((T_ROW, D_IN), lambda i: (i, 0)), pl.BlockSpec((T_ROW, D_MODEL), lambda i: (i, 0))],
        out_shape=[jax.ShapeDtypeStruct((seq, D_IN), F32), jax.ShapeDtypeStruct((seq, D_MODEL), BF16)],
        sem=("parallel",), vmem_mb=VMEM_MB)(x, gain, w)


def _proj_bwd(dproj, w_t, x, gain, dxo, name):
    seq = x.shape[0]

    def body(dp_ref, w_ref, x_ref, g_ref, dxo_ref, dx_ref, dg_ref):
        dh = _dot(dp_ref[...], w_ref[...])
        xf = x_ref[...]
        r = lax.rsqrt(jnp.mean(xf * xf, axis=-1, keepdims=True) + EPS)
        n = xf * r
        dn = dh * g_ref[...]
        dx_ref[...] = dxo_ref[...] + r * (dn - n * jnp.mean(dn * n, axis=-1, keepdims=True))

        @pl.when(pl.program_id(0) == 0)
        def _():
            dg_ref[...] = jnp.zeros_like(dg_ref)

        dg_ref[...] += jnp.sum(dh * n, axis=0, keepdims=True)

    row = pl.BlockSpec((T_ROW, D_MODEL), lambda i: (i, 0))
    vec = pl.BlockSpec((1, D_MODEL), lambda i: (0, 0))
    return _pcall(
        body, name=name, grid=(seq // T_ROW,),
        in_specs=[pl.BlockSpec((T_ROW, D_IN), lambda i: (i, 0)), pl.BlockSpec((D_IN, D_MODEL), lambda i: (0, 0)),
                  row, vec, row],
        out_specs=[row, vec],
        out_shape=[jax.ShapeDtypeStruct((seq, D_MODEL), F32), jax.ShapeDtypeStruct((1, D_MODEL), F32)],
        sem=("arbitrary",), vmem_mb=VMEM_MB)(dproj, w_t, x, gain, dxo)


def _matmul_acc(a, b, tn, name):
    seq, m = a.shape
    n = b.shape[1]
    ts = min(512, seq)

    def body(a_ref, b_ref, o_ref):
        @pl.when(pl.program_id(1) == 0)
        def _():
            o_ref[...] = jnp.zeros_like(o_ref)

        o_ref[...] += lax.dot_general(a_ref[...], b_ref[...], (((0,), (0,)), ((), ())), preferred_element_type=F32)

    return _pcall(
        body, name=name, grid=(n // tn, seq // ts),
        in_specs=[pl.BlockSpec((ts, m), lambda j, k: (k, 0)), pl.BlockSpec((ts, tn), lambda j, k: (k, j))],
        out_specs=pl.BlockSpec((m, tn), lambda j, k: (0, j)), out_shape=jax.ShapeDtypeStruct((m, n), F32),
        sem=("parallel", "arbitrary"), vmem_mb=VMEM_MB)(a, b)


def _rope_tables(seq):
    t = jnp.arange(seq, dtype=jnp.int32)
    row = (t // GRID_W).astype(F32)
    col = (t % GRID_W).astype(F32)
    half = HEAD_DIM // 4
    inv_freq = ROPE_THETA ** (-jnp.arange(half, dtype=F32) / half)
    ang_r = row[:, None] * inv_freq[None, :]
    ang_c = col[:, None] * inv_freq[None, :]
    cos = jnp.concatenate([jnp.cos(ang_r), jnp.cos(ang_r), jnp.cos(ang_c), jnp.cos(ang_c)], axis=-1)
    sin = jnp.concatenate([-jnp.sin(ang_r), jnp.sin(ang_r), -jnp.sin(ang_c), jnp.sin(ang_c)], axis=-1)
    j = jnp.arange(HEAD_DIM)
    partner = jnp.where((j % (2 * half)) < half, j + half, j - half)
    perm = (j[:, None] == partner[None, :]).astype(BF16)
    return cos, sin, perm


def _qk_specs(seq):
    t = min(T_PREP, seq)
    blk = pl.BlockSpec((None, t, HEAD_DIM), lambda h, i: (h, i, 0))
    vec = pl.BlockSpec((None, 1, HEAD_DIM), lambda h, i: (h, 0, 0))
    tab = pl.BlockSpec((t, HEAD_DIM), lambda h, i: (i, 0))
    mat = pl.BlockSpec((HEAD_DIM, HEAD_DIM), lambda h, i: (0, 0))
    return t, blk, vec, tab, mat


def _qk_prep_fwd(raw, gains, post, cos, sin, perm, name):
    seq = raw.shape[1]
    t, blk, vec, tab, mat = _qk_specs(seq)

    def body(x_ref, g_ref, p_ref, c_ref, s_ref, m_ref, o_ref):
        xf = x_ref[...]
        r = lax.rsqrt(jnp.mean(xf * xf, axis=-1, keepdims=True) + EPS)
        y = xf * r * g_ref[...]
        z = y * c_ref[...] + _split_dot(y, m_ref[...]) * s_ref[...]
        o_ref[...] = (z * p_ref[...]).astype(BF16)

    return _pcall(body, name=name, grid=(raw.shape[0], seq // t), in_specs=[blk, vec, vec, tab, tab, mat], out_specs=blk,
                  out_shape=jax.ShapeDtypeStruct(raw.shape, BF16), sem=("parallel", "parallel"))(
                      raw, gains, post, cos, sin, perm)


def _qk_prep_bwd(raw, dout, gains, post, cos, sin, perm, name):
    seq = raw.shape[1]
    t, blk, vec, tab, mat = _qk_specs(seq)

    def body(x_ref, d_ref, g_ref, p_ref, c_ref, s_ref, m_ref, dx_ref, dg_ref):
        xf = x_ref[...]
        r = lax.rsqrt(jnp.mean(xf * xf, axis=-1, keepdims=True) + EPS)
        n = xf * r
        dz = d_ref[...] * p_ref[...]
        dy = dz * c_ref[...] + _split_dot(dz * s_ref[...], m_ref[...])
        dn = dy * g_ref[...]
        dx_ref[...] = r * (dn - n * jnp.mean(dn * n, axis=-1, keepdims=True))

        @pl.when(pl.program_id(1) == 0)
        def _():
            dg_ref[...] = jnp.zeros_like(dg_ref)

        dg_ref[...] += jnp.sum(dy * n, axis=0, keepdims=True)

    return _pcall(body, name=name, grid=(raw.shape[0], seq // t), in_specs=[blk, blk, vec, vec, tab, tab, mat],
                  out_specs=[blk, vec],
                  out_shape=[jax.ShapeDtypeStruct(raw.shape, F32),
                             jax.ShapeDtypeStruct((raw.shape[0], 1, HEAD_DIM), F32)],
                  sem=("parallel", "arbitrary"))(raw, dout, gains, post, cos, sin, perm)


V_ROWS = HEAD_DIM + 8


def _unroll(nk, cap):
    u = 1
    while u * 2 <= cap and nk % (u * 2) == 0:
        u *= 2
    return u


def _attn_fwd(qs_t, k, v_t, name):
    seq = qs_t.shape[2]
    nk, bk = k.shape[1], k.shape[2]
    bq = min(BQ, seq)
    unroll = _unroll(nk, FWD_UNROLL)
    heads = range(HEADS_PER_STEP)

    def body(qt_ref, k_ref, vt_ref, ot_ref, lse_ref, s_scr):
        q_t = [qt_ref[h] for h in heads]

        def scores(j, slot):
            kj = k_ref[j]
            top = []
            for h in heads:
                s = _dot(kj, q_t[h])
                s_scr[slot, h] = s
                top.append(jnp.max(s, axis=0, keepdims=True))
            return tuple(top)

        def accumulate(j, slot, state, top):
            vtj = vt_ref[j]
            out = []
            for h in heads:
                m, acc = state[h]
                m_new = jnp.maximum(m, top[h])
                p = jnp.exp2(s_scr[slot, h] - m_new).astype(BF16)
                out.append((m_new, jnp.exp2(m - m_new) * acc + _dot(vtj, p)))
            return tuple(out)

        def step(t, carry):
            state, top = carry
            for u in range(unroll):
                nxt = unroll * t + u + 1
                top_next = scores(jnp.minimum(nxt, nk - 1) if u == unroll - 1 else nxt, (u + 1) % 2)
                state = accumulate(unroll * t + u, u % 2, state, top)
                top = top_next
            return state, top

        init = tuple((jnp.full((1, bq), -jnp.inf, F32), jnp.zeros((V_ROWS, bq), F32)) for _ in heads)
        state, _ = lax.fori_loop(0, nk // unroll, step, (init, scores(0, 0)))
        for h in heads:
            m, acc = state[h]
            l = acc[HEAD_DIM:HEAD_DIM + 1, :]
            ot_ref[h] = acc[:HEAD_DIM, :] / l
            lse_ref[h] = m + jnp.log2(l)

    kv_of = lambda g: g * HEADS_PER_STEP // GROUP
    return _pcall(
        body, name=name, grid=(ATT_HEADS // HEADS_PER_STEP, seq // bq),
        in_specs=[pl.BlockSpec((HEADS_PER_STEP, HEAD_DIM, bq), lambda g, i: (g, 0, i)),
                  pl.BlockSpec((None, nk, bk, HEAD_DIM), lambda g, i: (kv_of(g), 0, 0, 0)),
                  pl.BlockSpec((None, nk, V_ROWS, bk), lambda g, i: (kv_of(g), 0, 0, 0))],
        out_specs=[pl.BlockSpec((HEADS_PER_STEP, HEAD_DIM, bq), lambda g, i: (g, 0, i)),
                   pl.BlockSpec((HEADS_PER_STEP, 1, bq), lambda g, i: (g, 0, i))],
        out_shape=[jax.ShapeDtypeStruct((ATT_HEADS, HEAD_DIM, seq), F32),
                   jax.ShapeDtypeStruct((ATT_HEADS, 1, seq), F32)],
        scratch=[pltpu.VMEM((2, HEADS_PER_STEP, bk, bq), F32)],
        sem=("parallel", "parallel"), vmem_mb=VMEM_MB)(qs_t, k, v_t)


def _attn_bwd(qs, qs_t, do, do_t, o, lse, k, k_t, v_t, name):
    seq = qs.shape[1]
    nk, bk = k.shape[1], k.shape[2]
    bq = min(BQ, seq)
    nq = seq // bq

    unroll = _unroll(nk, BWD_UNROLL)
    heads = range(HEADS_PER_STEP)
    pairs = GROUP // HEADS_PER_STEP

    def body(q_ref, qt_ref, do_ref, dot_ref, o_ref, lse_ref, k_ref, kt_ref, vt_ref, dq_ref, dkt_ref, dvt_ref,
             s_scr, dp_scr):
        @pl.when((pl.program_id(1) == 0) & (pl.program_id(2) == 0))
        def _():
            dkt_ref[...] = jnp.zeros_like(dkt_ref)
            dvt_ref[...] = jnp.zeros_like(dvt_ref)

        q, q_t, do_t_b = [q_ref[h] for h in heads], [qt_ref[h] for h in heads], [dot_ref[h] for h in heads]
        do_l = [do_ref[h] * LN2 for h in heads]
        do_b = [d.astype(BF16) for d in do_l]
        delta = [jnp.sum(do_l[h] * o_ref[h], axis=-1, keepdims=True) for h in heads]
        lse_col = [jnp.max(lse_ref[h], axis=-1, keepdims=True) for h in heads]

        def products(j, slot):
            ktj, vtj = kt_ref[j], vt_ref[j]
            for h in heads:
                s_scr[slot, h] = _dot(q[h], ktj)
                dp_scr[slot, h] = _dot(do_b[h], vtj)

        def gradients(j, slot, dq):
            kj = k_ref[j]
            dvt = jnp.zeros((HEAD_DIM, bk), F32)
            dkt = jnp.zeros((HEAD_DIM, bk), F32)
            new = []
            for h in heads:
                p = jnp.exp2(s_scr[slot, h] - lse_col[h])
                ds = (p * (dp_scr[slot, h] - delta[h])).astype(BF16)
                dvt = dvt + _dot(do_t_b[h], p.astype(BF16))
                dkt = dkt + _dot(q_t[h], ds)
                new.append(dq[h] + _dot(ds, kj))
            dvt_ref[j] += dvt
            dkt_ref[j] += dkt
            return tuple(new)

        def step(t, dq):
            for u in range(unroll):
                nxt = unroll * t + u + 1
                products(jnp.minimum(nxt, nk - 1) if u == unroll - 1 else nxt, (u + 1) % 2)
                dq = gradients(unroll * t + u, u % 2, dq)
            return dq

        products(0, 0)
        res = lax.fori_loop(0, nk // unroll, step, tuple(jnp.zeros((bq, HEAD_DIM), F32) for _ in heads))
        for h in heads:
            dq_ref[h] = res[h]

    first = lambda g, hh: g * pairs + hh
    row = pl.BlockSpec((HEADS_PER_STEP, bq, HEAD_DIM), lambda g, hh, i: (first(g, hh), i, 0))
    col = pl.BlockSpec((HEADS_PER_STEP, HEAD_DIM, bq), lambda g, hh, i: (first(g, hh), 0, i))
    kv_rows = pl.BlockSpec((None, nk, bk, HEAD_DIM), lambda g, hh, i: (g, 0, 0, 0))
    kv_cols = pl.BlockSpec((None, nk, HEAD_DIM, bk), lambda g, hh, i: (g, 0, 0, 0))
    return _pcall(
        body, name=name, grid=(KV_HEADS, pairs, nq),
        in_specs=[row, col, row, col, row,
                  pl.BlockSpec((HEADS_PER_STEP, bq, LANES), lambda g, hh, i: (first(g, hh), i, 0)),
                  kv_rows, kv_cols, kv_cols],
        out_specs=[row, kv_cols, kv_cols],
        out_shape=[jax.ShapeDtypeStruct((ATT_HEADS, seq, HEAD_DIM), F32),
                   jax.ShapeDtypeStruct((KV_HEADS, nk, HEAD_DIM, bk), F32),
                   jax.ShapeDtypeStruct((KV_HEADS, nk, HEAD_DIM, bk), F32)],
        scratch=[pltpu.VMEM((2, HEADS_PER_STEP, bq, bk), F32), pltpu.VMEM((2, HEADS_PER_STEP, bq, bk), F32)],
        sem=("parallel", "arbitrary", "arbitrary"), vmem_mb=ATTN_BWD_VMEM_MB)(
            qs, qs_t, do, do_t, o, lse, k, k_t, v_t)


def _halo_specs(t, col, n_tiles):
    per = t // HALO
    last = n_tiles * per - 1
    before = pl.BlockSpec((HALO, CONV_W), lambda i: (jnp.maximum(i * per - 1, 0), col))
    after = pl.BlockSpec((HALO, CONV_W), lambda i: (jnp.minimum((i + 1) * per, last), col))
    return before, after


def _glu(a, b):
    return a * _sigmoid(b)


def _conv_taps(ext_ref, w_ref, t, flip):
    acc = jnp.zeros((t, CONV_W), F32)
    for k in range(CONV_K):
        off = (HALO + CONV_PAD - k) if flip else (HALO - CONV_PAD + k)
        acc = acc + w_ref[k:k + 1, :] * ext_ref[pl.ds(off, t), :]
    return acc


def _fill_ext(ext_ref, before, tile, after, t, i, n_tiles):
    ext_ref[pl.ds(0, HALO), :] = jnp.where(i > 0, before, 0.0)
    ext_ref[pl.ds(HALO, t), :] = tile
    ext_ref[pl.ds(HALO + t, HALO), :] = jnp.where(i < n_tiles - 1, after, 0.0)


def _conv_fwd(proj, w, bias, ln_g, ln_b, name):
    seq = proj.shape[0]
    t = min(T_GROUP, seq)
    n_tiles = seq // t

    def body(a_ref, b_ref, ap_ref, bp_ref, an_ref, bn_ref, gate_ref, w_ref, bias_ref, g_ref, beta_ref, o_ref, y_ref,
             ext_ref):
        i = pl.program_id(0)
        _fill_ext(ext_ref, _glu(ap_ref[...], bp_ref[...]), _glu(a_ref[...], b_ref[...]),
                  _glu(an_ref[...], bn_ref[...]), t, i, n_tiles)
        y = _conv_taps(ext_ref, w_ref, t, False) + bias_ref[...]
        y_ref[...] = y
        mu = jnp.mean(y, axis=-1, keepdims=True)
        yc = y - mu
        rs = lax.rsqrt(jnp.mean(yc * yc, axis=-1, keepdims=True) + EPS)
        z = yc * rs * g_ref[...] + beta_ref[...]
        o_ref[...] = _silu_and_grad(z)[0] * _silu_and_grad(gate_ref[...])[0]

    tile = lambda c: pl.BlockSpec((t, CONV_W), lambda i: (i, c))
    ab, aa = _halo_specs(t, COL_A, n_tiles)
    bb, ba = _halo_specs(t, COL_B, n_tiles)
    vec = pl.BlockSpec((1, CONV_W), lambda i: (0, 0))
    return _pcall(
        body, name=name, grid=(n_tiles,),
        in_specs=[tile(COL_A), tile(COL_B), ab, bb, aa, ba, tile(COL_GCONV),
                  pl.BlockSpec((CONV_K, CONV_W), lambda i: (0, 0)), vec, vec, vec],
        out_specs=[pl.BlockSpec((t, CONV_W), lambda i: (i, 0))] * 2,
        out_shape=[jax.ShapeDtypeStruct((seq, CONV_W), F32)] * 2,
        scratch=[pltpu.VMEM((t + 2 * HALO, CONV_W), F32)], sem=("parallel",))(
            proj, proj, proj, proj, proj, proj, proj, w, bias, ln_g, ln_b)


def _conv_bwd_a(proj, y_conv, dcnv, ln_g, ln_b, name):
    seq = proj.shape[0]
    t = min(T_GROUP, seq)
    n_tiles = seq // t

    def body(a_ref, b_ref, ap_ref, bp_ref, an_ref, bn_ref, gate_ref, y_ref, d_ref, g_ref, beta_ref,
             dy_ref, dgate_ref, dw_ref, dbias_ref, dg_ref, dbeta_ref, ext_ref):
        i = pl.program_id(0)
        _fill_ext(ext_ref, _glu(ap_ref[...], bp_ref[...]), _glu(a_ref[...], b_ref[...]),
                  _glu(an_ref[...], bn_ref[...]), t, i, n_tiles)
        y = y_ref[...]
        mu = jnp.mean(y, axis=-1, keepdims=True)
        yc = y - mu
        rs = lax.rsqrt(jnp.mean(yc * yc, axis=-1, keepdims=True) + EPS)
        n = yc * rs
        z = n * g_ref[...] + beta_ref[...]
        act, dact = _silu_and_grad(z)
        gate, dgate = _silu_and_grad(gate_ref[...])
        d = d_ref[...]
        dgate_ref[...] = d * act * dgate
        dz = d * gate * dact
        dn = dz * g_ref[...]
        dy = rs * (dn - jnp.mean(dn, axis=-1, keepdims=True) - n * jnp.mean(dn * n, axis=-1, keepdims=True))
        dy_ref[...] = dy

        @pl.when(i == 0)
        def _():
            dw_ref[...] = jnp.zeros_like(dw_ref)
            dbias_ref[...] = jnp.zeros_like(dbias_ref)
            dg_ref[...] = jnp.zeros_like(dg_ref)
            dbeta_ref[...] = jnp.zeros_like(dbeta_ref)

        dg_ref[...] += jnp.sum(dz * n, axis=0, keepdims=True)
        dbeta_ref[...] += jnp.sum(dz, axis=0, keepdims=True)
        dbias_ref[...] += jnp.sum(dy, axis=0, keepdims=True)
        for k in range(CONV_K):
            dw_ref[k:k + 1, :] += jnp.sum(dy * ext_ref[pl.ds(HALO - CONV_PAD + k, t), :], axis=0, keepdims=True)

    tile = lambda c: pl.BlockSpec((t, CONV_W), lambda i: (i, c))
    own = pl.BlockSpec((t, CONV_W), lambda i: (i, 0))
    ab, aa = _halo_specs(t, COL_A, n_tiles)
    bb, ba = _halo_specs(t, COL_B, n_tiles)
    vec = pl.BlockSpec((1, CONV_W), lambda i: (0, 0))
    taps = pl.BlockSpec((CONV_K, CONV_W), lambda i: (0, 0))
    vshape = jax.ShapeDtypeStruct((1, CONV_W), F32)
    return _pcall(
        body, name=name, grid=(n_tiles,),
        in_specs=[tile(COL_A), tile(COL_B), ab, bb, aa, ba, tile(COL_GCONV), own, own, vec, vec],
        out_specs=[own, own, taps, vec, vec, vec],
        out_shape=[jax.ShapeDtypeStruct((seq, CONV_W), F32), jax.ShapeDtypeStruct((seq, CONV_W), F32),
                   jax.ShapeDtypeStruct((CONV_K, CONV_W), F32), vshape, vshape, vshape],
        scratch=[pltpu.VMEM((t + 2 * HALO, CONV_W), F32)], sem=("arbitrary",))(
            proj, proj, proj, proj, proj, proj, proj, y_conv, dcnv, ln_g, ln_b)


def _conv_bwd_b(proj, dy, w, name):
    seq = proj.shape[0]
    t = min(T_GROUP, seq)
    n_tiles = seq // t

    def body(a_ref, b_ref, dy_ref, dyp_ref, dyn_ref, w_ref, da_ref, db_ref, ext_ref):
        i = pl.program_id(0)
        _fill_ext(ext_ref, dyp_ref[...], dy_ref[...], dyn_ref[...], t, i, n_tiles)
        dh = _conv_taps(ext_ref, w_ref, t, True)
        sig = _sigmoid(b_ref[...])
        da_ref[...] = dh * sig
        db_ref[...] = dh * a_ref[...] * sig * (1.0 - sig)

    tile = lambda c: pl.BlockSpec((t, CONV_W), lambda i: (i, c))
    own = pl.BlockSpec((t, CONV_W), lambda i: (i, 0))
    before, after = _halo_specs(t, 0, n_tiles)
    return _pcall(
        body, name=name, grid=(n_tiles,),
        in_specs=[tile(COL_A), tile(COL_B), own, before, after, pl.BlockSpec((CONV_K, CONV_W), lambda i: (0, 0))],
        out_specs=[own, own], out_shape=[jax.ShapeDtypeStruct((seq, CONV_W), F32)] * 2,
        scratch=[pltpu.VMEM((t + 2 * HALO, CONV_W), F32)], sem=("parallel",))(proj, proj, dy, dy, dy, w)


def _head_masks():
    lane_head = lax.broadcasted_iota(jnp.int32, (SG_CHUNK, SG_W), 1) // HEAD_DIM
    return [lane_head == h for h in range(SG_HEADS)]


def _sg_mix(mats_ref, rhs, masks):
    out = jnp.zeros((SG_CHUNK, SG_W), F32)
    for h in range(SG_HEADS):
        out = out + jnp.where(masks[h], _dot(mats_ref[h], rhs), 0.0)
    return out


def _sg_specs(seq):
    t = min(T_GROUP, seq)
    tile = lambda c: pl.BlockSpec((t, SG_W), lambda i: (i, c))
    own = pl.BlockSpec((t, SG_W), lambda i: (i, 0))
    vec = pl.BlockSpec((1, SG_W), lambda i: (0, 0))
    mats = pl.BlockSpec((SG_HEADS, SG_CHUNK, SG_CHUNK), lambda i: (0, 0, 0))
    full = pl.BlockSpec((SG_CHUNK, SG_W), lambda i: (0, 0))
    return t, tile, own, vec, mats, full


def _sg_fwd(proj, ln_g, ln_b, w_b, bias_full, name):
    seq = proj.shape[0]
    t, tile, own, vec, mats, full = _sg_specs(seq)

    def body(u_ref, v_ref, gate_ref, g_ref, beta_ref, w_ref, bias_ref, o_ref):
        masks = _head_masks()
        for c in range(t // SG_CHUNK):
            rows = pl.ds(c * SG_CHUNK, SG_CHUNK)
            vg = _gelu_and_grad(v_ref[rows, :])[0]
            mu = jnp.mean(vg, axis=-1, keepdims=True)
            vc = vg - mu
            rs = lax.rsqrt(jnp.mean(vc * vc, axis=-1, keepdims=True) + EPS)
            vln = vc * rs * g_ref[...] + beta_ref[...]
            mixed = _sg_mix(w_ref, vln.astype(BF16), masks) + bias_ref[...]
            o_ref[rows, :] = _gelu_and_grad(u_ref[rows, :])[0] * mixed * _silu_and_grad(gate_ref[rows, :])[0]

    return _pcall(body, name=name, grid=(seq // t,),
                  in_specs=[tile(COL_U), tile(COL_VSG), tile(COL_GSG), vec, vec, mats, full], out_specs=own,
                  out_shape=jax.ShapeDtypeStruct((seq, SG_W), F32), sem=("parallel",))(
                      proj, proj, proj, ln_g, ln_b, w_b, bias_full)


def _sg_bwd(proj, dsg, ln_g, ln_b, w_b, w_t_b, bias_full, fold, name):
    seq = proj.shape[0]
    t, tile, own, vec, mats, full = _sg_specs(seq)
    n_tiles = seq // t

    def body(u_ref, v_ref, gate_ref, d_ref, g_ref, beta_ref, w_ref, wt_ref, bias_ref, fold_ref,
             du_ref, dv_ref, dgate_ref, dg_ref, dbeta_ref, dw_ref, db_ref, dbias_acc):
        i = pl.program_id(0)

        @pl.when(i == 0)
        def _():
            dg_ref[...] = jnp.zeros_like(dg_ref)
            dbeta_ref[...] = jnp.zeros_like(dbeta_ref)
            dw_ref[...] = jnp.zeros_like(dw_ref)
            dbias_acc[...] = jnp.zeros_like(dbias_acc)

        masks = _head_masks()
        for c in range(t // SG_CHUNK):
            rows = pl.ds(c * SG_CHUNK, SG_CHUNK)
            ug, dug = _gelu_and_grad(u_ref[rows, :])
            vg, dvg = _gelu_and_grad(v_ref[rows, :])
            mu = jnp.mean(vg, axis=-1, keepdims=True)
            vc = vg - mu
            rs = lax.rsqrt(jnp.mean(vc * vc, axis=-1, keepdims=True) + EPS)
            vn = vc * rs
            vln_b = (vn * g_ref[...] + beta_ref[...]).astype(BF16)
            mixed = _sg_mix(w_ref, vln_b, masks) + bias_ref[...]
            gate, dgate = _silu_and_grad(gate_ref[rows, :])
            d = d_ref[rows, :]
            dgate_ref[rows, :] = d * ug * mixed * dgate
            du_ref[rows, :] = d * mixed * gate * dug
            dmixed = d * ug * gate
            dbias_acc[...] += dmixed
            dmixed_b = dmixed.astype(BF16)
            for h in range(SG_HEADS):
                dm_h = jnp.where(masks[h], dmixed_b, jnp.zeros_like(dmixed_b))
                dw_ref[h] += lax.dot_general(dm_h, vln_b, (((1,), (1,)), ((), ())), preferred_element_type=F32)
            dvln = _sg_mix(wt_ref, dmixed_b, masks)
            dg_ref[...] += jnp.sum(dvln * vn, axis=0, keepdims=True)
            dbeta_ref[...] += jnp.sum(dvln, axis=0, keepdims=True)
            dvn = dvln * g_ref[...]
            dvgelu = rs * (dvn - jnp.mean(dvn, axis=-1, keepdims=True) - vn * jnp.mean(dvn * vn, axis=-1, keepdims=True))
            dv_ref[rows, :] = dvgelu * dvg

        @pl.when(i == n_tiles - 1)
        def _():
            db_ref[...] = _split_dot(dbias_acc[...], fold_ref[...])

    sq = pl.BlockSpec((SG_CHUNK, SG_CHUNK), lambda i: (0, 0))
    vshape = jax.ShapeDtypeStruct((1, SG_W), F32)
    return _pcall(
        body, name=name, grid=(n_tiles,),
        in_specs=[tile(COL_U), tile(COL_VSG), tile(COL_GSG), own, vec, vec, mats, mats, full,
                  pl.BlockSpec((SG_W, SG_CHUNK), lambda i: (0, 0))],
        out_specs=[own, own, own, vec, vec, mats, sq],
        out_shape=[jax.ShapeDtypeStruct((seq, SG_W), F32)] * 3 + [
            vshape, vshape, jax.ShapeDtypeStruct((SG_HEADS, SG_CHUNK, SG_CHUNK), F32),
            jax.ShapeDtypeStruct((SG_CHUNK, SG_CHUNK), F32)],
        scratch=[pltpu.VMEM((SG_CHUNK, SG_W), F32)], sem=("arbitrary",))(
            proj, proj, proj, dsg, ln_g, ln_b, w_b, w_t_b, bias_full, fold)


def _out_fwd(att, proj, cnv, sgu, x, w, gain, name):
    seq = x.shape[0]

    def body(att_ref, g0_ref, g1_ref, cnv_ref, sgu_ref, x_ref, w_ref, gain_ref, xo_ref, mix_ref, cat_ref):
        gate = jnp.concatenate([_silu_and_grad(g0_ref[...])[0], _silu_and_grad(g1_ref[...])[0]], axis=-1)
        cat_ref[:, 0:ATT_W] = (att_ref[...] * gate).astype(BF16)
        cat_ref[:, ATT_W:ATT_W + CONV_W] = cnv_ref[...].astype(BF16)
        cat_ref[:, ATT_W + CONV_W:] = sgu_ref[...].astype(BF16)
        mix = _dot(cat_ref[...], w_ref[...])
        mix_ref[...] = mix
        r = lax.rsqrt(jnp.mean(mix * mix, axis=-1, keepdims=True) + EPS)
        xo_ref[...] = x_ref[...] + mix * r * gain_ref[...]

    row = lambda w_: pl.BlockSpec((T_ROW, w_), lambda i: (i, 0))
    gate_blk = lambda c: pl.BlockSpec((T_ROW, 256), lambda i: (i, c))
    return _pcall(
        body, name=name, grid=(seq // T_ROW,),
        in_specs=[row(ATT_W), gate_blk(COL_GATT), gate_blk(COL_GATT + 1), row(CONV_W), row(SG_W), row(D_MODEL),
                  pl.BlockSpec((D_MODEL, D_MODEL), lambda i: (0, 0)), pl.BlockSpec((1, D_MODEL), lambda i: (0, 0))],
        out_specs=[row(D_MODEL), row(D_MODEL), row(D_MODEL)],
        out_shape=[jax.ShapeDtypeStruct((seq, D_MODEL), F32), jax.ShapeDtypeStruct((seq, D_MODEL), F32),
                   jax.ShapeDtypeStruct((seq, D_MODEL), BF16)],
        sem=("parallel",), vmem_mb=VMEM_MB)(att, proj, proj, cnv, sgu, x, w, gain)


def _out_bwd(dxo, mix, gain, w_t, att, proj, name):
    seq = dxo.shape[0]

    def body(dxo_ref, mix_ref, gain_ref, w_ref, att_ref, g0_ref, g1_ref,
             dmix_ref, datt_ref, dgatt_ref, dcnv_ref, dsgu_ref, dgain_ref):
        mix = mix_ref[...]
        r = lax.rsqrt(jnp.mean(mix * mix, axis=-1, keepdims=True) + EPS)
        n = mix * r
        dout = dxo_ref[...]
        dn = dout * gain_ref[...]
        dmix = (r * (dn - n * jnp.mean(dn * n, axis=-1, keepdims=True))).astype(BF16)
        dmix_ref[...] = dmix

        @pl.when(pl.program_id(0) == 0)
        def _():
            dgain_ref[...] = jnp.zeros_like(dgain_ref)

        dgain_ref[...] += jnp.sum(dout * n, axis=0, keepdims=True)
        dcat = _dot(dmix, w_ref[...])
        g0, dg0 = _silu_and_grad(g0_ref[...])
        g1, dg1 = _silu_and_grad(g1_ref[...])
        gate = jnp.concatenate([g0, g1], axis=-1)
        dgate = jnp.concatenate([dg0, dg1], axis=-1)
        dca = dcat[:, 0:ATT_W]
        datt_ref[...] = dca * gate
        dgatt_ref[...] = dca * att_ref[...] * dgate
        dcnv_ref[...] = dcat[:, ATT_W:ATT_W + CONV_W]
        dsgu_ref[...] = dcat[:, ATT_W + CONV_W:]

    row = lambda w_: pl.BlockSpec((T_ROW, w_), lambda i: (i, 0))
    gate_blk = lambda c: pl.BlockSpec((T_ROW, 256), lambda i: (i, c))
    vec = pl.BlockSpec((1, D_MODEL), lambda i: (0, 0))
    return _pcall(
        body, name=name, grid=(seq // T_ROW,),
        in_specs=[row(D_MODEL), row(D_MODEL), vec, pl.BlockSpec((D_MODEL, D_MODEL), lambda i: (0, 0)), row(ATT_W),
                  gate_blk(COL_GATT), gate_blk(COL_GATT + 1)],
        out_specs=[row(D_MODEL), row(ATT_W), row(ATT_W), row(CONV_W), row(SG_W), vec],
        out_shape=[jax.ShapeDtypeStruct((seq, D_MODEL), BF16), jax.ShapeDtypeStruct((seq, ATT_W), F32),
                   jax.ShapeDtypeStruct((seq, ATT_W), F32), jax.ShapeDtypeStruct((seq, CONV_W), F32),
                   jax.ShapeDtypeStruct((seq, SG_W), F32), jax.ShapeDtypeStruct((1, D_MODEL), F32)],
        sem=("arbitrary",), vmem_mb=VMEM_MB)(dxo, mix, gain, w_t, att, proj, proj)


def _loss_head(y, target, name):
    seq = y.shape[0]
    t = min(T_GROUP, seq)

    def body(y_ref, t_ref, sse_ref, dy_ref):
        err = y_ref[...] - t_ref[...]
        dy_ref[...] = err * (1.0 / D_MODEL)

        @pl.when(pl.program_id(0) == 0)
        def _():
            sse_ref[...] = jnp.zeros_like(sse_ref)

        part = jnp.sum(jnp.sum(err * err, axis=0, keepdims=True), axis=-1, keepdims=True)
        sse_ref[...] += jnp.broadcast_to(part, (1, LANES))

    row = pl.BlockSpec((t, D_MODEL), lambda i: (i, 0))
    return _pcall(body, name=name, grid=(seq // t,), in_specs=[row, row],
                  out_specs=[pl.BlockSpec((1, LANES), lambda i: (0, 0)), row],
                  out_shape=[jax.ShapeDtypeStruct((1, LANES), F32), jax.ShapeDtypeStruct((seq, D_MODEL), F32)],
                  sem=("arbitrary",))(y, target)


def _to_heads(a, heads):
    return a.reshape(a.shape[0], heads, HEAD_DIM).transpose(1, 0, 2)


def _from_heads(a):
    return a.transpose(1, 0, 2).reshape(a.shape[1], a.shape[0] * HEAD_DIM)


def _row_blocks(a, bk):
    return a.reshape(a.shape[0], a.shape[1] // bk, bk, HEAD_DIM)


def _col_blocks(a, bk):
    return _row_blocks(a, bk).transpose(0, 1, 3, 2)


def _from_col_blocks(a):
    return a.transpose(0, 1, 3, 2).reshape(a.shape[0], a.shape[1] * a.shape[3], HEAD_DIM)


def _flat_rows(a, rows):
    flat = a.reshape(-1)
    return jnp.pad(flat, (0, rows * LANES - flat.shape[0])).reshape(rows, LANES)


SHARD_ROWS = {"w_in": 2 * D_MODEL * (D_IN // N_DEV) // LANES, "w_out": 2 * (D_MODEL // N_DEV) * D_MODEL // LANES,
              "conv_dw": 16}
REPL_SHAPES = [("pre_norm", (2, D_MODEL)), ("post_norm", (2, D_MODEL)), ("q_norm", (2, HEAD_DIM)),
               ("k_norm", (2, HEAD_DIM)), ("conv_dw_b", (2, CONV_W)), ("conv_ln_g", (2, CONV_W)),
               ("conv_ln_b", (2, CONV_W)), ("sg_ln_g", (2, SG_W)), ("sg_ln_b", (2, SG_W)),
               ("sg_w", (2, SG_HEADS, SG_CHUNK, SG_CHUNK)), ("sg_b", (2, SG_HEADS, SG_CHUNK))]
REPL_ROWS = 1088
WEIGHT_ORDER = ["pre_norm", "post_norm", "w_in", "w_out", "q_norm", "k_norm", "conv_dw", "conv_dw_b", "conv_ln_g",
                "conv_ln_b", "sg_ln_g", "sg_ln_b", "sg_w", "sg_b"]


def _pack_shard(parts):
    return jnp.concatenate([_flat_rows(parts[k], SHARD_ROWS[k]) for k in ("w_in", "w_out", "conv_dw")], axis=0)


def _unpack_shard(flat, shapes):
    out, at = {}, 0
    for k in ("w_in", "w_out", "conv_dw"):
        size = math.prod(shapes[k])
        out[k] = flat[at:at + SHARD_ROWS[k]].reshape(-1)[:size].reshape(shapes[k])
        at += SHARD_ROWS[k]
    return out


REPL_USED = sum(math.prod(shape) for _, shape in REPL_SHAPES)


def _pack_repl(parts, extra=None):
    tail = [] if extra is None else [extra.reshape(1)]
    flat = jnp.concatenate([parts[k].reshape(-1) for k, _ in REPL_SHAPES] + tail)
    return jnp.pad(flat, (0, REPL_ROWS * LANES - flat.shape[0])).reshape(REPL_ROWS, LANES)


def _unpack_repl(flat):
    out, at, flat = {}, 0, flat.reshape(-1)
    for k, shape in REPL_SHAPES:
        size = math.prod(shape)
        out[k] = flat[at:at + size].reshape(shape)
        at += size
    return out


def kernel(x, pre_norm, post_norm, w_in, w_out, q_norm, k_norm, conv_dw, conv_dw_b, conv_ln_g, conv_ln_b, sg_ln_g, sg_ln_b, sg_w, sg_b, loss_target, m_pre_norm, m_post_norm, m_w_in, m_w_out, m_q_norm, m_k_norm, m_conv_dw, m_conv_dw_b, m_conv_ln_g, m_conv_ln_b, m_sg_ln_g, m_sg_ln_b, m_sg_w, m_sg_b, v_pre_norm, v_post_norm, v_w_in, v_w_out, v_q_norm, v_k_norm, v_conv_dw, v_conv_dw_b, v_conv_ln_g, v_conv_ln_b, v_sg_ln_g, v_sg_ln_b, v_sg_w, v_sg_b):
    weights = dict(pre_norm=pre_norm, post_norm=post_norm, w_in=w_in, w_out=w_out, q_norm=q_norm, k_norm=k_norm,
                   conv_dw=conv_dw, conv_dw_b=conv_dw_b, conv_ln_g=conv_ln_g, conv_ln_b=conv_ln_b, sg_ln_g=sg_ln_g,
                   sg_ln_b=sg_ln_b, sg_w=sg_w, sg_b=sg_b)
    mom_m = dict(pre_norm=m_pre_norm, post_norm=m_post_norm, w_in=m_w_in, w_out=m_w_out, q_norm=m_q_norm,
                 k_norm=m_k_norm, conv_dw=m_conv_dw, conv_dw_b=m_conv_dw_b, conv_ln_g=m_conv_ln_g,
                 conv_ln_b=m_conv_ln_b, sg_ln_g=m_sg_ln_g, sg_ln_b=m_sg_ln_b, sg_w=m_sg_w, sg_b=m_sg_b)
    mom_v = dict(pre_norm=v_pre_norm, post_norm=v_post_norm, w_in=v_w_in, w_out=v_w_out, q_norm=v_q_norm,
                 k_norm=v_k_norm, conv_dw=v_conv_dw, conv_dw_b=v_conv_dw_b, conv_ln_g=v_conv_ln_g,
                 conv_ln_b=v_conv_ln_b, sg_ln_g=v_sg_ln_g, sg_ln_b=v_sg_ln_b, sg_w=v_sg_w, sg_b=v_sg_b)
    depth = pre_norm.shape[0]
    seq = x.shape[1]
    bk = min(BK, seq)
    x0 = x.reshape(seq, D_MODEL)
    target = loss_target.reshape(seq, D_MODEL)

    w_in_all, w_out_all, dw_all = _exchange(
        [], [w_in.astype(BF16), w_out.astype(BF16), jnp.pad(conv_dw, ((0, 0), (0, 1), (0, 0)))], "gather_weights")
    w_in_full = w_in_all.transpose(1, 2, 0, 3).reshape(depth, D_MODEL, D_IN)
    w_out_full = w_out_all.transpose(1, 0, 2, 3).reshape(depth, D_MODEL, D_MODEL)
    dw_full = dw_all[:, :, :CONV_K, :].transpose(1, 2, 0, 3).reshape(depth, CONV_K, CONV_W)

    cos, sin, perm = _rope_tables(seq)
    q_post = jnp.full((ATT_HEADS, 1, HEAD_DIM), Q_SCALE, F32)
    k_post = jnp.ones((KV_HEADS, 1, HEAD_DIM), F32)
    lane = jnp.arange(SG_W)
    fold = (lane[:, None] // HEAD_DIM == jnp.arange(SG_CHUNK)[None, :]).astype(BF16)

    def layer_consts(l):
        return dict(
            q_gain=jnp.broadcast_to(q_norm[l], (ATT_HEADS, 1, HEAD_DIM)),
            k_gain=jnp.broadcast_to(k_norm[l], (KV_HEADS, 1, HEAD_DIM)), sg_w_b=sg_w[l].astype(BF16), sg_wt_b=sg_w[l].transpose(0, 2, 1).astype(BF16),
            sg_bias=jnp.repeat(sg_b[l].T, HEAD_DIM, axis=1),
            vec=lambda a: a[l].reshape(1, -1))

    saved = []
    xc = x0
    for l in range(depth):
        c = layer_consts(l)
        proj, hb = _proj_fwd(xc, c["vec"](pre_norm), w_in_full[l], f"proj_fwd_{l}")
        raw_q = _to_heads(proj[:, :ATT_W], ATT_HEADS)
        raw_k = _to_heads(proj[:, ATT_W:ATT_W + KV_W], KV_HEADS)
        qs = _qk_prep_fwd(raw_q, c["q_gain"], q_post, cos, sin, perm, f"q_prep_fwd_{l}")
        kr = _qk_prep_fwd(raw_k, c["k_gain"], k_post, cos, sin, perm, f"k_prep_fwd_{l}")
        qs_t = qs.transpose(0, 2, 1)
        vh = _to_heads(proj[:, ATT_W + KV_W:ATT_W + 2 * KV_W], KV_HEADS).astype(BF16)
        k_rows, k_cols, v_cols = _row_blocks(kr, bk), _col_blocks(kr, bk), _col_blocks(vh, bk)
        v_ext = jnp.concatenate([v_cols, jnp.ones_like(v_cols[:, :, :1]), jnp.zeros_like(v_cols[:, :, :7])], axis=2)
        o_t, lse = _attn_fwd(qs_t, k_rows, v_ext, f"attn_fwd_{l}")
        att = o_t.reshape(ATT_W, seq).T
        o = o_t.transpose(0, 2, 1)
        lse = jnp.broadcast_to(lse.reshape(ATT_HEADS, seq, 1), (ATT_HEADS, seq, LANES))
        cnv, y_conv = _conv_fwd(proj, dw_full[l], c["vec"](conv_dw_b), c["vec"](conv_ln_g), c["vec"](conv_ln_b),
                                f"conv_fwd_{l}")
        sgu = _sg_fwd(proj, c["vec"](sg_ln_g), c["vec"](sg_ln_b), c["sg_w_b"], c["sg_bias"], f"sg_fwd_{l}")
        x_new, mix, cat_b = _out_fwd(att, proj, cnv, sgu, xc, w_out_full[l], c["vec"](post_norm), f"out_fwd_{l}")
        saved.append(dict(x=xc, proj=proj, hb=hb, raw_q=raw_q, raw_k=raw_k, qs=qs, qs_t=qs_t, k_rows=k_rows,
                          k_cols=k_cols, v_cols=v_cols, o=o, lse=lse, att=att, mix=mix, cat_b=cat_b, y_conv=y_conv))
        xc = x_new

    sse, dx = _loss_head(xc, target, "loss_head")

    grads = {k: [None] * depth for k in WEIGHT_ORDER}
    for l in reversed(range(depth)):
        c, s = layer_consts(l), saved[l]
        dmix_b, datt, dgatt, dcnv, dsgu, g_post = _out_bwd(
            dx, s["mix"], c["vec"](post_norm), w_out_full[l].T, s["att"], s["proj"], f"out_bwd_{l}")
        grads["post_norm"][l] = g_post.reshape(-1)
        grads["w_out"][l] = _matmul_acc(s["cat_b"], dmix_b, D_MODEL, f"grad_w_out_{l}")
        do = _to_heads(datt, ATT_HEADS)
        dqs, dkt, dvt = _attn_bwd(s["qs"], s["qs_t"], do, do.astype(BF16).transpose(0, 2, 1),
                                  s["o"], s["lse"], s["k_rows"], s["k_cols"], s["v_cols"], f"attn_bwd_{l}")
        d_raw_q, g_qgain = _qk_prep_bwd(s["raw_q"], dqs, c["q_gain"], q_post, cos, sin, perm, f"q_prep_bwd_{l}")
        d_raw_k, g_kgain = _qk_prep_bwd(s["raw_k"], _from_col_blocks(dkt), c["k_gain"], k_post, cos, sin, perm,
                                        f"k_prep_bwd_{l}")
        grads["q_norm"][l] = jnp.sum(g_qgain[:, 0], axis=0)
        grads["k_norm"][l] = jnp.sum(g_kgain[:, 0], axis=0)
        dy_conv, dg_conv, g_dw, g_dwb, g_clg, g_clb = _conv_bwd_a(
            s["proj"], s["y_conv"], dcnv, c["vec"](conv_ln_g), c["vec"](conv_ln_b), f"conv_bwd_a_{l}")
        da, db = _conv_bwd_b(s["proj"], dy_conv, dw_full[l], f"conv_bwd_b_{l}")
        grads["conv_dw"][l], grads["conv_dw_b"][l] = g_dw, g_dwb.reshape(-1)
        grads["conv_ln_g"][l], grads["conv_ln_b"][l] = g_clg.reshape(-1), g_clb.reshape(-1)
        du, dv_sg, dg_sg, g_slg, g_slb, g_sw, g_sb = _sg_bwd(
            s["proj"], dsgu, c["vec"](sg_ln_g), c["vec"](sg_ln_b), c["sg_w_b"], c["sg_wt_b"], c["sg_bias"], fold,
            f"sg_bwd_{l}")
        grads["sg_ln_g"][l], grads["sg_ln_b"][l] = g_slg.reshape(-1), g_slb.reshape(-1)
        grads["sg_w"][l], grads["sg_b"][l] = g_sw, g_sb[:, :SG_HEADS].T
        dproj = jnp.concatenate(
            [_from_heads(d_raw_q), _from_heads(d_raw_k), _from_heads(_from_col_blocks(dvt)),
             dgatt, da, db, dg_conv, du, dv_sg, dg_sg], axis=-1).astype(BF16)
        grads["w_in"][l] = _matmul_acc(s["hb"], dproj, D_IN // 2, f"grad_w_in_{l}")
        dx, g_pre = _proj_bwd(dproj, w_in_full[l].T, s["x"], c["vec"](pre_norm), dx, f"proj_bwd_{l}")
        grads["pre_norm"][l] = g_pre.reshape(-1)
    grad_x = dx.reshape(x.shape)
    grads = {k: jnp.stack(v) for k, v in grads.items()}

    shard_blocks = dict(
        w_in=grads["w_in"].reshape(depth, D_MODEL, N_DEV, D_IN // N_DEV).transpose(2, 0, 1, 3),
        w_out=grads["w_out"].reshape(depth, N_DEV, D_MODEL // N_DEV, D_MODEL).transpose(1, 0, 2, 3),
        conv_dw=grads["conv_dw"].reshape(depth, CONV_K, N_DEV, CONV_W // N_DEV).transpose(2, 0, 1, 3))
    scatter_src = jnp.stack([_pack_shard({k: a[d] for k, a in shard_blocks.items()})
                             for d in range(N_DEV)]).astype(BF16)
    shard_slots, repl_slots = _exchange([scatter_src], [_pack_repl(grads, sse[0, 0])], "exchange_grads")

    shard_shapes = {k: weights[k].shape for k in SHARD_ROWS}
    gs, ds_, ms, vs = _sum_adamw(shard_slots, _pack_shard(weights), _pack_shard(mom_m), _pack_shard(mom_v),
                                 "adamw_sharded")
    gr, dr, mr, vr = _sum_adamw(repl_slots, _pack_repl(weights), _pack_repl(mom_m), _pack_repl(mom_v),
                                "adamw_replicated")
    loss = gr.reshape(-1)[REPL_USED] * (0.5 / D_MODEL)
    results = []
    for shard_flat, repl_flat in ((gs, gr), (ds_, dr), (ms, mr), (vs, vr)):
        parts = {**_unpack_shard(shard_flat, shard_shapes), **_unpack_repl(repl_flat)}
        results.append([parts[k] for k in WEIGHT_ORDER])
    return (loss, grad_x, *results[0], *results[1], *results[2], *results[3])
```

```python
import math

import jax
import jax.numpy as jnp
from jax import lax
from jax.experimental import pallas as pl
from jax.experimental.pallas import tpu as pltpu

F32, BF16 = jnp.float32, jnp.bfloat16

N_DEV = 8
MESH_AXES = ("x", "y", "c")
EPS = 1e-6
D_MODEL = 1024
HEAD_DIM = 64
ATT_HEADS, KV_HEADS = 8, 2
QKV_HEADS = ATT_HEADS + 2 * KV_HEADS
QKV_W = QKV_HEADS * HEAD_DIM
GROUP = ATT_HEADS // KV_HEADS
ATT_W, KV_W, CONV_W, SG_W = 512, 128, 256, 256
CONV_K, CONV_PAD, HALO = 31, 15, 16
SG_HEADS, SG_CHUNK = 4, 128
D_IN = 2816
GRID_W = 64
ROPE_THETA = 10000.0
LOG2E, LN2 = math.log2(math.e), math.log(2.0)
Q_SCALE = HEAD_DIM ** -0.5 * LOG2E
LANES = 128

COL_GATT, COL_A, COL_B, COL_GCONV, COL_U, COL_VSG, COL_GSG = 3, 5, 6, 7, 8, 9, 10

ADAM_LR, ADAM_B1, ADAM_B2, ADAM_EPS, ADAM_WD, ADAM_STEP = 0.001, 0.9, 0.999, 1e-08, 0.01, 10

T_ROW = 256
T_PREP = 2048
T_GROUP = 512
BQ, BK = 512, 512
HEADS_PER_STEP = 2
FWD_UNROLL, BWD_UNROLL = 8, 4
VMEM_MB = 56
ATTN_BWD_VMEM_MB = 58


def _pcall(body, *, name, grid, in_specs, out_specs, out_shape, scratch=(), sem=None, vmem_mb=None):
    params = {}
    if sem is not None:
        params["dimension_semantics"] = sem
    if vmem_mb is not None:
        params["vmem_limit_bytes"] = vmem_mb << 20
    return pl.pallas_call(body, name=name, grid=grid, in_specs=in_specs, out_specs=out_specs, out_shape=out_shape,
                          scratch_shapes=list(scratch), compiler_params=pltpu.CompilerParams(**params))


def _dot(a, b):
    return jnp.dot(a, b, preferred_element_type=F32)


def _sigmoid(x):
    return 1.0 / (1.0 + jnp.exp(-x))


def _silu_and_grad(x):
    s = _sigmoid(x)
    return x * s, s * (1.0 + x * (1.0 - s))


def _gelu_and_grad(x):
    cdf = 0.5 * (1.0 + lax.erf(x * (1.0 / math.sqrt(2.0))))
    pdf = jnp.exp(-0.5 * x * x) * (1.0 / math.sqrt(2.0 * math.pi))
    return x * cdf, cdf + x * pdf


def _split_dot(y, mat):
    hi = y.astype(BF16)
    lo = (y - hi.astype(F32)).astype(BF16)
    return _dot(hi, mat) + _dot(lo, mat)


def _row_tile(rows, cap):
    best = 8
    for t in range(8, min(rows, cap) + 1, 8):
        if rows % t == 0:
            best = t
    return best


def _exchange(scatter, gather, name):
    n_s = len(scatter)
    arrs = list(scatter) + list(gather)
    n = len(arrs)
    flips = [(fx, fy, fc) for fx in (0, 1) for fy in (0, 1) for fc in (0, 1)][1:]
    n_peer = len(flips)

    def body(*refs):
        ins, outs = refs[:n], refs[n:2 * n]
        send_sems, recv_sems, local_sems = refs[2 * n:]
        pos = tuple(lax.axis_index(a) for a in MESH_AXES)

        def peer(flip):
            return tuple((1 - p) if f else p for p, f in zip(pos, flip))

        def slot(p):
            return 4 * p[0] + 2 * p[1] + p[2]

        def src(a, p):
            return ins[a].at[slot(p)] if a < n_s else ins[a]

        def remote(a, k, src_ref, dst_slot, to):
            return pltpu.make_async_remote_copy(
                src_ref=src_ref, dst_ref=outs[a].at[dst_slot], send_sem=send_sems.at[a * n_peer + k],
                recv_sem=recv_sems.at[a * n_peer + k], device_id=to, device_id_type=pl.DeviceIdType.MESH)

        local = [pltpu.make_async_copy(src(a, pos), outs[a].at[slot(pos)], local_sems.at[a]) for a in range(n)]
        for cp in local:
            cp.start()
        sends = [remote(a, k, src(a, peer(f)), slot(pos), peer(f)) for a in range(n) for k, f in enumerate(flips)]
        for cp in sends:
            cp.start()
        for a in range(n):
            for k, f in enumerate(flips):
                remote(a, k, src(a, peer(f)), slot(peer(f)), peer(f)).wait_recv()
        for cp in sends:
            cp.wait_send()
        for cp in local:
            cp.wait()

    out_shape = [jax.ShapeDtypeStruct((N_DEV,) + (a.shape[1:] if i < n_s else a.shape), a.dtype)
                 for i, a in enumerate(arrs)]
    any_spec = pl.BlockSpec(memory_space=pl.ANY)
    return pl.pallas_call(
        body, name=name, out_shape=out_shape, in_specs=[any_spec] * n, out_specs=[any_spec] * n,
        scratch_shapes=[pltpu.SemaphoreType.DMA((n * n_peer,)), pltpu.SemaphoreType.DMA((n * n_peer,)),
                        pltpu.SemaphoreType.DMA((n,))],
    )(*arrs)


def _sum_adamw(slots, w, m, v, name):
    rows = w.shape[0]
    tr = _row_tile(rows, 1024)
    c1 = 1.0 - ADAM_B1 ** ADAM_STEP
    c2 = 1.0 - ADAM_B2 ** ADAM_STEP

    def body(s_ref, w_ref, m_ref, v_ref, g_out, d_out, m_out, v_out):
        g = s_ref[0].astype(F32)
        for d in range(1, N_DEV):
            g = g + s_ref[d].astype(F32)
        m_new = ADAM_B1 * m_ref[...] + (1.0 - ADAM_B1) * g
        v_new = ADAM_B2 * v_ref[...] + (1.0 - ADAM_B2) * (g * g)
        m_hat = m_new / c1
        v_hat = v_new / c2
        g_out[...] = g
        d_out[...] = -ADAM_LR * (m_hat / (jnp.sqrt(v_hat) + ADAM_EPS) + ADAM_WD * w_ref[...])
        m_out[...] = m_new
        v_out[...] = v_new

    flat = pl.BlockSpec((tr, LANES), lambda i: (i, 0))
    return _pcall(
        body, name=name, grid=(rows // tr,),
        in_specs=[pl.BlockSpec((N_DEV, tr, LANES), lambda i: (0, i, 0)), flat, flat, flat],
        out_specs=[flat] * 4, out_shape=[jax.ShapeDtypeStruct((rows, LANES), F32)] * 4,
        sem=("parallel",), vmem_mb=VMEM_MB)(slots, w, m, v)


def _proj_fwd(x, gain, w, name):
    seq = x.shape[0]

    def body(x_ref, g_ref, w_ref, proj_ref, hb_ref, qkv_t_ref):
        xf = x_ref[...]
        r = lax.rsqrt(jnp.mean(xf * xf, axis=-1, keepdims=True) + EPS)
        h = (xf * r * g_ref[...]).astype(BF16)
        hb_ref[...] = h
        proj = _dot(h, w_ref[...])
        proj_ref[...] = proj
        qkv_t_ref[...] = proj[:, :QKV_W].T

    return _pcall(
        body, name=name, grid=(seq // T_ROW,),
        in_specs=[pl.BlockSpec((T_ROW, D_MODEL), lambda i: (i, 0)), pl.BlockSpec((1, D_MODEL), lambda i: (0, 0)),
                  pl.BlockSpec((D_MODEL, D_IN), lambda i: (0, 0))],
        out_specs=[pl.BlockSpec((T_ROW, D_IN), lambda i: (i, 0)), pl.BlockSpec((T_ROW, D_MODEL), lambda i: (i, 0)),
                   pl.BlockSpec((QKV_W, T_ROW), lambda i: (0, i))],
        out_shape=[jax.ShapeDtypeStruct((seq, D_IN), F32), jax.ShapeDtypeStruct((seq, D_MODEL), BF16),
                   jax.ShapeDtypeStruct((QKV_W, seq), F32)],
        sem=("parallel",), vmem_mb=VMEM_MB)(x, gain, w)


def _proj_bwd(dproj, w_t, x, gain, dxo, name):
    seq = x.shape[0]

    def body(dp_ref, w_ref, x_ref, g_ref, dxo_ref, dx_ref, dg_ref):
        dh = _dot(dp_ref[...], w_ref[...])
        xf = x_ref[...]
        r = lax.rsqrt(jnp.mean(xf * xf, axis=-1, keepdims=True) + EPS)
        n = xf * r
        dn = dh * g_ref[...]
        dx_ref[...] = dxo_ref[...] + r * (dn - n * jnp.mean(dn * n, axis=-1, keepdims=True))

        @pl.when(pl.program_id(0) == 0)
        def _():
            dg_ref[...] = jnp.zeros_like(dg_ref)

        dg_ref[...] += jnp.sum(dh * n, axis=0, keepdims=True)

    row = pl.BlockSpec((T_ROW, D_MODEL), lambda i: (i, 0))
    vec = pl.BlockSpec((1, D_MODEL), lambda i: (0, 0))
    return _pcall(
        body, name=name, grid=(seq // T_ROW,),
        in_specs=[pl.BlockSpec((T_ROW, D_IN), lambda i: (i, 0)), pl.BlockSpec((D_IN, D_MODEL), lambda i: (0, 0)),
                  row, vec, row],
        out_specs=[row, vec],
        out_shape=[jax.ShapeDtypeStruct((seq, D_MODEL), F32), jax.ShapeDtypeStruct((1, D_MODEL), F32)],
        sem=("arbitrary",), vmem_mb=VMEM_MB)(dproj, w_t, x, gain, dxo)


def _matmul_acc(a, b, tn, name):
    seq, m = a.shape
    n = b.shape[1]
    ts = min(512, seq)

    def body(a_ref, b_ref, o_ref):
        @pl.when(pl.program_id(1) == 0)
        def _():
            o_ref[...] = jnp.zeros_like(o_ref)

        o_ref[...] += lax.dot_general(a_ref[...], b_ref[...], (((0,), (0,)), ((), ())), preferred_element_type=F32)

    return _pcall(
        body, name=name, grid=(n // tn, seq // ts),
        in_specs=[pl.BlockSpec((ts, m), lambda j, k: (k, 0)), pl.BlockSpec((ts, tn), lambda j, k: (k, j))],
        out_specs=pl.BlockSpec((m, tn), lambda j, k: (0, j)), out_shape=jax.ShapeDtypeStruct((m, n), F32),
        sem=("parallel", "arbitrary"), vmem_mb=VMEM_MB)(a, b)


ROPE_HALF = HEAD_DIM // 4


def _rope_tables(seq):
    t = jnp.arange(seq, dtype=jnp.int32)
    row = (t // GRID_W).astype(F32)
    col = (t % GRID_W).astype(F32)
    inv_freq = ROPE_THETA ** (-jnp.arange(ROPE_HALF, dtype=F32) / ROPE_HALF)
    ang_r = row[:, None] * inv_freq[None, :]
    ang_c = col[:, None] * inv_freq[None, :]
    cos = jnp.concatenate([jnp.cos(ang_r), jnp.cos(ang_r), jnp.cos(ang_c), jnp.cos(ang_c)], axis=-1)
    sin = jnp.concatenate([-jnp.sin(ang_r), jnp.sin(ang_r), -jnp.sin(ang_c), jnp.sin(ang_c)], axis=-1)
    return cos.T, sin.T


def _rope_partner(y):
    h = ROPE_HALF
    return jnp.concatenate([y[h:2 * h], y[0:h], y[3 * h:4 * h], y[2 * h:3 * h]], axis=0)


def _qk_specs(seq, head0):
    t = min(T_PREP, seq)
    src = pl.BlockSpec((None, HEAD_DIM, t), lambda h, i: (h + head0, 0, i))
    own = pl.BlockSpec((None, HEAD_DIM, t), lambda h, i: (h, 0, i))
    nat = pl.BlockSpec((None, t, HEAD_DIM), lambda h, i: (h, i, 0))
    col = pl.BlockSpec((HEAD_DIM, 1), lambda h, i: (0, 0))
    tab = pl.BlockSpec((HEAD_DIM, t), lambda h, i: (0, i))
    return t, src, own, nat, col, tab


def _qk_prep_fwd(qkv_t, head0, heads, gain, scale, cos, sin, name):
    seq = qkv_t.shape[2]
    t, src, own, nat, col, tab = _qk_specs(seq, head0)

    def body(x_ref, g_ref, c_ref, s_ref, ot_ref, on_ref):
        xf = x_ref[...]
        r = lax.rsqrt(jnp.mean(xf * xf, axis=0, keepdims=True) + EPS)
        y = xf * r * g_ref[...]
        z = (y * c_ref[...] + _rope_partner(y) * s_ref[...]) * scale
        ot_ref[...] = z.astype(BF16)
        on_ref[...] = z.T.astype(BF16)

    return _pcall(body, name=name, grid=(heads, seq // t), in_specs=[src, col, tab, tab], out_specs=[own, nat],
                  out_shape=[jax.ShapeDtypeStruct((heads, HEAD_DIM, seq), BF16),
                             jax.ShapeDtypeStruct((heads, seq, HEAD_DIM), BF16)],
                  sem=("parallel", "parallel"))(qkv_t, gain, cos, sin)


def _qk_prep_bwd(qkv_t, head0, dout, dout_is_t, gain, scale, cos, sin, name):
    heads = dout.shape[0]
    seq = qkv_t.shape[2]
    t, src, own, nat, col, tab = _qk_specs(seq, head0)

    def body(x_ref, d_ref, g_ref, c_ref, s_ref, dx_ref, dg_ref):
        xf = x_ref[...]
        r = lax.rsqrt(jnp.mean(xf * xf, axis=0, keepdims=True) + EPS)
        n = xf * r
        d = d_ref[...] if dout_is_t else d_ref[...].T
        dz = d * scale
        dy = dz * c_ref[...] + _rope_partner(dz * s_ref[...])
        dn = dy * g_ref[...]
        dx_ref[...] = r * (dn - n * jnp.mean(dn * n, axis=0, keepdims=True))

        @pl.when(pl.program_id(1) == 0)
        def _():
            dg_ref[...] = jnp.zeros_like(dg_ref)

        dg_ref[...] += jnp.sum(dy * n, axis=1, keepdims=True)

    return _pcall(body, name=name, grid=(heads, seq // t),
                  in_specs=[src, own if dout_is_t else nat, col, tab, tab],
                  out_specs=[own, pl.BlockSpec((None, HEAD_DIM, 1), lambda h, i: (h, 0, 0))],
                  out_shape=[jax.ShapeDtypeStruct((heads, HEAD_DIM, seq), F32),
                             jax.ShapeDtypeStruct((heads, HEAD_DIM, 1), F32)],
                  sem=("parallel", "arbitrary"))(qkv_t, dout, gain, cos, sin)


V_ROWS = HEAD_DIM + 8


def _unroll(nk, cap):
    u = 1
    while u * 2 <= cap and nk % (u * 2) == 0:
        u *= 2
    return u


def _attn_fwd(qs_t, k, v_t, name):
    seq = qs_t.shape[2]
    nk, bk = k.shape[1], k.shape[2]
    bq = min(BQ, seq)
    unroll = _unroll(nk, FWD_UNROLL)
    heads = range(HEADS_PER_STEP)

    def body(qt_ref, k_ref, vt_ref, ot_ref, lse_ref, s_scr):
        q_t = [qt_ref[h] for h in heads]

        def scores(j, slot):
            kj = k_ref[j]
            top = []
            for h in heads:
                s = _dot(kj, q_t[h])
                s_scr[slot, h] = s
                top.append(jnp.max(s, axis=0, keepdims=True))
            return tuple(top)

        def accumulate(j, slot, state, top):
            vtj = vt_ref[j]
            out = []
            for h in heads:
                m, acc = state[h]
                m_new = jnp.maximum(m, top[h])
                p = jnp.exp2(s_scr[slot, h] - m_new).astype(BF16)
                out.append((m_new, jnp.exp2(m - m_new) * acc + _dot(vtj, p)))
            return tuple(out)

        def step(t, carry):
            state, top = carry
            for u in range(unroll):
                nxt = unroll * t + u + 1
                top_next = scores(jnp.minimum(nxt, nk - 1) if u == unroll - 1 else nxt, (u + 1) % 2)
                state = accumulate(unroll * t + u, u % 2, state, top)
                top = top_next
            return state, top

        init = tuple((jnp.full((1, bq), -jnp.inf, F32), jnp.zeros((V_ROWS, bq), F32)) for _ in heads)
        state, _ = lax.fori_loop(0, nk // unroll, step, (init, scores(0, 0)))
        for h in heads:
            m, acc = state[h]
            l = acc[HEAD_DIM:HEAD_DIM + 1, :]
            ot_ref[h] = acc[:HEAD_DIM, :] / l
            lse_ref[h] = m + jnp.log2(l)

    kv_of = lambda g: g * HEADS_PER_STEP // GROUP
    return _pcall(
        body, name=name, grid=(ATT_HEADS // HEADS_PER_STEP, seq // bq),
        in_specs=[pl.BlockSpec((HEADS_PER_STEP, HEAD_DIM, bq), lambda g, i: (g, 0, i)),
                  pl.BlockSpec((None, nk, bk, HEAD_DIM), lambda g, i: (kv_of(g), 0, 0, 0)),
                  pl.BlockSpec((None, nk, V_ROWS, bk), lambda g, i: (kv_of(g), 0, 0, 0))],
        out_specs=[pl.BlockSpec((HEADS_PER_STEP, HEAD_DIM, bq), lambda g, i: (g, 0, i)),
                   pl.BlockSpec((HEADS_PER_STEP, 1, bq), lambda g, i: (g, 0, i))],
        out_shape=[jax.ShapeDtypeStruct((ATT_HEADS, HEAD_DIM, seq), F32),
                   jax.ShapeDtypeStruct((ATT_HEADS, 1, seq), F32)],
        scratch=[pltpu.VMEM((2, HEADS_PER_STEP, bk, bq), F32)],
        sem=("parallel", "parallel"), vmem_mb=VMEM_MB)(qs_t, k, v_t)


def _attn_bwd(qs, qs_t, do, do_t, o, lse, k, k_t, v_t, name):
    seq = qs.shape[1]
    nk, bk = k.shape[1], k.shape[2]
    bq = min(BQ, seq)
    nq = seq // bq

    unroll = _unroll(nk, BWD_UNROLL)
    heads = range(HEADS_PER_STEP)
    pairs = GROUP // HEADS_PER_STEP

    def body(q_ref, qt_ref, do_ref, dot_ref, o_ref, lse_ref, k_ref, kt_ref, vt_ref, dq_ref, dkt_ref, dvt_ref,
             s_scr, dp_scr):
        @pl.when((pl.program_id(1) == 0) & (pl.program_id(2) == 0))
        def _():
            dkt_ref[...] = jnp.zeros_like(dkt_ref)
            dvt_ref[...] = jnp.zeros_like(dvt_ref)

        q, q_t, do_t_b = [q_ref[h] for h in heads], [qt_ref[h] for h in heads], [dot_ref[h] for h in heads]
        do_l = [do_ref[h] * LN2 for h in heads]
        do_b = [d.astype(BF16) for d in do_l]
        delta = [jnp.sum(do_l[h] * o_ref[h], axis=-1, keepdims=True) for h in heads]
        lse_col = [jnp.max(lse_ref[h], axis=-1, keepdims=True) for h in heads]

        def products(j, slot):
            ktj, vtj = kt_ref[j], vt_ref[j]
            for h in heads:
                s_scr[slot, h] = _dot(q[h], ktj)
                dp_scr[slot, h] = _dot(do_b[h], vtj)

        def gradients(j, slot, dq):
            kj = k_ref[j]
            dvt = jnp.zeros((HEAD_DIM, bk), F32)
            dkt = jnp.zeros((HEAD_DIM, bk), F32)
            new = []
            for h in heads:
                p = jnp.exp2(s_scr[slot, h] - lse_col[h])
                ds = (p * (dp_scr[slot, h] - delta[h])).astype(BF16)
                dvt = dvt + _dot(do_t_b[h], p.astype(BF16))
                dkt = dkt + _dot(q_t[h], ds)
                new.append(dq[h] + _dot(ds, kj))
            dvt_ref[j] += dvt
            dkt_ref[j] += dkt
            return tuple(new)

        def step(t, dq):
            for u in range(unroll):
                nxt = unroll * t + u + 1
                products(jnp.minimum(nxt, nk - 1) if u == unroll - 1 else nxt, (u + 1) % 2)
                dq = gradients(unroll * t + u, u % 2, dq)
            return dq

        products(0, 0)
        res = lax.fori_loop(0, nk // unroll, step, tuple(jnp.zeros((bq, HEAD_DIM), F32) for _ in heads))
        for h in heads:
            dq_ref[h] = res[h]

    first = lambda g, hh: g * pairs + hh
    row = pl.BlockSpec((HEADS_PER_STEP, bq, HEAD_DIM), lambda g, hh, i: (first(g, hh), i, 0))
    col = pl.BlockSpec((HEADS_PER_STEP, HEAD_DIM, bq), lambda g, hh, i: (first(g, hh), 0, i))
    kv_rows = pl.BlockSpec((None, nk, bk, HEAD_DIM), lambda g, hh, i: (g, 0, 0, 0))
    kv_cols = pl.BlockSpec((None, nk, HEAD_DIM, bk), lambda g, hh, i: (g, 0, 0, 0))
    return _pcall(
        body, name=name, grid=(KV_HEADS, pairs, nq),
        in_specs=[row, col, row, col, row,
                  pl.BlockSpec((HEADS_PER_STEP, bq, LANES), lambda g, hh, i: (first(g, hh), i, 0)),
                  kv_rows, kv_cols, kv_cols],
        out_specs=[row, kv_cols, kv_cols],
        out_shape=[jax.ShapeDtypeStruct((ATT_HEADS, seq, HEAD_DIM), F32),
                   jax.ShapeDtypeStruct((KV_HEADS, nk, HEAD_DIM, bk), F32),
                   jax.ShapeDtypeStruct((KV_HEADS, nk, HEAD_DIM, bk), F32)],
        scratch=[pltpu.VMEM((2, HEADS_PER_STEP, bq, bk), F32), pltpu.VMEM((2, HEADS_PER_STEP, bq, bk), F32)],
        sem=("parallel", "arbitrary", "arbitrary"), vmem_mb=ATTN_BWD_VMEM_MB)(
            qs, qs_t, do, do_t, o, lse, k, k_t, v_t)


def _halo_specs(t, col, n_tiles):
    per = t // HALO
    last = n_tiles * per - 1
    before = pl.BlockSpec((HALO, CONV_W), lambda i: (jnp.maximum(i * per - 1, 0), col))
    after = pl.BlockSpec((HALO, CONV_W), lambda i: (jnp.minimum((i + 1) * per, last), col))
    return before, after


def _glu(a, b):
    return a * _sigmoid(b)


def _conv_taps(ext_ref, w_ref, t, flip):
    acc = jnp.zeros((t, CONV_W), F32)
    for k in range(CONV_K):
        off = (HALO + CONV_PAD - k) if flip else (HALO - CONV_PAD + k)
        acc = acc + w_ref[k:k + 1, :] * ext_ref[pl.ds(off, t), :]
    return acc


def _fill_ext(ext_ref, before, tile, after, t, i, n_tiles):
    ext_ref[pl.ds(0, HALO), :] = jnp.where(i > 0, before, 0.0)
    ext_ref[pl.ds(HALO, t), :] = tile
    ext_ref[pl.ds(HALO + t, HALO), :] = jnp.where(i < n_tiles - 1, after, 0.0)


def _conv_fwd(proj, w, bias, ln_g, ln_b, name):
    seq = proj.shape[0]
    t = min(T_GROUP, seq)
    n_tiles = seq // t

    def body(a_ref, b_ref, ap_ref, bp_ref, an_ref, bn_ref, gate_ref, w_ref, bias_ref, g_ref, beta_ref, o_ref, y_ref,
             ext_ref):
        i = pl.program_id(0)
        _fill_ext(ext_ref, _glu(ap_ref[...], bp_ref[...]), _glu(a_ref[...], b_ref[...]),
                  _glu(an_ref[...], bn_ref[...]), t, i, n_tiles)
        y = _conv_taps(ext_ref, w_ref, t, False) + bias_ref[...]
        y_ref[...] = y
        mu = jnp.mean(y, axis=-1, keepdims=True)
        yc = y - mu
        rs = lax.rsqrt(jnp.mean(yc * yc, axis=-1, keepdims=True) + EPS)
        z = yc * rs * g_ref[...] + beta_ref[...]
        o_ref[...] = _silu_and_grad(z)[0] * _silu_and_grad(gate_ref[...])[0]

    tile = lambda c: pl.BlockSpec((t, CONV_W), lambda i: (i, c))
    ab, aa = _halo_specs(t, COL_A, n_tiles)
    bb, ba = _halo_specs(t, COL_B, n_tiles)
    vec = pl.BlockSpec((1, CONV_W), lambda i: (0, 0))
    return _pcall(
        body, name=name, grid=(n_tiles,),
        in_specs=[tile(COL_A), tile(COL_B), ab, bb, aa, ba, tile(COL_GCONV),
                  pl.BlockSpec((CONV_K, CONV_W), lambda i: (0, 0)), vec, vec, vec],
        out_specs=[pl.BlockSpec((t, CONV_W), lambda i: (i, 0))] * 2,
        out_shape=[jax.ShapeDtypeStruct((seq, CONV_W), F32)] * 2,
        scratch=[pltpu.VMEM((t + 2 * HALO, CONV_W), F32)], sem=("parallel",))(
            proj, proj, proj, proj, proj, proj, proj, w, bias, ln_g, ln_b)


def _conv_bwd_a(proj, y_conv, dcnv, ln_g, ln_b, name):
    seq = proj.shape[0]
    t = min(T_GROUP, seq)
    n_tiles = seq // t

    def body(a_ref, b_ref, ap_ref, bp_ref, an_ref, bn_ref, gate_ref, y_ref, d_ref, g_ref, beta_ref,
             dy_ref, dgate_ref, dw_ref, dbias_ref, dg_ref, dbeta_ref, ext_ref):
        i = pl.program_id(0)
        _fill_ext(ext_ref, _glu(ap_ref[...], bp_ref[...]), _glu(a_ref[...], b_ref[...]),
                  _glu(an_ref[...], bn_ref[...]), t, i, n_tiles)
        y = y_ref[...]
        mu = jnp.mean(y, axis=-1, keepdims=True)
        yc = y - mu
        rs = lax.rsqrt(jnp.mean(yc * yc, axis=-1, keepdims=True) + EPS)
        n = yc * rs
        z = n * g_ref[...] + beta_ref[...]
        act, dact = _silu_and_grad(z)
        gate, dgate = _silu_and_grad(gate_ref[...])
        d = d_ref[...]
        dgate_ref[...] = d * act * dgate
        dz = d * gate * dact
        dn = dz * g_ref[...]
        dy = rs * (dn - jnp.mean(dn, axis=-1, keepdims=True) - n * jnp.mean(dn * n, axis=-1, keepdims=True))
        dy_ref[...] = dy

        @pl.when(i == 0)
        def _():
            dw_ref[...] = jnp.zeros_like(dw_ref)
            dbias_ref[...] = jnp.zeros_like(dbias_ref)
            dg_ref[...] = jnp.zeros_like(dg_ref)
            dbeta_ref[...] = jnp.zeros_like(dbeta_ref)

        dg_ref[...] += jnp.sum(dz * n, axis=0, keepdims=True)
        dbeta_ref[...] += jnp.sum(dz, axis=0, keepdims=True)
        dbias_ref[...] += jnp.sum(dy, axis=0, keepdims=True)
        for k in range(CONV_K):
            dw_ref[k:k + 1, :] += jnp.sum(dy * ext_ref[pl.ds(HALO - CONV_PAD + k, t), :], axis=0, keepdims=True)

    tile = lambda c: pl.BlockSpec((t, CONV_W), lambda i: (i, c))
    own = pl.BlockSpec((t, CONV_W), lambda i: (i, 0))
    ab, aa = _halo_specs(t, COL_A, n_tiles)
    bb, ba = _halo_specs(t, COL_B, n_tiles)
    vec = pl.BlockSpec((1, CONV_W), lambda i: (0, 0))
    taps = pl.BlockSpec((CONV_K, CONV_W), lambda i: (0, 0))
    vshape = jax.ShapeDtypeStruct((1, CONV_W), F32)
    return _pcall(
        body, name=name, grid=(n_tiles,),
        in_specs=[tile(COL_A), tile(COL_B), ab, bb, aa, ba, tile(COL_GCONV), own, own, vec, vec],
        out_specs=[own, own, taps, vec, vec, vec],
        out_shape=[jax.ShapeDtypeStruct((seq, CONV_W), F32), jax.ShapeDtypeStruct((seq, CONV_W), F32),
                   jax.ShapeDtypeStruct((CONV_K, CONV_W), F32), vshape, vshape, vshape],
        scratch=[pltpu.VMEM((t + 2 * HALO, CONV_W), F32)], sem=("arbitrary",))(
            proj, proj, proj, proj, proj, proj, proj, y_conv, dcnv, ln_g, ln_b)


def _conv_bwd_b(proj, dy, w, name):
    seq = proj.shape[0]
    t = min(T_GROUP, seq)
    n_tiles = seq // t

    def body(a_ref, b_ref, dy_ref, dyp_ref, dyn_ref, w_ref, da_ref, db_ref, ext_ref):
        i = pl.program_id(0)
        _fill_ext(ext_ref, dyp_ref[...], dy_ref[...], dyn_ref[...], t, i, n_tiles)
        dh = _conv_taps(ext_ref, w_ref, t, True)
        sig = _sigmoid(b_ref[...])
        da_ref[...] = dh * sig
        db_ref[...] = dh * a_ref[...] * sig * (1.0 - sig)

    tile = lambda c: pl.BlockSpec((t, CONV_W), lambda i: (i, c))
    own = pl.BlockSpec((t, CONV_W), lambda i: (i, 0))
    before, after = _halo_specs(t, 0, n_tiles)
    return _pcall(
        body, name=name, grid=(n_tiles,),
        in_specs=[tile(COL_A), tile(COL_B), own, before, after, pl.BlockSpec((CONV_K, CONV_W), lambda i: (0, 0))],
        out_specs=[own, own], out_shape=[jax.ShapeDtypeStruct((seq, CONV_W), F32)] * 2,
        scratch=[pltpu.VMEM((t + 2 * HALO, CONV_W), F32)], sem=("parallel",))(proj, proj, dy, dy, dy, w)


def _head_masks():
    lane_head = lax.broadcasted_iota(jnp.int32, (SG_CHUNK, SG_W), 1) // HEAD_DIM
    return [lane_head == h for h in range(SG_HEADS)]


def _sg_mix(mats_ref, rhs, masks):
    out = jnp.zeros((SG_CHUNK, SG_W), F32)
    for h in range(SG_HEADS):
        out = out + jnp.where(masks[h], _dot(mats_ref[h], rhs), 0.0)
    return out


def _sg_specs(seq):
    t = min(T_GROUP, seq)
    tile = lambda c: pl.BlockSpec((t, SG_W), lambda i: (i, c))
    own = pl.BlockSpec((t, SG_W), lambda i: (i, 0))
    vec = pl.BlockSpec((1, SG_W), lambda i: (0, 0))
    mats = pl.BlockSpec((SG_HEADS, SG_CHUNK, SG_CHUNK), lambda i: (0, 0, 0))
    full = pl.BlockSpec((SG_CHUNK, SG_W), lambda i: (0, 0))
    return t, tile, own, vec, mats, full


def _sg_fwd(proj, ln_g, ln_b, w_b, bias_full, name):
    seq = proj.shape[0]
    t, tile, own, vec, mats, full = _sg_specs(seq)

    def body(u_ref, v_ref, gate_ref, g_ref, beta_ref, w_ref, bias_ref, o_ref):
        masks = _head_masks()
        for c in range(t // SG_CHUNK):
            rows = pl.ds(c * SG_CHUNK, SG_CHUNK)
            vg = _gelu_and_grad(v_ref[rows, :])[0]
            mu = jnp.mean(vg, axis=-1, keepdims=True)
            vc = vg - mu
            rs = lax.rsqrt(jnp.mean(vc * vc, axis=-1, keepdims=True) + EPS)
            vln = vc * rs * g_ref[...] + beta_ref[...]
            mixed = _sg_mix(w_ref, vln.astype(BF16), masks) + bias_ref[...]
            o_ref[rows, :] = _gelu_and_grad(u_ref[rows, :])[0] * mixed * _silu_and_grad(gate_ref[rows, :])[0]

    return _pcall(body, name=name, grid=(seq // t,),
                  in_specs=[tile(COL_U), tile(COL_VSG), tile(COL_GSG), vec, vec, mats, full], out_specs=own,
                  out_shape=jax.ShapeDtypeStruct((seq, SG_W), F32), sem=("parallel",))(
                      proj, proj, proj, ln_g, ln_b, w_b, bias_full)


def _sg_bwd(proj, dsg, ln_g, ln_b, w_b, w_t_b, bias_full, fold, name):
    seq = proj.shape[0]
    t, tile, own, vec, mats, full = _sg_specs(seq)
    n_tiles = seq // t

    def body(u_ref, v_ref, gate_ref, d_ref, g_ref, beta_ref, w_ref, wt_ref, bias_ref, fold_ref,
             du_ref, dv_ref, dgate_ref, dg_ref, dbeta_ref, dw_ref, db_ref, dbias_acc):
        i = pl.program_id(0)

        @pl.when(i == 0)
        def _():
            dg_ref[...] = jnp.zeros_like(dg_ref)
            dbeta_ref[...] = jnp.zeros_like(dbeta_ref)
            dw_ref[...] = jnp.zeros_like(dw_ref)
            dbias_acc[...] = jnp.zeros_like(dbias_acc)

        masks = _head_masks()
        for c in range(t // SG_CHUNK):
            rows = pl.ds(c * SG_CHUNK, SG_CHUNK)
            ug, dug = _gelu_and_grad(u_ref[rows, :])
            vg, dvg = _gelu_and_grad(v_ref[rows, :])
            mu = jnp.mean(vg, axis=-1, keepdims=True)
            vc = vg - mu
            rs = lax.rsqrt(jnp.mean(vc * vc, axis=-1, keepdims=True) + EPS)
            vn = vc * rs
            vln_b = (vn * g_ref[...] + beta_ref[...]).astype(BF16)
            mixed = _sg_mix(w_ref, vln_b, masks) + bias_ref[...]
            gate, dgate = _silu_and_grad(gate_ref[rows, :])
            d = d_ref[rows, :]
            dgate_ref[rows, :] = d * ug * mixed * dgate
            du_ref[rows, :] = d * mixed * gate * dug
            dmixed = d * ug * gate
            dbias_acc[...] += dmixed
            dmixed_b = dmixed.astype(BF16)
            for h in range(SG_HEADS):
                dm_h = jnp.where(masks[h], dmixed_b, jnp.zeros_like(dmixed_b))
                dw_ref[h] += lax.dot_general(dm_h, vln_b, (((1,), (1,)), ((), ())), preferred_element_type=F32)
            dvln = _sg_mix(wt_ref, dmixed_b, masks)
            dg_ref[...] += jnp.sum(dvln * vn, axis=0, keepdims=True)
            dbeta_ref[...] += jnp.sum(dvln, axis=0, keepdims=True)
            dvn = dvln * g_ref[...]
            dvgelu = rs * (dvn - jnp.mean(dvn, axis=-1, keepdims=True) - vn * jnp.mean(dvn * vn, axis=-1, keepdims=True))
            dv_ref[rows, :] = dvgelu * dvg

        @pl.when(i == n_tiles - 1)
        def _():
            db_ref[...] = _split_dot(dbias_acc[...], fold_ref[...])

    sq = pl.BlockSpec((SG_CHUNK, SG_CHUNK), lambda i: (0, 0))
    vshape = jax.ShapeDtypeStruct((1, SG_W), F32)
    return _pcall(
        body, name=name, grid=(n_tiles,),
        in_specs=[tile(COL_U), tile(COL_VSG), tile(COL_GSG), own, vec, vec, mats, mats, full,
                  pl.BlockSpec((SG_W, SG_CHUNK), lambda i: (0, 0))],
        out_specs=[own, own, own, vec, vec, mats, sq],
        out_shape=[jax.ShapeDtypeStruct((seq, SG_W), F32)] * 3 + [
            vshape, vshape, jax.ShapeDtypeStruct((SG_HEADS, SG_CHUNK, SG_CHUNK), F32),
            jax.ShapeDtypeStruct((SG_CHUNK, SG_CHUNK), F32)],
        scratch=[pltpu.VMEM((SG_CHUNK, SG_W), F32)], sem=("arbitrary",))(
            proj, proj, proj, dsg, ln_g, ln_b, w_b, w_t_b, bias_full, fold)


def _out_fwd(att, proj, cnv, sgu, x, w, gain, name):
    seq = x.shape[0]

    def body(att_ref, g0_ref, g1_ref, cnv_ref, sgu_ref, x_ref, w_ref, gain_ref, xo_ref, mix_ref, cat_ref):
        gate = jnp.concatenate([_silu_and_grad(g0_ref[...])[0], _silu_and_grad(g1_ref[...])[0]], axis=-1)
        cat_ref[:, 0:ATT_W] = (att_ref[...] * gate).astype(BF16)
        cat_ref[:, ATT_W:ATT_W + CONV_W] = cnv_ref[...].astype(BF16)
        cat_ref[:, ATT_W + CONV_W:] = sgu_ref[...].astype(BF16)
        mix = _dot(cat_ref[...], w_ref[...])
        mix_ref[...] = mix
        r = lax.rsqrt(jnp.mean(mix * mix, axis=-1, keepdims=True) + EPS)
        xo_ref[...] = x_ref[...] + mix * r * gain_ref[...]

    row = lambda w_: pl.BlockSpec((T_ROW, w_), lambda i: (i, 0))
    gate_blk = lambda c: pl.BlockSpec((T_ROW, 256), lambda i: (i, c))
    return _pcall(
        body, name=name, grid=(seq // T_ROW,),
        in_specs=[row(ATT_W), gate_blk(COL_GATT), gate_blk(COL_GATT + 1), row(CONV_W), row(SG_W), row(D_MODEL),
                  pl.BlockSpec((D_MODEL, D_MODEL), lambda i: (0, 0)), pl.BlockSpec((1, D_MODEL), lambda i: (0, 0))],
        out_specs=[row(D_MODEL), row(D_MODEL), row(D_MODEL)],
        out_shape=[jax.ShapeDtypeStruct((seq, D_MODEL), F32), jax.ShapeDtypeStruct((seq, D_MODEL), F32),
                   jax.ShapeDtypeStruct((seq, D_MODEL), BF16)],
        sem=("parallel",), vmem_mb=VMEM_MB)(att, proj, proj, cnv, sgu, x, w, gain)


def _out_bwd(dxo, mix, gain, w_t, att, proj, name):
    seq = dxo.shape[0]

    def body(dxo_ref, mix_ref, gain_ref, w_ref, att_ref, g0_ref, g1_ref,
             dmix_ref, datt_ref, dgatt_ref, dcnv_ref, dsgu_ref, dgain_ref):
        mix = mix_ref[...]
        r = lax.rsqrt(jnp.mean(mix * mix, axis=-1, keepdims=True) + EPS)
        n = mix * r
        dout = dxo_ref[...]
        dn = dout * gain_ref[...]
        dmix = (r * (dn - n * jnp.mean(dn * n, axis=-1, keepdims=True))).astype(BF16)
        dmix_ref[...] = dmix

        @pl.when(pl.program_id(0) == 0)
        def _():
            dgain_ref[...] = jnp.zeros_like(dgain_ref)

        dgain_ref[...] += jnp.sum(dout * n, axis=0, keepdims=True)
        dcat = _dot(dmix, w_ref[...])
        g0, dg0 = _silu_and_grad(g0_ref[...])
        g1, dg1 = _silu_and_grad(g1_ref[...])
        gate = jnp.concatenate([g0, g1], axis=-1)
        dgate = jnp.concatenate([dg0, dg1], axis=-1)
        dca = dcat[:, 0:ATT_W]
        datt_ref[...] = dca * gate
        dgatt_ref[...] = dca * att_ref[...] * dgate
        dcnv_ref[...] = dcat[:, ATT_W:ATT_W + CONV_W]
        dsgu_ref[...] = dcat[:, ATT_W + CONV_W:]

    row = lambda w_: pl.BlockSpec((T_ROW, w_), lambda i: (i, 0))
    gate_blk = lambda c: pl.BlockSpec((T_ROW, 256), lambda i: (i, c))
    vec = pl.BlockSpec((1, D_MODEL), lambda i: (0, 0))
    return _pcall(
        body, name=name, grid=(seq // T_ROW,),
        in_specs=[row(D_MODEL), row(D_MODEL), vec, pl.BlockSpec((D_MODEL, D_MODEL), lambda i: (0, 0)), row(ATT_W),
                  gate_blk(COL_GATT), gate_blk(COL_GATT + 1)],
        out_specs=[row(D_MODEL), row(ATT_W), row(ATT_W), row(CONV_W), row(SG_W), vec],
        out_shape=[jax.ShapeDtypeStruct((seq, D_MODEL), BF16), jax.ShapeDtypeStruct((seq, ATT_W), F32),
                   jax.ShapeDtypeStruct((seq, ATT_W), F32), jax.ShapeDtypeStruct((seq, CONV_W), F32),
                   jax.ShapeDtypeStruct((seq, SG_W), F32), jax.ShapeDtypeStruct((1, D_MODEL), F32)],
        sem=("arbitrary",), vmem_mb=VMEM_MB)(dxo, mix, gain, w_t, att, proj, proj)


def _loss_head(y, target, name):
    seq = y.shape[0]
    t = min(T_GROUP, seq)

    def body(y_ref, t_ref, sse_ref, dy_ref):
        err = y_ref[...] - t_ref[...]
        dy_ref[...] = err * (1.0 / D_MODEL)

        @pl.when(pl.program_id(0) == 0)
        def _():
            sse_ref[...] = jnp.zeros_like(sse_ref)

        part = jnp.sum(jnp.sum(err * err, axis=0, keepdims=True), axis=-1, keepdims=True)
        sse_ref[...] += jnp.broadcast_to(part, (1, LANES))

    row = pl.BlockSpec((t, D_MODEL), lambda i: (i, 0))
    return _pcall(body, name=name, grid=(seq // t,), in_specs=[row, row],
                  out_specs=[pl.BlockSpec((1, LANES), lambda i: (0, 0)), row],
                  out_shape=[jax.ShapeDtypeStruct((1, LANES), F32), jax.ShapeDtypeStruct((seq, D_MODEL), F32)],
                  sem=("arbitrary",))(y, target)


def _to_heads(a, heads):
    return a.reshape(a.shape[0], heads, HEAD_DIM).transpose(1, 0, 2)


def _row_blocks(a, bk):
    return a.reshape(a.shape[0], a.shape[1] // bk, bk, HEAD_DIM)


def _lane_blocks(a, bk):
    return a.reshape(a.shape[0], HEAD_DIM, a.shape[2] // bk, bk).transpose(0, 2, 1, 3)


def _from_lane_blocks(a):
    return a.transpose(0, 2, 1, 3).reshape(a.shape[0], HEAD_DIM, a.shape[1] * a.shape[3])


def _flat_rows(a, rows):
    flat = a.reshape(-1)
    return jnp.pad(flat, (0, rows * LANES - flat.shape[0])).reshape(rows, LANES)


SHARD_ROWS = {"w_in": 2 * D_MODEL * (D_IN // N_DEV) // LANES, "w_out": 2 * (D_MODEL // N_DEV) * D_MODEL // LANES,
              "conv_dw": 16}
REPL_SHAPES = [("pre_norm", (2, D_MODEL)), ("post_norm", (2, D_MODEL)), ("q_norm", (2, HEAD_DIM)),
               ("k_norm", (2, HEAD_DIM)), ("conv_dw_b", (2, CONV_W)), ("conv_ln_g", (2, CONV_W)),
               ("conv_ln_b", (2, CONV_W)), ("sg_ln_g", (2, SG_W)), ("sg_ln_b", (2, SG_W)),
               ("sg_w", (2, SG_HEADS, SG_CHUNK, SG_CHUNK)), ("sg_b", (2, SG_HEADS, SG_CHUNK))]
REPL_ROWS = 1088
WEIGHT_ORDER = ["pre_norm", "post_norm", "w_in", "w_out", "q_norm", "k_norm", "conv_dw", "conv_dw_b", "conv_ln_g",
                "conv_ln_b", "sg_ln_g", "sg_ln_b", "sg_w", "sg_b"]


def _pack_shard(parts):
    return jnp.concatenate([_flat_rows(parts[k], SHARD_ROWS[k]) for k in ("w_in", "w_out", "conv_dw")], axis=0)


def _unpack_shard(flat, shapes):
    out, at = {}, 0
    for k in ("w_in", "w_out", "conv_dw"):
        size = math.prod(shapes[k])
        out[k] = flat[at:at + SHARD_ROWS[k]].reshape(-1)[:size].reshape(shapes[k])
        at += SHARD_ROWS[k]
    return out


REPL_USED = sum(math.prod(shape) for _, shape in REPL_SHAPES)


def _pack_repl(parts, extra=None):
    tail = [] if extra is None else [extra.reshape(1)]
    flat = jnp.concatenate([parts[k].reshape(-1) for k, _ in REPL_SHAPES] + tail)
    return jnp.pad(flat, (0, REPL_ROWS * LANES - flat.shape[0])).reshape(REPL_ROWS, LANES)


def _unpack_repl(flat):
    out, at, flat = {}, 0, flat.reshape(-1)
    for k, shape in REPL_SHAPES:
        size = math.prod(shape)
        out[k] = flat[at:at + size].reshape(shape)
        at += size
    return out


def kernel(x, pre_norm, post_norm, w_in, w_out, q_norm, k_norm, conv_dw, conv_dw_b, conv_ln_g, conv_ln_b, sg_ln_g, sg_ln_b, sg_w, sg_b, loss_target, m_pre_norm, m_post_norm, m_w_in, m_w_out, m_q_norm, m_k_norm, m_conv_dw, m_conv_dw_b, m_conv_ln_g, m_conv_ln_b, m_sg_ln_g, m_sg_ln_b, m_sg_w, m_sg_b, v_pre_norm, v_post_norm, v_w_in, v_w_out, v_q_norm, v_k_norm, v_conv_dw, v_conv_dw_b, v_conv_ln_g, v_conv_ln_b, v_sg_ln_g, v_sg_ln_b, v_sg_w, v_sg_b):
    weights = dict(pre_norm=pre_norm, post_norm=post_norm, w_in=w_in, w_out=w_out, q_norm=q_norm, k_norm=k_norm,
                   conv_dw=conv_dw, conv_dw_b=conv_dw_b, conv_ln_g=conv_ln_g, conv_ln_b=conv_ln_b, sg_ln_g=sg_ln_g,
                   sg_ln_b=sg_ln_b, sg_w=sg_w, sg_b=sg_b)
    mom_m = dict(pre_norm=m_pre_norm, post_norm=m_post_norm, w_in=m_w_in, w_out=m_w_out, q_norm=m_q_norm,
                 k_norm=m_k_norm, conv_dw=m_conv_dw, conv_dw_b=m_conv_dw_b, conv_ln_g=m_conv_ln_g,
                 conv_ln_b=m_conv_ln_b, sg_ln_g=m_sg_ln_g, sg_ln_b=m_sg_ln_b, sg_w=m_sg_w, sg_b=m_sg_b)
    mom_v = dict(pre_norm=v_pre_norm, post_norm=v_post_norm, w_in=v_w_in, w_out=v_w_out, q_norm=v_q_norm,
                 k_norm=v_k_norm, conv_dw=v_conv_dw, conv_dw_b=v_conv_dw_b, conv_ln_g=v_conv_ln_g,
                 conv_ln_b=v_conv_ln_b, sg_ln_g=v_sg_ln_g, sg_ln_b=v_sg_ln_b, sg_w=v_sg_w, sg_b=v_sg_b)
    depth = pre_norm.shape[0]
    seq = x.shape[1]
    bk = min(BK, seq)
    x0 = x.reshape(seq, D_MODEL)
    target = loss_target.reshape(seq, D_MODEL)

    w_in_all, w_out_all, dw_all = _exchange(
        [], [w_in.astype(BF16), w_out.astype(BF16), jnp.pad(conv_dw, ((0, 0), (0, 1), (0, 0)))], "gather_weights")
    w_in_full = w_in_all.transpose(1, 2, 0, 3).reshape(depth, D_MODEL, D_IN)
    w_out_full = w_out_all.transpose(1, 0, 2, 3).reshape(depth, D_MODEL, D_MODEL)
    dw_full = dw_all[:, :, :CONV_K, :].transpose(1, 2, 0, 3).reshape(depth, CONV_K, CONV_W)

    cos, sin = _rope_tables(seq)
    lane = jnp.arange(SG_W)
    fold = (lane[:, None] // HEAD_DIM == jnp.arange(SG_CHUNK)[None, :]).astype(BF16)

    def layer_consts(l):
        return dict(
            q_gain=q_norm[l].reshape(HEAD_DIM, 1), k_gain=k_norm[l].reshape(HEAD_DIM, 1), sg_w_b=sg_w[l].astype(BF16), sg_wt_b=sg_w[l].transpose(0, 2, 1).astype(BF16),
            sg_bias=jnp.repeat(sg_b[l].T, HEAD_DIM, axis=1),
            vec=lambda a: a[l].reshape(1, -1))

    saved = []
    xc = x0
    for l in range(depth):
        c = layer_consts(l)
        proj, hb, qkv_t = _proj_fwd(xc, c["vec"](pre_norm), w_in_full[l], f"proj_fwd_{l}")
        qkv_t = qkv_t.reshape(QKV_HEADS, HEAD_DIM, seq)
        qs_t, qs = _qk_prep_fwd(qkv_t, 0, ATT_HEADS, c["q_gain"], Q_SCALE, cos, sin, f"q_prep_fwd_{l}")
        kr_t, kr = _qk_prep_fwd(qkv_t, ATT_HEADS, KV_HEADS, c["k_gain"], 1.0, cos, sin, f"k_prep_fwd_{l}")
        v_t = qkv_t[ATT_HEADS + KV_HEADS:].astype(BF16)
        k_rows, k_cols, v_cols = _row_blocks(kr, bk), _lane_blocks(kr_t, bk), _lane_blocks(v_t, bk)
        v_ext = jnp.concatenate([v_cols, jnp.ones_like(v_cols[:, :, :1]), jnp.zeros_like(v_cols[:, :, :7])], axis=2)
        o_t, lse = _attn_fwd(qs_t, k_rows, v_ext, f"attn_fwd_{l}")
        att = o_t.reshape(ATT_W, seq).T
        o = o_t.transpose(0, 2, 1)
        lse = jnp.broadcast_to(lse.reshape(ATT_HEADS, seq, 1), (ATT_HEADS, seq, LANES))
        cnv, y_conv = _conv_fwd(proj, dw_full[l], c["vec"](conv_dw_b), c["vec"](conv_ln_g), c["vec"](conv_ln_b),
                                f"conv_fwd_{l}")
        sgu = _sg_fwd(proj, c["vec"](sg_ln_g), c["vec"](sg_ln_b), c["sg_w_b"], c["sg_bias"], f"sg_fwd_{l}")
        x_new, mix, cat_b = _out_fwd(att, proj, cnv, sgu, xc, w_out_full[l], c["vec"](post_norm), f"out_fwd_{l}")
        saved.append(dict(x=xc, proj=proj, hb=hb, qkv_t=qkv_t, qs=qs, qs_t=qs_t, k_rows=k_rows,
                          k_cols=k_cols, v_cols=v_cols, o=o, lse=lse, att=att, mix=mix, cat_b=cat_b, y_conv=y_conv))
        xc = x_new

    sse, dx = _loss_head(xc, target, "loss_head")

    grads = {k: [None] * depth for k in WEIGHT_ORDER}
    for l in reversed(range(depth)):
        c, s = layer_consts(l), saved[l]
        dmix_b, datt, dgatt, dcnv, dsgu, g_post = _out_bwd(
            dx, s["mix"], c["vec"](post_norm), w_out_full[l].T, s["att"], s["proj"], f"out_bwd_{l}")
        grads["post_norm"][l] = g_post.reshape(-1)
        grads["w_out"][l] = _matmul_acc(s["cat_b"], dmix_b, D_MODEL, f"grad_w_out_{l}")
        do = _to_heads(datt, ATT_HEADS)
        dqs, dkt, dvt = _attn_bwd(s["qs"], s["qs_t"], do, do.astype(BF16).transpose(0, 2, 1),
                                  s["o"], s["lse"], s["k_rows"], s["k_cols"], s["v_cols"], f"attn_bwd_{l}")
        d_q_t, g_qgain = _qk_prep_bwd(s["qkv_t"], 0, dqs, False, c["q_gain"], Q_SCALE, cos, sin, f"q_prep_bwd_{l}")
        d_k_t, g_kgain = _qk_prep_bwd(s["qkv_t"], ATT_HEADS, _from_lane_blocks(dkt), True, c["k_gain"], 1.0, cos, sin,
                                      f"k_prep_bwd_{l}")
        grads["q_norm"][l] = jnp.sum(g_qgain[:, :, 0], axis=0)
        grads["k_norm"][l] = jnp.sum(g_kgain[:, :, 0], axis=0)
        d_qkv = jnp.concatenate([d_q_t, d_k_t, _from_lane_blocks(dvt)], axis=0).reshape(QKV_W, seq).T
        dy_conv, dg_conv, g_dw, g_dwb, g_clg, g_clb = _conv_bwd_a(
            s["proj"], s["y_conv"], dcnv, c["vec"](conv_ln_g), c["vec"](conv_ln_b), f"conv_bwd_a_{l}")
        da, db = _conv_bwd_b(s["proj"], dy_conv, dw_full[l], f"conv_bwd_b_{l}")
        grads["conv_dw"][l], grads["conv_dw_b"][l] = g_dw, g_dwb.reshape(-1)
        grads["conv_ln_g"][l], grads["conv_ln_b"][l] = g_clg.reshape(-1), g_clb.reshape(-1)
        du, dv_sg, dg_sg, g_slg, g_slb, g_sw, g_sb = _sg_bwd(
            s["proj"], dsgu, c["vec"](sg_ln_g), c["vec"](sg_ln_b), c["sg_w_b"], c["sg_wt_b"], c["sg_bias"], fold,
            f"sg_bwd_{l}")
        grads["sg_ln_g"][l], grads["sg_ln_b"][l] = g_slg.reshape(-1), g_slb.reshape(-1)
        grads["sg_w"][l], grads["sg_b"][l] = g_sw, g_sb[:, :SG_HEADS].T
        dproj = jnp.concatenate([d_qkv, dgatt, da, db, dg_conv, du, dv_sg, dg_sg], axis=-1).astype(BF16)
        grads["w_in"][l] = _matmul_acc(s["hb"], dproj, D_IN // 2, f"grad_w_in_{l}")
        dx, g_pre = _proj_bwd(dproj, w_in_full[l].T, s["x"], c["vec"](pre_norm), dx, f"proj_bwd_{l}")
        grads["pre_norm"][l] = g_pre.reshape(-1)
    grad_x = dx.reshape(x.shape)
    grads = {k: jnp.stack(v) for k, v in grads.items()}

    shard_blocks = dict(
        w_in=grads["w_in"].reshape(depth, D_MODEL, N_DEV, D_IN // N_DEV).transpose(2, 0, 1, 3),
        w_out=grads["w_out"].reshape(depth, N_DEV, D_MODEL // N_DEV, D_MODEL).transpose(1, 0, 2, 3),
        conv_dw=grads["conv_dw"].reshape(depth, CONV_K, N_DEV, CONV_W // N_DEV).transpose(2, 0, 1, 3))
    scatter_src = jnp.stack([_pack_shard({k: a[d] for k, a in shard_blocks.items()})
                             for d in range(N_DEV)]).astype(BF16)
    shard_slots, repl_slots = _exchange([scatter_src], [_pack_repl(grads, sse[0, 0])], "exchange_grads")

    shard_shapes = {k: weights[k].shape for k in SHARD_ROWS}
    gs, ds_, ms, vs = _sum_adamw(shard_slots, _pack_shard(weights), _pack_shard(mom_m), _pack_shard(mom_v),
                                 "adamw_sharded")
    gr, dr, mr, vr = _sum_adamw(repl_slots, _pack_repl(weights), _pack_repl(mom_m), _pack_repl(mom_v),
                                "adamw_replicated")
    loss = gr.reshape(-1)[REPL_USED] * (0.5 / D_MODEL)
    results = []
    for shard_flat, repl_flat in ((gs, gr), (ds_, dr), (ms, mr), (vs, vr)):
        parts = {**_unpack_shard(shard_flat, shard_shapes), **_unpack_repl(repl_flat)}
        results.append([parts[k] for k in WEIGHT_ORDER])
    return (loss, grad_x, *results[0], *results[1], *results[2], *results[3])
```

```python
import math

import jax
import jax.numpy as jnp
from jax import lax
from jax.experimental import pallas as pl
from jax.experimental.pallas import tpu as pltpu

F32, BF16 = jnp.float32, jnp.bfloat16

N_DEV = 8
MESH_AXES = ("x", "y", "c")
EPS = 1e-6
D_MODEL = 1024
HEAD_DIM = 64
ATT_HEADS, KV_HEADS = 8, 2
QKV_HEADS = ATT_HEADS + 2 * KV_HEADS
QKV_W = QKV_HEADS * HEAD_DIM
GROUP = ATT_HEADS // KV_HEADS
ATT_W, KV_W, CONV_W, SG_W = 512, 128, 256, 256
CONV_K, CONV_PAD, HALO = 31, 15, 16
SG_HEADS, SG_CHUNK = 4, 128
D_IN = 2816
GRID_W = 64
ROPE_THETA = 10000.0
LOG2E, LN2 = math.log2(math.e), math.log(2.0)
Q_SCALE = HEAD_DIM ** -0.5 * LOG2E
LANES = 128

COL_GATT, COL_A, COL_B, COL_GCONV, COL_U, COL_VSG, COL_GSG = 3, 5, 6, 7, 8, 9, 10

ADAM_LR, ADAM_B1, ADAM_B2, ADAM_EPS, ADAM_WD, ADAM_STEP = 0.001, 0.9, 0.999, 1e-08, 0.01, 10

T_ROW = 256
T_PREP = 2048
T_GROUP = 512
BQ, BK = 512, 512
HEADS_PER_STEP = 2
FWD_UNROLL, BWD_UNROLL = 8, 4
VMEM_MB = 56
ATTN_BWD_VMEM_MB = 58


def _pcall(body, *, name, grid, in_specs, out_specs, out_shape, scratch=(), sem=None, vmem_mb=None):
    params = {}
    if sem is not None:
        params["dimension_semantics"] = sem
    if vmem_mb is not None:
        params["vmem_limit_bytes"] = vmem_mb << 20
    return pl.pallas_call(body, name=name, grid=grid, in_specs=in_specs, out_specs=out_specs, out_shape=out_shape,
                          scratch_shapes=list(scratch), compiler_params=pltpu.CompilerParams(**params))


def _dot(a, b):
    return jnp.dot(a, b, preferred_element_type=F32)


def _sigmoid(x):
    return 1.0 / (1.0 + jnp.exp(-x))


def _silu_and_grad(x):
    s = _sigmoid(x)
    return x * s, s * (1.0 + x * (1.0 - s))


def _gelu_and_grad(x):
    cdf = 0.5 * (1.0 + lax.erf(x * (1.0 / math.sqrt(2.0))))
    pdf = jnp.exp(-0.5 * x * x) * (1.0 / math.sqrt(2.0 * math.pi))
    return x * cdf, cdf + x * pdf


def _split_dot(y, mat):
    hi = y.astype(BF16)
    lo = (y - hi.astype(F32)).astype(BF16)
    return _dot(hi, mat) + _dot(lo, mat)


def _row_tile(rows, cap):
    best = 8
    for t in range(8, min(rows, cap) + 1, 8):
        if rows % t == 0:
            best = t
    return best


def _exchange(scatter, gather, name):
    n_s = len(scatter)
    arrs = list(scatter) + list(gather)
    n = len(arrs)
    flips = [(fx, fy, fc) for fx in (0, 1) for fy in (0, 1) for fc in (0, 1)][1:]
    n_peer = len(flips)

    def body(*refs):
        ins, outs = refs[:n], refs[n:2 * n]
        send_sems, recv_sems, local_sems = refs[2 * n:]
        pos = tuple(lax.axis_index(a) for a in MESH_AXES)

        def peer(flip):
            return tuple((1 - p) if f else p for p, f in zip(pos, flip))

        def slot(p):
            return 4 * p[0] + 2 * p[1] + p[2]

        def src(a, p):
            return ins[a].at[slot(p)] if a < n_s else ins[a]

        def remote(a, k, src_ref, dst_slot, to):
            return pltpu.make_async_remote_copy(
                src_ref=src_ref, dst_ref=outs[a].at[dst_slot], send_sem=send_sems.at[a * n_peer + k],
                recv_sem=recv_sems.at[a * n_peer + k], device_id=to, device_id_type=pl.DeviceIdType.MESH)

        local = [pltpu.make_async_copy(src(a, pos), outs[a].at[slot(pos)], local_sems.at[a]) for a in range(n)]
        for cp in local:
            cp.start()
        sends = [remote(a, k, src(a, peer(f)), slot(pos), peer(f)) for a in range(n) for k, f in enumerate(flips)]
        for cp in sends:
            cp.start()
        for a in range(n):
            for k, f in enumerate(flips):
                remote(a, k, src(a, peer(f)), slot(peer(f)), peer(f)).wait_recv()
        for cp in sends:
            cp.wait_send()
        for cp in local:
            cp.wait()

    out_shape = [jax.ShapeDtypeStruct((N_DEV,) + (a.shape[1:] if i < n_s else a.shape), a.dtype)
                 for i, a in enumerate(arrs)]
    any_spec = pl.BlockSpec(memory_space=pl.ANY)
    return pl.pallas_call(
        body, name=name, out_shape=out_shape, in_specs=[any_spec] * n, out_specs=[any_spec] * n,
        scratch_shapes=[pltpu.SemaphoreType.DMA((n * n_peer,)), pltpu.SemaphoreType.DMA((n * n_peer,)),
                        pltpu.SemaphoreType.DMA((n,))],
    )(*arrs)


def _sum_adamw(slots, w, m, v, name):
    rows = w.shape[0]
    tr = _row_tile(rows, 1024)
    c1 = 1.0 - ADAM_B1 ** ADAM_STEP
    c2 = 1.0 - ADAM_B2 ** ADAM_STEP

    def body(s_ref, w_ref, m_ref, v_ref, g_out, d_out, m_out, v_out):
        g = s_ref[0].astype(F32)
        for d in range(1, N_DEV):
            g = g + s_ref[d].astype(F32)
        m_new = ADAM_B1 * m_ref[...] + (1.0 - ADAM_B1) * g
        v_new = ADAM_B2 * v_ref[...] + (1.0 - ADAM_B2) * (g * g)
        m_hat = m_new / c1
        v_hat = v_new / c2
        g_out[...] = g
        d_out[...] = -ADAM_LR * (m_hat / (jnp.sqrt(v_hat) + ADAM_EPS) + ADAM_WD * w_ref[...])
        m_out[...] = m_new
        v_out[...] = v_new

    flat = pl.BlockSpec((tr, LANES), lambda i: (i, 0))
    return _pcall(
        body, name=name, grid=(rows // tr,),
        in_specs=[pl.BlockSpec((N_DEV, tr, LANES), lambda i: (0, i, 0)), flat, flat, flat],
        out_specs=[flat] * 4, out_shape=[jax.ShapeDtypeStruct((rows, LANES), F32)] * 4,
        sem=("parallel",), vmem_mb=VMEM_MB)(slots, w, m, v)


def _proj_fwd(x, gain, w, name):
    seq = x.shape[0]

    def body(x_ref, g_ref, w_ref, proj_ref, hb_ref, qkv_t_ref):
        xf = x_ref[...]
        r = lax.rsqrt(jnp.mean(xf * xf, axis=-1, keepdims=True) + EPS)
        h = (xf * r * g_ref[...]).astype(BF16)
        hb_ref[...] = h
        proj = _dot(h, w_ref[...])
        proj_ref[...] = proj
        qkv_t_ref[...] = proj[:, :QKV_W].T

    return _pcall(
        body, name=name, grid=(seq // T_ROW,),
        in_specs=[pl.BlockSpec((T_ROW, D_MODEL), lambda i: (i, 0)), pl.BlockSpec((1, D_MODEL), lambda i: (0, 0)),
                  pl.BlockSpec((D_MODEL, D_IN), lambda i: (0, 0))],
        out_specs=[pl.BlockSpec((T_ROW, D_IN), lambda i: (i, 0)), pl.BlockSpec((T_ROW, D_MODEL), lambda i: (i, 0)),
                   pl.BlockSpec((QKV_W, T_ROW), lambda i: (0, i))],
        out_shape=[jax.ShapeDtypeStruct((seq, D_IN), F32), jax.ShapeDtypeStruct((seq, D_MODEL), BF16),
                   jax.ShapeDtypeStruct((QKV_W, seq), F32)],
        sem=("parallel",), vmem_mb=VMEM_MB)(x, gain, w)


def _proj_bwd(dproj, w_t, x, gain, dxo, name):
    seq = x.shape[0]

    def body(dp_ref, w_ref, x_ref, g_ref, dxo_ref, dx_ref, dg_ref):
        dh = _dot(dp_ref[...], w_ref[...])
        xf = x_ref[...]
        r = lax.rsqrt(jnp.mean(xf * xf, axis=-1, keepdims=True) + EPS)
        n = xf * r
        dn = dh * g_ref[...]
        dx_ref[...] = dxo_ref[...] + r * (dn - n * jnp.mean(dn * n, axis=-1, keepdims=True))

        @pl.when(pl.program_id(0) == 0)
        def _():
            dg_ref[...] = jnp.zeros_like(dg_ref)

        dg_ref[...] += jnp.sum(dh * n, axis=0, keepdims=True)

    row = pl.BlockSpec((T_ROW, D_MODEL), lambda i: (i, 0))
    vec = pl.BlockSpec((1, D_MODEL), lambda i: (0, 0))
    return _pcall(
        body, name=name, grid=(seq // T_ROW,),
        in_specs=[pl.BlockSpec((T_ROW, D_IN), lambda i: (i, 0)), pl.BlockSpec((D_IN, D_MODEL), lambda i: (0, 0)),
                  row, vec, row],
        out_specs=[row, vec],
        out_shape=[jax.ShapeDtypeStruct((seq, D_MODEL), F32), jax.ShapeDtypeStruct((1, D_MODEL), F32)],
        sem=("arbitrary",), vmem_mb=VMEM_MB)(dproj, w_t, x, gain, dxo)


def _matmul_acc(a, b, tn, name):
    seq, m = a.shape
    n = b.shape[1]
    ts = min(512, seq)

    def body(a_ref, b_ref, o_ref):
        @pl.when(pl.program_id(1) == 0)
        def _():
            o_ref[...] = jnp.zeros_like(o_ref)

        o_ref[...] += lax.dot_general(a_ref[...], b_ref[...], (((0,), (0,)), ((), ())), preferred_element_type=F32)

    return _pcall(
        body, name=name, grid=(n // tn, seq // ts),
        in_specs=[pl.BlockSpec((ts, m), lambda j, k: (k, 0)), pl.BlockSpec((ts, tn), lambda j, k: (k, j))],
        out_specs=pl.BlockSpec((m, tn), lambda j, k: (0, j)), out_shape=jax.ShapeDtypeStruct((m, n), F32),
        sem=("parallel", "arbitrary"), vmem_mb=VMEM_MB)(a, b)


ROPE_HALF = HEAD_DIM // 4


def _rope_tables(seq):
    t = jnp.arange(seq, dtype=jnp.int32)
    row = (t // GRID_W).astype(F32)
    col = (t % GRID_W).astype(F32)
    inv_freq = ROPE_THETA ** (-jnp.arange(ROPE_HALF, dtype=F32) / ROPE_HALF)
    ang_r = row[:, None] * inv_freq[None, :]
    ang_c = col[:, None] * inv_freq[None, :]
    cos = jnp.concatenate([jnp.cos(ang_r), jnp.cos(ang_r), jnp.cos(ang_c), jnp.cos(ang_c)], axis=-1)
    sin = jnp.concatenate([-jnp.sin(ang_r), jnp.sin(ang_r), -jnp.sin(ang_c), jnp.sin(ang_c)], axis=-1)
    return cos.T, sin.T


def _rope_partner(y):
    h = ROPE_HALF
    return jnp.concatenate([y[h:2 * h], y[0:h], y[3 * h:4 * h], y[2 * h:3 * h]], axis=0)


def _qk_specs(seq, head0):
    t = min(T_PREP, seq)
    src = pl.BlockSpec((None, HEAD_DIM, t), lambda h, i: (h + head0, 0, i))
    own = pl.BlockSpec((None, HEAD_DIM, t), lambda h, i: (h, 0, i))
    nat = pl.BlockSpec((None, t, HEAD_DIM), lambda h, i: (h, i, 0))
    col = pl.BlockSpec((HEAD_DIM, 1), lambda h, i: (0, 0))
    tab = pl.BlockSpec((HEAD_DIM, t), lambda h, i: (0, i))
    return t, src, own, nat, col, tab


def _qk_prep_fwd(qkv_t, head0, heads, gain, scale, cos, sin, name):
    seq = qkv_t.shape[2]
    t, src, own, nat, col, tab = _qk_specs(seq, head0)

    def body(x_ref, g_ref, c_ref, s_ref, ot_ref, on_ref):
        xf = x_ref[...]
        r = lax.rsqrt(jnp.mean(xf * xf, axis=0, keepdims=True) + EPS)
        y = xf * r * g_ref[...]
        z = (y * c_ref[...] + _rope_partner(y) * s_ref[...]) * scale
        ot_ref[...] = z.astype(BF16)
        on_ref[...] = z.T.astype(BF16)

    return _pcall(body, name=name, grid=(heads, seq // t), in_specs=[src, col, tab, tab], out_specs=[own, nat],
                  out_shape=[jax.ShapeDtypeStruct((heads, HEAD_DIM, seq), BF16),
                             jax.ShapeDtypeStruct((heads, seq, HEAD_DIM), BF16)],
                  sem=("parallel", "parallel"))(qkv_t, gain, cos, sin)


def _qk_prep_bwd(qkv_t, head0, dout, dout_is_t, gain, scale, cos, sin, name):
    heads = dout.shape[0]
    seq = qkv_t.shape[2]
    t, src, own, nat, col, tab = _qk_specs(seq, head0)

    def body(x_ref, d_ref, g_ref, c_ref, s_ref, dx_ref, dg_ref):
        xf = x_ref[...]
        r = lax.rsqrt(jnp.mean(xf * xf, axis=0, keepdims=True) + EPS)
        n = xf * r
        d = d_ref[...] if dout_is_t else d_ref[...].T
        dz = d * scale
        dy = dz * c_ref[...] + _rope_partner(dz * s_ref[...])
        dn = dy * g_ref[...]
        dx_ref[...] = r * (dn - n * jnp.mean(dn * n, axis=0, keepdims=True))

        @pl.when(pl.program_id(1) == 0)
        def _():
            dg_ref[...] = jnp.zeros_like(dg_ref)

        dg_ref[...] += jnp.sum(dy * n, axis=1, keepdims=True)

    return _pcall(body, name=name, grid=(heads, seq // t),
                  in_specs=[src, own if dout_is_t else nat, col, tab, tab],
                  out_specs=[own, pl.BlockSpec((None, HEAD_DIM, 1), lambda h, i: (h, 0, 0))],
                  out_shape=[jax.ShapeDtypeStruct((heads, HEAD_DIM, seq), F32),
                             jax.ShapeDtypeStruct((heads, HEAD_DIM, 1), F32)],
                  sem=("parallel", "arbitrary"))(qkv_t, dout, gain, cos, sin)


V_ROWS = HEAD_DIM + 8


def _unroll(nk, cap):
    u = 1
    while u * 2 <= cap and nk % (u * 2) == 0:
        u *= 2
    return u


def _attn_fwd(qs_t, k, v_t, name):
    seq = qs_t.shape[2]
    nk, bk = k.shape[1], k.shape[2]
    bq = min(BQ, seq)
    unroll = _unroll(nk, FWD_UNROLL)
    heads = range(HEADS_PER_STEP)

    def body(qt_ref, k_ref, vt_ref, ot_ref, lse_ref, s_scr):
        q_t = [qt_ref[h] for h in heads]

        def scores(j, slot):
            kj = k_ref[j]
            top = []
            for h in heads:
                s = _dot(kj, q_t[h])
                s_scr[slot, h] = s
                top.append(jnp.max(s, axis=0, keepdims=True))
            return tuple(top)

        def accumulate(j, slot, state, top):
            vtj = vt_ref[j]
            out = []
            for h in heads:
                m, acc = state[h]
                m_new = jnp.maximum(m, top[h])
                p = jnp.exp2(s_scr[slot, h] - m_new).astype(BF16)
                out.append((m_new, jnp.exp2(m - m_new) * acc + _dot(vtj, p)))
            return tuple(out)

        def step(t, carry):
            state, top = carry
            for u in range(unroll):
                nxt = unroll * t + u + 1
                top_next = scores(jnp.minimum(nxt, nk - 1) if u == unroll - 1 else nxt, (u + 1) % 2)
                state = accumulate(unroll * t + u, u % 2, state, top)
                top = top_next
            return state, top

        init = tuple((jnp.full((1, bq), -jnp.inf, F32), jnp.zeros((V_ROWS, bq), F32)) for _ in heads)
        state, _ = lax.fori_loop(0, nk // unroll, step, (init, scores(0, 0)))
        for h in heads:
            m, acc = state[h]
            l = acc[HEAD_DIM:HEAD_DIM + 1, :]
            ot_ref[h] = acc[:HEAD_DIM, :] / l
            lse_ref[h] = m + jnp.log2(l)

    kv_of = lambda g: g * HEADS_PER_STEP // GROUP
    return _pcall(
        body, name=name, grid=(ATT_HEADS // HEADS_PER_STEP, seq // bq),
        in_specs=[pl.BlockSpec((HEADS_PER_STEP, HEAD_DIM, bq), lambda g, i: (g, 0, i)),
                  pl.BlockSpec((None, nk, bk, HEAD_DIM), lambda g, i: (kv_of(g), 0, 0, 0)),
                  pl.BlockSpec((None, nk, V_ROWS, bk), lambda g, i: (kv_of(g), 0, 0, 0))],
        out_specs=[pl.BlockSpec((HEADS_PER_STEP, HEAD_DIM, bq), lambda g, i: (g, 0, i)),
                   pl.BlockSpec((HEADS_PER_STEP, 1, bq), lambda g, i: (g, 0, i))],
        out_shape=[jax.ShapeDtypeStruct((ATT_HEADS, HEAD_DIM, seq), F32),
                   jax.ShapeDtypeStruct((ATT_HEADS, 1, seq), F32)],
        scratch=[pltpu.VMEM((2, HEADS_PER_STEP, bk, bq), F32)],
        sem=("parallel", "parallel"), vmem_mb=VMEM_MB)(qs_t, k, v_t)


def _attn_bwd(qs, qs_t, do_t, o_t, lse, k, k_t, v_t, name):
    seq = qs.shape[1]
    nk, bk = k.shape[1], k.shape[2]
    bq = min(BQ, seq)
    nq = seq // bq

    unroll = _unroll(nk, BWD_UNROLL)
    heads = range(HEADS_PER_STEP)
    pairs = GROUP // HEADS_PER_STEP

    def body(q_ref, qt_ref, dot_ref, ot_ref, lse_ref, k_ref, kt_ref, vt_ref, dq_ref, dkt_ref, dvt_ref,
             s_scr, dp_scr):
        @pl.when((pl.program_id(1) == 0) & (pl.program_id(2) == 0))
        def _():
            dkt_ref[...] = jnp.zeros_like(dkt_ref)
            dvt_ref[...] = jnp.zeros_like(dvt_ref)

        q, q_t = [q_ref[h] for h in heads], [qt_ref[h] for h in heads]
        do_t_b = [dot_ref[h].astype(BF16) for h in heads]
        do_l = [dot_ref[h].T * LN2 for h in heads]
        do_b = [d.astype(BF16) for d in do_l]
        delta = [jnp.sum(do_l[h] * ot_ref[h].T, axis=-1, keepdims=True) for h in heads]
        lse_col = [jnp.max(jnp.broadcast_to(lse_ref[h], (LANES, bq)).T, axis=-1, keepdims=True) for h in heads]

        def products(j, slot):
            ktj, vtj = kt_ref[j], vt_ref[j]
            for h in heads:
                s_scr[slot, h] = _dot(q[h], ktj)
                dp_scr[slot, h] = _dot(do_b[h], vtj)

        def gradients(j, slot, dq):
            kj = k_ref[j]
            dvt = jnp.zeros((HEAD_DIM, bk), F32)
            dkt = jnp.zeros((HEAD_DIM, bk), F32)
            new = []
            for h in heads:
                p = jnp.exp2(s_scr[slot, h] - lse_col[h])
                ds = (p * (dp_scr[slot, h] - delta[h])).astype(BF16)
                dvt = dvt + _dot(do_t_b[h], p.astype(BF16))
                dkt = dkt + _dot(q_t[h], ds)
                new.append(dq[h] + _dot(ds, kj))
            dvt_ref[j] += dvt
            dkt_ref[j] += dkt
            return tuple(new)

        def step(t, dq):
            for u in range(unroll):
                nxt = unroll * t + u + 1
                products(jnp.minimum(nxt, nk - 1) if u == unroll - 1 else nxt, (u + 1) % 2)
                dq = gradients(unroll * t + u, u % 2, dq)
            return dq

        products(0, 0)
        res = lax.fori_loop(0, nk // unroll, step, tuple(jnp.zeros((bq, HEAD_DIM), F32) for _ in heads))
        for h in heads:
            dq_ref[h] = res[h]

    first = lambda g, hh: g * pairs + hh
    row = pl.BlockSpec((HEADS_PER_STEP, bq, HEAD_DIM), lambda g, hh, i: (first(g, hh), i, 0))
    col = pl.BlockSpec((HEADS_PER_STEP, HEAD_DIM, bq), lambda g, hh, i: (first(g, hh), 0, i))
    kv_rows = pl.BlockSpec((None, nk, bk, HEAD_DIM), lambda g, hh, i: (g, 0, 0, 0))
    kv_cols = pl.BlockSpec((None, nk, HEAD_DIM, bk), lambda g, hh, i: (g, 0, 0, 0))
    return _pcall(
        body, name=name, grid=(KV_HEADS, pairs, nq),
        in_specs=[row, col, col, col,
                  pl.BlockSpec((HEADS_PER_STEP, 1, bq), lambda g, hh, i: (first(g, hh), 0, i)),
                  kv_rows, kv_cols, kv_cols],
        out_specs=[row, kv_cols, kv_cols],
        out_shape=[jax.ShapeDtypeStruct((ATT_HEADS, seq, HEAD_DIM), F32),
                   jax.ShapeDtypeStruct((KV_HEADS, nk, HEAD_DIM, bk), F32),
                   jax.ShapeDtypeStruct((KV_HEADS, nk, HEAD_DIM, bk), F32)],
        scratch=[pltpu.VMEM((2, HEADS_PER_STEP, bq, bk), F32), pltpu.VMEM((2, HEADS_PER_STEP, bq, bk), F32)],
        sem=("parallel", "arbitrary", "arbitrary"), vmem_mb=ATTN_BWD_VMEM_MB)(
            qs, qs_t, do_t, o_t, lse, k, k_t, v_t)


def _halo_specs(t, col, n_tiles):
    per = t // HALO
    last = n_tiles * per - 1
    before = pl.BlockSpec((HALO, CONV_W), lambda i: (jnp.maximum(i * per - 1, 0), col))
    after = pl.BlockSpec((HALO, CONV_W), lambda i: (jnp.minimum((i + 1) * per, last), col))
    return before, after


def _glu(a, b):
    return a * _sigmoid(b)


def _conv_taps(ext_ref, w_ref, t, flip):
    acc = jnp.zeros((t, CONV_W), F32)
    for k in range(CONV_K):
        off = (HALO + CONV_PAD - k) if flip else (HALO - CONV_PAD + k)
        acc = acc + w_ref[k:k + 1, :] * ext_ref[pl.ds(off, t), :]
    return acc


def _fill_ext(ext_ref, before, tile, after, t, i, n_tiles):
    ext_ref[pl.ds(0, HALO), :] = jnp.where(i > 0, before, 0.0)
    ext_ref[pl.ds(HALO, t), :] = tile
    ext_ref[pl.ds(HALO + t, HALO), :] = jnp.where(i < n_tiles - 1, after, 0.0)


def _conv_fwd(proj, w, bias, ln_g, ln_b, name):
    seq = proj.shape[0]
    t = min(T_GROUP, seq)
    n_tiles = seq // t

    def body(a_ref, b_ref, ap_ref, bp_ref, an_ref, bn_ref, gate_ref, w_ref, bias_ref, g_ref, beta_ref, o_ref, y_ref,
             ext_ref):
        i = pl.program_id(0)
        _fill_ext(ext_ref, _glu(ap_ref[...], bp_ref[...]), _glu(a_ref[...], b_ref[...]),
                  _glu(an_ref[...], bn_ref[...]), t, i, n_tiles)
        y = _conv_taps(ext_ref, w_ref, t, False) + bias_ref[...]
        y_ref[...] = y
        mu = jnp.mean(y, axis=-1, keepdims=True)
        yc = y - mu
        rs = lax.rsqrt(jnp.mean(yc * yc, axis=-1, keepdims=True) + EPS)
        z = yc * rs * g_ref[...] + beta_ref[...]
        o_ref[...] = _silu_and_grad(z)[0] * _silu_and_grad(gate_ref[...])[0]

    tile = lambda c: pl.BlockSpec((t, CONV_W), lambda i: (i, c))
    ab, aa = _halo_specs(t, COL_A, n_tiles)
    bb, ba = _halo_specs(t, COL_B, n_tiles)
    vec = pl.BlockSpec((1, CONV_W), lambda i: (0, 0))
    return _pcall(
        body, name=name, grid=(n_tiles,),
        in_specs=[tile(COL_A), tile(COL_B), ab, bb, aa, ba, tile(COL_GCONV),
                  pl.BlockSpec((CONV_K, CONV_W), lambda i: (0, 0)), vec, vec, vec],
        out_specs=[pl.BlockSpec((t, CONV_W), lambda i: (i, 0))] * 2,
        out_shape=[jax.ShapeDtypeStruct((seq, CONV_W), F32)] * 2,
        scratch=[pltpu.VMEM((t + 2 * HALO, CONV_W), F32)], sem=("parallel",))(
            proj, proj, proj, proj, proj, proj, proj, w, bias, ln_g, ln_b)


def _conv_bwd_a(proj, y_conv, dcnv, ln_g, ln_b, name):
    seq = proj.shape[0]
    t = min(T_GROUP, seq)
    n_tiles = seq // t

    def body(a_ref, b_ref, ap_ref, bp_ref, an_ref, bn_ref, gate_ref, y_ref, d_ref, g_ref, beta_ref,
             dy_ref, dgate_ref, dw_ref, dbias_ref, dg_ref, dbeta_ref, ext_ref):
        i = pl.program_id(0)
        _fill_ext(ext_ref, _glu(ap_ref[...], bp_ref[...]), _glu(a_ref[...], b_ref[...]),
                  _glu(an_ref[...], bn_ref[...]), t, i, n_tiles)
        y = y_ref[...]
        mu = jnp.mean(y, axis=-1, keepdims=True)
        yc = y - mu
        rs = lax.rsqrt(jnp.mean(yc * yc, axis=-1, keepdims=True) + EPS)
        n = yc * rs
        z = n * g_ref[...] + beta_ref[...]
        act, dact = _silu_and_grad(z)
        gate, dgate = _silu_and_grad(gate_ref[...])
        d = d_ref[...]
        dgate_ref[...] = d * act * dgate
        dz = d * gate * dact
        dn = dz * g_ref[...]
        dy = rs * (dn - jnp.mean(dn, axis=-1, keepdims=True) - n * jnp.mean(dn * n, axis=-1, keepdims=True))
        dy_ref[...] = dy

        @pl.when(i == 0)
        def _():
            dw_ref[...] = jnp.zeros_like(dw_ref)
            dbias_ref[...] = jnp.zeros_like(dbias_ref)
            dg_ref[...] = jnp.zeros_like(dg_ref)
            dbeta_ref[...] = jnp.zeros_like(dbeta_ref)

        dg_ref[...] += jnp.sum(dz * n, axis=0, keepdims=True)
        dbeta_ref[...] += jnp.sum(dz, axis=0, keepdims=True)
        dbias_ref[...] += jnp.sum(dy, axis=0, keepdims=True)
        for k in range(CONV_K):
            dw_ref[k:k + 1, :] += jnp.sum(dy * ext_ref[pl.ds(HALO - CONV_PAD + k, t), :], axis=0, keepdims=True)

    tile = lambda c: pl.BlockSpec((t, CONV_W), lambda i: (i, c))
    own = pl.BlockSpec((t, CONV_W), lambda i: (i, 0))
    ab, aa = _halo_specs(t, COL_A, n_tiles)
    bb, ba = _halo_specs(t, COL_B, n_tiles)
    vec = pl.BlockSpec((1, CONV_W), lambda i: (0, 0))
    taps = pl.BlockSpec((CONV_K, CONV_W), lambda i: (0, 0))
    vshape = jax.ShapeDtypeStruct((1, CONV_W), F32)
    return _pcall(
        body, name=name, grid=(n_tiles,),
        in_specs=[tile(COL_A), tile(COL_B), ab, bb, aa, ba, tile(COL_GCONV), own, own, vec, vec],
        out_specs=[own, own, taps, vec, vec, vec],
        out_shape=[jax.ShapeDtypeStruct((seq, CONV_W), F32), jax.ShapeDtypeStruct((seq, CONV_W), F32),
                   jax.ShapeDtypeStruct((CONV_K, CONV_W), F32), vshape, vshape, vshape],
        scratch=[pltpu.VMEM((t + 2 * HALO, CONV_W), F32)], sem=("arbitrary",))(
            proj, proj, proj, proj, proj, proj, proj, y_conv, dcnv, ln_g, ln_b)


def _conv_bwd_b(proj, dy, w, name):
    seq = proj.shape[0]
    t = min(T_GROUP, seq)
    n_tiles = seq // t

    def body(a_ref, b_ref, dy_ref, dyp_ref, dyn_ref, w_ref, da_ref, db_ref, ext_ref):
        i = pl.program_id(0)
        _fill_ext(ext_ref, dyp_ref[...], dy_ref[...], dyn_ref[...], t, i, n_tiles)
        dh = _conv_taps(ext_ref, w_ref, t, True)
        sig = _sigmoid(b_ref[...])
        da_ref[...] = dh * sig
        db_ref[...] = dh * a_ref[...] * sig * (1.0 - sig)

    tile = lambda c: pl.BlockSpec((t, CONV_W), lambda i: (i, c))
    own = pl.BlockSpec((t, CONV_W), lambda i: (i, 0))
    before, after = _halo_specs(t, 0, n_tiles)
    return _pcall(
        body, name=name, grid=(n_tiles,),
        in_specs=[tile(COL_A), tile(COL_B), own, before, after, pl.BlockSpec((CONV_K, CONV_W), lambda i: (0, 0))],
        out_specs=[own, own], out_shape=[jax.ShapeDtypeStruct((seq, CONV_W), F32)] * 2,
        scratch=[pltpu.VMEM((t + 2 * HALO, CONV_W), F32)], sem=("parallel",))(proj, proj, dy, dy, dy, w)


def _head_masks():
    lane_head = lax.broadcasted_iota(jnp.int32, (SG_CHUNK, SG_W), 1) // HEAD_DIM
    return [lane_head == h for h in range(SG_HEADS)]


def _sg_mix(mats_ref, rhs, masks):
    out = jnp.zeros((SG_CHUNK, SG_W), F32)
    for h in range(SG_HEADS):
        out = out + jnp.where(masks[h], _dot(mats_ref[h], rhs), 0.0)
    return out


def _sg_specs(seq):
    t = min(T_GROUP, seq)
    tile = lambda c: pl.BlockSpec((t, SG_W), lambda i: (i, c))
    own = pl.BlockSpec((t, SG_W), lambda i: (i, 0))
    vec = pl.BlockSpec((1, SG_W), lambda i: (0, 0))
    mats = pl.BlockSpec((SG_HEADS, SG_CHUNK, SG_CHUNK), lambda i: (0, 0, 0))
    full = pl.BlockSpec((SG_CHUNK, SG_W), lambda i: (0, 0))
    return t, tile, own, vec, mats, full


def _sg_fwd(proj, ln_g, ln_b, w_b, bias_full, name):
    seq = proj.shape[0]
    t, tile, own, vec, mats, full = _sg_specs(seq)

    def body(u_ref, v_ref, gate_ref, g_ref, beta_ref, w_ref, bias_ref, o_ref):
        masks = _head_masks()
        for c in range(t // SG_CHUNK):
            rows = pl.ds(c * SG_CHUNK, SG_CHUNK)
            vg = _gelu_and_grad(v_ref[rows, :])[0]
            mu = jnp.mean(vg, axis=-1, keepdims=True)
            vc = vg - mu
            rs = lax.rsqrt(jnp.mean(vc * vc, axis=-1, keepdims=True) + EPS)
            vln = vc * rs * g_ref[...] + beta_ref[...]
            mixed = _sg_mix(w_ref, vln.astype(BF16), masks) + bias_ref[...]
            o_ref[rows, :] = _gelu_and_grad(u_ref[rows, :])[0] * mixed * _silu_and_grad(gate_ref[rows, :])[0]

    return _pcall(body, name=name, grid=(seq // t,),
                  in_specs=[tile(COL_U), tile(COL_VSG), tile(COL_GSG), vec, vec, mats, full], out_specs=own,
                  out_shape=jax.ShapeDtypeStruct((seq, SG_W), F32), sem=("parallel",))(
                      proj, proj, proj, ln_g, ln_b, w_b, bias_full)


def _sg_bwd(proj, dsg, ln_g, ln_b, w_b, w_t_b, bias_full, fold, name):
    seq = proj.shape[0]
    t, tile, own, vec, mats, full = _sg_specs(seq)
    n_tiles = seq // t

    def body(u_ref, v_ref, gate_ref, d_ref, g_ref, beta_ref, w_ref, wt_ref, bias_ref, fold_ref,
             du_ref, dv_ref, dgate_ref, dg_ref, dbeta_ref, dw_ref, db_ref, dbias_acc):
        i = pl.program_id(0)

        @pl.when(i == 0)
        def _():
            dg_ref[...] = jnp.zeros_like(dg_ref)
            dbeta_ref[...] = jnp.zeros_like(dbeta_ref)
            dw_ref[...] = jnp.zeros_like(dw_ref)
            dbias_acc[...] = jnp.zeros_like(dbias_acc)

        masks = _head_masks()
        for c in range(t // SG_CHUNK):
            rows = pl.ds(c * SG_CHUNK, SG_CHUNK)
            ug, dug = _gelu_and_grad(u_ref[rows, :])
            vg, dvg = _gelu_and_grad(v_ref[rows, :])
            mu = jnp.mean(vg, axis=-1, keepdims=True)
            vc = vg - mu
            rs = lax.rsqrt(jnp.mean(vc * vc, axis=-1, keepdims=True) + EPS)
            vn = vc * rs
            vln_b = (vn * g_ref[...] + beta_ref[...]).astype(BF16)
            mixed = _sg_mix(w_ref, vln_b, masks) + bias_ref[...]
            gate, dgate = _silu_and_grad(gate_ref[rows, :])
            d = d_ref[rows, :]
            dgate_ref[rows, :] = d * ug * mixed * dgate
            du_ref[rows, :] = d * mixed * gate * dug
            dmixed = d * ug * gate
            dbias_acc[...] += dmixed
            dmixed_b = dmixed.astype(BF16)
            for h in range(SG_HEADS):
                dm_h = jnp.where(masks[h], dmixed_b, jnp.zeros_like(dmixed_b))
                dw_ref[h] += lax.dot_general(dm_h, vln_b, (((1,), (1,)), ((), ())), preferred_element_type=F32)
            dvln = _sg_mix(wt_ref, dmixed_b, masks)
            dg_ref[...] += jnp.sum(dvln * vn, axis=0, keepdims=True)
            dbeta_ref[...] += jnp.sum(dvln, axis=0, keepdims=True)
            dvn = dvln * g_ref[...]
            dvgelu = rs * (dvn - jnp.mean(dvn, axis=-1, keepdims=True) - vn * jnp.mean(dvn * vn, axis=-1, keepdims=True))
            dv_ref[rows, :] = dvgelu * dvg

        @pl.when(i == n_tiles - 1)
        def _():
            db_ref[...] = _split_dot(dbias_acc[...], fold_ref[...])

    sq = pl.BlockSpec((SG_CHUNK, SG_CHUNK), lambda i: (0, 0))
    vshape = jax.ShapeDtypeStruct((1, SG_W), F32)
    return _pcall(
        body, name=name, grid=(n_tiles,),
        in_specs=[tile(COL_U), tile(COL_VSG), tile(COL_GSG), own, vec, vec, mats, mats, full,
                  pl.BlockSpec((SG_W, SG_CHUNK), lambda i: (0, 0))],
        out_specs=[own, own, own, vec, vec, mats, sq],
        out_shape=[jax.ShapeDtypeStruct((seq, SG_W), F32)] * 3 + [
            vshape, vshape, jax.ShapeDtypeStruct((SG_HEADS, SG_CHUNK, SG_CHUNK), F32),
            jax.ShapeDtypeStruct((SG_CHUNK, SG_CHUNK), F32)],
        scratch=[pltpu.VMEM((SG_CHUNK, SG_W), F32)], sem=("arbitrary",))(
            proj, proj, proj, dsg, ln_g, ln_b, w_b, w_t_b, bias_full, fold)


def _out_fwd(att_t, proj, cnv, sgu, x, w, gain, name):
    seq = x.shape[0]

    def body(att_ref, g0_ref, g1_ref, cnv_ref, sgu_ref, x_ref, w_ref, gain_ref, xo_ref, mix_ref, cat_ref):
        gate = jnp.concatenate([_silu_and_grad(g0_ref[...])[0], _silu_and_grad(g1_ref[...])[0]], axis=-1)
        cat_ref[:, 0:ATT_W] = (att_ref[...].T * gate).astype(BF16)
        cat_ref[:, ATT_W:ATT_W + CONV_W] = cnv_ref[...].astype(BF16)
        cat_ref[:, ATT_W + CONV_W:] = sgu_ref[...].astype(BF16)
        mix = _dot(cat_ref[...], w_ref[...])
        mix_ref[...] = mix
        r = lax.rsqrt(jnp.mean(mix * mix, axis=-1, keepdims=True) + EPS)
        xo_ref[...] = x_ref[...] + mix * r * gain_ref[...]

    row = lambda w_: pl.BlockSpec((T_ROW, w_), lambda i: (i, 0))
    gate_blk = lambda c: pl.BlockSpec((T_ROW, 256), lambda i: (i, c))
    return _pcall(
        body, name=name, grid=(seq // T_ROW,),
        in_specs=[pl.BlockSpec((ATT_W, T_ROW), lambda i: (0, i)), gate_blk(COL_GATT), gate_blk(COL_GATT + 1),
                  row(CONV_W), row(SG_W), row(D_MODEL),
                  pl.BlockSpec((D_MODEL, D_MODEL), lambda i: (0, 0)), pl.BlockSpec((1, D_MODEL), lambda i: (0, 0))],
        out_specs=[row(D_MODEL), row(D_MODEL), row(D_MODEL)],
        out_shape=[jax.ShapeDtypeStruct((seq, D_MODEL), F32), jax.ShapeDtypeStruct((seq, D_MODEL), F32),
                   jax.ShapeDtypeStruct((seq, D_MODEL), BF16)],
        sem=("parallel",), vmem_mb=VMEM_MB)(att_t, proj, proj, cnv, sgu, x, w, gain)


def _out_bwd(dxo, mix, gain, w_t, att_t, proj, name):
    seq = dxo.shape[0]

    def body(dxo_ref, mix_ref, gain_ref, w_ref, att_ref, g0_ref, g1_ref,
             dmix_ref, datt_ref, dgatt_ref, dcnv_ref, dsgu_ref, dgain_ref):
        mix = mix_ref[...]
        r = lax.rsqrt(jnp.mean(mix * mix, axis=-1, keepdims=True) + EPS)
        n = mix * r
        dout = dxo_ref[...]
        dn = dout * gain_ref[...]
        dmix = (r * (dn - n * jnp.mean(dn * n, axis=-1, keepdims=True))).astype(BF16)
        dmix_ref[...] = dmix

        @pl.when(pl.program_id(0) == 0)
        def _():
            dgain_ref[...] = jnp.zeros_like(dgain_ref)

        dgain_ref[...] += jnp.sum(dout * n, axis=0, keepdims=True)
        dcat = _dot(dmix, w_ref[...])
        g0, dg0 = _silu_and_grad(g0_ref[...])
        g1, dg1 = _silu_and_grad(g1_ref[...])
        gate = jnp.concatenate([g0, g1], axis=-1)
        dgate = jnp.concatenate([dg0, dg1], axis=-1)
        dca = dcat[:, 0:ATT_W]
        datt_ref[...] = (dca * gate).T
        dgatt_ref[...] = dca * att_ref[...].T * dgate
        dcnv_ref[...] = dcat[:, ATT_W:ATT_W + CONV_W]
        dsgu_ref[...] = dcat[:, ATT_W + CONV_W:]

    row = lambda w_: pl.BlockSpec((T_ROW, w_), lambda i: (i, 0))
    gate_blk = lambda c: pl.BlockSpec((T_ROW, 256), lambda i: (i, c))
    vec = pl.BlockSpec((1, D_MODEL), lambda i: (0, 0))
    heads_t = pl.BlockSpec((ATT_W, T_ROW), lambda i: (0, i))
    return _pcall(
        body, name=name, grid=(seq // T_ROW,),
        in_specs=[row(D_MODEL), row(D_MODEL), vec, pl.BlockSpec((D_MODEL, D_MODEL), lambda i: (0, 0)), heads_t,
                  gate_blk(COL_GATT), gate_blk(COL_GATT + 1)],
        out_specs=[row(D_MODEL), heads_t, row(ATT_W), row(CONV_W), row(SG_W), vec],
        out_shape=[jax.ShapeDtypeStruct((seq, D_MODEL), BF16), jax.ShapeDtypeStruct((ATT_W, seq), F32),
                   jax.ShapeDtypeStruct((seq, ATT_W), F32), jax.ShapeDtypeStruct((seq, CONV_W), F32),
                   jax.ShapeDtypeStruct((seq, SG_W), F32), jax.ShapeDtypeStruct((1, D_MODEL), F32)],
        sem=("arbitrary",), vmem_mb=VMEM_MB)(dxo, mix, gain, w_t, att_t, proj, proj)


def _loss_head(y, target, name):
    seq = y.shape[0]
    t = min(T_GROUP, seq)

    def body(y_ref, t_ref, sse_ref, dy_ref):
        err = y_ref[...] - t_ref[...]
        dy_ref[...] = err * (1.0 / D_MODEL)

        @pl.when(pl.program_id(0) == 0)
        def _():
            sse_ref[...] = jnp.zeros_like(sse_ref)

        part = jnp.sum(jnp.sum(err * err, axis=0, keepdims=True), axis=-1, keepdims=True)
        sse_ref[...] += jnp.broadcast_to(part, (1, LANES))

    row = pl.BlockSpec((t, D_MODEL), lambda i: (i, 0))
    return _pcall(body, name=name, grid=(seq // t,), in_specs=[row, row],
                  out_specs=[pl.BlockSpec((1, LANES), lambda i: (0, 0)), row],
                  out_shape=[jax.ShapeDtypeStruct((1, LANES), F32), jax.ShapeDtypeStruct((seq, D_MODEL), F32)],
                  sem=("arbitrary",))(y, target)


def _row_blocks(a, bk):
    return a.reshape(a.shape[0], a.shape[1] // bk, bk, HEAD_DIM)


def _lane_blocks(a, bk):
    return a.reshape(a.shape[0], HEAD_DIM, a.shape[2] // bk, bk).transpose(0, 2, 1, 3)


def _from_lane_blocks(a):
    return a.transpose(0, 2, 1, 3).reshape(a.shape[0], HEAD_DIM, a.shape[1] * a.shape[3])


def _flat_rows(a, rows):
    flat = a.reshape(-1)
    return jnp.pad(flat, (0, rows * LANES - flat.shape[0])).reshape(rows, LANES)


SHARD_ROWS = {"w_in": 2 * D_MODEL * (D_IN // N_DEV) // LANES, "w_out": 2 * (D_MODEL // N_DEV) * D_MODEL // LANES,
              "conv_dw": 16}
REPL_SHAPES = [("pre_norm", (2, D_MODEL)), ("post_norm", (2, D_MODEL)), ("q_norm", (2, HEAD_DIM)),
               ("k_norm", (2, HEAD_DIM)), ("conv_dw_b", (2, CONV_W)), ("conv_ln_g", (2, CONV_W)),
               ("conv_ln_b", (2, CONV_W)), ("sg_ln_g", (2, SG_W)), ("sg_ln_b", (2, SG_W)),
               ("sg_w", (2, SG_HEADS, SG_CHUNK, SG_CHUNK)), ("sg_b", (2, SG_HEADS, SG_CHUNK))]
REPL_ROWS = 1088
WEIGHT_ORDER = ["pre_norm", "post_norm", "w_in", "w_out", "q_norm", "k_norm", "conv_dw", "conv_dw_b", "conv_ln_g",
                "conv_ln_b", "sg_ln_g", "sg_ln_b", "sg_w", "sg_b"]


def _pack_shard(parts):
    return jnp.concatenate([_flat_rows(parts[k], SHARD_ROWS[k]) for k in ("w_in", "w_out", "conv_dw")], axis=0)


def _unpack_shard(flat, shapes):
    out, at = {}, 0
    for k in ("w_in", "w_out", "conv_dw"):
        size = math.prod(shapes[k])
        out[k] = flat[at:at + SHARD_ROWS[k]].reshape(-1)[:size].reshape(shapes[k])
        at += SHARD_ROWS[k]
    return out


REPL_USED = sum(math.prod(shape) for _, shape in REPL_SHAPES)


def _pack_repl(parts, extra=None):
    tail = [] if extra is None else [extra.reshape(1)]
    flat = jnp.concatenate([parts[k].reshape(-1) for k, _ in REPL_SHAPES] + tail)
    return jnp.pad(flat, (0, REPL_ROWS * LANES - flat.shape[0])).reshape(REPL_ROWS, LANES)


def _unpack_repl(flat):
    out, at, flat = {}, 0, flat.reshape(-1)
    for k, shape in REPL_SHAPES:
        size = math.prod(shape)
        out[k] = flat[at:at + size].reshape(shape)
        at += size
    return out


def kernel(x, pre_norm, post_norm, w_in, w_out, q_norm, k_norm, conv_dw, conv_dw_b, conv_ln_g, conv_ln_b, sg_ln_g, sg_ln_b, sg_w, sg_b, loss_target, m_pre_norm, m_post_norm, m_w_in, m_w_out, m_q_norm, m_k_norm, m_conv_dw, m_conv_dw_b, m_conv_ln_g, m_conv_ln_b, m_sg_ln_g, m_sg_ln_b, m_sg_w, m_sg_b, v_pre_norm, v_post_norm, v_w_in, v_w_out, v_q_norm, v_k_norm, v_conv_dw, v_conv_dw_b, v_conv_ln_g, v_conv_ln_b, v_sg_ln_g, v_sg_ln_b, v_sg_w, v_sg_b):
    weights = dict(pre_norm=pre_norm, post_norm=post_norm, w_in=w_in, w_out=w_out, q_norm=q_norm, k_norm=k_norm,
                   conv_dw=conv_dw, conv_dw_b=conv_dw_b, conv_ln_g=conv_ln_g, conv_ln_b=conv_ln_b, sg_ln_g=sg_ln_g,
                   sg_ln_b=sg_ln_b, sg_w=sg_w, sg_b=sg_b)
    mom_m = dict(pre_norm=m_pre_norm, post_norm=m_post_norm, w_in=m_w_in, w_out=m_w_out, q_norm=m_q_norm,
                 k_norm=m_k_norm, conv_dw=m_conv_dw, conv_dw_b=m_conv_dw_b, conv_ln_g=m_conv_ln_g,
                 conv_ln_b=m_conv_ln_b, sg_ln_g=m_sg_ln_g, sg_ln_b=m_sg_ln_b, sg_w=m_sg_w, sg_b=m_sg_b)
    mom_v = dict(pre_norm=v_pre_norm, post_norm=v_post_norm, w_in=v_w_in, w_out=v_w_out, q_norm=v_q_norm,
                 k_norm=v_k_norm, conv_dw=v_conv_dw, conv_dw_b=v_conv_dw_b, conv_ln_g=v_conv_ln_g,
                 conv_ln_b=v_conv_ln_b, sg_ln_g=v_sg_ln_g, sg_ln_b=v_sg_ln_b, sg_w=v_sg_w, sg_b=v_sg_b)
    depth = pre_norm.shape[0]
    seq = x.shape[1]
    bk = min(BK, seq)
    x0 = x.reshape(seq, D_MODEL)
    target = loss_target.reshape(seq, D_MODEL)

    w_in_all, w_out_all, dw_all = _exchange(
        [], [w_in.astype(BF16), w_out.astype(BF16), jnp.pad(conv_dw, ((0, 0), (0, 1), (0, 0)))], "gather_weights")
    w_in_full = w_in_all.transpose(1, 2, 0, 3).reshape(depth, D_MODEL, D_IN)
    w_out_full = w_out_all.transpose(1, 0, 2, 3).reshape(depth, D_MODEL, D_MODEL)
    dw_full = dw_all[:, :, :CONV_K, :].transpose(1, 2, 0, 3).reshape(depth, CONV_K, CONV_W)

    cos, sin = _rope_tables(seq)
    lane = jnp.arange(SG_W)
    fold = (lane[:, None] // HEAD_DIM == jnp.arange(SG_CHUNK)[None, :]).astype(BF16)

    def layer_consts(l):
        return dict(
            q_gain=q_norm[l].reshape(HEAD_DIM, 1), k_gain=k_norm[l].reshape(HEAD_DIM, 1), sg_w_b=sg_w[l].astype(BF16), sg_wt_b=sg_w[l].transpose(0, 2, 1).astype(BF16),
            sg_bias=jnp.repeat(sg_b[l].T, HEAD_DIM, axis=1),
            vec=lambda a: a[l].reshape(1, -1))

    saved = []
    xc = x0
    for l in range(depth):
        c = layer_consts(l)
        proj, hb, qkv_t = _proj_fwd(xc, c["vec"](pre_norm), w_in_full[l], f"proj_fwd_{l}")
        qkv_t = qkv_t.reshape(QKV_HEADS, HEAD_DIM, seq)
        qs_t, qs = _qk_prep_fwd(qkv_t, 0, ATT_HEADS, c["q_gain"], Q_SCALE, cos, sin, f"q_prep_fwd_{l}")
        kr_t, kr = _qk_prep_fwd(qkv_t, ATT_HEADS, KV_HEADS, c["k_gain"], 1.0, cos, sin, f"k_prep_fwd_{l}")
        v_t = qkv_t[ATT_HEADS + KV_HEADS:].astype(BF16)
        k_rows, k_cols, v_cols = _row_blocks(kr, bk), _lane_blocks(kr_t, bk), _lane_blocks(v_t, bk)
        v_ext = jnp.concatenate([v_cols, jnp.ones_like(v_cols[:, :, :1]), jnp.zeros_like(v_cols[:, :, :7])], axis=2)
        o_t, lse = _attn_fwd(qs_t, k_rows, v_ext, f"attn_fwd_{l}")
        att_t = o_t.reshape(ATT_W, seq)
        cnv, y_conv = _conv_fwd(proj, dw_full[l], c["vec"](conv_dw_b), c["vec"](conv_ln_g), c["vec"](conv_ln_b),
                                f"conv_fwd_{l}")
        sgu = _sg_fwd(proj, c["vec"](sg_ln_g), c["vec"](sg_ln_b), c["sg_w_b"], c["sg_bias"], f"sg_fwd_{l}")
        x_new, mix, cat_b = _out_fwd(att_t, proj, cnv, sgu, xc, w_out_full[l], c["vec"](post_norm), f"out_fwd_{l}")
        saved.append(dict(x=xc, proj=proj, hb=hb, qkv_t=qkv_t, qs=qs, qs_t=qs_t, k_rows=k_rows,
                          k_cols=k_cols, v_cols=v_cols, o_t=o_t, lse=lse, mix=mix, cat_b=cat_b, y_conv=y_conv))
        xc = x_new

    sse, dx = _loss_head(xc, target, "loss_head")

    grads = {k: [None] * depth for k in WEIGHT_ORDER}
    for l in reversed(range(depth)):
        c, s = layer_consts(l), saved[l]
        dmix_b, datt, dgatt, dcnv, dsgu, g_post = _out_bwd(
            dx, s["mix"], c["vec"](post_norm), w_out_full[l].T, s["o_t"].reshape(ATT_W, seq), s["proj"],
            f"out_bwd_{l}")
        grads["post_norm"][l] = g_post.reshape(-1)
        grads["w_out"][l] = _matmul_acc(s["cat_b"], dmix_b, D_MODEL, f"grad_w_out_{l}")
        dqs, dkt, dvt = _attn_bwd(s["qs"], s["qs_t"], datt.reshape(ATT_HEADS, HEAD_DIM, seq), s["o_t"], s["lse"],
                                  s["k_rows"], s["k_cols"], s["v_cols"], f"attn_bwd_{l}")
        d_q_t, g_qgain = _qk_prep_bwd(s["qkv_t"], 0, dqs, False, c["q_gain"], Q_SCALE, cos, sin, f"q_prep_bwd_{l}")
        d_k_t, g_kgain = _qk_prep_bwd(s["qkv_t"], ATT_HEADS, _from_lane_blocks(dkt), True, c["k_gain"], 1.0, cos, sin,
                                      f"k_prep_bwd_{l}")
        grads["q_norm"][l] = jnp.sum(g_qgain[:, :, 0], axis=0)
        grads["k_norm"][l] = jnp.sum(g_kgain[:, :, 0], axis=0)
        d_qkv = jnp.concatenate([d_q_t, d_k_t, _from_lane_blocks(dvt)], axis=0).reshape(QKV_W, seq).T
        dy_conv, dg_conv, g_dw, g_dwb, g_clg, g_clb = _conv_bwd_a(
            s["proj"], s["y_conv"], dcnv, c["vec"](conv_ln_g), c["vec"](conv_ln_b), f"conv_bwd_a_{l}")
        da, db = _conv_bwd_b(s["proj"], dy_conv, dw_full[l], f"conv_bwd_b_{l}")
        grads["conv_dw"][l], grads["conv_dw_b"][l] = g_dw, g_dwb.reshape(-1)
        grads["conv_ln_g"][l], grads["conv_ln_b"][l] = g_clg.reshape(-1), g_clb.reshape(-1)
        du, dv_sg, dg_sg, g_slg, g_slb, g_sw, g_sb = _sg_bwd(
            s["proj"], dsgu, c["vec"](sg_ln_g), c["vec"](sg_ln_b), c["sg_w_b"], c["sg_wt_b"], c["sg_bias"], fold,
            f"sg_bwd_{l}")
        grads["sg_ln_g"][l], grads["sg_ln_b"][l] = g_slg.reshape(-1), g_slb.reshape(-1)
        grads["sg_w"][l], grads["sg_b"][l] = g_sw, g_sb[:, :SG_HEADS].T
        dproj = jnp.concatenate([d_qkv, dgatt, da, db, dg_conv, du, dv_sg, dg_sg], axis=-1).astype(BF16)
        grads["w_in"][l] = _matmul_acc(s["hb"], dproj, D_IN // 2, f"grad_w_in_{l}")
        dx, g_pre = _proj_bwd(dproj, w_in_full[l].T, s["x"], c["vec"](pre_norm), dx, f"proj_bwd_{l}")
        grads["pre_norm"][l] = g_pre.reshape(-1)
    grad_x = dx.reshape(x.shape)
    grads = {k: jnp.stack(v) for k, v in grads.items()}

    shard_blocks = dict(
        w_in=grads["w_in"].reshape(depth, D_MODEL, N_DEV, D_IN // N_DEV).transpose(2, 0, 1, 3),
        w_out=grads["w_out"].reshape(depth, N_DEV, D_MODEL // N_DEV, D_MODEL).transpose(1, 0, 2, 3),
        conv_dw=grads["conv_dw"].reshape(depth, CONV_K, N_DEV, CONV_W // N_DEV).transpose(2, 0, 1, 3))
    scatter_src = jnp.stack([_pack_shard({k: a[d] for k, a in shard_blocks.items()})
                             for d in range(N_DEV)]).astype(BF16)
    shard_slots, repl_slots = _exchange([scatter_src], [_pack_repl(grads, sse[0, 0])], "exchange_grads")

    shard_shapes = {k: weights[k].shape for k in SHARD_ROWS}
    gs, ds_, ms, vs = _sum_adamw(shard_slots, _pack_shard(weights), _pack_shard(mom_m), _pack_shard(mom_v),
                                 "adamw_sharded")
    gr, dr, mr, vr = _sum_adamw(repl_slots, _pack_repl(weights), _pack_repl(mom_m), _pack_repl(mom_v),
                                "adamw_replicated")
    loss = gr.reshape(-1)[REPL_USED] * (0.5 / D_MODEL)
    results = []
    for shard_flat, repl_flat in ((gs, gr), (ds_, dr), (ms, mr), (vs, vr)):
        parts = {**_unpack_shard(shard_flat, shard_shapes), **_unpack_repl(repl_flat)}
        results.append([parts[k] for k in WEIGHT_ORDER])
    return (loss, grad_x, *results[0], *results[1], *results[2], *results[3])
```

```python
import math

import jax
import jax.numpy as jnp
from jax import lax
from jax.experimental import pallas as pl
from jax.experimental.pallas import tpu as pltpu

F32, BF16 = jnp.float32, jnp.bfloat16

N_DEV = 8
MESH_AXES = ("x", "y", "c")
EPS = 1e-6
D_MODEL = 1024
HEAD_DIM = 64
ATT_HEADS, KV_HEADS = 8, 2
QKV_HEADS = ATT_HEADS + 2 * KV_HEADS
QKV_W = QKV_HEADS * HEAD_DIM
GROUP = ATT_HEADS // KV_HEADS
ATT_W, KV_W, CONV_W, SG_W = 512, 128, 256, 256
CONV_K, CONV_PAD, HALO = 31, 15, 16
SG_HEADS, SG_CHUNK = 4, 128
D_IN = 2816
GRID_W = 64
ROPE_THETA = 10000.0
LOG2E, LN2 = math.log2(math.e), math.log(2.0)
Q_SCALE = HEAD_DIM ** -0.5 * LOG2E
LANES = 128

COL_GATT, COL_A, COL_B, COL_GCONV, COL_U, COL_VSG, COL_GSG = 3, 5, 6, 7, 8, 9, 10

ADAM_LR, ADAM_B1, ADAM_B2, ADAM_EPS, ADAM_WD, ADAM_STEP = 0.001, 0.9, 0.999, 1e-08, 0.01, 10

T_ROW = 512
T_PREP = 2048
T_GROUP = 512
BQ, BK = 512, 512
HEADS_PER_STEP = 2
FWD_UNROLL, BWD_UNROLL = 8, 4
VMEM_MB = 56
ATTN_BWD_VMEM_MB = 58


def _pcall(body, *, name, grid, in_specs, out_specs, out_shape, scratch=(), sem=None, vmem_mb=None):
    params = {}
    if sem is not None:
        params["dimension_semantics"] = sem
    if vmem_mb is not None:
        params["vmem_limit_bytes"] = vmem_mb << 20
    return pl.pallas_call(body, name=name, grid=grid, in_specs=in_specs, out_specs=out_specs, out_shape=out_shape,
                          scratch_shapes=list(scratch), compiler_params=pltpu.CompilerParams(**params))


def _dot(a, b):
    return jnp.dot(a, b, preferred_element_type=F32)


def _sigmoid(x):
    return 1.0 / (1.0 + jnp.exp(-x))


def _silu_and_grad(x):
    s = _sigmoid(x)
    return x * s, s * (1.0 + x * (1.0 - s))


def _gelu_and_grad(x):
    cdf = 0.5 * (1.0 + lax.erf(x * (1.0 / math.sqrt(2.0))))
    pdf = jnp.exp(-0.5 * x * x) * (1.0 / math.sqrt(2.0 * math.pi))
    return x * cdf, cdf + x * pdf


def _split_dot(y, mat):
    hi = y.astype(BF16)
    lo = (y - hi.astype(F32)).astype(BF16)
    return _dot(hi, mat) + _dot(lo, mat)


def _row_tile(rows, cap):
    best = 8
    for t in range(8, min(rows, cap) + 1, 8):
        if rows % t == 0:
            best = t
    return best


def _exchange(scatter, gather, name):
    n_s = len(scatter)
    arrs = list(scatter) + list(gather)
    n = len(arrs)
    flips = [(fx, fy, fc) for fx in (0, 1) for fy in (0, 1) for fc in (0, 1)][1:]
    n_peer = len(flips)

    def body(*refs):
        ins, outs = refs[:n], refs[n:2 * n]
        send_sems, recv_sems, local_sems = refs[2 * n:]
        pos = tuple(lax.axis_index(a) for a in MESH_AXES)

        def peer(flip):
            return tuple((1 - p) if f else p for p, f in zip(pos, flip))

        def slot(p):
            return 4 * p[0] + 2 * p[1] + p[2]

        def src(a, p):
            return ins[a].at[slot(p)] if a < n_s else ins[a]

        def remote(a, k, src_ref, dst_slot, to):
            return pltpu.make_async_remote_copy(
                src_ref=src_ref, dst_ref=outs[a].at[dst_slot], send_sem=send_sems.at[a * n_peer + k],
                recv_sem=recv_sems.at[a * n_peer + k], device_id=to, device_id_type=pl.DeviceIdType.MESH)

        local = [pltpu.make_async_copy(src(a, pos), outs[a].at[slot(pos)], local_sems.at[a]) for a in range(n)]
        for cp in local:
            cp.start()
        sends = [remote(a, k, src(a, peer(f)), slot(pos), peer(f)) for a in range(n) for k, f in enumerate(flips)]
        for cp in sends:
            cp.start()
        for a in range(n):
            for k, f in enumerate(flips):
                remote(a, k, src(a, peer(f)), slot(peer(f)), peer(f)).wait_recv()
        for cp in sends:
            cp.wait_send()
        for cp in local:
            cp.wait()

    out_shape = [jax.ShapeDtypeStruct((N_DEV,) + (a.shape[1:] if i < n_s else a.shape), a.dtype)
                 for i, a in enumerate(arrs)]
    any_spec = pl.BlockSpec(memory_space=pl.ANY)
    return pl.pallas_call(
        body, name=name, out_shape=out_shape, in_specs=[any_spec] * n, out_specs=[any_spec] * n,
        scratch_shapes=[pltpu.SemaphoreType.DMA((n * n_peer,)), pltpu.SemaphoreType.DMA((n * n_peer,)),
                        pltpu.SemaphoreType.DMA((n,))],
    )(*arrs)


def _sum_adamw(slots, w, m, v, name):
    rows = w.shape[0]
    tr = _row_tile(rows, 1024)
    c1 = 1.0 - ADAM_B1 ** ADAM_STEP
    c2 = 1.0 - ADAM_B2 ** ADAM_STEP

    def body(s_ref, w_ref, m_ref, v_ref, g_out, d_out, m_out, v_out):
        g = s_ref[0].astype(F32)
        for d in range(1, N_DEV):
            g = g + s_ref[d].astype(F32)
        m_new = ADAM_B1 * m_ref[...] + (1.0 - ADAM_B1) * g
        v_new = ADAM_B2 * v_ref[...] + (1.0 - ADAM_B2) * (g * g)
        m_hat = m_new / c1
        v_hat = v_new / c2
        g_out[...] = g
        d_out[...] = -ADAM_LR * (m_hat / (jnp.sqrt(v_hat) + ADAM_EPS) + ADAM_WD * w_ref[...])
        m_out[...] = m_new
        v_out[...] = v_new

    flat = pl.BlockSpec((tr, LANES), lambda i: (i, 0))
    return _pcall(
        body, name=name, grid=(rows // tr,),
        in_specs=[pl.BlockSpec((N_DEV, tr, LANES), lambda i: (0, i, 0)), flat, flat, flat],
        out_specs=[flat] * 4, out_shape=[jax.ShapeDtypeStruct((rows, LANES), F32)] * 4,
        sem=("parallel",), vmem_mb=VMEM_MB)(slots, w, m, v)


def _proj_fwd(x, gain, w, name):
    seq = x.shape[0]

    def body(x_ref, g_ref, w_ref, proj_ref, hb_ref, qkv_t_ref):
        xf = x_ref[...]
        r = lax.rsqrt(jnp.mean(xf * xf, axis=-1, keepdims=True) + EPS)
        h = (xf * r * g_ref[...]).astype(BF16)
        hb_ref[...] = h
        proj = _dot(h, w_ref[...])
        proj_ref[...] = proj
        qkv_t_ref[...] = proj[:, :QKV_W].T

    return _pcall(
        body, name=name, grid=(seq // T_ROW,),
        in_specs=[pl.BlockSpec((T_ROW, D_MODEL), lambda i: (i, 0)), pl.BlockSpec((1, D_MODEL), lambda i: (0, 0)),
                  pl.BlockSpec((D_MODEL, D_IN), lambda i: (0, 0))],
        out_specs=[pl.BlockSpec((T_ROW, D_IN), lambda i: (i, 0)), pl.BlockSpec((T_ROW, D_MODEL), lambda i: (i, 0)),
                   pl.BlockSpec((QKV_W, T_ROW), lambda i: (0, i))],
        out_shape=[jax.ShapeDtypeStruct((seq, D_IN), F32), jax.ShapeDtypeStruct((seq, D_MODEL), BF16),
                   jax.ShapeDtypeStruct((QKV_W, seq), F32)],
        sem=("parallel",), vmem_mb=VMEM_MB)(x, gain, w)


def _proj_bwd(dproj, w_t, x, gain, dxo, name):
    seq = x.shape[0]

    def body(dp_ref, w_ref, x_ref, g_ref, dxo_ref, dx_ref, dg_ref):
        dh = _dot(dp_ref[...], w_ref[...])
        xf = x_ref[...]
        r = lax.rsqrt(jnp.mean(xf * xf, axis=-1, keepdims=True) + EPS)
        n = xf * r
        dn = dh * g_ref[...]
        dx_ref[...] = dxo_ref[...] + r * (dn - n * jnp.mean(dn * n, axis=-1, keepdims=True))

        @pl.when(pl.program_id(0) == 0)
        def _():
            dg_ref[...] = jnp.zeros_like(dg_ref)

        dg_ref[...] += jnp.sum(dh * n, axis=0, keepdims=True)

    row = pl.BlockSpec((T_ROW, D_MODEL), lambda i: (i, 0))
    vec = pl.BlockSpec((1, D_MODEL), lambda i: (0, 0))
    return _pcall(
        body, name=name, grid=(seq // T_ROW,),
        in_specs=[pl.BlockSpec((T_ROW, D_IN), lambda i: (i, 0)), pl.BlockSpec((D_IN, D_MODEL), lambda i: (0, 0)),
                  row, vec, row],
        out_specs=[row, vec],
        out_shape=[jax.ShapeDtypeStruct((seq, D_MODEL), F32), jax.ShapeDtypeStruct((1, D_MODEL), F32)],
        sem=("arbitrary",), vmem_mb=VMEM_MB)(dproj, w_t, x, gain, dxo)


def _matmul_acc(a, b, tn, name):
    seq, m = a.shape
    n = b.shape[1]
    ts = min(1024, seq)

    def body(a_ref, b_ref, o_ref):
        @pl.when(pl.program_id(1) == 0)
        def _():
            o_ref[...] = jnp.zeros_like(o_ref)

        o_ref[...] += lax.dot_general(a_ref[...], b_ref[...], (((0,), (0,)), ((), ())), preferred_element_type=F32)

    return _pcall(
        body, name=name, grid=(n // tn, seq // ts),
        in_specs=[pl.BlockSpec((ts, m), lambda j, k: (k, 0)), pl.BlockSpec((ts, tn), lambda j, k: (k, j))],
        out_specs=pl.BlockSpec((m, tn), lambda j, k: (0, j)), out_shape=jax.ShapeDtypeStruct((m, n), F32),
        sem=("parallel", "arbitrary"), vmem_mb=VMEM_MB)(a, b)


ROPE_HALF = HEAD_DIM // 4


def _rope_tables(seq):
    t = jnp.arange(seq, dtype=jnp.int32)
    row = (t // GRID_W).astype(F32)
    col = (t % GRID_W).astype(F32)
    inv_freq = ROPE_THETA ** (-jnp.arange(ROPE_HALF, dtype=F32) / ROPE_HALF)
    ang_r = row[:, None] * inv_freq[None, :]
    ang_c = col[:, None] * inv_freq[None, :]
    cos = jnp.concatenate([jnp.cos(ang_r), jnp.cos(ang_r), jnp.cos(ang_c), jnp.cos(ang_c)], axis=-1)
    sin = jnp.concatenate([-jnp.sin(ang_r), jnp.sin(ang_r), -jnp.sin(ang_c), jnp.sin(ang_c)], axis=-1)
    return cos.T, sin.T


def _rope_partner(y):
    h = ROPE_HALF
    return jnp.concatenate([y[h:2 * h], y[0:h], y[3 * h:4 * h], y[2 * h:3 * h]], axis=0)


def _qk_specs(seq, head0):
    t = min(T_PREP, seq)
    src = pl.BlockSpec((None, HEAD_DIM, t), lambda h, i: (h + head0, 0, i))
    own = pl.BlockSpec((None, HEAD_DIM, t), lambda h, i: (h, 0, i))
    nat = pl.BlockSpec((None, t, HEAD_DIM), lambda h, i: (h, i, 0))
    col = pl.BlockSpec((HEAD_DIM, 1), lambda h, i: (0, 0))
    tab = pl.BlockSpec((HEAD_DIM, t), lambda h, i: (0, i))
    return t, src, own, nat, col, tab


def _qk_prep_fwd(qkv_t, head0, heads, gain, scale, cos, sin, name):
    seq = qkv_t.shape[2]
    t, src, own, nat, col, tab = _qk_specs(seq, head0)

    def body(x_ref, g_ref, c_ref, s_ref, ot_ref, on_ref):
        xf = x_ref[...]
        r = lax.rsqrt(jnp.mean(xf * xf, axis=0, keepdims=True) + EPS)
        y = xf * r * g_ref[...]
        z = (y * c_ref[...] + _rope_partner(y) * s_ref[...]) * scale
        ot_ref[...] = z.astype(BF16)
        on_ref[...] = z.T.astype(BF16)

    return _pcall(body, name=name, grid=(heads, seq // t), in_specs=[src, col, tab, tab], out_specs=[own, nat],
                  out_shape=[jax.ShapeDtypeStruct((heads, HEAD_DIM, seq), BF16),
                             jax.ShapeDtypeStruct((heads, seq, HEAD_DIM), BF16)],
                  sem=("parallel", "parallel"))(qkv_t, gain, cos, sin)


def _qk_prep_bwd(qkv_t, head0, dout, dout_is_t, gain, scale, cos, sin, name):
    heads = dout.shape[0]
    seq = qkv_t.shape[2]
    t, src, own, nat, col, tab = _qk_specs(seq, head0)

    def body(x_ref, d_ref, g_ref, c_ref, s_ref, dx_ref, dg_ref):
        xf = x_ref[...]
        r = lax.rsqrt(jnp.mean(xf * xf, axis=0, keepdims=True) + EPS)
        n = xf * r
        d = d_ref[...] if dout_is_t else d_ref[...].T
        dz = d * scale
        dy = dz * c_ref[...] + _rope_partner(dz * s_ref[...])
        dn = dy * g_ref[...]
        dx_ref[...] = r * (dn - n * jnp.mean(dn * n, axis=0, keepdims=True))

        @pl.when(pl.program_id(1) == 0)
        def _():
            dg_ref[...] = jnp.zeros_like(dg_ref)

        dg_ref[...] += jnp.sum(dy * n, axis=1, keepdims=True)

    return _pcall(body, name=name, grid=(heads, seq // t),
                  in_specs=[src, own if dout_is_t else nat, col, tab, tab],
                  out_specs=[own, pl.BlockSpec((None, HEAD_DIM, 1), lambda h, i: (h, 0, 0))],
                  out_shape=[jax.ShapeDtypeStruct((heads, HEAD_DIM, seq), F32),
                             jax.ShapeDtypeStruct((heads, HEAD_DIM, 1), F32)],
                  sem=("parallel", "arbitrary"))(qkv_t, dout, gain, cos, sin)


V_ROWS = HEAD_DIM + 8


def _unroll(nk, cap):
    u = 1
    while u * 2 <= cap and nk % (u * 2) == 0:
        u *= 2
    return u


def _attn_fwd(qs_t, k, v_t, name):
    seq = qs_t.shape[2]
    nk, bk = k.shape[1], k.shape[2]
    bq = min(BQ, seq)
    unroll = _unroll(nk, FWD_UNROLL)
    heads = range(HEADS_PER_STEP)

    def body(qt_ref, k_ref, vt_ref, ot_ref, lse_ref, s_scr):
        q_t = [qt_ref[h] for h in heads]

        def scores(j, slot):
            kj = k_ref[j]
            top = []
            for h in heads:
                s = _dot(kj, q_t[h])
                s_scr[slot, h] = s
                top.append(jnp.max(s, axis=0, keepdims=True))
            return tuple(top)

        def accumulate(j, slot, state, top):
            vtj = vt_ref[j]
            out = []
            for h in heads:
                m, acc = state[h]
                m_new = jnp.maximum(m, top[h])
                p = jnp.exp2(s_scr[slot, h] - m_new).astype(BF16)
                out.append((m_new, jnp.exp2(m - m_new) * acc + _dot(vtj, p)))
            return tuple(out)

        def step(t, carry):
            state, top = carry
            for u in range(unroll):
                nxt = unroll * t + u + 1
                top_next = scores(jnp.minimum(nxt, nk - 1) if u == unroll - 1 else nxt, (u + 1) % 2)
                state = accumulate(unroll * t + u, u % 2, state, top)
                top = top_next
            return state, top

        init = tuple((jnp.full((1, bq), -jnp.inf, F32), jnp.zeros((V_ROWS, bq), F32)) for _ in heads)
        state, _ = lax.fori_loop(0, nk // unroll, step, (init, scores(0, 0)))
        for h in heads:
            m, acc = state[h]
            l = acc[HEAD_DIM:HEAD_DIM + 1, :]
            ot_ref[h] = acc[:HEAD_DIM, :] / l
            lse_ref[h] = m + jnp.log2(l)

    kv_of = lambda g: g * HEADS_PER_STEP // GROUP
    return _pcall(
        body, name=name, grid=(ATT_HEADS // HEADS_PER_STEP, seq // bq),
        in_specs=[pl.BlockSpec((HEADS_PER_STEP, HEAD_DIM, bq), lambda g, i: (g, 0, i)),
                  pl.BlockSpec((None, nk, bk, HEAD_DIM), lambda g, i: (kv_of(g), 0, 0, 0)),
                  pl.BlockSpec((None, nk, V_ROWS, bk), lambda g, i: (kv_of(g), 0, 0, 0))],
        out_specs=[pl.BlockSpec((HEADS_PER_STEP, HEAD_DIM, bq), lambda g, i: (g, 0, i)),
                   pl.BlockSpec((HEADS_PER_STEP, 1, bq), lambda g, i: (g, 0, i))],
        out_shape=[jax.ShapeDtypeStruct((ATT_HEADS, HEAD_DIM, seq), F32),
                   jax.ShapeDtypeStruct((ATT_HEADS, 1, seq), F32)],
        scratch=[pltpu.VMEM((2, HEADS_PER_STEP, bk, bq), F32)],
        sem=("parallel", "parallel"), vmem_mb=VMEM_MB)(qs_t, k, v_t)


def _attn_bwd(qs, qs_t, do_t, o_t, lse, k, k_t, v_t, name):
    seq = qs.shape[1]
    nk, bk = k.shape[1], k.shape[2]
    bq = min(BQ, seq)
    nq = seq // bq

    unroll = _unroll(nk, BWD_UNROLL)
    heads = range(HEADS_PER_STEP)
    pairs = GROUP // HEADS_PER_STEP

    def body(q_ref, qt_ref, dot_ref, ot_ref, lse_ref, k_ref, kt_ref, vt_ref, dq_ref, dkt_ref, dvt_ref,
             s_scr, dp_scr):
        @pl.when((pl.program_id(1) == 0) & (pl.program_id(2) == 0))
        def _():
            dkt_ref[...] = jnp.zeros_like(dkt_ref)
            dvt_ref[...] = jnp.zeros_like(dvt_ref)

        q, q_t = [q_ref[h] for h in heads], [qt_ref[h] for h in heads]
        do_t_b = [dot_ref[h].astype(BF16) for h in heads]
        do_l = [dot_ref[h].T * LN2 for h in heads]
        do_b = [d.astype(BF16) for d in do_l]
        delta = [jnp.sum(do_l[h] * ot_ref[h].T, axis=-1, keepdims=True) for h in heads]
        lse_col = [jnp.max(jnp.broadcast_to(lse_ref[h], (LANES, bq)).T, axis=-1, keepdims=True) for h in heads]

        def products(j, slot):
            ktj, vtj = kt_ref[j], vt_ref[j]
            for h in heads:
                s_scr[slot, h] = _dot(q[h], ktj)
                dp_scr[slot, h] = _dot(do_b[h], vtj).astype(BF16)

        def gradients(j, slot, dq):
            kj = k_ref[j]
            dvt = jnp.zeros((HEAD_DIM, bk), F32)
            dkt = jnp.zeros((HEAD_DIM, bk), F32)
            new = []
            for h in heads:
                p = jnp.exp2(s_scr[slot, h] - lse_col[h])
                ds = (p * (dp_scr[slot, h].astype(F32) - delta[h])).astype(BF16)
                dvt = dvt + _dot(do_t_b[h], p.astype(BF16))
                dkt = dkt + _dot(q_t[h], ds)
                new.append(dq[h] + _dot(ds, kj))
            dvt_ref[j] += dvt
            dkt_ref[j] += dkt
            return tuple(new)

        def step(t, dq):
            for u in range(unroll):
                nxt = unroll * t + u + 1
                products(jnp.minimum(nxt, nk - 1) if u == unroll - 1 else nxt, (u + 1) % 2)
                dq = gradients(unroll * t + u, u % 2, dq)
            return dq

        products(0, 0)
        res = lax.fori_loop(0, nk // unroll, step, tuple(jnp.zeros((bq, HEAD_DIM), F32) for _ in heads))
        for h in heads:
            dq_ref[h] = res[h]

    first = lambda g, hh: g * pairs + hh
    row = pl.BlockSpec((HEADS_PER_STEP, bq, HEAD_DIM), lambda g, hh, i: (first(g, hh), i, 0))
    col = pl.BlockSpec((HEADS_PER_STEP, HEAD_DIM, bq), lambda g, hh, i: (first(g, hh), 0, i))
    kv_rows = pl.BlockSpec((None, nk, bk, HEAD_DIM), lambda g, hh, i: (g, 0, 0, 0))
    kv_cols = pl.BlockSpec((None, nk, HEAD_DIM, bk), lambda g, hh, i: (g, 0, 0, 0))
    return _pcall(
        body, name=name, grid=(KV_HEADS, pairs, nq),
        in_specs=[row, col, col, col,
                  pl.BlockSpec((HEADS_PER_STEP, 1, bq), lambda g, hh, i: (first(g, hh), 0, i)),
                  kv_rows, kv_cols, kv_cols],
        out_specs=[row, kv_cols, kv_cols],
        out_shape=[jax.ShapeDtypeStruct((ATT_HEADS, seq, HEAD_DIM), F32),
                   jax.ShapeDtypeStruct((KV_HEADS, nk, HEAD_DIM, bk), F32),
                   jax.ShapeDtypeStruct((KV_HEADS, nk, HEAD_DIM, bk), F32)],
        scratch=[pltpu.VMEM((2, HEADS_PER_STEP, bq, bk), F32), pltpu.VMEM((2, HEADS_PER_STEP, bq, bk), BF16)],
        sem=("parallel", "arbitrary", "arbitrary"), vmem_mb=ATTN_BWD_VMEM_MB)(
            qs, qs_t, do_t, o_t, lse, k, k_t, v_t)


def _halo_specs(t, col, n_tiles):
    per = t // HALO
    last = n_tiles * per - 1
    before = pl.BlockSpec((HALO, CONV_W), lambda i: (jnp.maximum(i * per - 1, 0), col))
    after = pl.BlockSpec((HALO, CONV_W), lambda i: (jnp.minimum((i + 1) * per, last), col))
    return before, after


def _glu(a, b):
    return a * _sigmoid(b)


def _conv_taps(ext_ref, w_ref, t, flip):
    acc = jnp.zeros((t, CONV_W), F32)
    for k in range(CONV_K):
        off = (HALO + CONV_PAD - k) if flip else (HALO - CONV_PAD + k)
        acc = acc + w_ref[k:k + 1, :] * ext_ref[pl.ds(off, t), :]
    return acc


def _fill_ext(ext_ref, before, tile, after, t, i, n_tiles):
    ext_ref[pl.ds(0, HALO), :] = jnp.where(i > 0, before, 0.0)
    ext_ref[pl.ds(HALO, t), :] = tile
    ext_ref[pl.ds(HALO + t, HALO), :] = jnp.where(i < n_tiles - 1, after, 0.0)


def _conv_fwd(proj, w, bias, ln_g, ln_b, name):
    seq = proj.shape[0]
    t = min(T_GROUP, seq)
    n_tiles = seq // t

    def body(a_ref, b_ref, ap_ref, bp_ref, an_ref, bn_ref, gate_ref, w_ref, bias_ref, g_ref, beta_ref, o_ref, y_ref,
             ext_ref):
        i = pl.program_id(0)
        _fill_ext(ext_ref, _glu(ap_ref[...], bp_ref[...]), _glu(a_ref[...], b_ref[...]),
                  _glu(an_ref[...], bn_ref[...]), t, i, n_tiles)
        y = _conv_taps(ext_ref, w_ref, t, False) + bias_ref[...]
        y_ref[...] = y
        mu = jnp.mean(y, axis=-1, keepdims=True)
        yc = y - mu
        rs = lax.rsqrt(jnp.mean(yc * yc, axis=-1, keepdims=True) + EPS)
        z = yc * rs * g_ref[...] + beta_ref[...]
        o_ref[...] = _silu_and_grad(z)[0] * _silu_and_grad(gate_ref[...])[0]

    tile = lambda c: pl.BlockSpec((t, CONV_W), lambda i: (i, c))
    ab, aa = _halo_specs(t, COL_A, n_tiles)
    bb, ba = _halo_specs(t, COL_B, n_tiles)
    vec = pl.BlockSpec((1, CONV_W), lambda i: (0, 0))
    return _pcall(
        body, name=name, grid=(n_tiles,),
        in_specs=[tile(COL_A), tile(COL_B), ab, bb, aa, ba, tile(COL_GCONV),
                  pl.BlockSpec((CONV_K, CONV_W), lambda i: (0, 0)), vec, vec, vec],
        out_specs=[pl.BlockSpec((t, CONV_W), lambda i: (i, 0))] * 2,
        out_shape=[jax.ShapeDtypeStruct((seq, CONV_W), F32)] * 2,
        scratch=[pltpu.VMEM((t + 2 * HALO, CONV_W), F32)], sem=("parallel",))(
            proj, proj, proj, proj, proj, proj, proj, w, bias, ln_g, ln_b)


def _conv_bwd_a(proj, y_conv, dcnv, ln_g, ln_b, name):
    seq = proj.shape[0]
    t = min(T_GROUP, seq)
    n_tiles = seq // t

    def body(a_ref, b_ref, ap_ref, bp_ref, an_ref, bn_ref, gate_ref, y_ref, d_ref, g_ref, beta_ref,
             dy_ref, dgate_ref, dw_ref, dbias_ref, dg_ref, dbeta_ref, ext_ref):
        i = pl.program_id(0)
        _fill_ext(ext_ref, _glu(ap_ref[...], bp_ref[...]), _glu(a_ref[...], b_ref[...]),
                  _glu(an_ref[...], bn_ref[...]), t, i, n_tiles)
        y = y_ref[...]
        mu = jnp.mean(y, axis=-1, keepdims=True)
        yc = y - mu
        rs = lax.rsqrt(jnp.mean(yc * yc, axis=-1, keepdims=True) + EPS)
        n = yc * rs
        z = n * g_ref[...] + beta_ref[...]
        act, dact = _silu_and_grad(z)
        gate, dgate = _silu_and_grad(gate_ref[...])
        d = d_ref[...]
        dgate_ref[...] = d * act * dgate
        dz = d * gate * dact
        dn = dz * g_ref[...]
        dy = rs * (dn - jnp.mean(dn, axis=-1, keepdims=True) - n * jnp.mean(dn * n, axis=-1, keepdims=True))
        dy_ref[...] = dy

        @pl.when(i == 0)
        def _():
            dw_ref[...] = jnp.zeros_like(dw_ref)
            dbias_ref[...] = jnp.zeros_like(dbias_ref)
            dg_ref[...] = jnp.zeros_like(dg_ref)
            dbeta_ref[...] = jnp.zeros_like(dbeta_ref)

        dg_ref[...] += jnp.sum(dz * n, axis=0, keepdims=True)
        dbeta_ref[...] += jnp.sum(dz, axis=0, keepdims=True)
        dbias_ref[...] += jnp.sum(dy, axis=0, keepdims=True)
        for k in range(CONV_K):
            dw_ref[k:k + 1, :] += jnp.sum(dy * ext_ref[pl.ds(HALO - CONV_PAD + k, t), :], axis=0, keepdims=True)

    tile = lambda c: pl.BlockSpec((t, CONV_W), lambda i: (i, c))
    own = pl.BlockSpec((t, CONV_W), lambda i: (i, 0))
    ab, aa = _halo_specs(t, COL_A, n_tiles)
    bb, ba = _halo_specs(t, COL_B, n_tiles)
    vec = pl.BlockSpec((1, CONV_W), lambda i: (0, 0))
    taps = pl.BlockSpec((CONV_K, CONV_W), lambda i: (0, 0))
    vshape = jax.ShapeDtypeStruct((1, CONV_W), F32)
    return _pcall(
        body, name=name, grid=(n_tiles,),
        in_specs=[tile(COL_A), tile(COL_B), ab, bb, aa, ba, tile(COL_GCONV), own, own, vec, vec],
        out_specs=[own, own, taps, vec, vec, vec],
        out_shape=[jax.ShapeDtypeStruct((seq, CONV_W), F32), jax.ShapeDtypeStruct((seq, CONV_W), F32),
                   jax.ShapeDtypeStruct((CONV_K, CONV_W), F32), vshape, vshape, vshape],
        scratch=[pltpu.VMEM((t + 2 * HALO, CONV_W), F32)], sem=("arbitrary",))(
            proj, proj, proj, proj, proj, proj, proj, y_conv, dcnv, ln_g, ln_b)


def _conv_bwd_b(proj, dy, w, name):
    seq = proj.shape[0]
    t = min(T_GROUP, seq)
    n_tiles = seq // t

    def body(a_ref, b_ref, dy_ref, dyp_ref, dyn_ref, w_ref, da_ref, db_ref, ext_ref):
        i = pl.program_id(0)
        _fill_ext(ext_ref, dyp_ref[...], dy_ref[...], dyn_ref[...], t, i, n_tiles)
        dh = _conv_taps(ext_ref, w_ref, t, True)
        sig = _sigmoid(b_ref[...])
        da_ref[...] = dh * sig
        db_ref[...] = dh * a_ref[...] * sig * (1.0 - sig)

    tile = lambda c: pl.BlockSpec((t, CONV_W), lambda i: (i, c))
    own = pl.BlockSpec((t, CONV_W), lambda i: (i, 0))
    before, after = _halo_specs(t, 0, n_tiles)
    return _pcall(
        body, name=name, grid=(n_tiles,),
        in_specs=[tile(COL_A), tile(COL_B), own, before, after, pl.BlockSpec((CONV_K, CONV_W), lambda i: (0, 0))],
        out_specs=[own, own], out_shape=[jax.ShapeDtypeStruct((seq, CONV_W), F32)] * 2,
        scratch=[pltpu.VMEM((t + 2 * HALO, CONV_W), F32)], sem=("parallel",))(proj, proj, dy, dy, dy, w)


def _head_masks():
    lane_head = lax.broadcasted_iota(jnp.int32, (SG_CHUNK, SG_W), 1) // HEAD_DIM
    return [lane_head == h for h in range(SG_HEADS)]


def _sg_mix(mats_ref, rhs, masks):
    out = jnp.zeros((SG_CHUNK, SG_W), F32)
    for h in range(SG_HEADS):
        out = out + jnp.where(masks[h], _dot(mats_ref[h], rhs), 0.0)
    return out


def _sg_specs(seq):
    t = min(T_GROUP, seq)
    tile = lambda c: pl.BlockSpec((t, SG_W), lambda i: (i, c))
    own = pl.BlockSpec((t, SG_W), lambda i: (i, 0))
    vec = pl.BlockSpec((1, SG_W), lambda i: (0, 0))
    mats = pl.BlockSpec((SG_HEADS, SG_CHUNK, SG_CHUNK), lambda i: (0, 0, 0))
    full = pl.BlockSpec((SG_CHUNK, SG_W), lambda i: (0, 0))
    return t, tile, own, vec, mats, full


def _sg_fwd(proj, ln_g, ln_b, w_b, bias_full, name):
    seq = proj.shape[0]
    t, tile, own, vec, mats, full = _sg_specs(seq)

    def body(u_ref, v_ref, gate_ref, g_ref, beta_ref, w_ref, bias_ref, o_ref):
        masks = _head_masks()
        for c in range(t // SG_CHUNK):
            rows = pl.ds(c * SG_CHUNK, SG_CHUNK)
            vg = _gelu_and_grad(v_ref[rows, :])[0]
            mu = jnp.mean(vg, axis=-1, keepdims=True)
            vc = vg - mu
            rs = lax.rsqrt(jnp.mean(vc * vc, axis=-1, keepdims=True) + EPS)
            vln = vc * rs * g_ref[...] + beta_ref[...]
            mixed = _sg_mix(w_ref, vln.astype(BF16), masks) + bias_ref[...]
            o_ref[rows, :] = _gelu_and_grad(u_ref[rows, :])[0] * mixed * _silu_and_grad(gate_ref[rows, :])[0]

    return _pcall(body, name=name, grid=(seq // t,),
                  in_specs=[tile(COL_U), tile(COL_VSG), tile(COL_GSG), vec, vec, mats, full], out_specs=own,
                  out_shape=jax.ShapeDtypeStruct((seq, SG_W), F32), sem=("parallel",))(
                      proj, proj, proj, ln_g, ln_b, w_b, bias_full)


def _sg_bwd(proj, dsg, ln_g, ln_b, w_b, w_t_b, bias_full, fold, name):
    seq = proj.shape[0]
    t, tile, own, vec, mats, full = _sg_specs(seq)
    n_tiles = seq // t

    def body(u_ref, v_ref, gate_ref, d_ref, g_ref, beta_ref, w_ref, wt_ref, bias_ref, fold_ref,
             du_ref, dv_ref, dgate_ref, dg_ref, dbeta_ref, dw_ref, db_ref, dbias_acc):
        i = pl.program_id(0)

        @pl.when(i == 0)
        def _():
            dg_ref[...] = jnp.zeros_like(dg_ref)
            dbeta_ref[...] = jnp.zeros_like(dbeta_ref)
            dw_ref[...] = jnp.zeros_like(dw_ref)
            dbias_acc[...] = jnp.zeros_like(dbias_acc)

        masks = _head_masks()
        for c in range(t // SG_CHUNK):
            rows = pl.ds(c * SG_CHUNK, SG_CHUNK)
            ug, dug = _gelu_and_grad(u_ref[rows, :])
            vg, dvg = _gelu_and_grad(v_ref[rows, :])
            mu = jnp.mean(vg, axis=-1, keepdims=True)
            vc = vg - mu
            rs = lax.rsqrt(jnp.mean(vc * vc, axis=-1, keepdims=True) + EPS)
            vn = vc * rs
            vln_b = (vn * g_ref[...] + beta_ref[...]).astype(BF16)
            mixed = _sg_mix(w_ref, vln_b, masks) + bias_ref[...]
            gate, dgate = _silu_and_grad(gate_ref[rows, :])
            d = d_ref[rows, :]
            dgate_ref[rows, :] = d * ug * mixed * dgate
            du_ref[rows, :] = d * mixed * gate * dug
            dmixed = d * ug * gate
            dbias_acc[...] += dmixed
            dmixed_b = dmixed.astype(BF16)
            for h in range(SG_HEADS):
                dm_h = jnp.where(masks[h], dmixed_b, jnp.zeros_like(dmixed_b))
                dw_ref[h] += lax.dot_general(dm_h, vln_b, (((1,), (1,)), ((), ())), preferred_element_type=F32)
            dvln = _sg_mix(wt_ref, dmixed_b, masks)
            dg_ref[...] += jnp.sum(dvln * vn, axis=0, keepdims=True)
            dbeta_ref[...] += jnp.sum(dvln, axis=0, keepdims=True)
            dvn = dvln * g_ref[...]
            dvgelu = rs * (dvn - jnp.mean(dvn, axis=-1, keepdims=True) - vn * jnp.mean(dvn * vn, axis=-1, keepdims=True))
            dv_ref[rows, :] = dvgelu * dvg

        @pl.when(i == n_tiles - 1)
        def _():
            db_ref[...] = _split_dot(dbias_acc[...], fold_ref[...])

    sq = pl.BlockSpec((SG_CHUNK, SG_CHUNK), lambda i: (0, 0))
    vshape = jax.ShapeDtypeStruct((1, SG_W), F32)
    return _pcall(
        body, name=name, grid=(n_tiles,),
        in_specs=[tile(COL_U), tile(COL_VSG), tile(COL_GSG), own, vec, vec, mats, mats, full,
                  pl.BlockSpec((SG_W, SG_CHUNK), lambda i: (0, 0))],
        out_specs=[own, own, own, vec, vec, mats, sq],
        out_shape=[jax.ShapeDtypeStruct((seq, SG_W), F32)] * 3 + [
            vshape, vshape, jax.ShapeDtypeStruct((SG_HEADS, SG_CHUNK, SG_CHUNK), F32),
            jax.ShapeDtypeStruct((SG_CHUNK, SG_CHUNK), F32)],
        scratch=[pltpu.VMEM((SG_CHUNK, SG_W), F32)], sem=("arbitrary",))(
            proj, proj, proj, dsg, ln_g, ln_b, w_b, w_t_b, bias_full, fold)


def _out_fwd(att_t, proj, cnv, sgu, x, w, gain, name):
    seq = x.shape[0]

    def body(att_ref, g0_ref, g1_ref, cnv_ref, sgu_ref, x_ref, w_ref, gain_ref, xo_ref, mix_ref, cat_ref):
        gate = jnp.concatenate([_silu_and_grad(g0_ref[...])[0], _silu_and_grad(g1_ref[...])[0]], axis=-1)
        cat_ref[:, 0:ATT_W] = (att_ref[...].T * gate).astype(BF16)
        cat_ref[:, ATT_W:ATT_W + CONV_W] = cnv_ref[...].astype(BF16)
        cat_ref[:, ATT_W + CONV_W:] = sgu_ref[...].astype(BF16)
        mix = _dot(cat_ref[...], w_ref[...])
        mix_ref[...] = mix
        r = lax.rsqrt(jnp.mean(mix * mix, axis=-1, keepdims=True) + EPS)
        xo_ref[...] = x_ref[...] + mix * r * gain_ref[...]

    row = lambda w_: pl.BlockSpec((T_ROW, w_), lambda i: (i, 0))
    gate_blk = lambda c: pl.BlockSpec((T_ROW, 256), lambda i: (i, c))
    return _pcall(
        body, name=name, grid=(seq // T_ROW,),
        in_specs=[pl.BlockSpec((ATT_W, T_ROW), lambda i: (0, i)), gate_blk(COL_GATT), gate_blk(COL_GATT + 1),
                  row(CONV_W), row(SG_W), row(D_MODEL),
                  pl.BlockSpec((D_MODEL, D_MODEL), lambda i: (0, 0)), pl.BlockSpec((1, D_MODEL), lambda i: (0, 0))],
        out_specs=[row(D_MODEL), row(D_MODEL), row(D_MODEL)],
        out_shape=[jax.ShapeDtypeStruct((seq, D_MODEL), F32), jax.ShapeDtypeStruct((seq, D_MODEL), F32),
                   jax.ShapeDtypeStruct((seq, D_MODEL), BF16)],
        sem=("parallel",), vmem_mb=VMEM_MB)(att_t, proj, proj, cnv, sgu, x, w, gain)


def _out_bwd(dxo, mix, gain, w_t, att_t, proj, name):
    seq = dxo.shape[0]

    def body(dxo_ref, mix_ref, gain_ref, w_ref, att_ref, g0_ref, g1_ref,
             dmix_ref, datt_ref, dgatt_ref, dcnv_ref, dsgu_ref, dgain_ref):
        mix = mix_ref[...]
        r = lax.rsqrt(jnp.mean(mix * mix, axis=-1, keepdims=True) + EPS)
        n = mix * r
        dout = dxo_ref[...]
        dn = dout * gain_ref[...]
        dmix = (r * (dn - n * jnp.mean(dn * n, axis=-1, keepdims=True))).astype(BF16)
        dmix_ref[...] = dmix

        @pl.when(pl.program_id(0) == 0)
        def _():
            dgain_ref[...] = jnp.zeros_like(dgain_ref)

        dgain_ref[...] += jnp.sum(dout * n, axis=0, keepdims=True)
        dcat = _dot(dmix, w_ref[...])
        g0, dg0 = _silu_and_grad(g0_ref[...])
        g1, dg1 = _silu_and_grad(g1_ref[...])
        gate = jnp.concatenate([g0, g1], axis=-1)
        dgate = jnp.concatenate([dg0, dg1], axis=-1)
        dca = dcat[:, 0:ATT_W]
        datt_ref[...] = (dca * gate).T
        dgatt_ref[...] = dca * att_ref[...].T * dgate
        dcnv_ref[...] = dcat[:, ATT_W:ATT_W + CONV_W]
        dsgu_ref[...] = dcat[:, ATT_W + CONV_W:]

    row = lambda w_: pl.BlockSpec((T_ROW, w_), lambda i: (i, 0))
    gate_blk = lambda c: pl.BlockSpec((T_ROW, 256), lambda i: (i, c))
    vec = pl.BlockSpec((1, D_MODEL), lambda i: (0, 0))
    heads_t = pl.BlockSpec((ATT_W, T_ROW), lambda i: (0, i))
    return _pcall(
        body, name=name, grid=(seq // T_ROW,),
        in_specs=[row(D_MODEL), row(D_MODEL), vec, pl.BlockSpec((D_MODEL, D_MODEL), lambda i: (0, 0)), heads_t,
                  gate_blk(COL_GATT), gate_blk(COL_GATT + 1)],
        out_specs=[row(D_MODEL), heads_t, row(ATT_W), row(CONV_W), row(SG_W), vec],
        out_shape=[jax.ShapeDtypeStruct((seq, D_MODEL), BF16), jax.ShapeDtypeStruct((ATT_W, seq), F32),
                   jax.ShapeDtypeStruct((seq, ATT_W), F32), jax.ShapeDtypeStruct((seq, CONV_W), F32),
                   jax.ShapeDtypeStruct((seq, SG_W), F32), jax.ShapeDtypeStruct((1, D_MODEL), F32)],
        sem=("arbitrary",), vmem_mb=VMEM_MB)(dxo, mix, gain, w_t, att_t, proj, proj)


def _loss_head(y, target, name):
    seq = y.shape[0]
    t = min(T_GROUP, seq)

    def body(y_ref, t_ref, sse_ref, dy_ref):
        err = y_ref[...] - t_ref[...]
        dy_ref[...] = err * (1.0 / D_MODEL)

        @pl.when(pl.program_id(0) == 0)
        def _():
            sse_ref[...] = jnp.zeros_like(sse_ref)

        part = jnp.sum(jnp.sum(err * err, axis=0, keepdims=True), axis=-1, keepdims=True)
        sse_ref[...] += jnp.broadcast_to(part, (1, LANES))

    row = pl.BlockSpec((t, D_MODEL), lambda i: (i, 0))
    return _pcall(body, name=name, grid=(seq // t,), in_specs=[row, row],
                  out_specs=[pl.BlockSpec((1, LANES), lambda i: (0, 0)), row],
                  out_shape=[jax.ShapeDtypeStruct((1, LANES), F32), jax.ShapeDtypeStruct((seq, D_MODEL), F32)],
                  sem=("arbitrary",))(y, target)


def _row_blocks(a, bk):
    return a.reshape(a.shape[0], a.shape[1] // bk, bk, HEAD_DIM)


def _lane_blocks(a, bk):
    return a.reshape(a.shape[0], HEAD_DIM, a.shape[2] // bk, bk).transpose(0, 2, 1, 3)


def _from_lane_blocks(a):
    return a.transpose(0, 2, 1, 3).reshape(a.shape[0], HEAD_DIM, a.shape[1] * a.shape[3])


def _flat_rows(a, rows):
    flat = a.reshape(-1)
    return jnp.pad(flat, (0, rows * LANES - flat.shape[0])).reshape(rows, LANES)


SHARD_ROWS = {"w_in": 2 * D_MODEL * (D_IN // N_DEV) // LANES, "w_out": 2 * (D_MODEL // N_DEV) * D_MODEL // LANES,
              "conv_dw": 16}
REPL_SHAPES = [("pre_norm", (2, D_MODEL)), ("post_norm", (2, D_MODEL)), ("q_norm", (2, HEAD_DIM)),
               ("k_norm", (2, HEAD_DIM)), ("conv_dw_b", (2, CONV_W)), ("conv_ln_g", (2, CONV_W)),
               ("conv_ln_b", (2, CONV_W)), ("sg_ln_g", (2, SG_W)), ("sg_ln_b", (2, SG_W)),
               ("sg_w", (2, SG_HEADS, SG_CHUNK, SG_CHUNK)), ("sg_b", (2, SG_HEADS, SG_CHUNK))]
REPL_ROWS = 1088
WEIGHT_ORDER = ["pre_norm", "post_norm", "w_in", "w_out", "q_norm", "k_norm", "conv_dw", "conv_dw_b", "conv_ln_g",
                "conv_ln_b", "sg_ln_g", "sg_ln_b", "sg_w", "sg_b"]


def _pack_shard(parts):
    return jnp.concatenate([_flat_rows(parts[k], SHARD_ROWS[k]) for k in ("w_in", "w_out", "conv_dw")], axis=0)


def _unpack_shard(flat, shapes):
    out, at = {}, 0
    for k in ("w_in", "w_out", "conv_dw"):
        size = math.prod(shapes[k])
        out[k] = flat[at:at + SHARD_ROWS[k]].reshape(-1)[:size].reshape(shapes[k])
        at += SHARD_ROWS[k]
    return out


REPL_USED = sum(math.prod(shape) for _, shape in REPL_SHAPES)


def _pack_repl(parts, extra=None):
    tail = [] if extra is None else [extra.reshape(1)]
    flat = jnp.concatenate([parts[k].reshape(-1) for k, _ in REPL_SHAPES] + tail)
    return jnp.pad(flat, (0, REPL_ROWS * LANES - flat.shape[0])).reshape(REPL_ROWS, LANES)


def _unpack_repl(flat):
    out, at, flat = {}, 0, flat.reshape(-1)
    for k, shape in REPL_SHAPES:
        size = math.prod(shape)
        out[k] = flat[at:at + size].reshape(shape)
        at += size
    return out


def kernel(x, pre_norm, post_norm, w_in, w_out, q_norm, k_norm, conv_dw, conv_dw_b, conv_ln_g, conv_ln_b, sg_ln_g, sg_ln_b, sg_w, sg_b, loss_target, m_pre_norm, m_post_norm, m_w_in, m_w_out, m_q_norm, m_k_norm, m_conv_dw, m_conv_dw_b, m_conv_ln_g, m_conv_ln_b, m_sg_ln_g, m_sg_ln_b, m_sg_w, m_sg_b, v_pre_norm, v_post_norm, v_w_in, v_w_out, v_q_norm, v_k_norm, v_conv_dw, v_conv_dw_b, v_conv_ln_g, v_conv_ln_b, v_sg_ln_g, v_sg_ln_b, v_sg_w, v_sg_b):
    weights = dict(pre_norm=pre_norm, post_norm=post_norm, w_in=w_in, w_out=w_out, q_norm=q_norm, k_norm=k_norm,
                   conv_dw=conv_dw, conv_dw_b=conv_dw_b, conv_ln_g=conv_ln_g, conv_ln_b=conv_ln_b, sg_ln_g=sg_ln_g,
                   sg_ln_b=sg_ln_b, sg_w=sg_w, sg_b=sg_b)
    mom_m = dict(pre_norm=m_pre_norm, post_norm=m_post_norm, w_in=m_w_in, w_out=m_w_out, q_norm=m_q_norm,
                 k_norm=m_k_norm, conv_dw=m_conv_dw, conv_dw_b=m_conv_dw_b, conv_ln_g=m_conv_ln_g,
                 conv_ln_b=m_conv_ln_b, sg_ln_g=m_sg_ln_g, sg_ln_b=m_sg_ln_b, sg_w=m_sg_w, sg_b=m_sg_b)
    mom_v = dict(pre_norm=v_pre_norm, post_norm=v_post_norm, w_in=v_w_in, w_out=v_w_out, q_norm=v_q_norm,
                 k_norm=v_k_norm, conv_dw=v_conv_dw, conv_dw_b=v_conv_dw_b, conv_ln_g=v_conv_ln_g,
                 conv_ln_b=v_conv_ln_b, sg_ln_g=v_sg_ln_g, sg_ln_b=v_sg_ln_b, sg_w=v_sg_w, sg_b=v_sg_b)
    depth = pre_norm.shape[0]
    seq = x.shape[1]
    bk = min(BK, seq)
    x0 = x.reshape(seq, D_MODEL)
    target = loss_target.reshape(seq, D_MODEL)

    w_in_all, w_out_all, dw_all = _exchange(
        [], [w_in.astype(BF16), w_out.astype(BF16), jnp.pad(conv_dw, ((0, 0), (0, 1), (0, 0)))], "gather_weights")
    w_in_full = w_in_all.transpose(1, 2, 0, 3).reshape(depth, D_MODEL, D_IN)
    w_out_full = w_out_all.transpose(1, 0, 2, 3).reshape(depth, D_MODEL, D_MODEL)
    dw_full = dw_all[:, :, :CONV_K, :].transpose(1, 2, 0, 3).reshape(depth, CONV_K, CONV_W)

    cos, sin = _rope_tables(seq)
    lane = jnp.arange(SG_W)
    fold = (lane[:, None] // HEAD_DIM == jnp.arange(SG_CHUNK)[None, :]).astype(BF16)

    def layer_consts(l):
        return dict(
            q_gain=q_norm[l].reshape(HEAD_DIM, 1), k_gain=k_norm[l].reshape(HEAD_DIM, 1), sg_w_b=sg_w[l].astype(BF16), sg_wt_b=sg_w[l].transpose(0, 2, 1).astype(BF16),
            sg_bias=jnp.repeat(sg_b[l].T, HEAD_DIM, axis=1),
            vec=lambda a: a[l].reshape(1, -1))

    saved = []
    xc = x0
    for l in range(depth):
        c = layer_consts(l)
        proj, hb, qkv_t = _proj_fwd(xc, c["vec"](pre_norm), w_in_full[l], f"proj_fwd_{l}")
        qkv_t = qkv_t.reshape(QKV_HEADS, HEAD_DIM, seq)
        qs_t, qs = _qk_prep_fwd(qkv_t, 0, ATT_HEADS, c["q_gain"], Q_SCALE, cos, sin, f"q_prep_fwd_{l}")
        kr_t, kr = _qk_prep_fwd(qkv_t, ATT_HEADS, KV_HEADS, c["k_gain"], 1.0, cos, sin, f"k_prep_fwd_{l}")
        v_t = qkv_t[ATT_HEADS + KV_HEADS:].astype(BF16)
        k_rows, k_cols, v_cols = _row_blocks(kr, bk), _lane_blocks(kr_t, bk), _lane_blocks(v_t, bk)
        v_ext = jnp.concatenate([v_cols, jnp.ones_like(v_cols[:, :, :1]), jnp.zeros_like(v_cols[:, :, :7])], axis=2)
        o_t, lse = _attn_fwd(qs_t, k_rows, v_ext, f"attn_fwd_{l}")
        att_t = o_t.reshape(ATT_W, seq)
        cnv, y_conv = _conv_fwd(proj, dw_full[l], c["vec"](conv_dw_b), c["vec"](conv_ln_g), c["vec"](conv_ln_b),
                                f"conv_fwd_{l}")
        sgu = _sg_fwd(proj, c["vec"](sg_ln_g), c["vec"](sg_ln_b), c["sg_w_b"], c["sg_bias"], f"sg_fwd_{l}")
        x_new, mix, cat_b = _out_fwd(att_t, proj, cnv, sgu, xc, w_out_full[l], c["vec"](post_norm), f"out_fwd_{l}")
        saved.append(dict(x=xc, proj=proj, hb=hb, qkv_t=qkv_t, qs=qs, qs_t=qs_t, k_rows=k_rows,
                          k_cols=k_cols, v_cols=v_cols, o_t=o_t, lse=lse, mix=mix, cat_b=cat_b, y_conv=y_conv))
        xc = x_new

    sse, dx = _loss_head(xc, target, "loss_head")

    grads = {k: [None] * depth for k in WEIGHT_ORDER}
    for l in reversed(range(depth)):
        c, s = layer_consts(l), saved[l]
        dmix_b, datt, dgatt, dcnv, dsgu, g_post = _out_bwd(
            dx, s["mix"], c["vec"](post_norm), w_out_full[l].T, s["o_t"].reshape(ATT_W, seq), s["proj"],
            f"out_bwd_{l}")
        grads["post_norm"][l] = g_post.reshape(-1)
        grads["w_out"][l] = _matmul_acc(s["cat_b"], dmix_b, D_MODEL, f"grad_w_out_{l}")
        dqs, dkt, dvt = _attn_bwd(s["qs"], s["qs_t"], datt.reshape(ATT_HEADS, HEAD_DIM, seq), s["o_t"], s["lse"],
                                  s["k_rows"], s["k_cols"], s["v_cols"], f"attn_bwd_{l}")
        d_q_t, g_qgain = _qk_prep_bwd(s["qkv_t"], 0, dqs, False, c["q_gain"], Q_SCALE, cos, sin, f"q_prep_bwd_{l}")
        d_k_t, g_kgain = _qk_prep_bwd(s["qkv_t"], ATT_HEADS, _from_lane_blocks(dkt), True, c["k_gain"], 1.0, cos, sin,
                                      f"k_prep_bwd_{l}")
        grads["q_norm"][l] = jnp.sum(g_qgain[:, :, 0], axis=0)
        grads["k_norm"][l] = jnp.sum(g_kgain[:, :, 0], axis=0)
        d_qkv = jnp.concatenate([d_q_t, d_k_t, _from_lane_blocks(dvt)], axis=0).reshape(QKV_W, seq).T
        dy_conv, dg_conv, g_dw, g_dwb, g_clg, g_clb = _conv_bwd_a(
            s["proj"], s["y_conv"], dcnv, c["vec"](conv_ln_g), c["vec"](conv_ln_b), f"conv_bwd_a_{l}")
        da, db = _conv_bwd_b(s["proj"], dy_conv, dw_full[l], f"conv_bwd_b_{l}")
        grads["conv_dw"][l], grads["conv_dw_b"][l] = g_dw, g_dwb.reshape(-1)
        grads["conv_ln_g"][l], grads["conv_ln_b"][l] = g_clg.reshape(-1), g_clb.reshape(-1)
        du, dv_sg, dg_sg, g_slg, g_slb, g_sw, g_sb = _sg_bwd(
            s["proj"], dsgu, c["vec"](sg_ln_g), c["vec"](sg_ln_b), c["sg_w_b"], c["sg_wt_b"], c["sg_bias"], fold,
            f"sg_bwd_{l}")
        grads["sg_ln_g"][l], grads["sg_ln_b"][l] = g_slg.reshape(-1), g_slb.reshape(-1)
        grads["sg_w"][l], grads["sg_b"][l] = g_sw, g_sb[:, :SG_HEADS].T
        dproj = jnp.concatenate([d_qkv, dgatt, da, db, dg_conv, du, dv_sg, dg_sg], axis=-1).astype(BF16)
        grads["w_in"][l] = _matmul_acc(s["hb"], dproj, D_IN // 2, f"grad_w_in_{l}")
        dx, g_pre = _proj_bwd(dproj, w_in_full[l].T, s["x"], c["vec"](pre_norm), dx, f"proj_bwd_{l}")
        grads["pre_norm"][l] = g_pre.reshape(-1)
    grad_x = dx.reshape(x.shape)
    grads = {k: jnp.stack(v) for k, v in grads.items()}

    shard_blocks = dict(
        w_in=grads["w_in"].reshape(depth, D_MODEL, N_DEV, D_IN // N_DEV).transpose(2, 0, 1, 3),
        w_out=grads["w_out"].reshape(depth, N_DEV, D_MODEL // N_DEV, D_MODEL).transpose(1, 0, 2, 3),
        conv_dw=grads["conv_dw"].reshape(depth, CONV_K, N_DEV, CONV_W // N_DEV).transpose(2, 0, 1, 3))
    scatter_src = jnp.stack([_pack_shard({k: a[d] for k, a in shard_blocks.items()})
                             for d in range(N_DEV)]).astype(BF16)
    shard_slots, repl_slots = _exchange([scatter_src], [_pack_repl(grads, sse[0, 0])], "exchange_grads")

    shard_shapes = {k: weights[k].shape for k in SHARD_ROWS}
    gs, ds_, ms, vs = _sum_adamw(shard_slots, _pack_shard(weights), _pack_shard(mom_m), _pack_shard(mom_v),
                                 "adamw_sharded")
    gr, dr, mr, vr = _sum_adamw(repl_slots, _pack_repl(weights), _pack_repl(mom_m), _pack_repl(mom_v),
                                "adamw_replicated")
    loss = gr.reshape(-1)[REPL_USED] * (0.5 / D_MODEL)
    results = []
    for shard_flat, repl_flat in ((gs, gr), (ds_, dr), (ms, mr), (vs, vr)):
        parts = {**_unpack_shard(shard_flat, shard_shapes), **_unpack_repl(repl_flat)}
        results.append([parts[k] for k in WEIGHT_ORDER])
    return (loss, grad_x, *results[0], *results[1], *results[2], *results[3])
```

```python
import math

import jax
import jax.numpy as jnp
from jax import lax
from jax.experimental import pallas as pl
from jax.experimental.pallas import tpu as pltpu

F32, BF16 = jnp.float32, jnp.bfloat16

N_DEV = 8
MESH_AXES = ("x", "y", "c")
EPS = 1e-6
D_MODEL = 1024
HEAD_DIM = 64
ATT_HEADS, KV_HEADS = 8, 2
QKV_HEADS = ATT_HEADS + 2 * KV_HEADS
QKV_W = QKV_HEADS * HEAD_DIM
GROUP = ATT_HEADS // KV_HEADS
ATT_W, KV_W, CONV_W, SG_W = 512, 128, 256, 256
CONV_K, CONV_PAD, HALO = 31, 15, 16
SG_HEADS, SG_CHUNK = 4, 128
D_IN = 2816
GRID_W = 64
ROPE_THETA = 10000.0
LOG2E, LN2 = math.log2(math.e), math.log(2.0)
Q_SCALE = HEAD_DIM ** -0.5 * LOG2E
LANES = 128

COL_GATT, COL_A, COL_B, COL_GCONV, COL_U, COL_VSG, COL_GSG = 3, 5, 6, 7, 8, 9, 10

ADAM_LR, ADAM_B1, ADAM_B2, ADAM_EPS, ADAM_WD, ADAM_STEP = 0.001, 0.9, 0.999, 1e-08, 0.01, 10

T_ROW = 512
T_PREP = 2048
T_GROUP = 512
BQ, BK = 512, 512
HEADS_PER_STEP = 2
FWD_HEADS_PER_STEP = 2
FWD_UNROLL, BWD_UNROLL = 8, 4
VMEM_MB = 56
ATTN_BWD_VMEM_MB = 58


def _pcall(body, *, name, grid, in_specs, out_specs, out_shape, scratch=(), sem=None, vmem_mb=None):
    params = {}
    if sem is not None:
        params["dimension_semantics"] = sem
    if vmem_mb is not None:
        params["vmem_limit_bytes"] = vmem_mb << 20
    return pl.pallas_call(body, name=name, grid=grid, in_specs=in_specs, out_specs=out_specs, out_shape=out_shape,
                          scratch_shapes=list(scratch), compiler_params=pltpu.CompilerParams(**params))


def _dot(a, b):
    return jnp.dot(a, b, preferred_element_type=F32)


def _sigmoid(x):
    return 1.0 / (1.0 + jnp.exp(-x))


def _silu_and_grad(x):
    s = _sigmoid(x)
    return x * s, s * (1.0 + x * (1.0 - s))


def _gelu_and_grad(x):
    cdf = 0.5 * (1.0 + lax.erf(x * (1.0 / math.sqrt(2.0))))
    pdf = jnp.exp(-0.5 * x * x) * (1.0 / math.sqrt(2.0 * math.pi))
    return x * cdf, cdf + x * pdf


def _split_dot(y, mat):
    hi = y.astype(BF16)
    lo = (y - hi.astype(F32)).astype(BF16)
    return _dot(hi, mat) + _dot(lo, mat)


def _row_tile(rows, cap):
    best = 8
    for t in range(8, min(rows, cap) + 1, 8):
        if rows % t == 0:
            best = t
    return best


def _exchange(scatter, gather, name):
    n_s = len(scatter)
    arrs = list(scatter) + list(gather)
    n = len(arrs)
    flips = [(fx, fy, fc) for fx in (0, 1) for fy in (0, 1) for fc in (0, 1)][1:]
    n_peer = len(flips)

    def body(*refs):
        ins, outs = refs[:n], refs[n:2 * n]
        send_sems, recv_sems, local_sems = refs[2 * n:]
        pos = tuple(lax.axis_index(a) for a in MESH_AXES)

        def peer(flip):
            return tuple((1 - p) if f else p for p, f in zip(pos, flip))

        def slot(p):
            return 4 * p[0] + 2 * p[1] + p[2]

        def src(a, p):
            return ins[a].at[slot(p)] if a < n_s else ins[a]

        def remote(a, k, src_ref, dst_slot, to):
            return pltpu.make_async_remote_copy(
                src_ref=src_ref, dst_ref=outs[a].at[dst_slot], send_sem=send_sems.at[a * n_peer + k],
                recv_sem=recv_sems.at[a * n_peer + k], device_id=to, device_id_type=pl.DeviceIdType.MESH)

        local = [pltpu.make_async_copy(src(a, pos), outs[a].at[slot(pos)], local_sems.at[a]) for a in range(n)]
        for cp in local:
            cp.start()
        sends = [remote(a, k, src(a, peer(f)), slot(pos), peer(f)) for a in range(n) for k, f in enumerate(flips)]
        for cp in sends:
            cp.start()
        for a in range(n):
            for k, f in enumerate(flips):
                remote(a, k, src(a, peer(f)), slot(peer(f)), peer(f)).wait_recv()
        for cp in sends:
            cp.wait_send()
        for cp in local:
            cp.wait()

    out_shape = [jax.ShapeDtypeStruct((N_DEV,) + (a.shape[1:] if i < n_s else a.shape), a.dtype)
                 for i, a in enumerate(arrs)]
    any_spec = pl.BlockSpec(memory_space=pl.ANY)
    return pl.pallas_call(
        body, name=name, out_shape=out_shape, in_specs=[any_spec] * n, out_specs=[any_spec] * n,
        scratch_shapes=[pltpu.SemaphoreType.DMA((n * n_peer,)), pltpu.SemaphoreType.DMA((n * n_peer,)),
                        pltpu.SemaphoreType.DMA((n,))],
    )(*arrs)


def _sum_adamw(slots, w, m, v, name):
    rows = w.shape[0]
    tr = _row_tile(rows, 1024)
    c1 = 1.0 - ADAM_B1 ** ADAM_STEP
    c2 = 1.0 - ADAM_B2 ** ADAM_STEP

    def body(s_ref, w_ref, m_ref, v_ref, g_out, d_out, m_out, v_out):
        g = s_ref[0].astype(F32)
        for d in range(1, N_DEV):
            g = g + s_ref[d].astype(F32)
        m_new = ADAM_B1 * m_ref[...] + (1.0 - ADAM_B1) * g
        v_new = ADAM_B2 * v_ref[...] + (1.0 - ADAM_B2) * (g * g)
        m_hat = m_new / c1
        v_hat = v_new / c2
        g_out[...] = g
        d_out[...] = -ADAM_LR * (m_hat / (jnp.sqrt(v_hat) + ADAM_EPS) + ADAM_WD * w_ref[...])
        m_out[...] = m_new
        v_out[...] = v_new

    flat = pl.BlockSpec((tr, LANES), lambda i: (i, 0))
    return _pcall(
        body, name=name, grid=(rows // tr,),
        in_specs=[pl.BlockSpec((N_DEV, tr, LANES), lambda i: (0, i, 0)), flat, flat, flat],
        out_specs=[flat] * 4, out_shape=[jax.ShapeDtypeStruct((rows, LANES), F32)] * 4,
        sem=("parallel",), vmem_mb=VMEM_MB)(slots, w, m, v)


def _proj_fwd(x, gain, w, name):
    seq = x.shape[0]

    def body(x_ref, g_ref, w_ref, proj_ref, hb_ref, qkv_t_ref):
        xf = x_ref[...]
        r = lax.rsqrt(jnp.mean(xf * xf, axis=-1, keepdims=True) + EPS)
        h = (xf * r * g_ref[...]).astype(BF16)
        hb_ref[...] = h
        proj = _dot(h, w_ref[...])
        proj_ref[...] = proj
        qkv_t_ref[...] = proj[:, :QKV_W].T

    return _pcall(
        body, name=name, grid=(seq // T_ROW,),
        in_specs=[pl.BlockSpec((T_ROW, D_MODEL), lambda i: (i, 0)), pl.BlockSpec((1, D_MODEL), lambda i: (0, 0)),
                  pl.BlockSpec((D_MODEL, D_IN), lambda i: (0, 0))],
        out_specs=[pl.BlockSpec((T_ROW, D_IN), lambda i: (i, 0)), pl.BlockSpec((T_ROW, D_MODEL), lambda i: (i, 0)),
                   pl.BlockSpec((QKV_W, T_ROW), lambda i: (0, i))],
        out_shape=[jax.ShapeDtypeStruct((seq, D_IN), F32), jax.ShapeDtypeStruct((seq, D_MODEL), BF16),
                   jax.ShapeDtypeStruct((QKV_W, seq), F32)],
        sem=("parallel",), vmem_mb=VMEM_MB)(x, gain, w)


def _proj_bwd(dproj, w_t, x, gain, dxo, name):
    seq = x.shape[0]

    def body(dp_ref, w_ref, x_ref, g_ref, dxo_ref, dx_ref, dg_ref):
        dh = _dot(dp_ref[...], w_ref[...])
        xf = x_ref[...]
        r = lax.rsqrt(jnp.mean(xf * xf, axis=-1, keepdims=True) + EPS)
        n = xf * r
        dn = dh * g_ref[...]
        dx_ref[...] = dxo_ref[...] + r * (dn - n * jnp.mean(dn * n, axis=-1, keepdims=True))

        @pl.when(pl.program_id(0) == 0)
        def _():
            dg_ref[...] = jnp.zeros_like(dg_ref)

        dg_ref[...] += jnp.sum(dh * n, axis=0, keepdims=True)

    row = pl.BlockSpec((T_ROW, D_MODEL), lambda i: (i, 0))
    vec = pl.BlockSpec((1, D_MODEL), lambda i: (0, 0))
    return _pcall(
        body, name=name, grid=(seq // T_ROW,),
        in_specs=[pl.BlockSpec((T_ROW, D_IN), lambda i: (i, 0)), pl.BlockSpec((D_IN, D_MODEL), lambda i: (0, 0)),
                  row, vec, row],
        out_specs=[row, vec],
        out_shape=[jax.ShapeDtypeStruct((seq, D_MODEL), F32), jax.ShapeDtypeStruct((1, D_MODEL), F32)],
        sem=("arbitrary",), vmem_mb=VMEM_MB)(dproj, w_t, x, gain, dxo)


def _matmul_acc(a, b, tn, name):
    seq, m = a.shape
    n = b.shape[1]
    ts = min(1024, seq)

    def body(a_ref, b_ref, o_ref):
        @pl.when(pl.program_id(1) == 0)
        def _():
            o_ref[...] = jnp.zeros_like(o_ref)

        o_ref[...] += lax.dot_general(a_ref[...], b_ref[...], (((0,), (0,)), ((), ())), preferred_element_type=F32)

    return _pcall(
        body, name=name, grid=(n // tn, seq // ts),
        in_specs=[pl.BlockSpec((ts, m), lambda j, k: (k, 0)), pl.BlockSpec((ts, tn), lambda j, k: (k, j))],
        out_specs=pl.BlockSpec((m, tn), lambda j, k: (0, j)), out_shape=jax.ShapeDtypeStruct((m, n), F32),
        sem=("parallel", "arbitrary"), vmem_mb=VMEM_MB)(a, b)


ROPE_HALF = HEAD_DIM // 4


def _rope_tables(seq):
    t = jnp.arange(seq, dtype=jnp.int32)
    row = (t // GRID_W).astype(F32)
    col = (t % GRID_W).astype(F32)
    inv_freq = ROPE_THETA ** (-jnp.arange(ROPE_HALF, dtype=F32) / ROPE_HALF)
    ang_r = row[:, None] * inv_freq[None, :]
    ang_c = col[:, None] * inv_freq[None, :]
    cos = jnp.concatenate([jnp.cos(ang_r), jnp.cos(ang_r), jnp.cos(ang_c), jnp.cos(ang_c)], axis=-1)
    sin = jnp.concatenate([-jnp.sin(ang_r), jnp.sin(ang_r), -jnp.sin(ang_c), jnp.sin(ang_c)], axis=-1)
    return cos.T, sin.T


def _rope_partner(y):
    h = ROPE_HALF
    return jnp.concatenate([y[h:2 * h], y[0:h], y[3 * h:4 * h], y[2 * h:3 * h]], axis=0)


def _qk_specs(seq, head0):
    t = min(T_PREP, seq)
    src = pl.BlockSpec((None, HEAD_DIM, t), lambda h, i: (h + head0, 0, i))
    own = pl.BlockSpec((None, HEAD_DIM, t), lambda h, i: (h, 0, i))
    nat = pl.BlockSpec((None, t, HEAD_DIM), lambda h, i: (h, i, 0))
    col = pl.BlockSpec((HEAD_DIM, 1), lambda h, i: (0, 0))
    tab = pl.BlockSpec((HEAD_DIM, t), lambda h, i: (0, i))
    return t, src, own, nat, col, tab


def _qk_prep_fwd(qkv_t, head0, heads, gain, scale, cos, sin, name):
    seq = qkv_t.shape[2]
    t, src, own, nat, col, tab = _qk_specs(seq, head0)

    def body(x_ref, g_ref, c_ref, s_ref, ot_ref, on_ref):
        xf = x_ref[...]
        r = lax.rsqrt(jnp.mean(xf * xf, axis=0, keepdims=True) + EPS)
        y = xf * r * g_ref[...]
        z = (y * c_ref[...] + _rope_partner(y) * s_ref[...]) * scale
        ot_ref[...] = z.astype(BF16)
        on_ref[...] = z.T.astype(BF16)

    return _pcall(body, name=name, grid=(heads, seq // t), in_specs=[src, col, tab, tab], out_specs=[own, nat],
                  out_shape=[jax.ShapeDtypeStruct((heads, HEAD_DIM, seq), BF16),
                             jax.ShapeDtypeStruct((heads, seq, HEAD_DIM), BF16)],
                  sem=("parallel", "parallel"))(qkv_t, gain, cos, sin)


def _qk_prep_bwd(qkv_t, head0, dout, dout_is_t, gain, scale, cos, sin, name):
    heads = dout.shape[0]
    seq = qkv_t.shape[2]
    t, src, own, nat, col, tab = _qk_specs(seq, head0)

    def body(x_ref, d_ref, g_ref, c_ref, s_ref, dx_ref, dg_ref):
        xf = x_ref[...]
        r = lax.rsqrt(jnp.mean(xf * xf, axis=0, keepdims=True) + EPS)
        n = xf * r
        d = d_ref[...] if dout_is_t else d_ref[...].T
        dz = d * scale
        dy = dz * c_ref[...] + _rope_partner(dz * s_ref[...])
        dn = dy * g_ref[...]
        dx_ref[...] = r * (dn - n * jnp.mean(dn * n, axis=0, keepdims=True))

        @pl.when(pl.program_id(1) == 0)
        def _():
            dg_ref[...] = jnp.zeros_like(dg_ref)

        dg_ref[...] += jnp.sum(dy * n, axis=1, keepdims=True)

    return _pcall(body, name=name, grid=(heads, seq // t),
                  in_specs=[src, own if dout_is_t else nat, col, tab, tab],
                  out_specs=[own, pl.BlockSpec((None, HEAD_DIM, 1), lambda h, i: (h, 0, 0))],
                  out_shape=[jax.ShapeDtypeStruct((heads, HEAD_DIM, seq), F32),
                             jax.ShapeDtypeStruct((heads, HEAD_DIM, 1), F32)],
                  sem=("parallel", "arbitrary"))(qkv_t, dout, gain, cos, sin)


V_ROWS = HEAD_DIM + 8


def _unroll(nk, cap):
    u = 1
    while u * 2 <= cap and nk % (u * 2) == 0:
        u *= 2
    return u


def _attn_fwd(qs_t, k, v_t, name):
    seq = qs_t.shape[2]
    nk, bk = k.shape[1], k.shape[2]
    bq = min(BQ, seq)
    unroll = _unroll(nk, FWD_UNROLL)
    hps = FWD_HEADS_PER_STEP
    heads = range(hps)

    def body(qt_ref, k_ref, vt_ref, ot_ref, lse_ref, s_scr):
        q_t = [qt_ref[h] for h in heads]

        def scores(j, slot):
            kj = k_ref[j]
            top = []
            for h in heads:
                s = _dot(kj, q_t[h])
                s_scr[slot, h] = s
                top.append(jnp.max(s, axis=0, keepdims=True))
            return tuple(top)

        def accumulate(j, slot, state, top):
            vtj = vt_ref[j]
            out = []
            for h in heads:
                m, acc = state[h]
                m_new = jnp.maximum(m, top[h])
                p = jnp.exp2(s_scr[slot, h] - m_new).astype(BF16)
                out.append((m_new, jnp.exp2(m - m_new) * acc + _dot(vtj, p)))
            return tuple(out)

        def step(t, carry):
            state, top = carry
            for u in range(unroll):
                nxt = unroll * t + u + 1
                top_next = scores(jnp.minimum(nxt, nk - 1) if u == unroll - 1 else nxt, (u + 1) % 2)
                state = accumulate(unroll * t + u, u % 2, state, top)
                top = top_next
            return state, top

        init = tuple((jnp.full((1, bq), -jnp.inf, F32), jnp.zeros((V_ROWS, bq), F32)) for _ in heads)
        state, _ = lax.fori_loop(0, nk // unroll, step, (init, scores(0, 0)))
        for h in heads:
            m, acc = state[h]
            l = acc[HEAD_DIM:HEAD_DIM + 1, :]
            ot_ref[h] = acc[:HEAD_DIM, :] / l
            lse_ref[h] = m + jnp.log2(l)

    kv_of = lambda g: g * hps // GROUP
    return _pcall(
        body, name=name, grid=(ATT_HEADS // hps, seq // bq),
        in_specs=[pl.BlockSpec((hps, HEAD_DIM, bq), lambda g, i: (g, 0, i)),
                  pl.BlockSpec((None, nk, bk, HEAD_DIM), lambda g, i: (kv_of(g), 0, 0, 0)),
                  pl.BlockSpec((None, nk, V_ROWS, bk), lambda g, i: (kv_of(g), 0, 0, 0))],
        out_specs=[pl.BlockSpec((hps, HEAD_DIM, bq), lambda g, i: (g, 0, i)),
                   pl.BlockSpec((hps, 1, bq), lambda g, i: (g, 0, i))],
        out_shape=[jax.ShapeDtypeStruct((ATT_HEADS, HEAD_DIM, seq), F32),
                   jax.ShapeDtypeStruct((ATT_HEADS, 1, seq), F32)],
        scratch=[pltpu.VMEM((2, hps, bk, bq), F32)],
        sem=("parallel", "parallel"), vmem_mb=VMEM_MB)(qs_t, k, v_t)


def _attn_bwd(qs, qs_t, do_t, o_t, lse, k, k_t, v_t, name):
    seq = qs.shape[1]
    nk, bk = k.shape[1], k.shape[2]
    bq = min(BQ, seq)
    nq = seq // bq

    unroll = _unroll(nk, BWD_UNROLL)
    heads = range(HEADS_PER_STEP)
    pairs = GROUP // HEADS_PER_STEP

    def body(q_ref, qt_ref, dot_ref, ot_ref, lse_ref, k_ref, kt_ref, vt_ref, dq_ref, dkt_ref, dvt_ref,
             s_scr, dp_scr):
        @pl.when((pl.program_id(1) == 0) & (pl.program_id(2) == 0))
        def _():
            dkt_ref[...] = jnp.zeros_like(dkt_ref)
            dvt_ref[...] = jnp.zeros_like(dvt_ref)

        q, q_t = [q_ref[h] for h in heads], [qt_ref[h] for h in heads]
        do_t_b = [dot_ref[h].astype(BF16) for h in heads]
        do_l = [dot_ref[h].T * LN2 for h in heads]
        do_b = [d.astype(BF16) for d in do_l]
        delta = [jnp.sum(do_l[h] * ot_ref[h].T, axis=-1, keepdims=True) for h in heads]
        lse_col = [jnp.max(jnp.broadcast_to(lse_ref[h], (LANES, bq)).T, axis=-1, keepdims=True) for h in heads]

        def products(j, slot):
            ktj, vtj = kt_ref[j], vt_ref[j]
            for h in heads:
                s_scr[slot, h] = _dot(q[h], ktj)
                dp_scr[slot, h] = _dot(do_b[h], vtj).astype(BF16)

        def gradients(j, slot, dq):
            kj = k_ref[j]
            dvt = jnp.zeros((HEAD_DIM, bk), F32)
            dkt = jnp.zeros((HEAD_DIM, bk), F32)
            new = []
            for h in heads:
                p = jnp.exp2(s_scr[slot, h] - lse_col[h])
                ds = (p * (dp_scr[slot, h].astype(F32) - delta[h])).astype(BF16)
                dvt = dvt + _dot(do_t_b[h], p.astype(BF16))
                dkt = dkt + _dot(q_t[h], ds)
                new.append(dq[h] + _dot(ds, kj))
            dvt_ref[j] += dvt
            dkt_ref[j] += dkt
            return tuple(new)

        def step(t, dq):
            for u in range(unroll):
                nxt = unroll * t + u + 1
                products(jnp.minimum(nxt, nk - 1) if u == unroll - 1 else nxt, (u + 1) % 2)
                dq = gradients(unroll * t + u, u % 2, dq)
            return dq

        products(0, 0)
        res = lax.fori_loop(0, nk // unroll, step, tuple(jnp.zeros((bq, HEAD_DIM), F32) for _ in heads))
        for h in heads:
            dq_ref[h] = res[h]

    first = lambda g, hh: g * pairs + hh
    row = pl.BlockSpec((HEADS_PER_STEP, bq, HEAD_DIM), lambda g, hh, i: (first(g, hh), i, 0))
    col = pl.BlockSpec((HEADS_PER_STEP, HEAD_DIM, bq), lambda g, hh, i: (first(g, hh), 0, i))
    kv_rows = pl.BlockSpec((None, nk, bk, HEAD_DIM), lambda g, hh, i: (g, 0, 0, 0))
    kv_cols = pl.BlockSpec((None, nk, HEAD_DIM, bk), lambda g, hh, i: (g, 0, 0, 0))
    return _pcall(
        body, name=name, grid=(KV_HEADS, pairs, nq),
        in_specs=[row, col, col, col,
                  pl.BlockSpec((HEADS_PER_STEP, 1, bq), lambda g, hh, i: (first(g, hh), 0, i)),
                  kv_rows, kv_cols, kv_cols],
        out_specs=[row, kv_cols, kv_cols],
        out_shape=[jax.ShapeDtypeStruct((ATT_HEADS, seq, HEAD_DIM), F32),
                   jax.ShapeDtypeStruct((KV_HEADS, nk, HEAD_DIM, bk), F32),
                   jax.ShapeDtypeStruct((KV_HEADS, nk, HEAD_DIM, bk), F32)],
        scratch=[pltpu.VMEM((2, HEADS_PER_STEP, bq, bk), F32), pltpu.VMEM((2, HEADS_PER_STEP, bq, bk), BF16)],
        sem=("parallel", "arbitrary", "arbitrary"), vmem_mb=ATTN_BWD_VMEM_MB)(
            qs, qs_t, do_t, o_t, lse, k, k_t, v_t)


def _halo_specs(t, col, n_tiles):
    per = t // HALO
    last = n_tiles * per - 1
    before = pl.BlockSpec((HALO, CONV_W), lambda i: (jnp.maximum(i * per - 1, 0), col))
    after = pl.BlockSpec((HALO, CONV_W), lambda i: (jnp.minimum((i + 1) * per, last), col))
    return before, after


def _glu(a, b):
    return a * _sigmoid(b)


SUBLANES = 8
SHIFT_ROWS = 2 * HALO - SUBLANES


def _shifted(sh_ref, off, t):
    return sh_ref[off % SUBLANES, pl.ds(off - off % SUBLANES, t), :]


def _conv_taps(sh_ref, w_ref, t, flip):
    acc = jnp.zeros((t, CONV_W), F32)
    for k in range(CONV_K):
        off = (HALO + CONV_PAD - k) if flip else (HALO - CONV_PAD + k)
        acc = acc + w_ref[k:k + 1, :] * _shifted(sh_ref, off, t)
    return acc


def _fill_ext(ext_ref, sh_ref, before, tile, after, t, i, n_tiles):
    ext_ref[pl.ds(0, HALO), :] = jnp.where(i > 0, before, 0.0)
    ext_ref[pl.ds(HALO, t), :] = tile
    ext_ref[pl.ds(HALO + t, HALO), :] = jnp.where(i < n_tiles - 1, after, 0.0)
    for b in range(SUBLANES):
        sh_ref[b] = ext_ref[pl.ds(b, t + SHIFT_ROWS), :]


def _conv_scratch(t):
    return [pltpu.VMEM((t + 2 * HALO, CONV_W), F32), pltpu.VMEM((SUBLANES, t + SHIFT_ROWS, CONV_W), F32)]


def _conv_fwd(proj, w, bias, ln_g, ln_b, name):
    seq = proj.shape[0]
    t = min(T_GROUP, seq)
    n_tiles = seq // t

    def body(a_ref, b_ref, ap_ref, bp_ref, an_ref, bn_ref, gate_ref, w_ref, bias_ref, g_ref, beta_ref, o_ref, y_ref,
             ext_ref, sh_ref):
        i = pl.program_id(0)
        _fill_ext(ext_ref, sh_ref, _glu(ap_ref[...], bp_ref[...]), _glu(a_ref[...], b_ref[...]),
                  _glu(an_ref[...], bn_ref[...]), t, i, n_tiles)
        y = _conv_taps(sh_ref, w_ref, t, False) + bias_ref[...]
        y_ref[...] = y
        mu = jnp.mean(y, axis=-1, keepdims=True)
        yc = y - mu
        rs = lax.rsqrt(jnp.mean(yc * yc, axis=-1, keepdims=True) + EPS)
        z = yc * rs * g_ref[...] + beta_ref[...]
        o_ref[...] = _silu_and_grad(z)[0] * _silu_and_grad(gate_ref[...])[0]

    tile = lambda c: pl.BlockSpec((t, CONV_W), lambda i: (i, c))
    ab, aa = _halo_specs(t, COL_A, n_tiles)
    bb, ba = _halo_specs(t, COL_B, n_tiles)
    vec = pl.BlockSpec((1, CONV_W), lambda i: (0, 0))
    return _pcall(
        body, name=name, grid=(n_tiles,),
        in_specs=[tile(COL_A), tile(COL_B), ab, bb, aa, ba, tile(COL_GCONV),
                  pl.BlockSpec((CONV_K, CONV_W), lambda i: (0, 0)), vec, vec, vec],
        out_specs=[pl.BlockSpec((t, CONV_W), lambda i: (i, 0))] * 2,
        out_shape=[jax.ShapeDtypeStruct((seq, CONV_W), F32)] * 2,
        scratch=_conv_scratch(t), sem=("parallel",))(
            proj, proj, proj, proj, proj, proj, proj, w, bias, ln_g, ln_b)


def _conv_bwd_a(proj, y_conv, dcnv, ln_g, ln_b, name):
    seq = proj.shape[0]
    t = min(T_GROUP, seq)
    n_tiles = seq // t

    def body(a_ref, b_ref, ap_ref, bp_ref, an_ref, bn_ref, gate_ref, y_ref, d_ref, g_ref, beta_ref,
             dy_ref, dgate_ref, dw_ref, dbias_ref, dg_ref, dbeta_ref, ext_ref, sh_ref):
        i = pl.program_id(0)
        _fill_ext(ext_ref, sh_ref, _glu(ap_ref[...], bp_ref[...]), _glu(a_ref[...], b_ref[...]),
                  _glu(an_ref[...], bn_ref[...]), t, i, n_tiles)
        y = y_ref[...]
        mu = jnp.mean(y, axis=-1, keepdims=True)
        yc = y - mu
        rs = lax.rsqrt(jnp.mean(yc * yc, axis=-1, keepdims=True) + EPS)
        n = yc * rs
        z = n * g_ref[...] + beta_ref[...]
        act, dact = _silu_and_grad(z)
        gate, dgate = _silu_and_grad(gate_ref[...])
        d = d_ref[...]
        dgate_ref[...] = d * act * dgate
        dz = d * gate * dact
        dn = dz * g_ref[...]
        dy = rs * (dn - jnp.mean(dn, axis=-1, keepdims=True) - n * jnp.mean(dn * n, axis=-1, keepdims=True))
        dy_ref[...] = dy

        @pl.when(i == 0)
        def _():
            dw_ref[...] = jnp.zeros_like(dw_ref)
            dbias_ref[...] = jnp.zeros_like(dbias_ref)
            dg_ref[...] = jnp.zeros_like(dg_ref)
            dbeta_ref[...] = jnp.zeros_like(dbeta_ref)

        dg_ref[...] += jnp.sum(dz * n, axis=0, keepdims=True)
        dbeta_ref[...] += jnp.sum(dz, axis=0, keepdims=True)
        dbias_ref[...] += jnp.sum(dy, axis=0, keepdims=True)
        for k in range(CONV_K):
            dw_ref[k:k + 1, :] += jnp.sum(dy * _shifted(sh_ref, HALO - CONV_PAD + k, t), axis=0, keepdims=True)

    tile = lambda c: pl.BlockSpec((t, CONV_W), lambda i: (i, c))
    own = pl.BlockSpec((t, CONV_W), lambda i: (i, 0))
    ab, aa = _halo_specs(t, COL_A, n_tiles)
    bb, ba = _halo_specs(t, COL_B, n_tiles)
    vec = pl.BlockSpec((1, CONV_W), lambda i: (0, 0))
    taps = pl.BlockSpec((CONV_K, CONV_W), lambda i: (0, 0))
    vshape = jax.ShapeDtypeStruct((1, CONV_W), F32)
    return _pcall(
        body, name=name, grid=(n_tiles,),
        in_specs=[tile(COL_A), tile(COL_B), ab, bb, aa, ba, tile(COL_GCONV), own, own, vec, vec],
        out_specs=[own, own, taps, vec, vec, vec],
        out_shape=[jax.ShapeDtypeStruct((seq, CONV_W), F32), jax.ShapeDtypeStruct((seq, CONV_W), F32),
                   jax.ShapeDtypeStruct((CONV_K, CONV_W), F32), vshape, vshape, vshape],
        scratch=_conv_scratch(t), sem=("arbitrary",))(
            proj, proj, proj, proj, proj, proj, proj, y_conv, dcnv, ln_g, ln_b)


def _conv_bwd_b(proj, dy, w, name):
    seq = proj.shape[0]
    t = min(T_GROUP, seq)
    n_tiles = seq // t

    def body(a_ref, b_ref, dy_ref, dyp_ref, dyn_ref, w_ref, da_ref, db_ref, ext_ref, sh_ref):
        i = pl.program_id(0)
        _fill_ext(ext_ref, sh_ref, dyp_ref[...], dy_ref[...], dyn_ref[...], t, i, n_tiles)
        dh = _conv_taps(sh_ref, w_ref, t, True)
        sig = _sigmoid(b_ref[...])
        da_ref[...] = dh * sig
        db_ref[...] = dh * a_ref[...] * sig * (1.0 - sig)

    tile = lambda c: pl.BlockSpec((t, CONV_W), lambda i: (i, c))
    own = pl.BlockSpec((t, CONV_W), lambda i: (i, 0))
    before, after = _halo_specs(t, 0, n_tiles)
    return _pcall(
        body, name=name, grid=(n_tiles,),
        in_specs=[tile(COL_A), tile(COL_B), own, before, after, pl.BlockSpec((CONV_K, CONV_W), lambda i: (0, 0))],
        out_specs=[own, own], out_shape=[jax.ShapeDtypeStruct((seq, CONV_W), F32)] * 2,
        scratch=_conv_scratch(t), sem=("parallel",))(proj, proj, dy, dy, dy, w)


def _head_masks():
    lane_head = lax.broadcasted_iota(jnp.int32, (SG_CHUNK, SG_W), 1) // HEAD_DIM
    return [lane_head == h for h in range(SG_HEADS)]


def _sg_mix(mats_ref, rhs, masks):
    out = jnp.zeros((SG_CHUNK, SG_W), F32)
    for h in range(SG_HEADS):
        out = out + jnp.where(masks[h], _dot(mats_ref[h], rhs), 0.0)
    return out


def _sg_specs(seq):
    t = min(T_GROUP, seq)
    tile = lambda c: pl.BlockSpec((t, SG_W), lambda i: (i, c))
    own = pl.BlockSpec((t, SG_W), lambda i: (i, 0))
    vec = pl.BlockSpec((1, SG_W), lambda i: (0, 0))
    mats = pl.BlockSpec((SG_HEADS, SG_CHUNK, SG_CHUNK), lambda i: (0, 0, 0))
    full = pl.BlockSpec((SG_CHUNK, SG_W), lambda i: (0, 0))
    return t, tile, own, vec, mats, full


def _sg_fwd(proj, ln_g, ln_b, w_b, bias_full, name):
    seq = proj.shape[0]
    t, tile, own, vec, mats, full = _sg_specs(seq)

    def body(u_ref, v_ref, gate_ref, g_ref, beta_ref, w_ref, bias_ref, o_ref):
        masks = _head_masks()
        for c in range(t // SG_CHUNK):
            rows = pl.ds(c * SG_CHUNK, SG_CHUNK)
            vg = _gelu_and_grad(v_ref[rows, :])[0]
            mu = jnp.mean(vg, axis=-1, keepdims=True)
            vc = vg - mu
            rs = lax.rsqrt(jnp.mean(vc * vc, axis=-1, keepdims=True) + EPS)
            vln = vc * rs * g_ref[...] + beta_ref[...]
            mixed = _sg_mix(w_ref, vln.astype(BF16), masks) + bias_ref[...]
            o_ref[rows, :] = _gelu_and_grad(u_ref[rows, :])[0] * mixed * _silu_and_grad(gate_ref[rows, :])[0]

    return _pcall(body, name=name, grid=(seq // t,),
                  in_specs=[tile(COL_U), tile(COL_VSG), tile(COL_GSG), vec, vec, mats, full], out_specs=own,
                  out_shape=jax.ShapeDtypeStruct((seq, SG_W), F32), sem=("parallel",))(
                      proj, proj, proj, ln_g, ln_b, w_b, bias_full)


def _sg_bwd(proj, dsg, ln_g, ln_b, w_b, w_t_b, bias_full, fold, name):
    seq = proj.shape[0]
    t, tile, own, vec, mats, full = _sg_specs(seq)
    n_tiles = seq // t

    def body(u_ref, v_ref, gate_ref, d_ref, g_ref, beta_ref, w_ref, wt_ref, bias_ref, fold_ref,
             du_ref, dv_ref, dgate_ref, dg_ref, dbeta_ref, dw_ref, db_ref, dbias_acc):
        i = pl.program_id(0)

        @pl.when(i == 0)
        def _():
            dg_ref[...] = jnp.zeros_like(dg_ref)
            dbeta_ref[...] = jnp.zeros_like(dbeta_ref)
            dw_ref[...] = jnp.zeros_like(dw_ref)
            dbias_acc[...] = jnp.zeros_like(dbias_acc)

        masks = _head_masks()
        for c in range(t // SG_CHUNK):
            rows = pl.ds(c * SG_CHUNK, SG_CHUNK)
            ug, dug = _gelu_and_grad(u_ref[rows, :])
            vg, dvg = _gelu_and_grad(v_ref[rows, :])
            mu = jnp.mean(vg, axis=-1, keepdims=True)
            vc = vg - mu
            rs = lax.rsqrt(jnp.mean(vc * vc, axis=-1, keepdims=True) + EPS)
            vn = vc * rs
            vln_b = (vn * g_ref[...] + beta_ref[...]).astype(BF16)
            mixed = _sg_mix(w_ref, vln_b, masks) + bias_ref[...]
            gate, dgate = _silu_and_grad(gate_ref[rows, :])
            d = d_ref[rows, :]
            dgate_ref[rows, :] = d * ug * mixed * dgate
            du_ref[rows, :] = d * mixed * gate * dug
            dmixed = d * ug * gate
            dbias_acc[...] += dmixed
            dmixed_b = dmixed.astype(BF16)
            for h in range(SG_HEADS):
                dm_h = jnp.where(masks[h], dmixed_b, jnp.zeros_like(dmixed_b))
                dw_ref[h] += lax.dot_general(dm_h, vln_b, (((1,), (1,)), ((), ())), preferred_element_type=F32)
            dvln = _sg_mix(wt_ref, dmixed_b, masks)
            dg_ref[...] += jnp.sum(dvln * vn, axis=0, keepdims=True)
            dbeta_ref[...] += jnp.sum(dvln, axis=0, keepdims=True)
            dvn = dvln * g_ref[...]
            dvgelu = rs * (dvn - jnp.mean(dvn, axis=-1, keepdims=True) - vn * jnp.mean(dvn * vn, axis=-1, keepdims=True))
            dv_ref[rows, :] = dvgelu * dvg

        @pl.when(i == n_tiles - 1)
        def _():
            db_ref[...] = _split_dot(dbias_acc[...], fold_ref[...])

    sq = pl.BlockSpec((SG_CHUNK, SG_CHUNK), lambda i: (0, 0))
    vshape = jax.ShapeDtypeStruct((1, SG_W), F32)
    return _pcall(
        body, name=name, grid=(n_tiles,),
        in_specs=[tile(COL_U), tile(COL_VSG), tile(COL_GSG), own, vec, vec, mats, mats, full,
                  pl.BlockSpec((SG_W, SG_CHUNK), lambda i: (0, 0))],
        out_specs=[own, own, own, vec, vec, mats, sq],
        out_shape=[jax.ShapeDtypeStruct((seq, SG_W), F32)] * 3 + [
            vshape, vshape, jax.ShapeDtypeStruct((SG_HEADS, SG_CHUNK, SG_CHUNK), F32),
            jax.ShapeDtypeStruct((SG_CHUNK, SG_CHUNK), F32)],
        scratch=[pltpu.VMEM((SG_CHUNK, SG_W), F32)], sem=("arbitrary",))(
            proj, proj, proj, dsg, ln_g, ln_b, w_b, w_t_b, bias_full, fold)


def _out_fwd(att_t, proj, cnv, sgu, x, w, gain, name):
    seq = x.shape[0]

    def body(att_ref, g0_ref, g1_ref, cnv_ref, sgu_ref, x_ref, w_ref, gain_ref, xo_ref, mix_ref, cat_ref):
        gate = jnp.concatenate([_silu_and_grad(g0_ref[...])[0], _silu_and_grad(g1_ref[...])[0]], axis=-1)
        cat_ref[:, 0:ATT_W] = (att_ref[...].T * gate).astype(BF16)
        cat_ref[:, ATT_W:ATT_W + CONV_W] = cnv_ref[...].astype(BF16)
        cat_ref[:, ATT_W + CONV_W:] = sgu_ref[...].astype(BF16)
        mix = _dot(cat_ref[...], w_ref[...])
        mix_ref[...] = mix
        r = lax.rsqrt(jnp.mean(mix * mix, axis=-1, keepdims=True) + EPS)
        xo_ref[...] = x_ref[...] + mix * r * gain_ref[...]

    row = lambda w_: pl.BlockSpec((T_ROW, w_), lambda i: (i, 0))
    gate_blk = lambda c: pl.BlockSpec((T_ROW, 256), lambda i: (i, c))
    return _pcall(
        body, name=name, grid=(seq // T_ROW,),
        in_specs=[pl.BlockSpec((ATT_W, T_ROW), lambda i: (0, i)), gate_blk(COL_GATT), gate_blk(COL_GATT + 1),
                  row(CONV_W), row(SG_W), row(D_MODEL),
                  pl.BlockSpec((D_MODEL, D_MODEL), lambda i: (0, 0)), pl.BlockSpec((1, D_MODEL), lambda i: (0, 0))],
        out_specs=[row(D_MODEL), row(D_MODEL), row(D_MODEL)],
        out_shape=[jax.ShapeDtypeStruct((seq, D_MODEL), F32), jax.ShapeDtypeStruct((seq, D_MODEL), F32),
                   jax.ShapeDtypeStruct((seq, D_MODEL), BF16)],
        sem=("parallel",), vmem_mb=VMEM_MB)(att_t, proj, proj, cnv, sgu, x, w, gain)


def _out_bwd(dxo, mix, gain, w_t, att_t, proj, name):
    seq = dxo.shape[0]

    def body(dxo_ref, mix_ref, gain_ref, w_ref, att_ref, g0_ref, g1_ref,
             dmix_ref, datt_ref, dgatt_ref, dcnv_ref, dsgu_ref, dgain_ref):
        mix = mix_ref[...]
        r = lax.rsqrt(jnp.mean(mix * mix, axis=-1, keepdims=True) + EPS)
        n = mix * r
        dout = dxo_ref[...]
        dn = dout * gain_ref[...]
        dmix = (r * (dn - n * jnp.mean(dn * n, axis=-1, keepdims=True))).astype(BF16)
        dmix_ref[...] = dmix

        @pl.when(pl.program_id(0) == 0)
        def _():
            dgain_ref[...] = jnp.zeros_like(dgain_ref)

        dgain_ref[...] += jnp.sum(dout * n, axis=0, keepdims=True)
        dcat = _dot(dmix, w_ref[...])
        g0, dg0 = _silu_and_grad(g0_ref[...])
        g1, dg1 = _silu_and_grad(g1_ref[...])
        gate = jnp.concatenate([g0, g1], axis=-1)
        dgate = jnp.concatenate([dg0, dg1], axis=-1)
        dca = dcat[:, 0:ATT_W]
        datt_ref[...] = (dca * gate).T
        dgatt_ref[...] = dca * att_ref[...].T * dgate
        dcnv_ref[...] = dcat[:, ATT_W:ATT_W + CONV_W]
        dsgu_ref[...] = dcat[:, ATT_W + CONV_W:]

    row = lambda w_: pl.BlockSpec((T_ROW, w_), lambda i: (i, 0))
    gate_blk = lambda c: pl.BlockSpec((T_ROW, 256), lambda i: (i, c))
    vec = pl.BlockSpec((1, D_MODEL), lambda i: (0, 0))
    heads_t = pl.BlockSpec((ATT_W, T_ROW), lambda i: (0, i))
    return _pcall(
        body, name=name, grid=(seq // T_ROW,),
        in_specs=[row(D_MODEL), row(D_MODEL), vec, pl.BlockSpec((D_MODEL, D_MODEL), lambda i: (0, 0)), heads_t,
                  gate_blk(COL_GATT), gate_blk(COL_GATT + 1)],
        out_specs=[row(D_MODEL), heads_t, row(ATT_W), row(CONV_W), row(SG_W), vec],
        out_shape=[jax.ShapeDtypeStruct((seq, D_MODEL), BF16), jax.ShapeDtypeStruct((ATT_W, seq), F32),
                   jax.ShapeDtypeStruct((seq, ATT_W), F32), jax.ShapeDtypeStruct((seq, CONV_W), F32),
                   jax.ShapeDtypeStruct((seq, SG_W), F32), jax.ShapeDtypeStruct((1, D_MODEL), F32)],
        sem=("arbitrary",), vmem_mb=VMEM_MB)(dxo, mix, gain, w_t, att_t, proj, proj)


def _loss_head(y, target, name):
    seq = y.shape[0]
    t = min(T_GROUP, seq)

    def body(y_ref, t_ref, sse_ref, dy_ref):
        err = y_ref[...] - t_ref[...]
        dy_ref[...] = err * (1.0 / D_MODEL)

        @pl.when(pl.program_id(0) == 0)
        def _():
            sse_ref[...] = jnp.zeros_like(sse_ref)

        part = jnp.sum(jnp.sum(err * err, axis=0, keepdims=True), axis=-1, keepdims=True)
        sse_ref[...] += jnp.broadcast_to(part, (1, LANES))

    row = pl.BlockSpec((t, D_MODEL), lambda i: (i, 0))
    return _pcall(body, name=name, grid=(seq // t,), in_specs=[row, row],
                  out_specs=[pl.BlockSpec((1, LANES), lambda i: (0, 0)), row],
                  out_shape=[jax.ShapeDtypeStruct((1, LANES), F32), jax.ShapeDtypeStruct((seq, D_MODEL), F32)],
                  sem=("arbitrary",))(y, target)


def _row_blocks(a, bk):
    return a.reshape(a.shape[0], a.shape[1] // bk, bk, HEAD_DIM)


def _lane_blocks(a, bk):
    return a.reshape(a.shape[0], HEAD_DIM, a.shape[2] // bk, bk).transpose(0, 2, 1, 3)


def _from_lane_blocks(a):
    return a.transpose(0, 2, 1, 3).reshape(a.shape[0], HEAD_DIM, a.shape[1] * a.shape[3])


def _flat_rows(a, rows):
    flat = a.reshape(-1)
    return jnp.pad(flat, (0, rows * LANES - flat.shape[0])).reshape(rows, LANES)


SHARD_ROWS = {"w_in": 2 * D_MODEL * (D_IN // N_DEV) // LANES, "w_out": 2 * (D_MODEL // N_DEV) * D_MODEL // LANES,
              "conv_dw": 16}
REPL_SHAPES = [("pre_norm", (2, D_MODEL)), ("post_norm", (2, D_MODEL)), ("q_norm", (2, HEAD_DIM)),
               ("k_norm", (2, HEAD_DIM)), ("conv_dw_b", (2, CONV_W)), ("conv_ln_g", (2, CONV_W)),
               ("conv_ln_b", (2, CONV_W)), ("sg_ln_g", (2, SG_W)), ("sg_ln_b", (2, SG_W)),
               ("sg_w", (2, SG_HEADS, SG_CHUNK, SG_CHUNK)), ("sg_b", (2, SG_HEADS, SG_CHUNK))]
REPL_ROWS = 1088
WEIGHT_ORDER = ["pre_norm", "post_norm", "w_in", "w_out", "q_norm", "k_norm", "conv_dw", "conv_dw_b", "conv_ln_g",
                "conv_ln_b", "sg_ln_g", "sg_ln_b", "sg_w", "sg_b"]


def _pack_shard(parts):
    return jnp.concatenate([_flat_rows(parts[k], SHARD_ROWS[k]) for k in ("w_in", "w_out", "conv_dw")], axis=0)


def _unpack_shard(flat, shapes):
    out, at = {}, 0
    for k in ("w_in", "w_out", "conv_dw"):
        size = math.prod(shapes[k])
        out[k] = flat[at:at + SHARD_ROWS[k]].reshape(-1)[:size].reshape(shapes[k])
        at += SHARD_ROWS[k]
    return out


REPL_USED = sum(math.prod(shape) for _, shape in REPL_SHAPES)


def _pack_repl(parts, extra=None):
    tail = [] if extra is None else [extra.reshape(1)]
    flat = jnp.concatenate([parts[k].reshape(-1) for k, _ in REPL_SHAPES] + tail)
    return jnp.pad(flat, (0, REPL_ROWS * LANES - flat.shape[0])).reshape(REPL_ROWS, LANES)


def _unpack_repl(flat):
    out, at, flat = {}, 0, flat.reshape(-1)
    for k, shape in REPL_SHAPES:
        size = math.prod(shape)
        out[k] = flat[at:at + size].reshape(shape)
        at += size
    return out


def kernel(x, pre_norm, post_norm, w_in, w_out, q_norm, k_norm, conv_dw, conv_dw_b, conv_ln_g, conv_ln_b, sg_ln_g, sg_ln_b, sg_w, sg_b, loss_target, m_pre_norm, m_post_norm, m_w_in, m_w_out, m_q_norm, m_k_norm, m_conv_dw, m_conv_dw_b, m_conv_ln_g, m_conv_ln_b, m_sg_ln_g, m_sg_ln_b, m_sg_w, m_sg_b, v_pre_norm, v_post_norm, v_w_in, v_w_out, v_q_norm, v_k_norm, v_conv_dw, v_conv_dw_b, v_conv_ln_g, v_conv_ln_b, v_sg_ln_g, v_sg_ln_b, v_sg_w, v_sg_b):
    weights = dict(pre_norm=pre_norm, post_norm=post_norm, w_in=w_in, w_out=w_out, q_norm=q_norm, k_norm=k_norm,
                   conv_dw=conv_dw, conv_dw_b=conv_dw_b, conv_ln_g=conv_ln_g, conv_ln_b=conv_ln_b, sg_ln_g=sg_ln_g,
                   sg_ln_b=sg_ln_b, sg_w=sg_w, sg_b=sg_b)
    mom_m = dict(pre_norm=m_pre_norm, post_norm=m_post_norm, w_in=m_w_in, w_out=m_w_out, q_norm=m_q_norm,
                 k_norm=m_k_norm, conv_dw=m_conv_dw, conv_dw_b=m_conv_dw_b, conv_ln_g=m_conv_ln_g,
                 conv_ln_b=m_conv_ln_b, sg_ln_g=m_sg_ln_g, sg_ln_b=m_sg_ln_b, sg_w=m_sg_w, sg_b=m_sg_b)
    mom_v = dict(pre_norm=v_pre_norm, post_norm=v_post_norm, w_in=v_w_in, w_out=v_w_out, q_norm=v_q_norm,
                 k_norm=v_k_norm, conv_dw=v_conv_dw, conv_dw_b=v_conv_dw_b, conv_ln_g=v_conv_ln_g,
                 conv_ln_b=v_conv_ln_b, sg_ln_g=v_sg_ln_g, sg_ln_b=v_sg_ln_b, sg_w=v_sg_w, sg_b=v_sg_b)
    depth = pre_norm.shape[0]
    seq = x.shape[1]
    bk = min(BK, seq)
    x0 = x.reshape(seq, D_MODEL)
    target = loss_target.reshape(seq, D_MODEL)

    w_in_all, w_out_all, dw_all = _exchange(
        [], [w_in.astype(BF16), w_out.astype(BF16), jnp.pad(conv_dw, ((0, 0), (0, 1), (0, 0)))], "gather_weights")
    w_in_full = w_in_all.transpose(1, 2, 0, 3).reshape(depth, D_MODEL, D_IN)
    w_out_full = w_out_all.transpose(1, 0, 2, 3).reshape(depth, D_MODEL, D_MODEL)
    dw_full = dw_all[:, :, :CONV_K, :].transpose(1, 2, 0, 3).reshape(depth, CONV_K, CONV_W)

    cos, sin = _rope_tables(seq)
    lane = jnp.arange(SG_W)
    fold = (lane[:, None] // HEAD_DIM == jnp.arange(SG_CHUNK)[None, :]).astype(BF16)

    def layer_consts(l):
        return dict(
            q_gain=q_norm[l].reshape(HEAD_DIM, 1), k_gain=k_norm[l].reshape(HEAD_DIM, 1), sg_w_b=sg_w[l].astype(BF16), sg_wt_b=sg_w[l].transpose(0, 2, 1).astype(BF16),
            sg_bias=jnp.repeat(sg_b[l].T, HEAD_DIM, axis=1),
            vec=lambda a: a[l].reshape(1, -1))

    saved = []
    xc = x0
    for l in range(depth):
        c = layer_consts(l)
        proj, hb, qkv_t = _proj_fwd(xc, c["vec"](pre_norm), w_in_full[l], f"proj_fwd_{l}")
        qkv_t = qkv_t.reshape(QKV_HEADS, HEAD_DIM, seq)
        qs_t, qs = _qk_prep_fwd(qkv_t, 0, ATT_HEADS, c["q_gain"], Q_SCALE, cos, sin, f"q_prep_fwd_{l}")
        kr_t, kr = _qk_prep_fwd(qkv_t, ATT_HEADS, KV_HEADS, c["k_gain"], 1.0, cos, sin, f"k_prep_fwd_{l}")
        v_t = qkv_t[ATT_HEADS + KV_HEADS:].astype(BF16)
        k_rows, k_cols, v_cols = _row_blocks(kr, bk), _lane_blocks(kr_t, bk), _lane_blocks(v_t, bk)
        v_ext = jnp.concatenate([v_cols, jnp.ones_like(v_cols[:, :, :1]), jnp.zeros_like(v_cols[:, :, :7])], axis=2)
        o_t, lse = _attn_fwd(qs_t, k_rows, v_ext, f"attn_fwd_{l}")
        att_t = o_t.reshape(ATT_W, seq)
        cnv, y_conv = _conv_fwd(proj, dw_full[l], c["vec"](conv_dw_b), c["vec"](conv_ln_g), c["vec"](conv_ln_b),
                                f"conv_fwd_{l}")
        sgu = _sg_fwd(proj, c["vec"](sg_ln_g), c["vec"](sg_ln_b), c["sg_w_b"], c["sg_bias"], f"sg_fwd_{l}")
        x_new, mix, cat_b = _out_fwd(att_t, proj, cnv, sgu, xc, w_out_full[l], c["vec"](post_norm), f"out_fwd_{l}")
        saved.append(dict(x=xc, proj=proj, hb=hb, qkv_t=qkv_t, qs=qs, qs_t=qs_t, k_rows=k_rows,
                          k_cols=k_cols, v_cols=v_cols, o_t=o_t, lse=lse, mix=mix, cat_b=cat_b, y_conv=y_conv))
        xc = x_new

    sse, dx = _loss_head(xc, target, "loss_head")

    grads = {k: [None] * depth for k in WEIGHT_ORDER}
    for l in reversed(range(depth)):
        c, s = layer_consts(l), saved[l]
        dmix_b, datt, dgatt, dcnv, dsgu, g_post = _out_bwd(
            dx, s["mix"], c["vec"](post_norm), w_out_full[l].T, s["o_t"].reshape(ATT_W, seq), s["proj"],
            f"out_bwd_{l}")
        grads["post_norm"][l] = g_post.reshape(-1)
        grads["w_out"][l] = _matmul_acc(s["cat_b"], dmix_b, D_MODEL, f"grad_w_out_{l}")
        dqs, dkt, dvt = _attn_bwd(s["qs"], s["qs_t"], datt.reshape(ATT_HEADS, HEAD_DIM, seq), s["o_t"], s["lse"],
                                  s["k_rows"], s["k_cols"], s["v_cols"], f"attn_bwd_{l}")
        d_q_t, g_qgain = _qk_prep_bwd(s["qkv_t"], 0, dqs, False, c["q_gain"], Q_SCALE, cos, sin, f"q_prep_bwd_{l}")
        d_k_t, g_kgain = _qk_prep_bwd(s["qkv_t"], ATT_HEADS, _from_lane_blocks(dkt), True, c["k_gain"], 1.0, cos, sin,
                                      f"k_prep_bwd_{l}")
        grads["q_norm"][l] = jnp.sum(g_qgain[:, :, 0], axis=0)
        grads["k_norm"][l] = jnp.sum(g_kgain[:, :, 0], axis=0)
        d_qkv = jnp.concatenate([d_q_t, d_k_t, _from_lane_blocks(dvt)], axis=0).reshape(QKV_W, seq).T
        dy_conv, dg_conv, g_dw, g_dwb, g_clg, g_clb = _conv_bwd_a(
            s["proj"], s["y_conv"], dcnv, c["vec"](conv_ln_g), c["vec"](conv_ln_b), f"conv_bwd_a_{l}")
        da, db = _conv_bwd_b(s["proj"], dy_conv, dw_full[l], f"conv_bwd_b_{l}")
        grads["conv_dw"][l], grads["conv_dw_b"][l] = g_dw, g_dwb.reshape(-1)
        grads["conv_ln_g"][l], grads["conv_ln_b"][l] = g_clg.reshape(-1), g_clb.reshape(-1)
        du, dv_sg, dg_sg, g_slg, g_slb, g_sw, g_sb = _sg_bwd(
            s["proj"], dsgu, c["vec"](sg_ln_g), c["vec"](sg_ln_b), c["sg_w_b"], c["sg_wt_b"], c["sg_bias"], fold,
            f"sg_bwd_{l}")
        grads["sg_ln_g"][l], grads["sg_ln_b"][l] = g_slg.reshape(-1), g_slb.reshape(-1)
        grads["sg_w"][l], grads["sg_b"][l] = g_sw, g_sb[:, :SG_HEADS].T
        dproj = jnp.concatenate([d_qkv, dgatt, da, db, dg_conv, du, dv_sg, dg_sg], axis=-1).astype(BF16)
        grads["w_in"][l] = _matmul_acc(s["hb"], dproj, D_IN // 2, f"grad_w_in_{l}")
        dx, g_pre = _proj_bwd(dproj, w_in_full[l].T, s["x"], c["vec"](pre_norm), dx, f"proj_bwd_{l}")
        grads["pre_norm"][l] = g_pre.reshape(-1)
    grad_x = dx.reshape(x.shape)
    grads = {k: jnp.stack(v) for k, v in grads.items()}

    shard_blocks = dict(
        w_in=grads["w_in"].reshape(depth, D_MODEL, N_DEV, D_IN // N_DEV).transpose(2, 0, 1, 3),
        w_out=grads["w_out"].reshape(depth, N_DEV, D_MODEL // N_DEV, D_MODEL).transpose(1, 0, 2, 3),
        conv_dw=grads["conv_dw"].reshape(depth, CONV_K, N_DEV, CONV_W // N_DEV).transpose(2, 0, 1, 3))
    scatter_src = jnp.stack([_pack_shard({k: a[d] for k, a in shard_blocks.items()})
                             for d in range(N_DEV)]).astype(BF16)
    shard_slots, repl_slots = _exchange([scatter_src], [_pack_repl(grads, sse[0, 0])], "exchange_grads")

    shard_shapes = {k: weights[k].shape for k in SHARD_ROWS}
    gs, ds_, ms, vs = _sum_adamw(shard_slots, _pack_shard(weights), _pack_shard(mom_m), _pack_shard(mom_v),
                                 "adamw_sharded")
    gr, dr, mr, vr = _sum_adamw(repl_slots, _pack_repl(weights), _pack_repl(mom_m), _pack_repl(mom_v),
                                "adamw_replicated")
    loss = gr.reshape(-1)[REPL_USED] * (0.5 / D_MODEL)
    results = []
    for shard_flat, repl_flat in ((gs, gr), (ds_, dr), (ms, mr), (vs, vr)):
        parts = {**_unpack_shard(shard_flat, shard_shapes), **_unpack_repl(repl_flat)}
        results.append([parts[k] for k in WEIGHT_ORDER])
    return (loss, grad_x, *results[0], *results[1], *results[2], *results[3])
```

```python
import math

import jax
import jax.numpy as jnp
from jax import lax
from jax.experimental import pallas as pl
from jax.experimental.pallas import tpu as pltpu

F32, BF16 = jnp.float32, jnp.bfloat16

N_DEV = 8
MESH_AXES = ("x", "y", "c")
EPS = 1e-6
D_MODEL = 1024
HEAD_DIM = 64
ATT_HEADS, KV_HEADS = 8, 2
QKV_HEADS = ATT_HEADS + 2 * KV_HEADS
QKV_W = QKV_HEADS * HEAD_DIM
GROUP = ATT_HEADS // KV_HEADS
ATT_W, KV_W, CONV_W, SG_W = 512, 128, 256, 256
CONV_K, CONV_PAD, HALO = 31, 15, 16
SG_HEADS, SG_CHUNK = 4, 128
D_IN = 2816
GRID_W = 64
ROPE_THETA = 10000.0
LOG2E, LN2 = math.log2(math.e), math.log(2.0)
Q_SCALE = HEAD_DIM ** -0.5 * LOG2E
LANES = 128

COL_GATT, COL_A, COL_B, COL_GCONV, COL_U, COL_VSG, COL_GSG = 3, 5, 6, 7, 8, 9, 10

ADAM_LR, ADAM_B1, ADAM_B2, ADAM_EPS, ADAM_WD, ADAM_STEP = 0.001, 0.9, 0.999, 1e-08, 0.01, 10

T_ROW = 512
T_PREP = 2048
T_GROUP = 512
BQ, BK = 512, 512
HEADS_PER_STEP = 2
FWD_HEADS_PER_STEP = 2
FWD_UNROLL, BWD_UNROLL = 8, 4
VMEM_MB = 56
ATTN_BWD_VMEM_MB = 58


def _pcall(body, *, name, grid, in_specs, out_specs, out_shape, scratch=(), sem=None, vmem_mb=None):
    params = {}
    if sem is not None:
        params["dimension_semantics"] = sem
    if vmem_mb is not None:
        params["vmem_limit_bytes"] = vmem_mb << 20
    return pl.pallas_call(body, name=name, grid=grid, in_specs=in_specs, out_specs=out_specs, out_shape=out_shape,
                          scratch_shapes=list(scratch), compiler_params=pltpu.CompilerParams(**params))


def _dot(a, b):
    return jnp.dot(a, b, preferred_element_type=F32)


def _sigmoid(x):
    return 1.0 / (1.0 + jnp.exp(-x))


def _silu_and_grad(x):
    s = _sigmoid(x)
    return x * s, s * (1.0 + x * (1.0 - s))


def _gelu_and_grad(x):
    cdf = 0.5 * (1.0 + lax.erf(x * (1.0 / math.sqrt(2.0))))
    pdf = jnp.exp(-0.5 * x * x) * (1.0 / math.sqrt(2.0 * math.pi))
    return x * cdf, cdf + x * pdf


def _split_dot(y, mat):
    hi = y.astype(BF16)
    lo = (y - hi.astype(F32)).astype(BF16)
    return _dot(hi, mat) + _dot(lo, mat)


def _row_tile(rows, cap):
    best = 8
    for t in range(8, min(rows, cap) + 1, 8):
        if rows % t == 0:
            best = t
    return best


def _exchange(scatter, gather, name):
    n_s = len(scatter)
    arrs = list(scatter) + list(gather)
    n = len(arrs)
    flips = [(fx, fy, fc) for fx in (0, 1) for fy in (0, 1) for fc in (0, 1)][1:]
    n_peer = len(flips)

    def body(*refs):
        ins, outs = refs[:n], refs[n:2 * n]
        send_sems, recv_sems, local_sems = refs[2 * n:]
        pos = tuple(lax.axis_index(a) for a in MESH_AXES)

        def peer(flip):
            return tuple((1 - p) if f else p for p, f in zip(pos, flip))

        def slot(p):
            return 4 * p[0] + 2 * p[1] + p[2]

        def src(a, p):
            return ins[a].at[slot(p)] if a < n_s else ins[a]

        def remote(a, k, src_ref, dst_slot, to):
            return pltpu.make_async_remote_copy(
                src_ref=src_ref, dst_ref=outs[a].at[dst_slot], send_sem=send_sems.at[a * n_peer + k],
                recv_sem=recv_sems.at[a * n_peer + k], device_id=to, device_id_type=pl.DeviceIdType.MESH)

        local = [pltpu.make_async_copy(src(a, pos), outs[a].at[slot(pos)], local_sems.at[a]) for a in range(n)]
        for cp in local:
            cp.start()
        sends = [remote(a, k, src(a, peer(f)), slot(pos), peer(f)) for a in range(n) for k, f in enumerate(flips)]
        for cp in sends:
            cp.start()
        for a in range(n):
            for k, f in enumerate(flips):
                remote(a, k, src(a, peer(f)), slot(peer(f)), peer(f)).wait_recv()
        for cp in sends:
            cp.wait_send()
        for cp in local:
            cp.wait()

    out_shape = [jax.ShapeDtypeStruct((N_DEV,) + (a.shape[1:] if i < n_s else a.shape), a.dtype)
                 for i, a in enumerate(arrs)]
    any_spec = pl.BlockSpec(memory_space=pl.ANY)
    return pl.pallas_call(
        body, name=name, out_shape=out_shape, in_specs=[any_spec] * n, out_specs=[any_spec] * n,
        scratch_shapes=[pltpu.SemaphoreType.DMA((n * n_peer,)), pltpu.SemaphoreType.DMA((n * n_peer,)),
                        pltpu.SemaphoreType.DMA((n,))],
    )(*arrs)


def _sum_adamw(slots, w, m, v, name):
    rows = w.shape[0]
    tr = _row_tile(rows, 1024)
    c1 = 1.0 - ADAM_B1 ** ADAM_STEP
    c2 = 1.0 - ADAM_B2 ** ADAM_STEP

    def body(s_ref, w_ref, m_ref, v_ref, g_out, d_out, m_out, v_out):
        g = s_ref[0].astype(F32)
        for d in range(1, N_DEV):
            g = g + s_ref[d].astype(F32)
        m_new = ADAM_B1 * m_ref[...] + (1.0 - ADAM_B1) * g
        v_new = ADAM_B2 * v_ref[...] + (1.0 - ADAM_B2) * (g * g)
        m_hat = m_new / c1
        v_hat = v_new / c2
        g_out[...] = g
        d_out[...] = -ADAM_LR * (m_hat / (jnp.sqrt(v_hat) + ADAM_EPS) + ADAM_WD * w_ref[...])
        m_out[...] = m_new
        v_out[...] = v_new

    flat = pl.BlockSpec((tr, LANES), lambda i: (i, 0))
    return _pcall(
        body, name=name, grid=(rows // tr,),
        in_specs=[pl.BlockSpec((N_DEV, tr, LANES), lambda i: (0, i, 0)), flat, flat, flat],
        out_specs=[flat] * 4, out_shape=[jax.ShapeDtypeStruct((rows, LANES), F32)] * 4,
        sem=("parallel",), vmem_mb=VMEM_MB)(slots, w, m, v)


def _proj_fwd(x, gain, w, name):
    seq = x.shape[0]

    def body(x_ref, g_ref, w_ref, proj_ref, hb_ref, qkv_t_ref):
        xf = x_ref[...]
        r = lax.rsqrt(jnp.mean(xf * xf, axis=-1, keepdims=True) + EPS)
        h = (xf * r * g_ref[...]).astype(BF16)
        hb_ref[...] = h
        proj = _dot(h, w_ref[...])
        proj_ref[...] = proj
        qkv_t_ref[...] = proj[:, :QKV_W].T

    return _pcall(
        body, name=name, grid=(seq // T_ROW,),
        in_specs=[pl.BlockSpec((T_ROW, D_MODEL), lambda i: (i, 0)), pl.BlockSpec((1, D_MODEL), lambda i: (0, 0)),
                  pl.BlockSpec((D_MODEL, D_IN), lambda i: (0, 0))],
        out_specs=[pl.BlockSpec((T_ROW, D_IN), lambda i: (i, 0)), pl.BlockSpec((T_ROW, D_MODEL), lambda i: (i, 0)),
                   pl.BlockSpec((QKV_W, T_ROW), lambda i: (0, i))],
        out_shape=[jax.ShapeDtypeStruct((seq, D_IN), F32), jax.ShapeDtypeStruct((seq, D_MODEL), BF16),
                   jax.ShapeDtypeStruct((QKV_W, seq), F32)],
        sem=("parallel",), vmem_mb=VMEM_MB)(x, gain, w)


def _proj_bwd(dproj, w_t, x, gain, dxo, name):
    seq = x.shape[0]

    def body(dp_ref, w_ref, x_ref, g_ref, dxo_ref, dx_ref, dg_ref):
        dh = _dot(dp_ref[...], w_ref[...])
        xf = x_ref[...]
        r = lax.rsqrt(jnp.mean(xf * xf, axis=-1, keepdims=True) + EPS)
        n = xf * r
        dn = dh * g_ref[...]
        dx_ref[...] = dxo_ref[...] + r * (dn - n * jnp.mean(dn * n, axis=-1, keepdims=True))

        @pl.when(pl.program_id(0) == 0)
        def _():
            dg_ref[...] = jnp.zeros_like(dg_ref)

        dg_ref[...] += jnp.sum(dh * n, axis=0, keepdims=True)

    row = pl.BlockSpec((T_ROW, D_MODEL), lambda i: (i, 0))
    vec = pl.BlockSpec((1, D_MODEL), lambda i: (0, 0))
    return _pcall(
        body, name=name, grid=(seq // T_ROW,),
        in_specs=[pl.BlockSpec((T_ROW, D_IN), lambda i: (i, 0)), pl.BlockSpec((D_IN, D_MODEL), lambda i: (0, 0)),
                  row, vec, row],
        out_specs=[row, vec],
        out_shape=[jax.ShapeDtypeStruct((seq, D_MODEL), F32), jax.ShapeDtypeStruct((1, D_MODEL), F32)],
        sem=("arbitrary",), vmem_mb=VMEM_MB)(dproj, w_t, x, gain, dxo)


def _matmul_acc(a, b, tn, name):
    seq, m = a.shape
    n = b.shape[1]
    ts = min(1024, seq)

    def body(a_ref, b_ref, o_ref):
        @pl.when(pl.program_id(1) == 0)
        def _():
            o_ref[...] = jnp.zeros_like(o_ref)

        o_ref[...] += lax.dot_general(a_ref[...], b_ref[...], (((0,), (0,)), ((), ())), preferred_element_type=F32)

    return _pcall(
        body, name=name, grid=(n // tn, seq // ts),
        in_specs=[pl.BlockSpec((ts, m), lambda j, k: (k, 0)), pl.BlockSpec((ts, tn), lambda j, k: (k, j))],
        out_specs=pl.BlockSpec((m, tn), lambda j, k: (0, j)), out_shape=jax.ShapeDtypeStruct((m, n), F32),
        sem=("parallel", "arbitrary"), vmem_mb=VMEM_MB)(a, b)


ROPE_HALF = HEAD_DIM // 4


def _rope_tables(seq):
    t = jnp.arange(seq, dtype=jnp.int32)
    row = (t // GRID_W).astype(F32)
    col = (t % GRID_W).astype(F32)
    inv_freq = ROPE_THETA ** (-jnp.arange(ROPE_HALF, dtype=F32) / ROPE_HALF)
    ang_r = row[:, None] * inv_freq[None, :]
    ang_c = col[:, None] * inv_freq[None, :]
    cos = jnp.concatenate([jnp.cos(ang_r), jnp.cos(ang_r), jnp.cos(ang_c), jnp.cos(ang_c)], axis=-1)
    sin = jnp.concatenate([-jnp.sin(ang_r), jnp.sin(ang_r), -jnp.sin(ang_c), jnp.sin(ang_c)], axis=-1)
    return cos.T, sin.T


def _rope_partner(y):
    h = ROPE_HALF
    return jnp.concatenate([y[h:2 * h], y[0:h], y[3 * h:4 * h], y[2 * h:3 * h]], axis=0)


def _qk_specs(seq, head0):
    t = min(T_PREP, seq)
    src = pl.BlockSpec((None, HEAD_DIM, t), lambda i, h: (h + head0, 0, i))
    own = pl.BlockSpec((None, HEAD_DIM, t), lambda i, h: (h, 0, i))
    nat = pl.BlockSpec((None, t, HEAD_DIM), lambda i, h: (h, i, 0))
    col = pl.BlockSpec((HEAD_DIM, 1), lambda i, h: (0, 0))
    tab = pl.BlockSpec((HEAD_DIM, t), lambda i, h: (0, i))
    return t, src, own, nat, col, tab


def _qk_prep_fwd(qkv_t, head0, heads, gain, scale, cos, sin, name):
    seq = qkv_t.shape[2]
    t, src, own, nat, col, tab = _qk_specs(seq, head0)

    def body(x_ref, g_ref, c_ref, s_ref, ot_ref, on_ref):
        xf = x_ref[...]
        r = lax.rsqrt(jnp.mean(xf * xf, axis=0, keepdims=True) + EPS)
        y = xf * r * g_ref[...]
        z = (y * c_ref[...] + _rope_partner(y) * s_ref[...]) * scale
        ot_ref[...] = z.astype(BF16)
        on_ref[...] = z.T.astype(BF16)

    return _pcall(body, name=name, grid=(seq // t, heads), in_specs=[src, col, tab, tab], out_specs=[own, nat],
                  out_shape=[jax.ShapeDtypeStruct((heads, HEAD_DIM, seq), BF16),
                             jax.ShapeDtypeStruct((heads, seq, HEAD_DIM), BF16)],
                  sem=("parallel", "parallel"))(qkv_t, gain, cos, sin)


def _qk_prep_bwd(qkv_t, head0, dout, dout_is_t, gain, scale, cos, sin, name):
    heads = dout.shape[0]
    seq = qkv_t.shape[2]
    t, src, own, nat, col, tab = _qk_specs(seq, head0)

    def body(x_ref, d_ref, g_ref, c_ref, s_ref, dx_ref, dg_ref):
        xf = x_ref[...]
        r = lax.rsqrt(jnp.mean(xf * xf, axis=0, keepdims=True) + EPS)
        n = xf * r
        d = d_ref[...] if dout_is_t else d_ref[...].T
        dz = d * scale
        dy = dz * c_ref[...] + _rope_partner(dz * s_ref[...])
        dn = dy * g_ref[...]
        dx_ref[...] = r * (dn - n * jnp.mean(dn * n, axis=0, keepdims=True))

        @pl.when((pl.program_id(0) == 0) & (pl.program_id(1) == 0))
        def _():
            dg_ref[...] = jnp.zeros_like(dg_ref)

        dg_ref[pl.program_id(1)] += jnp.sum(dy * n, axis=1, keepdims=True)

    return _pcall(body, name=name, grid=(seq // t, heads),
                  in_specs=[src, own if dout_is_t else nat, col, tab, tab],
                  out_specs=[own, pl.BlockSpec((heads, HEAD_DIM, 1), lambda i, h: (0, 0, 0))],
                  out_shape=[jax.ShapeDtypeStruct((heads, HEAD_DIM, seq), F32),
                             jax.ShapeDtypeStruct((heads, HEAD_DIM, 1), F32)],
                  sem=("arbitrary", "arbitrary"))(qkv_t, dout, gain, cos, sin)


V_ROWS = HEAD_DIM + 8


def _unroll(nk, cap):
    u = 1
    while u * 2 <= cap and nk % (u * 2) == 0:
        u *= 2
    return u


def _attn_fwd(qs_t, k, v_t, name):
    seq = qs_t.shape[2]
    nk, bk = k.shape[1], k.shape[2]
    bq = min(BQ, seq)
    unroll = _unroll(nk, FWD_UNROLL)
    hps = FWD_HEADS_PER_STEP
    heads = range(hps)

    def body(qt_ref, k_ref, vt_ref, ot_ref, lse_ref, s_scr):
        q_t = [qt_ref[h] for h in heads]

        def scores(j, slot):
            kj = k_ref[j]
            top = []
            for h in heads:
                s = _dot(kj, q_t[h])
                s_scr[slot, h] = s
                top.append(jnp.max(s, axis=0, keepdims=True))
            return tuple(top)

        def accumulate(j, slot, state, top):
            vtj = vt_ref[j]
            out = []
            for h in heads:
                m, acc = state[h]
                m_new = jnp.maximum(m, top[h])
                p = jnp.exp2(s_scr[slot, h] - m_new).astype(BF16)
                out.append((m_new, jnp.exp2(m - m_new) * acc + _dot(vtj, p)))
            return tuple(out)

        def step(t, carry):
            state, top = carry
            for u in range(unroll):
                nxt = unroll * t + u + 1
                top_next = scores(jnp.minimum(nxt, nk - 1) if u == unroll - 1 else nxt, (u + 1) % 2)
                state = accumulate(unroll * t + u, u % 2, state, top)
                top = top_next
            return state, top

        init = tuple((jnp.full((1, bq), -jnp.inf, F32), jnp.zeros((V_ROWS, bq), F32)) for _ in heads)
        state, _ = lax.fori_loop(0, nk // unroll, step, (init, scores(0, 0)))
        for h in heads:
            m, acc = state[h]
            l = acc[HEAD_DIM:HEAD_DIM + 1, :]
            ot_ref[h] = acc[:HEAD_DIM, :] / l
            lse_ref[h] = m + jnp.log2(l)

    kv_of = lambda g: g * hps // GROUP
    return _pcall(
        body, name=name, grid=(ATT_HEADS // hps, seq // bq),
        in_specs=[pl.BlockSpec((hps, HEAD_DIM, bq), lambda g, i: (g, 0, i)),
                  pl.BlockSpec((None, nk, bk, HEAD_DIM), lambda g, i: (kv_of(g), 0, 0, 0)),
                  pl.BlockSpec((None, nk, V_ROWS, bk), lambda g, i: (kv_of(g), 0, 0, 0))],
        out_specs=[pl.BlockSpec((hps, HEAD_DIM, bq), lambda g, i: (g, 0, i)),
                   pl.BlockSpec((hps, 1, bq), lambda g, i: (g, 0, i))],
        out_shape=[jax.ShapeDtypeStruct((ATT_HEADS, HEAD_DIM, seq), F32),
                   jax.ShapeDtypeStruct((ATT_HEADS, 1, seq), F32)],
        scratch=[pltpu.VMEM((2, hps, bk, bq), F32)],
        sem=("parallel", "parallel"), vmem_mb=VMEM_MB)(qs_t, k, v_t)


def _attn_bwd(qs, qs_t, do_t, o_t, lse, k, k_t, v_t, name):
    seq = qs.shape[1]
    nk, bk = k.shape[1], k.shape[2]
    bq = min(BQ, seq)
    nq = seq // bq

    unroll = _unroll(nk, BWD_UNROLL)
    heads = range(HEADS_PER_STEP)
    pairs = GROUP // HEADS_PER_STEP

    def body(q_ref, qt_ref, dot_ref, ot_ref, lse_ref, k_ref, kt_ref, vt_ref, dq_ref, dkt_ref, dvt_ref,
             s_scr, dp_scr):
        @pl.when((pl.program_id(1) == 0) & (pl.program_id(2) == 0))
        def _():
            dkt_ref[...] = jnp.zeros_like(dkt_ref)
            dvt_ref[...] = jnp.zeros_like(dvt_ref)

        q, q_t = [q_ref[h] for h in heads], [qt_ref[h] for h in heads]
        do_t_b = [dot_ref[h].astype(BF16) for h in heads]
        do_l = [dot_ref[h].T * LN2 for h in heads]
        do_b = [d.astype(BF16) for d in do_l]
        delta = [jnp.sum(do_l[h] * ot_ref[h].T, axis=-1, keepdims=True) for h in heads]
        lse_col = [jnp.max(jnp.broadcast_to(lse_ref[h], (LANES, bq)).T, axis=-1, keepdims=True) for h in heads]

        def products(j, slot):
            ktj, vtj = kt_ref[j], vt_ref[j]
            for h in heads:
                s_scr[slot, h] = _dot(q[h], ktj)
                dp_scr[slot, h] = _dot(do_b[h], vtj).astype(BF16)

        def gradients(j, slot, dq):
            kj = k_ref[j]
            dvt = jnp.zeros((HEAD_DIM, bk), F32)
            dkt = jnp.zeros((HEAD_DIM, bk), F32)
            new = []
            for h in heads:
                p = jnp.exp2(s_scr[slot, h] - lse_col[h])
                ds = (p * (dp_scr[slot, h].astype(F32) - delta[h])).astype(BF16)
                dvt = dvt + _dot(do_t_b[h], p.astype(BF16))
                dkt = dkt + _dot(q_t[h], ds)
                new.append(dq[h] + _dot(ds, kj))
            dvt_ref[j] += dvt
            dkt_ref[j] += dkt
            return tuple(new)

        def step(t, dq):
            for u in range(unroll):
                nxt = unroll * t + u + 1
                products(jnp.minimum(nxt, nk - 1) if u == unroll - 1 else nxt, (u + 1) % 2)
                dq = gradients(unroll * t + u, u % 2, dq)
            return dq

        products(0, 0)
        res = lax.fori_loop(0, nk // unroll, step, tuple(jnp.zeros((bq, HEAD_DIM), F32) for _ in heads))
        for h in heads:
            dq_ref[h] = res[h]

    first = lambda g, hh: g * pairs + hh
    row = pl.BlockSpec((HEADS_PER_STEP, bq, HEAD_DIM), lambda g, hh, i: (first(g, hh), i, 0))
    col = pl.BlockSpec((HEADS_PER_STEP, HEAD_DIM, bq), lambda g, hh, i: (first(g, hh), 0, i))
    kv_rows = pl.BlockSpec((None, nk, bk, HEAD_DIM), lambda g, hh, i: (g, 0, 0, 0))
    kv_cols = pl.BlockSpec((None, nk, HEAD_DIM, bk), lambda g, hh, i: (g, 0, 0, 0))
    return _pcall(
        body, name=name, grid=(KV_HEADS, pairs, nq),
        in_specs=[row, col, col, col,
                  pl.BlockSpec((HEADS_PER_STEP, 1, bq), lambda g, hh, i: (first(g, hh), 0, i)),
                  kv_rows, kv_cols, kv_cols],
        out_specs=[row, kv_cols, kv_cols],
        out_shape=[jax.ShapeDtypeStruct((ATT_HEADS, seq, HEAD_DIM), F32),
                   jax.ShapeDtypeStruct((KV_HEADS, nk, HEAD_DIM, bk), F32),
                   jax.ShapeDtypeStruct((KV_HEADS, nk, HEAD_DIM, bk), F32)],
        scratch=[pltpu.VMEM((2, HEADS_PER_STEP, bq, bk), F32), pltpu.VMEM((2, HEADS_PER_STEP, bq, bk), BF16)],
        sem=("parallel", "arbitrary", "arbitrary"), vmem_mb=ATTN_BWD_VMEM_MB)(
            qs, qs_t, do_t, o_t, lse, k, k_t, v_t)


def _halo_specs(t, col, n_tiles):
    per = t // HALO
    last = n_tiles * per - 1
    before = pl.BlockSpec((HALO, CONV_W), lambda i: (jnp.maximum(i * per - 1, 0), col))
    after = pl.BlockSpec((HALO, CONV_W), lambda i: (jnp.minimum((i + 1) * per, last), col))
    return before, after


def _glu(a, b):
    return a * _sigmoid(b)


SUBLANES = 8
SHIFT_ROWS = 2 * HALO - SUBLANES


def _shifted(sh_ref, off, t):
    return sh_ref[off % SUBLANES, pl.ds(off - off % SUBLANES, t), :]


def _conv_taps(sh_ref, w_ref, t, flip):
    acc = jnp.zeros((t, CONV_W), F32)
    for k in range(CONV_K):
        off = (HALO + CONV_PAD - k) if flip else (HALO - CONV_PAD + k)
        acc = acc + w_ref[k:k + 1, :] * _shifted(sh_ref, off, t)
    return acc


def _fill_ext(ext_ref, sh_ref, before, tile, after, t, i, n_tiles):
    ext_ref[pl.ds(0, HALO), :] = jnp.where(i > 0, before, 0.0)
    ext_ref[pl.ds(HALO, t), :] = tile
    ext_ref[pl.ds(HALO + t, HALO), :] = jnp.where(i < n_tiles - 1, after, 0.0)
    for b in range(SUBLANES):
        sh_ref[b] = ext_ref[pl.ds(b, t + SHIFT_ROWS), :]


def _conv_scratch(t):
    return [pltpu.VMEM((t + 2 * HALO, CONV_W), F32), pltpu.VMEM((SUBLANES, t + SHIFT_ROWS, CONV_W), F32)]


def _conv_fwd(proj, w, bias, ln_g, ln_b, name):
    seq = proj.shape[0]
    t = min(T_GROUP, seq)
    n_tiles = seq // t

    def body(a_ref, b_ref, ap_ref, bp_ref, an_ref, bn_ref, gate_ref, w_ref, bias_ref, g_ref, beta_ref, o_ref, y_ref,
             ext_ref, sh_ref):
        i = pl.program_id(0)
        _fill_ext(ext_ref, sh_ref, _glu(ap_ref[...], bp_ref[...]), _glu(a_ref[...], b_ref[...]),
                  _glu(an_ref[...], bn_ref[...]), t, i, n_tiles)
        y = _conv_taps(sh_ref, w_ref, t, False) + bias_ref[...]
        y_ref[...] = y
        mu = jnp.mean(y, axis=-1, keepdims=True)
        yc = y - mu
        rs = lax.rsqrt(jnp.mean(yc * yc, axis=-1, keepdims=True) + EPS)
        z = yc * rs * g_ref[...] + beta_ref[...]
        o_ref[...] = _silu_and_grad(z)[0] * _silu_and_grad(gate_ref[...])[0]

    tile = lambda c: pl.BlockSpec((t, CONV_W), lambda i: (i, c))
    ab, aa = _halo_specs(t, COL_A, n_tiles)
    bb, ba = _halo_specs(t, COL_B, n_tiles)
    vec = pl.BlockSpec((1, CONV_W), lambda i: (0, 0))
    return _pcall(
        body, name=name, grid=(n_tiles,),
        in_specs=[tile(COL_A), tile(COL_B), ab, bb, aa, ba, tile(COL_GCONV),
                  pl.BlockSpec((CONV_K, CONV_W), lambda i: (0, 0)), vec, vec, vec],
        out_specs=[pl.BlockSpec((t, CONV_W), lambda i: (i, 0))] * 2,
        out_shape=[jax.ShapeDtypeStruct((seq, CONV_W), F32)] * 2,
        scratch=_conv_scratch(t), sem=("parallel",))(
            proj, proj, proj, proj, proj, proj, proj, w, bias, ln_g, ln_b)


def _conv_bwd_a(proj, y_conv, dcnv, ln_g, ln_b, name):
    seq = proj.shape[0]
    t = min(T_GROUP, seq)
    n_tiles = seq // t

    def body(a_ref, b_ref, ap_ref, bp_ref, an_ref, bn_ref, gate_ref, y_ref, d_ref, g_ref, beta_ref,
             dy_ref, dgate_ref, dw_ref, dbias_ref, dg_ref, dbeta_ref, ext_ref, sh_ref):
        i = pl.program_id(0)
        _fill_ext(ext_ref, sh_ref, _glu(ap_ref[...], bp_ref[...]), _glu(a_ref[...], b_ref[...]),
                  _glu(an_ref[...], bn_ref[...]), t, i, n_tiles)
        y = y_ref[...]
        mu = jnp.mean(y, axis=-1, keepdims=True)
        yc = y - mu
        rs = lax.rsqrt(jnp.mean(yc * yc, axis=-1, keepdims=True) + EPS)
        n = yc * rs
        z = n * g_ref[...] + beta_ref[...]
        act, dact = _silu_and_grad(z)
        gate, dgate = _silu_and_grad(gate_ref[...])
        d = d_ref[...]
        dgate_ref[...] = d * act * dgate
        dz = d * gate * dact
        dn = dz * g_ref[...]
        dy = rs * (dn - jnp.mean(dn, axis=-1, keepdims=True) - n * jnp.mean(dn * n, axis=-1, keepdims=True))
        dy_ref[...] = dy

        @pl.when(i == 0)
        def _():
            dw_ref[...] = jnp.zeros_like(dw_ref)
            dbias_ref[...] = jnp.zeros_like(dbias_ref)
            dg_ref[...] = jnp.zeros_like(dg_ref)
            dbeta_ref[...] = jnp.zeros_like(dbeta_ref)

        dg_ref[...] += jnp.sum(dz * n, axis=0, keepdims=True)
        dbeta_ref[...] += jnp.sum(dz, axis=0, keepdims=True)
        dbias_ref[...] += jnp.sum(dy, axis=0, keepdims=True)
        for k in range(CONV_K):
            dw_ref[k:k + 1, :] += jnp.sum(dy * _shifted(sh_ref, HALO - CONV_PAD + k, t), axis=0, keepdims=True)

    tile = lambda c: pl.BlockSpec((t, CONV_W), lambda i: (i, c))
    own = pl.BlockSpec((t, CONV_W), lambda i: (i, 0))
    ab, aa = _halo_specs(t, COL_A, n_tiles)
    bb, ba = _halo_specs(t, COL_B, n_tiles)
    vec = pl.BlockSpec((1, CONV_W), lambda i: (0, 0))
    taps = pl.BlockSpec((CONV_K, CONV_W), lambda i: (0, 0))
    vshape = jax.ShapeDtypeStruct((1, CONV_W), F32)
    return _pcall(
        body, name=name, grid=(n_tiles,),
        in_specs=[tile(COL_A), tile(COL_B), ab, bb, aa, ba, tile(COL_GCONV), own, own, vec, vec],
        out_specs=[own, own, taps, vec, vec, vec],
        out_shape=[jax.ShapeDtypeStruct((seq, CONV_W), F32), jax.ShapeDtypeStruct((seq, CONV_W), F32),
                   jax.ShapeDtypeStruct((CONV_K, CONV_W), F32), vshape, vshape, vshape],
        scratch=_conv_scratch(t), sem=("arbitrary",))(
            proj, proj, proj, proj, proj, proj, proj, y_conv, dcnv, ln_g, ln_b)


def _conv_bwd_b(proj, dy, w, name):
    seq = proj.shape[0]
    t = min(T_GROUP, seq)
    n_tiles = seq // t

    def body(a_ref, b_ref, dy_ref, dyp_ref, dyn_ref, w_ref, da_ref, db_ref, ext_ref, sh_ref):
        i = pl.program_id(0)
        _fill_ext(ext_ref, sh_ref, dyp_ref[...], dy_ref[...], dyn_ref[...], t, i, n_tiles)
        dh = _conv_taps(sh_ref, w_ref, t, True)
        sig = _sigmoid(b_ref[...])
        da_ref[...] = dh * sig
        db_ref[...] = dh * a_ref[...] * sig * (1.0 - sig)

    tile = lambda c: pl.BlockSpec((t, CONV_W), lambda i: (i, c))
    own = pl.BlockSpec((t, CONV_W), lambda i: (i, 0))
    before, after = _halo_specs(t, 0, n_tiles)
    return _pcall(
        body, name=name, grid=(n_tiles,),
        in_specs=[tile(COL_A), tile(COL_B), own, before, after, pl.BlockSpec((CONV_K, CONV_W), lambda i: (0, 0))],
        out_specs=[own, own], out_shape=[jax.ShapeDtypeStruct((seq, CONV_W), F32)] * 2,
        scratch=_conv_scratch(t), sem=("parallel",))(proj, proj, dy, dy, dy, w)


def _head_masks():
    lane_head = lax.broadcasted_iota(jnp.int32, (SG_CHUNK, SG_W), 1) // HEAD_DIM
    return [lane_head == h for h in range(SG_HEADS)]


def _sg_mix(mats_ref, rhs, masks):
    out = jnp.zeros((SG_CHUNK, SG_W), F32)
    for h in range(SG_HEADS):
        out = out + jnp.where(masks[h], _dot(mats_ref[h], rhs), 0.0)
    return out


def _sg_specs(seq):
    t = min(T_GROUP, seq)
    tile = lambda c: pl.BlockSpec((t, SG_W), lambda i: (i, c))
    own = pl.BlockSpec((t, SG_W), lambda i: (i, 0))
    vec = pl.BlockSpec((1, SG_W), lambda i: (0, 0))
    mats = pl.BlockSpec((SG_HEADS, SG_CHUNK, SG_CHUNK), lambda i: (0, 0, 0))
    full = pl.BlockSpec((SG_CHUNK, SG_W), lambda i: (0, 0))
    return t, tile, own, vec, mats, full


def _sg_fwd(proj, ln_g, ln_b, w_b, bias_full, name):
    seq = proj.shape[0]
    t, tile, own, vec, mats, full = _sg_specs(seq)

    def body(u_ref, v_ref, gate_ref, g_ref, beta_ref, w_ref, bias_ref, o_ref):
        masks = _head_masks()
        for c in range(t // SG_CHUNK):
            rows = pl.ds(c * SG_CHUNK, SG_CHUNK)
            vg = _gelu_and_grad(v_ref[rows, :])[0]
            mu = jnp.mean(vg, axis=-1, keepdims=True)
            vc = vg - mu
            rs = lax.rsqrt(jnp.mean(vc * vc, axis=-1, keepdims=True) + EPS)
            vln = vc * rs * g_ref[...] + beta_ref[...]
            mixed = _sg_mix(w_ref, vln.astype(BF16), masks) + bias_ref[...]
            o_ref[rows, :] = _gelu_and_grad(u_ref[rows, :])[0] * mixed * _silu_and_grad(gate_ref[rows, :])[0]

    return _pcall(body, name=name, grid=(seq // t,),
                  in_specs=[tile(COL_U), tile(COL_VSG), tile(COL_GSG), vec, vec, mats, full], out_specs=own,
                  out_shape=jax.ShapeDtypeStruct((seq, SG_W), F32), sem=("parallel",))(
                      proj, proj, proj, ln_g, ln_b, w_b, bias_full)


def _sg_bwd(proj, dsg, ln_g, ln_b, w_b, w_t_b, bias_full, fold, name):
    seq = proj.shape[0]
    t, tile, own, vec, mats, full = _sg_specs(seq)
    n_tiles = seq // t

    def body(u_ref, v_ref, gate_ref, d_ref, g_ref, beta_ref, w_ref, wt_ref, bias_ref, fold_ref,
             du_ref, dv_ref, dgate_ref, dg_ref, dbeta_ref, dw_ref, db_ref, dbias_acc):
        i = pl.program_id(0)

        @pl.when(i == 0)
        def _():
            dg_ref[...] = jnp.zeros_like(dg_ref)
            dbeta_ref[...] = jnp.zeros_like(dbeta_ref)
            dw_ref[...] = jnp.zeros_like(dw_ref)
            dbias_acc[...] = jnp.zeros_like(dbias_acc)

        masks = _head_masks()
        for c in range(t // SG_CHUNK):
            rows = pl.ds(c * SG_CHUNK, SG_CHUNK)
            ug, dug = _gelu_and_grad(u_ref[rows, :])
            vg, dvg = _gelu_and_grad(v_ref[rows, :])
            mu = jnp.mean(vg, axis=-1, keepdims=True)
            vc = vg - mu
            rs = lax.rsqrt(jnp.mean(vc * vc, axis=-1, keepdims=True) + EPS)
            vn = vc * rs
            vln_b = (vn * g_ref[...] + beta_ref[...]).astype(BF16)
            mixed = _sg_mix(w_ref, vln_b, masks) + bias_ref[...]
            gate, dgate = _silu_and_grad(gate_ref[rows, :])
            d = d_ref[rows, :]
            dgate_ref[rows, :] = d * ug * mixed * dgate
            du_ref[rows, :] = d * mixed * gate * dug
            dmixed = d * ug * gate
            dbias_acc[...] += dmixed
            dmixed_b = dmixed.astype(BF16)
            for h in range(SG_HEADS):
                dm_h = jnp.where(masks[h], dmixed_b, jnp.zeros_like(dmixed_b))
                dw_ref[h] += lax.dot_general(dm_h, vln_b, (((1,), (1,)), ((), ())), preferred_element_type=F32)
            dvln = _sg_mix(wt_ref, dmixed_b, masks)
            dg_ref[...] += jnp.sum(dvln * vn, axis=0, keepdims=True)
            dbeta_ref[...] += jnp.sum(dvln, axis=0, keepdims=True)
            dvn = dvln * g_ref[...]
            dvgelu = rs * (dvn - jnp.mean(dvn, axis=-1, keepdims=True) - vn * jnp.mean(dvn * vn, axis=-1, keepdims=True))
            dv_ref[rows, :] = dvgelu * dvg

        @pl.when(i == n_tiles - 1)
        def _():
            db_ref[...] = _split_dot(dbias_acc[...], fold_ref[...])

    sq = pl.BlockSpec((SG_CHUNK, SG_CHUNK), lambda i: (0, 0))
    vshape = jax.ShapeDtypeStruct((1, SG_W), F32)
    return _pcall(
        body, name=name, grid=(n_tiles,),
        in_specs=[tile(COL_U), tile(COL_VSG), tile(COL_GSG), own, vec, vec, mats, mats, full,
                  pl.BlockSpec((SG_W, SG_CHUNK), lambda i: (0, 0))],
        out_specs=[own, own, own, vec, vec, mats, sq],
        out_shape=[jax.ShapeDtypeStruct((seq, SG_W), F32)] * 3 + [
            vshape, vshape, jax.ShapeDtypeStruct((SG_HEADS, SG_CHUNK, SG_CHUNK), F32),
            jax.ShapeDtypeStruct((SG_CHUNK, SG_CHUNK), F32)],
        scratch=[pltpu.VMEM((SG_CHUNK, SG_W), F32)], sem=("arbitrary",))(
            proj, proj, proj, dsg, ln_g, ln_b, w_b, w_t_b, bias_full, fold)


def _out_fwd(att_t, proj, cnv, sgu, x, w, gain, name):
    seq = x.shape[0]

    def body(att_ref, g0_ref, g1_ref, cnv_ref, sgu_ref, x_ref, w_ref, gain_ref, xo_ref, mix_ref, cat_ref):
        gate = jnp.concatenate([_silu_and_grad(g0_ref[...])[0], _silu_and_grad(g1_ref[...])[0]], axis=-1)
        cat_ref[:, 0:ATT_W] = (att_ref[...].T * gate).astype(BF16)
        cat_ref[:, ATT_W:ATT_W + CONV_W] = cnv_ref[...].astype(BF16)
        cat_ref[:, ATT_W + CONV_W:] = sgu_ref[...].astype(BF16)
        mix = _dot(cat_ref[...], w_ref[...])
        mix_ref[...] = mix
        r = lax.rsqrt(jnp.mean(mix * mix, axis=-1, keepdims=True) + EPS)
        xo_ref[...] = x_ref[...] + mix * r * gain_ref[...]

    row = lambda w_: pl.BlockSpec((T_ROW, w_), lambda i: (i, 0))
    gate_blk = lambda c: pl.BlockSpec((T_ROW, 256), lambda i: (i, c))
    return _pcall(
        body, name=name, grid=(seq // T_ROW,),
        in_specs=[pl.BlockSpec((ATT_W, T_ROW), lambda i: (0, i)), gate_blk(COL_GATT), gate_blk(COL_GATT + 1),
                  row(CONV_W), row(SG_W), row(D_MODEL),
                  pl.BlockSpec((D_MODEL, D_MODEL), lambda i: (0, 0)), pl.BlockSpec((1, D_MODEL), lambda i: (0, 0))],
        out_specs=[row(D_MODEL), row(D_MODEL), row(D_MODEL)],
        out_shape=[jax.ShapeDtypeStruct((seq, D_MODEL), F32), jax.ShapeDtypeStruct((seq, D_MODEL), F32),
                   jax.ShapeDtypeStruct((seq, D_MODEL), BF16)],
        sem=("parallel",), vmem_mb=VMEM_MB)(att_t, proj, proj, cnv, sgu, x, w, gain)


def _out_bwd(dxo, mix, gain, w_t, att_t, proj, name):
    seq = dxo.shape[0]

    def body(dxo_ref, mix_ref, gain_ref, w_ref, att_ref, g0_ref, g1_ref,
             dmix_ref, datt_ref, dgatt_ref, dcnv_ref, dsgu_ref, dgain_ref):
        mix = mix_ref[...]
        r = lax.rsqrt(jnp.mean(mix * mix, axis=-1, keepdims=True) + EPS)
        n = mix * r
        dout = dxo_ref[...]
        dn = dout * gain_ref[...]
        dmix = (r * (dn - n * jnp.mean(dn * n, axis=-1, keepdims=True))).astype(BF16)
        dmix_ref[...] = dmix

        @pl.when(pl.program_id(0) == 0)
        def _():
            dgain_ref[...] = jnp.zeros_like(dgain_ref)

        dgain_ref[...] += jnp.sum(dout * n, axis=0, keepdims=True)
        dcat = _dot(dmix, w_ref[...])
        g0, dg0 = _silu_and_grad(g0_ref[...])
        g1, dg1 = _silu_and_grad(g1_ref[...])
        gate = jnp.concatenate([g0, g1], axis=-1)
        dgate = jnp.concatenate([dg0, dg1], axis=-1)
        dca = dcat[:, 0:ATT_W]
        datt_ref[...] = (dca * gate).T
        dgatt_ref[...] = dca * att_ref[...].T * dgate
        dcnv_ref[...] = dcat[:, ATT_W:ATT_W + CONV_W]
        dsgu_ref[...] = dcat[:, ATT_W + CONV_W:]

    row = lambda w_: pl.BlockSpec((T_ROW, w_), lambda i: (i, 0))
    gate_blk = lambda c: pl.BlockSpec((T_ROW, 256), lambda i: (i, c))
    vec = pl.BlockSpec((1, D_MODEL), lambda i: (0, 0))
    heads_t = pl.BlockSpec((ATT_W, T_ROW), lambda i: (0, i))
    return _pcall(
        body, name=name, grid=(seq // T_ROW,),
        in_specs=[row(D_MODEL), row(D_MODEL), vec, pl.BlockSpec((D_MODEL, D_MODEL), lambda i: (0, 0)), heads_t,
                  gate_blk(COL_GATT), gate_blk(COL_GATT + 1)],
        out_specs=[row(D_MODEL), heads_t, row(ATT_W), row(CONV_W), row(SG_W), vec],
        out_shape=[jax.ShapeDtypeStruct((seq, D_MODEL), BF16), jax.ShapeDtypeStruct((ATT_W, seq), F32),
                   jax.ShapeDtypeStruct((seq, ATT_W), F32), jax.ShapeDtypeStruct((seq, CONV_W), F32),
                   jax.ShapeDtypeStruct((seq, SG_W), F32), jax.ShapeDtypeStruct((1, D_MODEL), F32)],
        sem=("arbitrary",), vmem_mb=VMEM_MB)(dxo, mix, gain, w_t, att_t, proj, proj)


def _loss_head(y, target, name):
    seq = y.shape[0]
    t = min(T_GROUP, seq)

    def body(y_ref, t_ref, sse_ref, dy_ref):
        err = y_ref[...] - t_ref[...]
        dy_ref[...] = err * (1.0 / D_MODEL)

        @pl.when(pl.program_id(0) == 0)
        def _():
            sse_ref[...] = jnp.zeros_like(sse_ref)

        part = jnp.sum(jnp.sum(err * err, axis=0, keepdims=True), axis=-1, keepdims=True)
        sse_ref[...] += jnp.broadcast_to(part, (1, LANES))

    row = pl.BlockSpec((t, D_MODEL), lambda i: (i, 0))
    return _pcall(body, name=name, grid=(seq // t,), in_specs=[row, row],
                  out_specs=[pl.BlockSpec((1, LANES), lambda i: (0, 0)), row],
                  out_shape=[jax.ShapeDtypeStruct((1, LANES), F32), jax.ShapeDtypeStruct((seq, D_MODEL), F32)],
                  sem=("arbitrary",))(y, target)


def _row_blocks(a, bk):
    return a.reshape(a.shape[0], a.shape[1] // bk, bk, HEAD_DIM)


def _lane_blocks(a, bk):
    return a.reshape(a.shape[0], HEAD_DIM, a.shape[2] // bk, bk).transpose(0, 2, 1, 3)


def _from_lane_blocks(a):
    return a.transpose(0, 2, 1, 3).reshape(a.shape[0], HEAD_DIM, a.shape[1] * a.shape[3])


def _flat_rows(a, rows):
    flat = a.reshape(-1)
    return jnp.pad(flat, (0, rows * LANES - flat.shape[0])).reshape(rows, LANES)


SHARD_ROWS = {"w_in": 2 * D_MODEL * (D_IN // N_DEV) // LANES, "w_out": 2 * (D_MODEL // N_DEV) * D_MODEL // LANES,
              "conv_dw": 16}
REPL_SHAPES = [("pre_norm", (2, D_MODEL)), ("post_norm", (2, D_MODEL)), ("q_norm", (2, HEAD_DIM)),
               ("k_norm", (2, HEAD_DIM)), ("conv_dw_b", (2, CONV_W)), ("conv_ln_g", (2, CONV_W)),
               ("conv_ln_b", (2, CONV_W)), ("sg_ln_g", (2, SG_W)), ("sg_ln_b", (2, SG_W)),
               ("sg_w", (2, SG_HEADS, SG_CHUNK, SG_CHUNK)), ("sg_b", (2, SG_HEADS, SG_CHUNK))]
REPL_ROWS = 1088
WEIGHT_ORDER = ["pre_norm", "post_norm", "w_in", "w_out", "q_norm", "k_norm", "conv_dw", "conv_dw_b", "conv_ln_g",
                "conv_ln_b", "sg_ln_g", "sg_ln_b", "sg_w", "sg_b"]


def _pack_shard(parts):
    return jnp.concatenate([_flat_rows(parts[k], SHARD_ROWS[k]) for k in ("w_in", "w_out", "conv_dw")], axis=0)


def _unpack_shard(flat, shapes):
    out, at = {}, 0
    for k in ("w_in", "w_out", "conv_dw"):
        size = math.prod(shapes[k])
        out[k] = flat[at:at + SHARD_ROWS[k]].reshape(-1)[:size].reshape(shapes[k])
        at += SHARD_ROWS[k]
    return out


REPL_USED = sum(math.prod(shape) for _, shape in REPL_SHAPES)


def _pack_repl(parts, extra=None):
    tail = [] if extra is None else [extra.reshape(1)]
    flat = jnp.concatenate([parts[k].reshape(-1) for k, _ in REPL_SHAPES] + tail)
    return jnp.pad(flat, (0, REPL_ROWS * LANES - flat.shape[0])).reshape(REPL_ROWS, LANES)


def _unpack_repl(flat):
    out, at, flat = {}, 0, flat.reshape(-1)
    for k, shape in REPL_SHAPES:
        size = math.prod(shape)
        out[k] = flat[at:at + size].reshape(shape)
        at += size
    return out


def kernel(x, pre_norm, post_norm, w_in, w_out, q_norm, k_norm, conv_dw, conv_dw_b, conv_ln_g, conv_ln_b, sg_ln_g, sg_ln_b, sg_w, sg_b, loss_target, m_pre_norm, m_post_norm, m_w_in, m_w_out, m_q_norm, m_k_norm, m_conv_dw, m_conv_dw_b, m_conv_ln_g, m_conv_ln_b, m_sg_ln_g, m_sg_ln_b, m_sg_w, m_sg_b, v_pre_norm, v_post_norm, v_w_in, v_w_out, v_q_norm, v_k_norm, v_conv_dw, v_conv_dw_b, v_conv_ln_g, v_conv_ln_b, v_sg_ln_g, v_sg_ln_b, v_sg_w, v_sg_b):
    weights = dict(pre_norm=pre_norm, post_norm=post_norm, w_in=w_in, w_out=w_out, q_norm=q_norm, k_norm=k_norm,
                   conv_dw=conv_dw, conv_dw_b=conv_dw_b, conv_ln_g=conv_ln_g, conv_ln_b=conv_ln_b, sg_ln_g=sg_ln_g,
                   sg_ln_b=sg_ln_b, sg_w=sg_w, sg_b=sg_b)
    mom_m = dict(pre_norm=m_pre_norm, post_norm=m_post_norm, w_in=m_w_in, w_out=m_w_out, q_norm=m_q_norm,
                 k_norm=m_k_norm, conv_dw=m_conv_dw, conv_dw_b=m_conv_dw_b, conv_ln_g=m_conv_ln_g,
                 conv_ln_b=m_conv_ln_b, sg_ln_g=m_sg_ln_g, sg_ln_b=m_sg_ln_b, sg_w=m_sg_w, sg_b=m_sg_b)
    mom_v = dict(pre_norm=v_pre_norm, post_norm=v_post_norm, w_in=v_w_in, w_out=v_w_out, q_norm=v_q_norm,
                 k_norm=v_k_norm, conv_dw=v_conv_dw, conv_dw_b=v_conv_dw_b, conv_ln_g=v_conv_ln_g,
                 conv_ln_b=v_conv_ln_b, sg_ln_g=v_sg_ln_g, sg_ln_b=v_sg_ln_b, sg_w=v_sg_w, sg_b=v_sg_b)
    depth = pre_norm.shape[0]
    seq = x.shape[1]
    bk = min(BK, seq)
    x0 = x.reshape(seq, D_MODEL)
    target = loss_target.reshape(seq, D_MODEL)

    w_in_all, w_out_all, dw_all = _exchange(
        [], [w_in.astype(BF16), w_out.astype(BF16), jnp.pad(conv_dw, ((0, 0), (0, 1), (0, 0)))], "gather_weights")
    w_in_full = w_in_all.transpose(1, 2, 0, 3).reshape(depth, D_MODEL, D_IN)
    w_out_full = w_out_all.transpose(1, 0, 2, 3).reshape(depth, D_MODEL, D_MODEL)
    dw_full = dw_all[:, :, :CONV_K, :].transpose(1, 2, 0, 3).reshape(depth, CONV_K, CONV_W)

    cos, sin = _rope_tables(seq)
    lane = jnp.arange(SG_W)
    fold = (lane[:, None] // HEAD_DIM == jnp.arange(SG_CHUNK)[None, :]).astype(BF16)

    def layer_consts(l):
        return dict(
            q_gain=q_norm[l].reshape(HEAD_DIM, 1), k_gain=k_norm[l].reshape(HEAD_DIM, 1), sg_w_b=sg_w[l].astype(BF16), sg_wt_b=sg_w[l].transpose(0, 2, 1).astype(BF16),
            sg_bias=jnp.repeat(sg_b[l].T, HEAD_DIM, axis=1),
            vec=lambda a: a[l].reshape(1, -1))

    saved = []
    xc = x0
    for l in range(depth):
        c = layer_consts(l)
        proj, hb, qkv_t = _proj_fwd(xc, c["vec"](pre_norm), w_in_full[l], f"proj_fwd_{l}")
        qkv_t = qkv_t.reshape(QKV_HEADS, HEAD_DIM, seq)
        qs_t, qs = _qk_prep_fwd(qkv_t, 0, ATT_HEADS, c["q_gain"], Q_SCALE, cos, sin, f"q_prep_fwd_{l}")
        kr_t, kr = _qk_prep_fwd(qkv_t, ATT_HEADS, KV_HEADS, c["k_gain"], 1.0, cos, sin, f"k_prep_fwd_{l}")
        v_t = qkv_t[ATT_HEADS + KV_HEADS:].astype(BF16)
        k_rows, k_cols, v_cols = _row_blocks(kr, bk), _lane_blocks(kr_t, bk), _lane_blocks(v_t, bk)
        v_ext = jnp.concatenate([v_cols, jnp.ones_like(v_cols[:, :, :1]), jnp.zeros_like(v_cols[:, :, :7])], axis=2)
        o_t, lse = _attn_fwd(qs_t, k_rows, v_ext, f"attn_fwd_{l}")
        att_t = o_t.reshape(ATT_W, seq)
        cnv, y_conv = _conv_fwd(proj, dw_full[l], c["vec"](conv_dw_b), c["vec"](conv_ln_g), c["vec"](conv_ln_b),
                                f"conv_fwd_{l}")
        sgu = _sg_fwd(proj, c["vec"](sg_ln_g), c["vec"](sg_ln_b), c["sg_w_b"], c["sg_bias"], f"sg_fwd_{l}")
        x_new, mix, cat_b = _out_fwd(att_t, proj, cnv, sgu, xc, w_out_full[l], c["vec"](post_norm), f"out_fwd_{l}")
        saved.append(dict(x=xc, proj=proj, hb=hb, qkv_t=qkv_t, qs=qs, qs_t=qs_t, k_rows=k_rows,
                          k_cols=k_cols, v_cols=v_cols, o_t=o_t, lse=lse, mix=mix, cat_b=cat_b, y_conv=y_conv))
        xc = x_new

    sse, dx = _loss_head(xc, target, "loss_head")

    grads = {k: [None] * depth for k in WEIGHT_ORDER}
    for l in reversed(range(depth)):
        c, s = layer_consts(l), saved[l]
        dmix_b, datt, dgatt, dcnv, dsgu, g_post = _out_bwd(
            dx, s["mix"], c["vec"](post_norm), w_out_full[l].T, s["o_t"].reshape(ATT_W, seq), s["proj"],
            f"out_bwd_{l}")
        grads["post_norm"][l] = g_post.reshape(-1)
        grads["w_out"][l] = _matmul_acc(s["cat_b"], dmix_b, D_MODEL, f"grad_w_out_{l}")
        dqs, dkt, dvt = _attn_bwd(s["qs"], s["qs_t"], datt.reshape(ATT_HEADS, HEAD_DIM, seq), s["o_t"], s["lse"],
                                  s["k_rows"], s["k_cols"], s["v_cols"], f"attn_bwd_{l}")
        d_q_t, g_qgain = _qk_prep_bwd(s["qkv_t"], 0, dqs, False, c["q_gain"], Q_SCALE, cos, sin, f"q_prep_bwd_{l}")
        d_k_t, g_kgain = _qk_prep_bwd(s["qkv_t"], ATT_HEADS, _from_lane_blocks(dkt), True, c["k_gain"], 1.0, cos, sin,
                                      f"k_prep_bwd_{l}")
        grads["q_norm"][l] = jnp.sum(g_qgain[:, :, 0], axis=0)
        grads["k_norm"][l] = jnp.sum(g_kgain[:, :, 0], axis=0)
        d_qkv = jnp.concatenate([d_q_t, d_k_t, _from_lane_blocks(dvt)], axis=0).reshape(QKV_W, seq).T
        dy_conv, dg_conv, g_dw, g_dwb, g_clg, g_clb = _conv_bwd_a(
            s["proj"], s["y_conv"], dcnv, c["vec"](conv_ln_g), c["vec"](conv_ln_b), f"conv_bwd_a_{l}")
        da, db = _conv_bwd_b(s["proj"], dy_conv, dw_full[l], f"conv_bwd_b_{l}")
        grads["conv_dw"][l], grads["conv_dw_b"][l] = g_dw, g_dwb.reshape(-1)
        grads["conv_ln_g"][l], grads["conv_ln_b"][l] = g_clg.reshape(-1), g_clb.reshape(-1)
        du, dv_sg, dg_sg, g_slg, g_slb, g_sw, g_sb = _sg_bwd(
            s["proj"], dsgu, c["vec"](sg_ln_g), c["vec"](sg_ln_b), c["sg_w_b"], c["sg_wt_b"], c["sg_bias"], fold,
            f"sg_bwd_{l}")
        grads["sg_ln_g"][l], grads["sg_ln_b"][l] = g_slg.reshape(-1), g_slb.reshape(-1)
        grads["sg_w"][l], grads["sg_b"][l] = g_sw, g_sb[:, :SG_HEADS].T
        dproj = jnp.concatenate([d_qkv, dgatt, da, db, dg_conv, du, dv_sg, dg_sg], axis=-1).astype(BF16)
        grads["w_in"][l] = _matmul_acc(s["hb"], dproj, D_IN // 2, f"grad_w_in_{l}")
        dx, g_pre = _proj_bwd(dproj, w_in_full[l].T, s["x"], c["vec"](pre_norm), dx, f"proj_bwd_{l}")
        grads["pre_norm"][l] = g_pre.reshape(-1)
    grad_x = dx.reshape(x.shape)
    grads = {k: jnp.stack(v) for k, v in grads.items()}

    shard_blocks = dict(
        w_in=grads["w_in"].reshape(depth, D_MODEL, N_DEV, D_IN // N_DEV).transpose(2, 0, 1, 3),
        w_out=grads["w_out"].reshape(depth, N_DEV, D_MODEL // N_DEV, D_MODEL).transpose(1, 0, 2, 3),
        conv_dw=grads["conv_dw"].reshape(depth, CONV_K, N_DEV, CONV_W // N_DEV).transpose(2, 0, 1, 3))
    scatter_src = jnp.stack([_pack_shard({k: a[d] for k, a in shard_blocks.items()})
                             for d in range(N_DEV)]).astype(BF16)
    shard_slots, repl_slots = _exchange([scatter_src], [_pack_repl(grads, sse[0, 0])], "exchange_grads")

    shard_shapes = {k: weights[k].shape for k in SHARD_ROWS}
    gs, ds_, ms, vs = _sum_adamw(shard_slots, _pack_shard(weights), _pack_shard(mom_m), _pack_shard(mom_v),
                                 "adamw_sharded")
    gr, dr, mr, vr = _sum_adamw(repl_slots, _pack_repl(weights), _pack_repl(mom_m), _pack_repl(mom_v),
                                "adamw_replicated")
    loss = gr.reshape(-1)[REPL_USED] * (0.5 / D_MODEL)
    results = []
    for shard_flat, repl_flat in ((gs, gr), (ds_, dr), (ms, mr), (vs, vr)):
        parts = {**_unpack_shard(shard_flat, shard_shapes), **_unpack_repl(repl_flat)}
        results.append([parts[k] for k in WEIGHT_ORDER])
    return (loss, grad_x, *results[0], *results[1], *results[2], *results[3])
```

```python
import math

import jax
import jax.numpy as jnp
from jax import lax
from jax.experimental import pallas as pl
from jax.experimental.pallas import tpu as pltpu

F32, BF16 = jnp.float32, jnp.bfloat16

N_DEV = 8
MESH_AXES = ("x", "y", "c")
EPS = 1e-6
D_MODEL = 1024
HEAD_DIM = 64
ATT_HEADS, KV_HEADS = 8, 2
QKV_HEADS = ATT_HEADS + 2 * KV_HEADS
QKV_W = QKV_HEADS * HEAD_DIM
GROUP = ATT_HEADS // KV_HEADS
ATT_W, KV_W, CONV_W, SG_W = 512, 128, 256, 256
CONV_K, CONV_PAD, HALO = 31, 15, 16
SG_HEADS, SG_CHUNK = 4, 128
D_IN = 2816
GRID_W = 64
ROPE_THETA = 10000.0
LOG2E, LN2 = math.log2(math.e), math.log(2.0)
Q_SCALE = HEAD_DIM ** -0.5 * LOG2E
LANES = 128

COL_GATT, COL_A, COL_B, COL_GCONV, COL_U, COL_VSG, COL_GSG = 3, 5, 6, 7, 8, 9, 10

ADAM_LR, ADAM_B1, ADAM_B2, ADAM_EPS, ADAM_WD, ADAM_STEP = 0.001, 0.9, 0.999, 1e-08, 0.01, 10

T_ROW = 512
T_PREP = 2048
T_GROUP = 512
BQ, BK = 512, 512
HEADS_PER_STEP = 2
FWD_HEADS_PER_STEP = 2
FWD_UNROLL, BWD_UNROLL = 8, 4
VMEM_MB = 56
ATTN_BWD_VMEM_MB = 58


def _pcall(body, *, name, grid, in_specs, out_specs, out_shape, scratch=(), sem=None, vmem_mb=None):
    params = {}
    if sem is not None:
        params["dimension_semantics"] = sem
    if vmem_mb is not None:
        params["vmem_limit_bytes"] = vmem_mb << 20
    return pl.pallas_call(body, name=name, grid=grid, in_specs=in_specs, out_specs=out_specs, out_shape=out_shape,
                          scratch_shapes=list(scratch), compiler_params=pltpu.CompilerParams(**params))


def _dot(a, b):
    return jnp.dot(a, b, preferred_element_type=F32)


def _sigmoid(x):
    return 1.0 / (1.0 + jnp.exp(-x))


def _silu_and_grad(x):
    s = _sigmoid(x)
    return x * s, s * (1.0 + x * (1.0 - s))


def _gelu_and_grad(x):
    cdf = 0.5 * (1.0 + lax.erf(x * (1.0 / math.sqrt(2.0))))
    pdf = jnp.exp(-0.5 * x * x) * (1.0 / math.sqrt(2.0 * math.pi))
    return x * cdf, cdf + x * pdf


def _split_dot(y, mat):
    hi = y.astype(BF16)
    lo = (y - hi.astype(F32)).astype(BF16)
    return _dot(hi, mat) + _dot(lo, mat)


def _row_tile(rows, cap):
    best = 8
    for t in range(8, min(rows, cap) + 1, 8):
        if rows % t == 0:
            best = t
    return best


def _exchange(scatter, gather, name):
    n_s = len(scatter)
    arrs = list(scatter) + list(gather)
    n = len(arrs)
    flips = [(fx, fy, fc) for fx in (0, 1) for fy in (0, 1) for fc in (0, 1)][1:]
    n_peer = len(flips)

    def body(*refs):
        ins, outs = refs[:n], refs[n:2 * n]
        send_sems, recv_sems, local_sems = refs[2 * n:]
        pos = tuple(lax.axis_index(a) for a in MESH_AXES)

        def peer(flip):
            return tuple((1 - p) if f else p for p, f in zip(pos, flip))

        def slot(p):
            return 4 * p[0] + 2 * p[1] + p[2]

        def src(a, p):
            return ins[a].at[slot(p)] if a < n_s else ins[a]

        def remote(a, k, src_ref, dst_slot, to):
            return pltpu.make_async_remote_copy(
                src_ref=src_ref, dst_ref=outs[a].at[dst_slot], send_sem=send_sems.at[a * n_peer + k],
                recv_sem=recv_sems.at[a * n_peer + k], device_id=to, device_id_type=pl.DeviceIdType.MESH)

        local = [pltpu.make_async_copy(src(a, pos), outs[a].at[slot(pos)], local_sems.at[a]) for a in range(n)]
        for cp in local:
            cp.start()
        sends = [remote(a, k, src(a, peer(f)), slot(pos), peer(f)) for a in range(n) for k, f in enumerate(flips)]
        for cp in sends:
            cp.start()
        for a in range(n):
            for k, f in enumerate(flips):
                remote(a, k, src(a, peer(f)), slot(peer(f)), peer(f)).wait_recv()
        for cp in sends:
            cp.wait_send()
        for cp in local:
            cp.wait()

    out_shape = [jax.ShapeDtypeStruct((N_DEV,) + (a.shape[1:] if i < n_s else a.shape), a.dtype)
                 for i, a in enumerate(arrs)]
    any_spec = pl.BlockSpec(memory_space=pl.ANY)
    return pl.pallas_call(
        body, name=name, out_shape=out_shape, in_specs=[any_spec] * n, out_specs=[any_spec] * n,
        scratch_shapes=[pltpu.SemaphoreType.DMA((n * n_peer,)), pltpu.SemaphoreType.DMA((n * n_peer,)),
                        pltpu.SemaphoreType.DMA((n,))],
    )(*arrs)


def _sum_adamw(slots, w, m, v, name):
    rows = w.shape[0]
    tr = _row_tile(rows, 1024)
    c1 = 1.0 - ADAM_B1 ** ADAM_STEP
    c2 = 1.0 - ADAM_B2 ** ADAM_STEP

    def body(s_ref, w_ref, m_ref, v_ref, g_out, d_out, m_out, v_out):
        g = s_ref[0].astype(F32)
        for d in range(1, N_DEV):
            g = g + s_ref[d].astype(F32)
        m_new = ADAM_B1 * m_ref[...] + (1.0 - ADAM_B1) * g
        v_new = ADAM_B2 * v_ref[...] + (1.0 - ADAM_B2) * (g * g)
        m_hat = m_new / c1
        v_hat = v_new / c2
        g_out[...] = g
        d_out[...] = -ADAM_LR * (m_hat / (jnp.sqrt(v_hat) + ADAM_EPS) + ADAM_WD * w_ref[...])
        m_out[...] = m_new
        v_out[...] = v_new

    flat = pl.BlockSpec((tr, LANES), lambda i: (i, 0))
    return _pcall(
        body, name=name, grid=(rows // tr,),
        in_specs=[pl.BlockSpec((N_DEV, tr, LANES), lambda i: (0, i, 0)), flat, flat, flat],
        out_specs=[flat] * 4, out_shape=[jax.ShapeDtypeStruct((rows, LANES), F32)] * 4,
        sem=("parallel",), vmem_mb=VMEM_MB)(slots, w, m, v)


def _proj_fwd(x, gain, w, name):
    seq = x.shape[0]

    def body(x_ref, g_ref, w_ref, proj_ref, hb_ref, qkv_t_ref):
        xf = x_ref[...]
        r = lax.rsqrt(jnp.mean(xf * xf, axis=-1, keepdims=True) + EPS)
        h = (xf * r * g_ref[...]).astype(BF16)
        hb_ref[...] = h
        proj = _dot(h, w_ref[...])
        proj_ref[...] = proj
        qkv_t_ref[...] = proj[:, :QKV_W].T

    return _pcall(
        body, name=name, grid=(seq // T_ROW,),
        in_specs=[pl.BlockSpec((T_ROW, D_MODEL), lambda i: (i, 0)), pl.BlockSpec((1, D_MODEL), lambda i: (0, 0)),
                  pl.BlockSpec((D_MODEL, D_IN), lambda i: (0, 0))],
        out_specs=[pl.BlockSpec((T_ROW, D_IN), lambda i: (i, 0)), pl.BlockSpec((T_ROW, D_MODEL), lambda i: (i, 0)),
                   pl.BlockSpec((QKV_W, T_ROW), lambda i: (0, i))],
        out_shape=[jax.ShapeDtypeStruct((seq, D_IN), F32), jax.ShapeDtypeStruct((seq, D_MODEL), BF16),
                   jax.ShapeDtypeStruct((QKV_W, seq), F32)],
        sem=("parallel",), vmem_mb=VMEM_MB)(x, gain, w)


def _proj_bwd(dproj, w_t, x, gain, dxo, name):
    seq = x.shape[0]

    def body(dp_ref, w_ref, x_ref, g_ref, dxo_ref, dx_ref, dg_ref):
        dh = _dot(dp_ref[...], w_ref[...])
        xf = x_ref[...]
        r = lax.rsqrt(jnp.mean(xf * xf, axis=-1, keepdims=True) + EPS)
        n = xf * r
        dn = dh * g_ref[...]
        dx_ref[...] = dxo_ref[...] + r * (dn - n * jnp.mean(dn * n, axis=-1, keepdims=True))

        @pl.when(pl.program_id(0) == 0)
        def _():
            dg_ref[...] = jnp.zeros_like(dg_ref)

        dg_ref[...] += jnp.sum(dh * n, axis=0, keepdims=True)

    row = pl.BlockSpec((T_ROW, D_MODEL), lambda i: (i, 0))
    vec = pl.BlockSpec((1, D_MODEL), lambda i: (0, 0))
    return _pcall(
        body, name=name, grid=(seq // T_ROW,),
        in_specs=[pl.BlockSpec((T_ROW, D_IN), lambda i: (i, 0)), pl.BlockSpec((D_IN, D_MODEL), lambda i: (0, 0)),
                  row, vec, row],
        out_specs=[row, vec],
        out_shape=[jax.ShapeDtypeStruct((seq, D_MODEL), F32), jax.ShapeDtypeStruct((1, D_MODEL), F32)],
        sem=("arbitrary",), vmem_mb=VMEM_MB)(dproj, w_t, x, gain, dxo)


def _matmul_acc(a, b, tn, name):
    seq, m = a.shape
    n = b.shape[1]
    ts = min(1024, seq)

    def body(a_ref, b_ref, o_ref):
        @pl.when(pl.program_id(1) == 0)
        def _():
            o_ref[...] = jnp.zeros_like(o_ref)

        o_ref[...] += lax.dot_general(a_ref[...], b_ref[...], (((0,), (0,)), ((), ())), preferred_element_type=F32)

    return _pcall(
        body, name=name, grid=(n // tn, seq // ts),
        in_specs=[pl.BlockSpec((ts, m), lambda j, k: (k, 0)), pl.BlockSpec((ts, tn), lambda j, k: (k, j))],
        out_specs=pl.BlockSpec((m, tn), lambda j, k: (0, j)), out_shape=jax.ShapeDtypeStruct((m, n), F32),
        sem=("parallel", "arbitrary"), vmem_mb=VMEM_MB)(a, b)


ROPE_HALF = HEAD_DIM // 4


def _rope_tables(seq):
    t = jnp.arange(seq, dtype=jnp.int32)
    row = (t // GRID_W).astype(F32)
    col = (t % GRID_W).astype(F32)
    inv_freq = ROPE_THETA ** (-jnp.arange(ROPE_HALF, dtype=F32) / ROPE_HALF)
    ang_r = row[:, None] * inv_freq[None, :]
    ang_c = col[:, None] * inv_freq[None, :]
    cos = jnp.concatenate([jnp.cos(ang_r), jnp.cos(ang_r), jnp.cos(ang_c), jnp.cos(ang_c)], axis=-1)
    sin = jnp.concatenate([-jnp.sin(ang_r), jnp.sin(ang_r), -jnp.sin(ang_c), jnp.sin(ang_c)], axis=-1)
    return cos.T, sin.T


def _rope_partner(y):
    h = ROPE_HALF
    return jnp.concatenate([y[h:2 * h], y[0:h], y[3 * h:4 * h], y[2 * h:3 * h]], axis=0)


def _qk_specs(seq, head0):
    t = min(T_PREP, seq)
    src = pl.BlockSpec((None, HEAD_DIM, t), lambda i, h: (h + head0, 0, i))
    own = pl.BlockSpec((None, HEAD_DIM, t), lambda i, h: (h, 0, i))
    nat = pl.BlockSpec((None, t, HEAD_DIM), lambda i, h: (h, i, 0))
    col = pl.BlockSpec((HEAD_DIM, 1), lambda i, h: (0, 0))
    tab = pl.BlockSpec((HEAD_DIM, t), lambda i, h: (0, i))
    return t, src, own, nat, col, tab


def _qk_prep_fwd(qkv_t, head0, heads, gain, scale, cos, sin, name):
    seq = qkv_t.shape[2]
    t, src, own, nat, col, tab = _qk_specs(seq, head0)

    def body(x_ref, g_ref, c_ref, s_ref, ot_ref, on_ref):
        xf = x_ref[...]
        r = lax.rsqrt(jnp.mean(xf * xf, axis=0, keepdims=True) + EPS)
        y = xf * r * g_ref[...]
        z = (y * c_ref[...] + _rope_partner(y) * s_ref[...]) * scale
        ot_ref[...] = z.astype(BF16)
        on_ref[...] = z.T.astype(BF16)

    return _pcall(body, name=name, grid=(seq // t, heads), in_specs=[src, col, tab, tab], out_specs=[own, nat],
                  out_shape=[jax.ShapeDtypeStruct((heads, HEAD_DIM, seq), BF16),
                             jax.ShapeDtypeStruct((heads, seq, HEAD_DIM), BF16)],
                  sem=("parallel", "parallel"))(qkv_t, gain, cos, sin)


def _qk_prep_bwd(qkv_t, head0, dout, dout_is_t, gain, scale, cos, sin, name):
    heads = dout.shape[0]
    seq = qkv_t.shape[2]
    t, src, own, nat, col, tab = _qk_specs(seq, head0)

    def body(x_ref, d_ref, g_ref, c_ref, s_ref, dx_ref, dg_ref):
        xf = x_ref[...]
        r = lax.rsqrt(jnp.mean(xf * xf, axis=0, keepdims=True) + EPS)
        n = xf * r
        d = d_ref[...] if dout_is_t else d_ref[...].T
        dz = d * scale
        dy = dz * c_ref[...] + _rope_partner(dz * s_ref[...])
        dn = dy * g_ref[...]
        dx_ref[...] = r * (dn - n * jnp.mean(dn * n, axis=0, keepdims=True))

        @pl.when((pl.program_id(0) == 0) & (pl.program_id(1) == 0))
        def _():
            dg_ref[...] = jnp.zeros_like(dg_ref)

        dg_ref[pl.program_id(1)] += jnp.sum(dy * n, axis=1, keepdims=True)

    return _pcall(body, name=name, grid=(seq // t, heads),
                  in_specs=[src, own if dout_is_t else nat, col, tab, tab],
                  out_specs=[own, pl.BlockSpec((heads, HEAD_DIM, 1), lambda i, h: (0, 0, 0))],
                  out_shape=[jax.ShapeDtypeStruct((heads, HEAD_DIM, seq), F32),
                             jax.ShapeDtypeStruct((heads, HEAD_DIM, 1), F32)],
                  sem=("arbitrary", "arbitrary"))(qkv_t, dout, gain, cos, sin)


V_ROWS = HEAD_DIM + 8


def _unroll(nk, cap):
    u = 1
    while u * 2 <= cap and nk % (u * 2) == 0:
        u *= 2
    return u


def _attn_fwd(qs_t, k, v_t, name):
    seq = qs_t.shape[2]
    nk, bk = k.shape[1], k.shape[2]
    bq = min(BQ, seq)
    unroll = _unroll(nk, FWD_UNROLL)
    hps = FWD_HEADS_PER_STEP
    heads = range(hps)

    def body(qt_ref, k_ref, vt_ref, ot_ref, lse_ref, s_scr):
        q_t = [qt_ref[h] for h in heads]

        def scores(j, slot):
            kj = k_ref[j]
            top = []
            for h in heads:
                s = _dot(kj, q_t[h])
                s_scr[slot, h] = s
                top.append(jnp.max(s, axis=0, keepdims=True))
            return tuple(top)

        def accumulate(j, slot, state, top):
            vtj = vt_ref[j]
            out = []
            for h in heads:
                m, acc = state[h]
                m_new = jnp.maximum(m, top[h])
                p = jnp.exp2(s_scr[slot, h] - m_new).astype(BF16)
                out.append((m_new, jnp.exp2(m - m_new) * acc + _dot(vtj, p)))
            return tuple(out)

        def step(t, carry):
            state, top = carry
            for u in range(unroll):
                nxt = unroll * t + u + 1
                top_next = scores(jnp.minimum(nxt, nk - 1) if u == unroll - 1 else nxt, (u + 1) % 2)
                state = accumulate(unroll * t + u, u % 2, state, top)
                top = top_next
            return state, top

        init = tuple((jnp.full((1, bq), -jnp.inf, F32), jnp.zeros((V_ROWS, bq), F32)) for _ in heads)
        state, _ = lax.fori_loop(0, nk // unroll, step, (init, scores(0, 0)))
        for h in heads:
            m, acc = state[h]
            l = acc[HEAD_DIM:HEAD_DIM + 1, :]
            ot_ref[h] = acc[:HEAD_DIM, :] / l
            lse_ref[h] = m + jnp.log2(l)

    kv_of = lambda g: g * hps // GROUP
    return _pcall(
        body, name=name, grid=(ATT_HEADS // hps, seq // bq),
        in_specs=[pl.BlockSpec((hps, HEAD_DIM, bq), lambda g, i: (g, 0, i)),
                  pl.BlockSpec((None, nk, bk, HEAD_DIM), lambda g, i: (kv_of(g), 0, 0, 0)),
                  pl.BlockSpec((None, nk, V_ROWS, bk), lambda g, i: (kv_of(g), 0, 0, 0))],
        out_specs=[pl.BlockSpec((hps, HEAD_DIM, bq), lambda g, i: (g, 0, i)),
                   pl.BlockSpec((hps, 1, bq), lambda g, i: (g, 0, i))],
        out_shape=[jax.ShapeDtypeStruct((ATT_HEADS, HEAD_DIM, seq), F32),
                   jax.ShapeDtypeStruct((ATT_HEADS, 1, seq), F32)],
        scratch=[pltpu.VMEM((2, hps, bk, bq), F32)],
        sem=("parallel", "parallel"), vmem_mb=VMEM_MB)(qs_t, k, v_t)


def _attn_bwd(qs, qs_t, do_t, o_t, lse, k, k_t, v_t, name):
    seq = qs.shape[1]
    nk, bk = k.shape[1], k.shape[2]
    bq = min(BQ, seq)
    nq = seq // bq

    unroll = _unroll(nk, BWD_UNROLL)
    heads = range(HEADS_PER_STEP)
    pairs = GROUP // HEADS_PER_STEP

    def body(q_ref, qt_ref, dot_ref, ot_ref, lse_ref, k_ref, kt_ref, vt_ref, dq_ref, dkt_ref, dvt_ref,
             s_scr, dp_scr):
        @pl.when((pl.program_id(1) == 0) & (pl.program_id(2) == 0))
        def _():
            dkt_ref[...] = jnp.zeros_like(dkt_ref)
            dvt_ref[...] = jnp.zeros_like(dvt_ref)

        q, q_t = [q_ref[h] for h in heads], [qt_ref[h] for h in heads]
        do_t_b = [dot_ref[h].astype(BF16) for h in heads]
        do_l = [dot_ref[h].T * LN2 for h in heads]
        do_b = [d.astype(BF16) for d in do_l]
        delta = [jnp.sum(do_l[h] * ot_ref[h].T, axis=-1, keepdims=True) for h in heads]
        lse_col = [jnp.max(jnp.broadcast_to(lse_ref[h], (LANES, bq)).T, axis=-1, keepdims=True) for h in heads]

        def products(j, slot):
            ktj, vtj = kt_ref[j], vt_ref[j]
            for h in heads:
                s_scr[slot, h] = _dot(q[h], ktj)
                dp_scr[slot, h] = _dot(do_b[h], vtj).astype(BF16)

        def gradients(j, slot, dq):
            kj = k_ref[j]
            dvt = jnp.zeros((HEAD_DIM, bk), F32)
            dkt = jnp.zeros((HEAD_DIM, bk), F32)
            new = []
            for h in heads:
                p = jnp.exp2(s_scr[slot, h] - lse_col[h])
                ds = (p * (dp_scr[slot, h].astype(F32) - delta[h])).astype(BF16)
                dvt = dvt + _dot(do_t_b[h], p.astype(BF16))
                dkt = dkt + _dot(q_t[h], ds)
                new.append(dq[h] + _dot(ds, kj))
            dvt_ref[j] += dvt
            dkt_ref[j] += dkt
            return tuple(new)

        def step(t, dq):
            for u in range(unroll):
                nxt = unroll * t + u + 1
                products(jnp.minimum(nxt, nk - 1) if u == unroll - 1 else nxt, (u + 1) % 2)
                dq = gradients(unroll * t + u, u % 2, dq)
            return dq

        products(0, 0)
        res = lax.fori_loop(0, nk // unroll, step, tuple(jnp.zeros((bq, HEAD_DIM), F32) for _ in heads))
        for h in heads:
            dq_ref[h] = res[h]

    first = lambda g, hh: g * pairs + hh
    row = pl.BlockSpec((HEADS_PER_STEP, bq, HEAD_DIM), lambda g, hh, i: (first(g, hh), i, 0))
    col = pl.BlockSpec((HEADS_PER_STEP, HEAD_DIM, bq), lambda g, hh, i: (first(g, hh), 0, i))
    kv_rows = pl.BlockSpec((None, nk, bk, HEAD_DIM), lambda g, hh, i: (g, 0, 0, 0))
    kv_cols = pl.BlockSpec((None, nk, HEAD_DIM, bk), lambda g, hh, i: (g, 0, 0, 0))
    return _pcall(
        body, name=name, grid=(KV_HEADS, pairs, nq),
        in_specs=[row, col, col, col,
                  pl.BlockSpec((HEADS_PER_STEP, 1, bq), lambda g, hh, i: (first(g, hh), 0, i)),
                  kv_rows, kv_cols, kv_cols],
        out_specs=[row, kv_cols, kv_cols],
        out_shape=[jax.ShapeDtypeStruct((ATT_HEADS, seq, HEAD_DIM), F32),
                   jax.ShapeDtypeStruct((KV_HEADS, nk, HEAD_DIM, bk), F32),
                   jax.ShapeDtypeStruct((KV_HEADS, nk, HEAD_DIM, bk), F32)],
        scratch=[pltpu.VMEM((2, HEADS_PER_STEP, bq, bk), F32), pltpu.VMEM((2, HEADS_PER_STEP, bq, bk), BF16)],
        sem=("parallel", "arbitrary", "arbitrary"), vmem_mb=ATTN_BWD_VMEM_MB)(
            qs, qs_t, do_t, o_t, lse, k, k_t, v_t)


def _halo_specs(t, col, n_tiles):
    per = t // HALO
    last = n_tiles * per - 1
    before = pl.BlockSpec((HALO, CONV_W), lambda i: (jnp.maximum(i * per - 1, 0), col))
    after = pl.BlockSpec((HALO, CONV_W), lambda i: (jnp.minimum((i + 1) * per, last), col))
    return before, after


def _glu(a, b):
    return a * _sigmoid(b)


SUBLANES = 8
SHIFT_ROWS = 2 * HALO - SUBLANES


def _shifted(sh_ref, off, t):
    return sh_ref[off % SUBLANES, pl.ds(off - off % SUBLANES, t), :]


def _conv_taps(sh_ref, w_ref, t, flip):
    acc = jnp.zeros((t, CONV_W), F32)
    for k in range(CONV_K):
        off = (HALO + CONV_PAD - k) if flip else (HALO - CONV_PAD + k)
        acc = acc + w_ref[k:k + 1, :] * _shifted(sh_ref, off, t)
    return acc


def _fill_ext(ext_ref, sh_ref, before, tile, after, t, i, n_tiles):
    ext_ref[pl.ds(0, HALO), :] = jnp.where(i > 0, before, 0.0)
    ext_ref[pl.ds(HALO, t), :] = tile
    ext_ref[pl.ds(HALO + t, HALO), :] = jnp.where(i < n_tiles - 1, after, 0.0)
    for b in range(SUBLANES):
        sh_ref[b] = ext_ref[pl.ds(b, t + SHIFT_ROWS), :]


def _conv_scratch(t):
    return [pltpu.VMEM((t + 2 * HALO, CONV_W), F32), pltpu.VMEM((SUBLANES, t + SHIFT_ROWS, CONV_W), F32)]


def _conv_fwd(proj, w, bias, ln_g, ln_b, name):
    seq = proj.shape[0]
    t = min(T_GROUP, seq)
    n_tiles = seq // t

    def body(a_ref, b_ref, ap_ref, bp_ref, an_ref, bn_ref, gate_ref, w_ref, bias_ref, g_ref, beta_ref, o_ref, y_ref,
             ext_ref, sh_ref):
        i = pl.program_id(0)
        _fill_ext(ext_ref, sh_ref, _glu(ap_ref[...], bp_ref[...]), _glu(a_ref[...], b_ref[...]),
                  _glu(an_ref[...], bn_ref[...]), t, i, n_tiles)
        y = _conv_taps(sh_ref, w_ref, t, False) + bias_ref[...]
        y_ref[...] = y
        mu = jnp.mean(y, axis=-1, keepdims=True)
        yc = y - mu
        rs = lax.rsqrt(jnp.mean(yc * yc, axis=-1, keepdims=True) + EPS)
        z = yc * rs * g_ref[...] + beta_ref[...]
        o_ref[...] = _silu_and_grad(z)[0] * _silu_and_grad(gate_ref[...])[0]

    tile = lambda c: pl.BlockSpec((t, CONV_W), lambda i: (i, c))
    ab, aa = _halo_specs(t, COL_A, n_tiles)
    bb, ba = _halo_specs(t, COL_B, n_tiles)
    vec = pl.BlockSpec((1, CONV_W), lambda i: (0, 0))
    return _pcall(
        body, name=name, grid=(n_tiles,),
        in_specs=[tile(COL_A), tile(COL_B), ab, bb, aa, ba, tile(COL_GCONV),
                  pl.BlockSpec((CONV_K, CONV_W), lambda i: (0, 0)), vec, vec, vec],
        out_specs=[pl.BlockSpec((t, CONV_W), lambda i: (i, 0))] * 2,
        out_shape=[jax.ShapeDtypeStruct((seq, CONV_W), F32)] * 2,
        scratch=_conv_scratch(t), sem=("parallel",))(
            proj, proj, proj, proj, proj, proj, proj, w, bias, ln_g, ln_b)


def _conv_bwd_a(proj, y_conv, dcnv, ln_g, ln_b, name):
    seq = proj.shape[0]
    t = min(T_GROUP, seq)
    n_tiles = seq // t

    def body(a_ref, b_ref, ap_ref, bp_ref, an_ref, bn_ref, gate_ref, y_ref, d_ref, g_ref, beta_ref,
             dy_ref, dgate_ref, dw_ref, dbias_ref, dg_ref, dbeta_ref, ext_ref, sh_ref):
        i = pl.program_id(0)
        _fill_ext(ext_ref, sh_ref, _glu(ap_ref[...], bp_ref[...]), _glu(a_ref[...], b_ref[...]),
                  _glu(an_ref[...], bn_ref[...]), t, i, n_tiles)
        y = y_ref[...]
        mu = jnp.mean(y, axis=-1, keepdims=True)
        yc = y - mu
        rs = lax.rsqrt(jnp.mean(yc * yc, axis=-1, keepdims=True) + EPS)
        n = yc * rs
        z = n * g_ref[...] + beta_ref[...]
        act, dact = _silu_and_grad(z)
        gate, dgate = _silu_and_grad(gate_ref[...])
        d = d_ref[...]
        dgate_ref[...] = d * act * dgate
        dz = d * gate * dact
        dn = dz * g_ref[...]
        dy = rs * (dn - jnp.mean(dn, axis=-1, keepdims=True) - n * jnp.mean(dn * n, axis=-1, keepdims=True))
        dy_ref[...] = dy

        @pl.when(i == 0)
        def _():
            dw_ref[...] = jnp.zeros_like(dw_ref)
            dbias_ref[...] = jnp.zeros_like(dbias_ref)
            dg_ref[...] = jnp.zeros_like(dg_ref)
            dbeta_ref[...] = jnp.zeros_like(dbeta_ref)

        dg_ref[...] += jnp.sum(dz * n, axis=0, keepdims=True)
        dbeta_ref[...] += jnp.sum(dz, axis=0, keepdims=True)
        dbias_ref[...] += jnp.sum(dy, axis=0, keepdims=True)
        for k in range(CONV_K):
            dw_ref[k:k + 1, :] += jnp.sum(dy * _shifted(sh_ref, HALO - CONV_PAD + k, t), axis=0, keepdims=True)

    tile = lambda c: pl.BlockSpec((t, CONV_W), lambda i: (i, c))
    own = pl.BlockSpec((t, CONV_W), lambda i: (i, 0))
    ab, aa = _halo_specs(t, COL_A, n_tiles)
    bb, ba = _halo_specs(t, COL_B, n_tiles)
    vec = pl.BlockSpec((1, CONV_W), lambda i: (0, 0))
    taps = pl.BlockSpec((CONV_K, CONV_W), lambda i: (0, 0))
    vshape = jax.ShapeDtypeStruct((1, CONV_W), F32)
    return _pcall(
        body, name=name, grid=(n_tiles,),
        in_specs=[tile(COL_A), tile(COL_B), ab, bb, aa, ba, tile(COL_GCONV), own, own, vec, vec],
        out_specs=[own, own, taps, vec, vec, vec],
        out_shape=[jax.ShapeDtypeStruct((seq, CONV_W), F32), jax.ShapeDtypeStruct((seq, CONV_W), F32),
                   jax.ShapeDtypeStruct((CONV_K, CONV_W), F32), vshape, vshape, vshape],
        scratch=_conv_scratch(t), sem=("arbitrary",))(
            proj, proj, proj, proj, proj, proj, proj, y_conv, dcnv, ln_g, ln_b)


def _conv_bwd_b(proj, dy, w, name):
    seq = proj.shape[0]
    t = min(T_GROUP, seq)
    n_tiles = seq // t

    def body(a_ref, b_ref, dy_ref, dyp_ref, dyn_ref, w_ref, da_ref, db_ref, ext_ref, sh_ref):
        i = pl.program_id(0)
        _fill_ext(ext_ref, sh_ref, dyp_ref[...], dy_ref[...], dyn_ref[...], t, i, n_tiles)
        dh = _conv_taps(sh_ref, w_ref, t, True)
        sig = _sigmoid(b_ref[...])
        da_ref[...] = dh * sig
        db_ref[...] = dh * a_ref[...] * sig * (1.0 - sig)

    tile = lambda c: pl.BlockSpec((t, CONV_W), lambda i: (i, c))
    own = pl.BlockSpec((t, CONV_W), lambda i: (i, 0))
    before, after = _halo_specs(t, 0, n_tiles)
    return _pcall(
        body, name=name, grid=(n_tiles,),
        in_specs=[tile(COL_A), tile(COL_B), own, before, after, pl.BlockSpec((CONV_K, CONV_W), lambda i: (0, 0))],
        out_specs=[own, own], out_shape=[jax.ShapeDtypeStruct((seq, CONV_W), F32)] * 2,
        scratch=_conv_scratch(t), sem=("parallel",))(proj, proj, dy, dy, dy, w)


def _head_masks():
    lane_head = lax.broadcasted_iota(jnp.int32, (SG_CHUNK, SG_W), 1) // HEAD_DIM
    return [lane_head == h for h in range(SG_HEADS)]


def _sg_mix(mats_ref, rhs, masks):
    out = jnp.zeros((SG_CHUNK, SG_W), F32)
    for h in range(SG_HEADS):
        out = out + jnp.where(masks[h], _dot(mats_ref[h], rhs), 0.0)
    return out


def _sg_specs(seq):
    t = min(T_GROUP, seq)
    tile = lambda c: pl.BlockSpec((t, SG_W), lambda i: (i, c))
    own = pl.BlockSpec((t, SG_W), lambda i: (i, 0))
    vec = pl.BlockSpec((1, SG_W), lambda i: (0, 0))
    mats = pl.BlockSpec((SG_HEADS, SG_CHUNK, SG_CHUNK), lambda i: (0, 0, 0))
    full = pl.BlockSpec((SG_CHUNK, SG_W), lambda i: (0, 0))
    return t, tile, own, vec, mats, full


def _sg_fwd(proj, ln_g, ln_b, w_b, bias_full, name):
    seq = proj.shape[0]
    t, tile, own, vec, mats, full = _sg_specs(seq)

    def body(u_ref, v_ref, gate_ref, g_ref, beta_ref, w_ref, bias_ref, o_ref):
        masks = _head_masks()
        for c in range(t // SG_CHUNK):
            rows = pl.ds(c * SG_CHUNK, SG_CHUNK)
            vg = _gelu_and_grad(v_ref[rows, :])[0]
            mu = jnp.mean(vg, axis=-1, keepdims=True)
            vc = vg - mu
            rs = lax.rsqrt(jnp.mean(vc * vc, axis=-1, keepdims=True) + EPS)
            vln = vc * rs * g_ref[...] + beta_ref[...]
            mixed = _sg_mix(w_ref, vln.astype(BF16), masks) + bias_ref[...]
            o_ref[rows, :] = _gelu_and_grad(u_ref[rows, :])[0] * mixed * _silu_and_grad(gate_ref[rows, :])[0]

    return _pcall(body, name=name, grid=(seq // t,),
                  in_specs=[tile(COL_U), tile(COL_VSG), tile(COL_GSG), vec, vec, mats, full], out_specs=own,
                  out_shape=jax.ShapeDtypeStruct((seq, SG_W), F32), sem=("parallel",))(
                      proj, proj, proj, ln_g, ln_b, w_b, bias_full)


def _sg_bwd(proj, dsg, ln_g, ln_b, w_b, w_t_b, bias_full, fold, name):
    seq = proj.shape[0]
    t, tile, own, vec, mats, full = _sg_specs(seq)
    n_tiles = seq // t

    def body(u_ref, v_ref, gate_ref, d_ref, g_ref, beta_ref, w_ref, wt_ref, bias_ref, fold_ref,
             du_ref, dv_ref, dgate_ref, dg_ref, dbeta_ref, dw_ref, db_ref, dbias_acc):
        i = pl.program_id(0)

        @pl.when(i == 0)
        def _():
            dg_ref[...] = jnp.zeros_like(dg_ref)
            dbeta_ref[...] = jnp.zeros_like(dbeta_ref)
            dw_ref[...] = jnp.zeros_like(dw_ref)
            dbias_acc[...] = jnp.zeros_like(dbias_acc)

        masks = _head_masks()
        for c in range(t // SG_CHUNK):
            rows = pl.ds(c * SG_CHUNK, SG_CHUNK)
            ug, dug = _gelu_and_grad(u_ref[rows, :])
            vg, dvg = _gelu_and_grad(v_ref[rows, :])
            mu = jnp.mean(vg, axis=-1, keepdims=True)
            vc = vg - mu
            rs = lax.rsqrt(jnp.mean(vc * vc, axis=-1, keepdims=True) + EPS)
            vn = vc * rs
            vln_b = (vn * g_ref[...] + beta_ref[...]).astype(BF16)
            mixed = _sg_mix(w_ref, vln_b, masks) + bias_ref[...]
            gate, dgate = _silu_and_grad(gate_ref[rows, :])
            d = d_ref[rows, :]
            dgate_ref[rows, :] = d * ug * mixed * dgate
            du_ref[rows, :] = d * mixed * gate * dug
            dmixed = d * ug * gate
            dbias_acc[...] += dmixed
            dmixed_b = dmixed.astype(BF16)
            for h in range(SG_HEADS):
                dm_h = jnp.where(masks[h], dmixed_b, jnp.zeros_like(dmixed_b))
                dw_ref[h] += lax.dot_general(dm_h, vln_b, (((1,), (1,)), ((), ())), preferred_element_type=F32)
            dvln = _sg_mix(wt_ref, dmixed_b, masks)
            dg_ref[...] += jnp.sum(dvln * vn, axis=0, keepdims=True)
            dbeta_ref[...] += jnp.sum(dvln, axis=0, keepdims=True)
            dvn = dvln * g_ref[...]
            dvgelu = rs * (dvn - jnp.mean(dvn, axis=-1, keepdims=True) - vn * jnp.mean(dvn * vn, axis=-1, keepdims=True))
            dv_ref[rows, :] = dvgelu * dvg

        @pl.when(i == n_tiles - 1)
        def _():
            db_ref[...] = _split_dot(dbias_acc[...], fold_ref[...])

    sq = pl.BlockSpec((SG_CHUNK, SG_CHUNK), lambda i: (0, 0))
    vshape = jax.ShapeDtypeStruct((1, SG_W), F32)
    return _pcall(
        body, name=name, grid=(n_tiles,),
        in_specs=[tile(COL_U), tile(COL_VSG), tile(COL_GSG), own, vec, vec, mats, mats, full,
                  pl.BlockSpec((SG_W, SG_CHUNK), lambda i: (0, 0))],
        out_specs=[own, own, own, vec, vec, mats, sq],
        out_shape=[jax.ShapeDtypeStruct((seq, SG_W), F32)] * 3 + [
            vshape, vshape, jax.ShapeDtypeStruct((SG_HEADS, SG_CHUNK, SG_CHUNK), F32),
            jax.ShapeDtypeStruct((SG_CHUNK, SG_CHUNK), F32)],
        scratch=[pltpu.VMEM((SG_CHUNK, SG_W), F32)], sem=("arbitrary",))(
            proj, proj, proj, dsg, ln_g, ln_b, w_b, w_t_b, bias_full, fold)


def _out_fwd(att_t, proj, cnv, sgu, x, w, gain, name, target=None):
    seq = x.shape[0]
    with_loss = target is not None

    def body(att_ref, g0_ref, g1_ref, cnv_ref, sgu_ref, x_ref, w_ref, gain_ref, *rest):
        mix_ref, cat_ref = rest[-2:]
        gate = jnp.concatenate([_silu_and_grad(g0_ref[...])[0], _silu_and_grad(g1_ref[...])[0]], axis=-1)
        cat_ref[:, 0:ATT_W] = (att_ref[...].T * gate).astype(BF16)
        cat_ref[:, ATT_W:ATT_W + CONV_W] = cnv_ref[...].astype(BF16)
        cat_ref[:, ATT_W + CONV_W:] = sgu_ref[...].astype(BF16)
        mix = _dot(cat_ref[...], w_ref[...])
        mix_ref[...] = mix
        r = lax.rsqrt(jnp.mean(mix * mix, axis=-1, keepdims=True) + EPS)
        x_new = x_ref[...] + mix * r * gain_ref[...]
        if not with_loss:
            rest[0][...] = x_new
            return
        t_ref, sse_ref, dy_ref = rest[:3]
        err = x_new - t_ref[...]
        dy_ref[...] = err * (1.0 / D_MODEL)

        @pl.when(pl.program_id(0) == 0)
        def _():
            sse_ref[...] = jnp.zeros_like(sse_ref)

        part = jnp.sum(jnp.sum(err * err, axis=0, keepdims=True), axis=-1, keepdims=True)
        sse_ref[...] += jnp.broadcast_to(part, (1, LANES))

    row = lambda w_: pl.BlockSpec((T_ROW, w_), lambda i: (i, 0))
    gate_blk = lambda c: pl.BlockSpec((T_ROW, 256), lambda i: (i, c))
    in_specs = [pl.BlockSpec((ATT_W, T_ROW), lambda i: (0, i)), gate_blk(COL_GATT), gate_blk(COL_GATT + 1),
                row(CONV_W), row(SG_W), row(D_MODEL),
                pl.BlockSpec((D_MODEL, D_MODEL), lambda i: (0, 0)), pl.BlockSpec((1, D_MODEL), lambda i: (0, 0))]
    tail_specs = [row(D_MODEL), row(D_MODEL)]
    tail_shape = [jax.ShapeDtypeStruct((seq, D_MODEL), F32), jax.ShapeDtypeStruct((seq, D_MODEL), BF16)]
    full = jax.ShapeDtypeStruct((seq, D_MODEL), F32)
    if with_loss:
        return _pcall(
            body, name=name, grid=(seq // T_ROW,), in_specs=in_specs + [row(D_MODEL)],
            out_specs=[pl.BlockSpec((1, LANES), lambda i: (0, 0)), row(D_MODEL)] + tail_specs,
            out_shape=[jax.ShapeDtypeStruct((1, LANES), F32), full] + tail_shape,
            sem=("arbitrary",), vmem_mb=VMEM_MB)(att_t, proj, proj, cnv, sgu, x, w, gain, target)
    return _pcall(
        body, name=name, grid=(seq // T_ROW,), in_specs=in_specs, out_specs=[row(D_MODEL)] + tail_specs,
        out_shape=[full] + tail_shape, sem=("parallel",), vmem_mb=VMEM_MB)(att_t, proj, proj, cnv, sgu, x, w, gain)


def _out_bwd(dxo, mix, gain, w_t, att_t, proj, name):
    seq = dxo.shape[0]

    def body(dxo_ref, mix_ref, gain_ref, w_ref, att_ref, g0_ref, g1_ref,
             dmix_ref, datt_ref, dgatt_ref, dcnv_ref, dsgu_ref, dgain_ref):
        mix = mix_ref[...]
        r = lax.rsqrt(jnp.mean(mix * mix, axis=-1, keepdims=True) + EPS)
        n = mix * r
        dout = dxo_ref[...]
        dn = dout * gain_ref[...]
        dmix = (r * (dn - n * jnp.mean(dn * n, axis=-1, keepdims=True))).astype(BF16)
        dmix_ref[...] = dmix

        @pl.when(pl.program_id(0) == 0)
        def _():
            dgain_ref[...] = jnp.zeros_like(dgain_ref)

        dgain_ref[...] += jnp.sum(dout * n, axis=0, keepdims=True)
        dcat = _dot(dmix, w_ref[...])
        g0, dg0 = _silu_and_grad(g0_ref[...])
        g1, dg1 = _silu_and_grad(g1_ref[...])
        gate = jnp.concatenate([g0, g1], axis=-1)
        dgate = jnp.concatenate([dg0, dg1], axis=-1)
        dca = dcat[:, 0:ATT_W]
        datt_ref[...] = (dca * gate).T
        dgatt_ref[...] = dca * att_ref[...].T * dgate
        dcnv_ref[...] = dcat[:, ATT_W:ATT_W + CONV_W]
        dsgu_ref[...] = dcat[:, ATT_W + CONV_W:]

    row = lambda w_: pl.BlockSpec((T_ROW, w_), lambda i: (i, 0))
    gate_blk = lambda c: pl.BlockSpec((T_ROW, 256), lambda i: (i, c))
    vec = pl.BlockSpec((1, D_MODEL), lambda i: (0, 0))
    heads_t = pl.BlockSpec((ATT_W, T_ROW), lambda i: (0, i))
    return _pcall(
        body, name=name, grid=(seq // T_ROW,),
        in_specs=[row(D_MODEL), row(D_MODEL), vec, pl.BlockSpec((D_MODEL, D_MODEL), lambda i: (0, 0)), heads_t,
                  gate_blk(COL_GATT), gate_blk(COL_GATT + 1)],
        out_specs=[row(D_MODEL), heads_t, row(ATT_W), row(CONV_W), row(SG_W), vec],
        out_shape=[jax.ShapeDtypeStruct((seq, D_MODEL), BF16), jax.ShapeDtypeStruct((ATT_W, seq), F32),
                   jax.ShapeDtypeStruct((seq, ATT_W), F32), jax.ShapeDtypeStruct((seq, CONV_W), F32),
                   jax.ShapeDtypeStruct((seq, SG_W), F32), jax.ShapeDtypeStruct((1, D_MODEL), F32)],
        sem=("arbitrary",), vmem_mb=VMEM_MB)(dxo, mix, gain, w_t, att_t, proj, proj)


def _row_blocks(a, bk):
    return a.reshape(a.shape[0], a.shape[1] // bk, bk, HEAD_DIM)


def _lane_blocks(a, bk):
    return a.reshape(a.shape[0], HEAD_DIM, a.shape[2] // bk, bk).transpose(0, 2, 1, 3)


def _from_lane_blocks(a):
    return a.transpose(0, 2, 1, 3).reshape(a.shape[0], HEAD_DIM, a.shape[1] * a.shape[3])


def _flat_rows(a, rows):
    flat = a.reshape(-1)
    return jnp.pad(flat, (0, rows * LANES - flat.shape[0])).reshape(rows, LANES)


SHARD_ROWS = {"w_in": 2 * D_MODEL * (D_IN // N_DEV) // LANES, "w_out": 2 * (D_MODEL // N_DEV) * D_MODEL // LANES,
              "conv_dw": 16}
REPL_SHAPES = [("pre_norm", (2, D_MODEL)), ("post_norm", (2, D_MODEL)), ("q_norm", (2, HEAD_DIM)),
               ("k_norm", (2, HEAD_DIM)), ("conv_dw_b", (2, CONV_W)), ("conv_ln_g", (2, CONV_W)),
               ("conv_ln_b", (2, CONV_W)), ("sg_ln_g", (2, SG_W)), ("sg_ln_b", (2, SG_W)),
               ("sg_w", (2, SG_HEADS, SG_CHUNK, SG_CHUNK)), ("sg_b", (2, SG_HEADS, SG_CHUNK))]
REPL_ROWS = 1088
WEIGHT_ORDER = ["pre_norm", "post_norm", "w_in", "w_out", "q_norm", "k_norm", "conv_dw", "conv_dw_b", "conv_ln_g",
                "conv_ln_b", "sg_ln_g", "sg_ln_b", "sg_w", "sg_b"]


def _pack_shard(parts):
    return jnp.concatenate([_flat_rows(parts[k], SHARD_ROWS[k]) for k in ("w_in", "w_out", "conv_dw")], axis=0)


def _unpack_shard(flat, shapes):
    out, at = {}, 0
    for k in ("w_in", "w_out", "conv_dw"):
        size = math.prod(shapes[k])
        out[k] = flat[at:at + SHARD_ROWS[k]].reshape(-1)[:size].reshape(shapes[k])
        at += SHARD_ROWS[k]
    return out


REPL_USED = sum(math.prod(shape) for _, shape in REPL_SHAPES)


def _pack_repl(parts, extra=None):
    tail = [] if extra is None else [extra.reshape(1)]
    flat = jnp.concatenate([parts[k].reshape(-1) for k, _ in REPL_SHAPES] + tail)
    return jnp.pad(flat, (0, REPL_ROWS * LANES - flat.shape[0])).reshape(REPL_ROWS, LANES)


def _unpack_repl(flat):
    out, at, flat = {}, 0, flat.reshape(-1)
    for k, shape in REPL_SHAPES:
        size = math.prod(shape)
        out[k] = flat[at:at + size].reshape(shape)
        at += size
    return out


def kernel(x, pre_norm, post_norm, w_in, w_out, q_norm, k_norm, conv_dw, conv_dw_b, conv_ln_g, conv_ln_b, sg_ln_g, sg_ln_b, sg_w, sg_b, loss_target, m_pre_norm, m_post_norm, m_w_in, m_w_out, m_q_norm, m_k_norm, m_conv_dw, m_conv_dw_b, m_conv_ln_g, m_conv_ln_b, m_sg_ln_g, m_sg_ln_b, m_sg_w, m_sg_b, v_pre_norm, v_post_norm, v_w_in, v_w_out, v_q_norm, v_k_norm, v_conv_dw, v_conv_dw_b, v_conv_ln_g, v_conv_ln_b, v_sg_ln_g, v_sg_ln_b, v_sg_w, v_sg_b):
    weights = dict(pre_norm=pre_norm, post_norm=post_norm, w_in=w_in, w_out=w_out, q_norm=q_norm, k_norm=k_norm,
                   conv_dw=conv_dw, conv_dw_b=conv_dw_b, conv_ln_g=conv_ln_g, conv_ln_b=conv_ln_b, sg_ln_g=sg_ln_g,
                   sg_ln_b=sg_ln_b, sg_w=sg_w, sg_b=sg_b)
    mom_m = dict(pre_norm=m_pre_norm, post_norm=m_post_norm, w_in=m_w_in, w_out=m_w_out, q_norm=m_q_norm,
                 k_norm=m_k_norm, conv_dw=m_conv_dw, conv_dw_b=m_conv_dw_b, conv_ln_g=m_conv_ln_g,
                 conv_ln_b=m_conv_ln_b, sg_ln_g=m_sg_ln_g, sg_ln_b=m_sg_ln_b, sg_w=m_sg_w, sg_b=m_sg_b)
    mom_v = dict(pre_norm=v_pre_norm, post_norm=v_post_norm, w_in=v_w_in, w_out=v_w_out, q_norm=v_q_norm,
                 k_norm=v_k_norm, conv_dw=v_conv_dw, conv_dw_b=v_conv_dw_b, conv_ln_g=v_conv_ln_g,
                 conv_ln_b=v_conv_ln_b, sg_ln_g=v_sg_ln_g, sg_ln_b=v_sg_ln_b, sg_w=v_sg_w, sg_b=v_sg_b)
    depth = pre_norm.shape[0]
    seq = x.shape[1]
    bk = min(BK, seq)
    x0 = x.reshape(seq, D_MODEL)
    target = loss_target.reshape(seq, D_MODEL)

    w_in_all, w_out_all, dw_all = _exchange(
        [], [w_in.astype(BF16), w_out.astype(BF16), jnp.pad(conv_dw, ((0, 0), (0, 1), (0, 0)))], "gather_weights")
    w_in_full = w_in_all.transpose(1, 2, 0, 3).reshape(depth, D_MODEL, D_IN)
    w_out_full = w_out_all.transpose(1, 0, 2, 3).reshape(depth, D_MODEL, D_MODEL)
    dw_full = dw_all[:, :, :CONV_K, :].transpose(1, 2, 0, 3).reshape(depth, CONV_K, CONV_W)

    cos, sin = _rope_tables(seq)
    lane = jnp.arange(SG_W)
    fold = (lane[:, None] // HEAD_DIM == jnp.arange(SG_CHUNK)[None, :]).astype(BF16)

    def layer_consts(l):
        return dict(
            q_gain=q_norm[l].reshape(HEAD_DIM, 1), k_gain=k_norm[l].reshape(HEAD_DIM, 1), sg_w_b=sg_w[l].astype(BF16), sg_wt_b=sg_w[l].transpose(0, 2, 1).astype(BF16),
            sg_bias=jnp.repeat(sg_b[l].T, HEAD_DIM, axis=1),
            vec=lambda a: a[l].reshape(1, -1))

    saved = []
    xc = x0
    for l in range(depth):
        c = layer_consts(l)
        proj, hb, qkv_t = _proj_fwd(xc, c["vec"](pre_norm), w_in_full[l], f"proj_fwd_{l}")
        qkv_t = qkv_t.reshape(QKV_HEADS, HEAD_DIM, seq)
        qs_t, qs = _qk_prep_fwd(qkv_t, 0, ATT_HEADS, c["q_gain"], Q_SCALE, cos, sin, f"q_prep_fwd_{l}")
        kr_t, kr = _qk_prep_fwd(qkv_t, ATT_HEADS, KV_HEADS, c["k_gain"], 1.0, cos, sin, f"k_prep_fwd_{l}")
        v_t = qkv_t[ATT_HEADS + KV_HEADS:].astype(BF16)
        k_rows, k_cols, v_cols = _row_blocks(kr, bk), _lane_blocks(kr_t, bk), _lane_blocks(v_t, bk)
        v_ext = jnp.concatenate([v_cols, jnp.ones_like(v_cols[:, :, :1]), jnp.zeros_like(v_cols[:, :, :7])], axis=2)
        o_t, lse = _attn_fwd(qs_t, k_rows, v_ext, f"attn_fwd_{l}")
        att_t = o_t.reshape(ATT_W, seq)
        cnv, y_conv = _conv_fwd(proj, dw_full[l], c["vec"](conv_dw_b), c["vec"](conv_ln_g), c["vec"](conv_ln_b),
                                f"conv_fwd_{l}")
        sgu = _sg_fwd(proj, c["vec"](sg_ln_g), c["vec"](sg_ln_b), c["sg_w_b"], c["sg_bias"], f"sg_fwd_{l}")
        if l < depth - 1:
            x_new, mix, cat_b = _out_fwd(att_t, proj, cnv, sgu, xc, w_out_full[l], c["vec"](post_norm),
                                         f"out_fwd_{l}")
        else:
            sse, dx, mix, cat_b = _out_fwd(att_t, proj, cnv, sgu, xc, w_out_full[l], c["vec"](post_norm),
                                           f"out_fwd_{l}", target=target)
            x_new = None
        saved.append(dict(x=xc, proj=proj, hb=hb, qkv_t=qkv_t, qs=qs, qs_t=qs_t, k_rows=k_rows,
                          k_cols=k_cols, v_cols=v_cols, o_t=o_t, lse=lse, mix=mix, cat_b=cat_b, y_conv=y_conv))
        xc = x_new

    grads = {k: [None] * depth for k in WEIGHT_ORDER}
    for l in reversed(range(depth)):
        c, s = layer_consts(l), saved[l]
        dmix_b, datt, dgatt, dcnv, dsgu, g_post = _out_bwd(
            dx, s["mix"], c["vec"](post_norm), w_out_full[l].T, s["o_t"].reshape(ATT_W, seq), s["proj"],
            f"out_bwd_{l}")
        grads["post_norm"][l] = g_post.reshape(-1)
        grads["w_out"][l] = _matmul_acc(s["cat_b"], dmix_b, D_MODEL, f"grad_w_out_{l}")
        dqs, dkt, dvt = _attn_bwd(s["qs"], s["qs_t"], datt.reshape(ATT_HEADS, HEAD_DIM, seq), s["o_t"], s["lse"],
                                  s["k_rows"], s["k_cols"], s["v_cols"], f"attn_bwd_{l}")
        d_q_t, g_qgain = _qk_prep_bwd(s["qkv_t"], 0, dqs, False, c["q_gain"], Q_SCALE, cos, sin, f"q_prep_bwd_{l}")
        d_k_t, g_kgain = _qk_prep_bwd(s["qkv_t"], ATT_HEADS, _from_lane_blocks(dkt), True, c["k_gain"], 1.0, cos, sin,
                                      f"k_prep_bwd_{l}")
        grads["q_norm"][l] = jnp.sum(g_qgain[:, :, 0], axis=0)
        grads["k_norm"][l] = jnp.sum(g_kgain[:, :, 0], axis=0)
        d_qkv = jnp.concatenate([d_q_t, d_k_t, _from_lane_blocks(dvt)], axis=0).reshape(QKV_W, seq).T
        dy_conv, dg_conv, g_dw, g_dwb, g_clg, g_clb = _conv_bwd_a(
            s["proj"], s["y_conv"], dcnv, c["vec"](conv_ln_g), c["vec"](conv_ln_b), f"conv_bwd_a_{l}")
        da, db = _conv_bwd_b(s["proj"], dy_conv, dw_full[l], f"conv_bwd_b_{l}")
        grads["conv_dw"][l], grads["conv_dw_b"][l] = g_dw, g_dwb.reshape(-1)
        grads["conv_ln_g"][l], grads["conv_ln_b"][l] = g_clg.reshape(-1), g_clb.reshape(-1)
        du, dv_sg, dg_sg, g_slg, g_slb, g_sw, g_sb = _sg_bwd(
            s["proj"], dsgu, c["vec"](sg_ln_g), c["vec"](sg_ln_b), c["sg_w_b"], c["sg_wt_b"], c["sg_bias"], fold,
            f"sg_bwd_{l}")
        grads["sg_ln_g"][l], grads["sg_ln_b"][l] = g_slg.reshape(-1), g_slb.reshape(-1)
        grads["sg_w"][l], grads["sg_b"][l] = g_sw, g_sb[:, :SG_HEADS].T
        dproj = jnp.concatenate([d_qkv, dgatt, da, db, dg_conv, du, dv_sg, dg_sg], axis=-1).astype(BF16)
        grads["w_in"][l] = _matmul_acc(s["hb"], dproj, D_IN // 2, f"grad_w_in_{l}")
        dx, g_pre = _proj_bwd(dproj, w_in_full[l].T, s["x"], c["vec"](pre_norm), dx, f"proj_bwd_{l}")
        grads["pre_norm"][l] = g_pre.reshape(-1)
    grad_x = dx.reshape(x.shape)
    grads = {k: jnp.stack(v) for k, v in grads.items()}

    shard_blocks = dict(
        w_in=grads["w_in"].reshape(depth, D_MODEL, N_DEV, D_IN // N_DEV).transpose(2, 0, 1, 3),
        w_out=grads["w_out"].reshape(depth, N_DEV, D_MODEL // N_DEV, D_MODEL).transpose(1, 0, 2, 3),
        conv_dw=grads["conv_dw"].reshape(depth, CONV_K, N_DEV, CONV_W // N_DEV).transpose(2, 0, 1, 3))
    scatter_src = jnp.stack([_pack_shard({k: a[d] for k, a in shard_blocks.items()})
                             for d in range(N_DEV)]).astype(BF16)
    shard_slots, repl_slots = _exchange([scatter_src], [_pack_repl(grads, sse[0, 0])], "exchange_grads")

    shard_shapes = {k: weights[k].shape for k in SHARD_ROWS}
    gs, ds_, ms, vs = _sum_adamw(shard_slots, _pack_shard(weights), _pack_shard(mom_m), _pack_shard(mom_v),
                                 "adamw_sharded")
    gr, dr, mr, vr = _sum_adamw(repl_slots, _pack_repl(weights), _pack_repl(mom_m), _pack_repl(mom_v),
                                "adamw_replicated")
    loss = gr.reshape(-1)[REPL_USED] * (0.5 / D_MODEL)
    results = []
    for shard_flat, repl_flat in ((gs, gr), (ds_, dr), (ms, mr), (vs, vr)):
        parts = {**_unpack_shard(shard_flat, shard_shapes), **_unpack_repl(repl_flat)}
        results.append([parts[k] for k in WEIGHT_ORDER])
    return (loss, grad_x, *results[0], *results[1], *results[2], *results[3])
```

```python
import math

import jax
import jax.numpy as jnp
from jax import lax
from jax.experimental import pallas as pl
from jax.experimental.pallas import tpu as pltpu

F32, BF16 = jnp.float32, jnp.bfloat16

N_DEV = 8
MESH_AXES = ("x", "y", "c")
EPS = 1e-6
D_MODEL = 1024
HEAD_DIM = 64
ATT_HEADS, KV_HEADS = 8, 2
QKV_HEADS = ATT_HEADS + 2 * KV_HEADS
QKV_W = QKV_HEADS * HEAD_DIM
GROUP = ATT_HEADS // KV_HEADS
ATT_W, KV_W, CONV_W, SG_W = 512, 128, 256, 256
CONV_K, CONV_PAD, HALO = 31, 15, 16
SG_HEADS, SG_CHUNK = 4, 128
D_IN = 2816
GRID_W = 64
ROPE_THETA = 10000.0
LOG2E, LN2 = math.log2(math.e), math.log(2.0)
Q_SCALE = HEAD_DIM ** -0.5 * LOG2E
LANES = 128

COL_GATT, COL_A, COL_B, COL_GCONV, COL_U, COL_VSG, COL_GSG = 3, 5, 6, 7, 8, 9, 10

ADAM_LR, ADAM_B1, ADAM_B2, ADAM_EPS, ADAM_WD, ADAM_STEP = 0.001, 0.9, 0.999, 1e-08, 0.01, 10

T_ROW = 512
T_PREP = 2048
T_GROUP = 512
BQ, BK = 512, 512
HEADS_PER_STEP = 2
FWD_HEADS_PER_STEP = 2
FWD_UNROLL, BWD_UNROLL = 8, 4
VMEM_MB = 56
ATTN_BWD_VMEM_MB = 58


def _pcall(body, *, name, grid, in_specs, out_specs, out_shape, scratch=(), sem=None, vmem_mb=None):
    params = {}
    if sem is not None:
        params["dimension_semantics"] = sem
    if vmem_mb is not None:
        params["vmem_limit_bytes"] = vmem_mb << 20
    return pl.pallas_call(body, name=name, grid=grid, in_specs=in_specs, out_specs=out_specs, out_shape=out_shape,
                          scratch_shapes=list(scratch), compiler_params=pltpu.CompilerParams(**params))


def _dot(a, b):
    return jnp.dot(a, b, preferred_element_type=F32)


def _sigmoid(x):
    return 1.0 / (1.0 + jnp.exp(-x))


def _silu_and_grad(x):
    s = _sigmoid(x)
    return x * s, s * (1.0 + x * (1.0 - s))


def _gelu_and_grad(x):
    cdf = 0.5 * (1.0 + lax.erf(x * (1.0 / math.sqrt(2.0))))
    pdf = jnp.exp(-0.5 * x * x) * (1.0 / math.sqrt(2.0 * math.pi))
    return x * cdf, cdf + x * pdf


def _split_dot(y, mat):
    hi = y.astype(BF16)
    lo = (y - hi.astype(F32)).astype(BF16)
    return _dot(hi, mat) + _dot(lo, mat)


def _row_tile(rows, cap):
    best = 8
    for t in range(8, min(rows, cap) + 1, 8):
        if rows % t == 0:
            best = t
    return best


def _exchange(scatter, gather, name):
    n_s = len(scatter)
    arrs = list(scatter) + list(gather)
    n = len(arrs)
    flips = [(fx, fy, fc) for fx in (0, 1) for fy in (0, 1) for fc in (0, 1)][1:]
    n_peer = len(flips)

    def body(*refs):
        ins, outs = refs[:n], refs[n:2 * n]
        send_sems, recv_sems, local_sems = refs[2 * n:]
        pos = tuple(lax.axis_index(a) for a in MESH_AXES)

        def peer(flip):
            return tuple((1 - p) if f else p for p, f in zip(pos, flip))

        def slot(p):
            return 4 * p[0] + 2 * p[1] + p[2]

        def src(a, p):
            return ins[a].at[slot(p)] if a < n_s else ins[a]

        def remote(a, k, src_ref, dst_slot, to):
            return pltpu.make_async_remote_copy(
                src_ref=src_ref, dst_ref=outs[a].at[dst_slot], send_sem=send_sems.at[a * n_peer + k],
                recv_sem=recv_sems.at[a * n_peer + k], device_id=to, device_id_type=pl.DeviceIdType.MESH)

        local = [pltpu.make_async_copy(src(a, pos), outs[a].at[slot(pos)], local_sems.at[a]) for a in range(n)]
        for cp in local:
            cp.start()
        sends = [remote(a, k, src(a, peer(f)), slot(pos), peer(f)) for a in range(n) for k, f in enumerate(flips)]
        for cp in sends:
            cp.start()
        for a in range(n):
            for k, f in enumerate(flips):
                remote(a, k, src(a, peer(f)), slot(peer(f)), peer(f)).wait_recv()
        for cp in sends:
            cp.wait_send()
        for cp in local:
            cp.wait()

    out_shape = [jax.ShapeDtypeStruct((N_DEV,) + (a.shape[1:] if i < n_s else a.shape), a.dtype)
                 for i, a in enumerate(arrs)]
    any_spec = pl.BlockSpec(memory_space=pl.ANY)
    return pl.pallas_call(
        body, name=name, out_shape=out_shape, in_specs=[any_spec] * n, out_specs=[any_spec] * n,
        scratch_shapes=[pltpu.SemaphoreType.DMA((n * n_peer,)), pltpu.SemaphoreType.DMA((n * n_peer,)),
                        pltpu.SemaphoreType.DMA((n,))],
    )(*arrs)


def _sum_adamw(slots, w, m, v, name):
    rows = w.shape[0]
    tr = _row_tile(rows, 1024)
    c1 = 1.0 - ADAM_B1 ** ADAM_STEP
    c2 = 1.0 - ADAM_B2 ** ADAM_STEP

    def body(s_ref, w_ref, m_ref, v_ref, g_out, d_out, m_out, v_out):
        g = s_ref[0].astype(F32)
        for d in range(1, N_DEV):
            g = g + s_ref[d].astype(F32)
        m_new = ADAM_B1 * m_ref[...] + (1.0 - ADAM_B1) * g
        v_new = ADAM_B2 * v_ref[...] + (1.0 - ADAM_B2) * (g * g)
        m_hat = m_new / c1
        v_hat = v_new / c2
        g_out[...] = g
        d_out[...] = -ADAM_LR * (m_hat / (jnp.sqrt(v_hat) + ADAM_EPS) + ADAM_WD * w_ref[...])
        m_out[...] = m_new
        v_out[...] = v_new

    flat = pl.BlockSpec((tr, LANES), lambda i: (i, 0))
    return _pcall(
        body, name=name, grid=(rows // tr,),
        in_specs=[pl.BlockSpec((N_DEV, tr, LANES), lambda i: (0, i, 0)), flat, flat, flat],
        out_specs=[flat] * 4, out_shape=[jax.ShapeDtypeStruct((rows, LANES), F32)] * 4,
        sem=("parallel",), vmem_mb=VMEM_MB)(slots, w, m, v)


def _proj_fwd(x, gain, w, name):
    seq = x.shape[0]

    def body(x_ref, g_ref, w_ref, proj_ref, hb_ref, qkv_t_ref):
        xf = x_ref[...]
        r = lax.rsqrt(jnp.mean(xf * xf, axis=-1, keepdims=True) + EPS)
        h = (xf * r * g_ref[...]).astype(BF16)
        hb_ref[...] = h
        proj = _dot(h, w_ref[...])
        proj_ref[...] = proj
        qkv_t_ref[...] = proj[:, :QKV_W].T

    return _pcall(
        body, name=name, grid=(seq // T_ROW,),
        in_specs=[pl.BlockSpec((T_ROW, D_MODEL), lambda i: (i, 0)), pl.BlockSpec((1, D_MODEL), lambda i: (0, 0)),
                  pl.BlockSpec((D_MODEL, D_IN), lambda i: (0, 0))],
        out_specs=[pl.BlockSpec((T_ROW, D_IN), lambda i: (i, 0)), pl.BlockSpec((T_ROW, D_MODEL), lambda i: (i, 0)),
                   pl.BlockSpec((QKV_W, T_ROW), lambda i: (0, i))],
        out_shape=[jax.ShapeDtypeStruct((seq, D_IN), F32), jax.ShapeDtypeStruct((seq, D_MODEL), BF16),
                   jax.ShapeDtypeStruct((QKV_W, seq), F32)],
        sem=("parallel",), vmem_mb=VMEM_MB)(x, gain, w)


def _proj_bwd(dproj, w_t, x, gain, dxo, name):
    seq = x.shape[0]

    def body(dp_ref, w_ref, x_ref, g_ref, dxo_ref, dx_ref, dg_ref):
        dh = _dot(dp_ref[...], w_ref[...])
        xf = x_ref[...]
        r = lax.rsqrt(jnp.mean(xf * xf, axis=-1, keepdims=True) + EPS)
        n = xf * r
        dn = dh * g_ref[...]
        dx_ref[...] = dxo_ref[...] + r * (dn - n * jnp.mean(dn * n, axis=-1, keepdims=True))

        @pl.when(pl.program_id(0) == 0)
        def _():
            dg_ref[...] = jnp.zeros_like(dg_ref)

        dg_ref[...] += jnp.sum(dh * n, axis=0, keepdims=True)

    row = pl.BlockSpec((T_ROW, D_MODEL), lambda i: (i, 0))
    vec = pl.BlockSpec((1, D_MODEL), lambda i: (0, 0))
    return _pcall(
        body, name=name, grid=(seq // T_ROW,),
        in_specs=[pl.BlockSpec((T_ROW, D_IN), lambda i: (i, 0)), pl.BlockSpec((D_IN, D_MODEL), lambda i: (0, 0)),
                  row, vec, row],
        out_specs=[row, vec],
        out_shape=[jax.ShapeDtypeStruct((seq, D_MODEL), F32), jax.ShapeDtypeStruct((1, D_MODEL), F32)],
        sem=("arbitrary",), vmem_mb=VMEM_MB)(dproj, w_t, x, gain, dxo)


def _matmul_acc(a, b, tn, name):
    seq, m = a.shape
    n = b.shape[1]
    ts = min(1024, seq)

    def body(a_ref, b_ref, o_ref):
        @pl.when(pl.program_id(1) == 0)
        def _():
            o_ref[...] = jnp.zeros_like(o_ref)

        o_ref[...] += lax.dot_general(a_ref[...], b_ref[...], (((0,), (0,)), ((), ())), preferred_element_type=F32)

    return _pcall(
        body, name=name, grid=(n // tn, seq // ts),
        in_specs=[pl.BlockSpec((ts, m), lambda j, k: (k, 0)), pl.BlockSpec((ts, tn), lambda j, k: (k, j))],
        out_specs=pl.BlockSpec((m, tn), lambda j, k: (0, j)), out_shape=jax.ShapeDtypeStruct((m, n), F32),
        sem=("parallel", "arbitrary"), vmem_mb=VMEM_MB)(a, b)


ROPE_HALF = HEAD_DIM // 4


def _rope_tables(seq):
    t = jnp.arange(seq, dtype=jnp.int32)
    row = (t // GRID_W).astype(F32)
    col = (t % GRID_W).astype(F32)
    inv_freq = ROPE_THETA ** (-jnp.arange(ROPE_HALF, dtype=F32) / ROPE_HALF)
    ang_r = row[:, None] * inv_freq[None, :]
    ang_c = col[:, None] * inv_freq[None, :]
    cos = jnp.concatenate([jnp.cos(ang_r), jnp.cos(ang_r), jnp.cos(ang_c), jnp.cos(ang_c)], axis=-1)
    sin = jnp.concatenate([-jnp.sin(ang_r), jnp.sin(ang_r), -jnp.sin(ang_c), jnp.sin(ang_c)], axis=-1)
    return cos.T, sin.T


def _rope_partner(y):
    h = ROPE_HALF
    return jnp.concatenate([y[h:2 * h], y[0:h], y[3 * h:4 * h], y[2 * h:3 * h]], axis=0)


def _qk_specs(seq, head0):
    t = min(T_PREP, seq)
    src = pl.BlockSpec((None, HEAD_DIM, t), lambda i, h: (h + head0, 0, i))
    own = pl.BlockSpec((None, HEAD_DIM, t), lambda i, h: (h, 0, i))
    nat = pl.BlockSpec((None, t, HEAD_DIM), lambda i, h: (h, i, 0))
    col = pl.BlockSpec((HEAD_DIM, 1), lambda i, h: (0, 0))
    tab = pl.BlockSpec((HEAD_DIM, t), lambda i, h: (0, i))
    return t, src, own, nat, col, tab


def _qk_prep_fwd(qkv_t, head0, heads, gain, scale, cos, sin, name):
    seq = qkv_t.shape[2]
    t, src, own, nat, col, tab = _qk_specs(seq, head0)

    def body(x_ref, g_ref, c_ref, s_ref, ot_ref, on_ref):
        xf = x_ref[...]
        r = lax.rsqrt(jnp.mean(xf * xf, axis=0, keepdims=True) + EPS)
        y = xf * r * g_ref[...]
        z = (y * c_ref[...] + _rope_partner(y) * s_ref[...]) * scale
        ot_ref[...] = z.astype(BF16)
        on_ref[...] = z.T.astype(BF16)

    return _pcall(body, name=name, grid=(seq // t, heads), in_specs=[src, col, tab, tab], out_specs=[own, nat],
                  out_shape=[jax.ShapeDtypeStruct((heads, HEAD_DIM, seq), BF16),
                             jax.ShapeDtypeStruct((heads, seq, HEAD_DIM), BF16)],
                  sem=("parallel", "parallel"))(qkv_t, gain, cos, sin)


def _qk_prep_bwd(qkv_t, head0, dout, dout_is_t, gain, scale, cos, sin, name):
    heads = dout.shape[0]
    seq = qkv_t.shape[2]
    t, src, own, nat, col, tab = _qk_specs(seq, head0)

    def body(x_ref, d_ref, g_ref, c_ref, s_ref, dx_ref, dg_ref):
        xf = x_ref[...]
        r = lax.rsqrt(jnp.mean(xf * xf, axis=0, keepdims=True) + EPS)
        n = xf * r
        d = d_ref[...] if dout_is_t else d_ref[...].T
        dz = d * scale
        dy = dz * c_ref[...] + _rope_partner(dz * s_ref[...])
        dn = dy * g_ref[...]
        dx_ref[...] = r * (dn - n * jnp.mean(dn * n, axis=0, keepdims=True))

        @pl.when((pl.program_id(0) == 0) & (pl.program_id(1) == 0))
        def _():
            dg_ref[...] = jnp.zeros_like(dg_ref)

        dg_ref[pl.program_id(1)] += jnp.sum(dy * n, axis=1, keepdims=True)

    return _pcall(body, name=name, grid=(seq // t, heads),
                  in_specs=[src, own if dout_is_t else nat, col, tab, tab],
                  out_specs=[own, pl.BlockSpec((heads, HEAD_DIM, 1), lambda i, h: (0, 0, 0))],
                  out_shape=[jax.ShapeDtypeStruct((heads, HEAD_DIM, seq), F32),
                             jax.ShapeDtypeStruct((heads, HEAD_DIM, 1), F32)],
                  sem=("arbitrary", "arbitrary"))(qkv_t, dout, gain, cos, sin)


V_ROWS = HEAD_DIM + 8


def _unroll(nk, cap):
    u = 1
    while u * 2 <= cap and nk % (u * 2) == 0:
        u *= 2
    return u


def _attn_fwd(qs_t, k, v_t, name):
    seq = qs_t.shape[2]
    nk, bk = k.shape[1], k.shape[2]
    bq = min(BQ, seq)
    unroll = _unroll(nk, FWD_UNROLL)
    hps = FWD_HEADS_PER_STEP
    heads = range(hps)

    def body(qt_ref, k_ref, vt_ref, ot_ref, lse_ref, s_scr):
        q_t = [qt_ref[h] for h in heads]

        def scores(j, slot):
            kj = k_ref[j]
            top = []
            for h in heads:
                s = _dot(kj, q_t[h])
                s_scr[slot, h] = s
                top.append(jnp.max(s, axis=0, keepdims=True))
            return tuple(top)

        def accumulate(j, slot, state, top):
            vtj = vt_ref[j]
            out = []
            for h in heads:
                m, acc = state[h]
                m_new = jnp.maximum(m, top[h])
                p = jnp.exp2(s_scr[slot, h] - m_new).astype(BF16)
                out.append((m_new, jnp.exp2(m - m_new) * acc + _dot(vtj, p)))
            return tuple(out)

        def step(t, carry):
            state, top = carry
            for u in range(unroll):
                nxt = unroll * t + u + 1
                top_next = scores(jnp.minimum(nxt, nk - 1) if u == unroll - 1 else nxt, (u + 1) % 2)
                state = accumulate(unroll * t + u, u % 2, state, top)
                top = top_next
            return state, top

        init = tuple((jnp.full((1, bq), -jnp.inf, F32), jnp.zeros((V_ROWS, bq), F32)) for _ in heads)
        state, _ = lax.fori_loop(0, nk // unroll, step, (init, scores(0, 0)))
        for h in heads:
            m, acc = state[h]
            l = acc[HEAD_DIM:HEAD_DIM + 1, :]
            ot_ref[h] = acc[:HEAD_DIM, :] / l
            lse_ref[h] = m + jnp.log2(l)

    kv_of = lambda g: g * hps // GROUP
    return _pcall(
        body, name=name, grid=(ATT_HEADS // hps, seq // bq),
        in_specs=[pl.BlockSpec((hps, HEAD_DIM, bq), lambda g, i: (g, 0, i)),
                  pl.BlockSpec((None, nk, bk, HEAD_DIM), lambda g, i: (kv_of(g), 0, 0, 0)),
                  pl.BlockSpec((None, nk, V_ROWS, bk), lambda g, i: (kv_of(g), 0, 0, 0))],
        out_specs=[pl.BlockSpec((hps, HEAD_DIM, bq), lambda g, i: (g, 0, i)),
                   pl.BlockSpec((hps, 1, bq), lambda g, i: (g, 0, i))],
        out_shape=[jax.ShapeDtypeStruct((ATT_HEADS, HEAD_DIM, seq), F32),
                   jax.ShapeDtypeStruct((ATT_HEADS, 1, seq), F32)],
        scratch=[pltpu.VMEM((2, hps, bk, bq), F32)],
        sem=("parallel", "parallel"), vmem_mb=VMEM_MB)(qs_t, k, v_t)


def _attn_bwd(qs, qs_t, do_t, o_t, lse, k, k_t, v_t, name):
    seq = qs.shape[1]
    nk, bk = k.shape[1], k.shape[2]
    bq = min(BQ, seq)
    nq = seq // bq

    unroll = _unroll(nk, BWD_UNROLL)
    heads = range(HEADS_PER_STEP)
    pairs = GROUP // HEADS_PER_STEP

    def body(q_ref, qt_ref, dot_ref, ot_ref, lse_ref, k_ref, kt_ref, vt_ref, dq_ref, dkt_ref, dvt_ref,
             s_scr, dp_scr):
        @pl.when((pl.program_id(1) == 0) & (pl.program_id(2) == 0))
        def _():
            dkt_ref[...] = jnp.zeros_like(dkt_ref)
            dvt_ref[...] = jnp.zeros_like(dvt_ref)

        q, q_t = [q_ref[h] for h in heads], [qt_ref[h] for h in heads]
        do_t_b = [dot_ref[h].astype(BF16) for h in heads]
        do_l = [dot_ref[h].T * LN2 for h in heads]
        do_b = [d.astype(BF16) for d in do_l]
        delta = [jnp.sum(do_l[h] * ot_ref[h].T, axis=-1, keepdims=True) for h in heads]
        lse_col = [jnp.max(jnp.broadcast_to(lse_ref[h], (LANES, bq)).T, axis=-1, keepdims=True) for h in heads]

        def products(j, slot):
            ktj, vtj = kt_ref[j], vt_ref[j]
            for h in heads:
                s_scr[slot, h] = _dot(q[h], ktj)
                dp_scr[slot, h] = _dot(do_b[h], vtj).astype(BF16)

        def gradients(j, slot, dq):
            kj = k_ref[j]
            dvt = jnp.zeros((HEAD_DIM, bk), F32)
            dkt = jnp.zeros((HEAD_DIM, bk), F32)
            new = []
            for h in heads:
                p = jnp.exp2(s_scr[slot, h] - lse_col[h])
                ds = (p * (dp_scr[slot, h].astype(F32) - delta[h])).astype(BF16)
                dvt = dvt + _dot(do_t_b[h], p.astype(BF16))
                dkt = dkt + _dot(q_t[h], ds)
                new.append(dq[h] + _dot(ds, kj))
            dvt_ref[j] += dvt
            dkt_ref[j] += dkt
            return tuple(new)

        def step(t, dq):
            for u in range(unroll):
                nxt = unroll * t + u + 1
                products(jnp.minimum(nxt, nk - 1) if u == unroll - 1 else nxt, (u + 1) % 2)
                dq = gradients(unroll * t + u, u % 2, dq)
            return dq

        products(0, 0)
        res = lax.fori_loop(0, nk // unroll, step, tuple(jnp.zeros((bq, HEAD_DIM), F32) for _ in heads))
        for h in heads:
            dq_ref[h] = res[h]

    first = lambda g, hh: g * pairs + hh
    row = pl.BlockSpec((HEADS_PER_STEP, bq, HEAD_DIM), lambda g, hh, i: (first(g, hh), i, 0))
    col = pl.BlockSpec((HEADS_PER_STEP, HEAD_DIM, bq), lambda g, hh, i: (first(g, hh), 0, i))
    kv_rows = pl.BlockSpec((None, nk, bk, HEAD_DIM), lambda g, hh, i: (g, 0, 0, 0))
    kv_cols = pl.BlockSpec((None, nk, HEAD_DIM, bk), lambda g, hh, i: (g, 0, 0, 0))
    return _pcall(
        body, name=name, grid=(KV_HEADS, pairs, nq),
        in_specs=[row, col, col, col,
                  pl.BlockSpec((HEADS_PER_STEP, 1, bq), lambda g, hh, i: (first(g, hh), 0, i)),
                  kv_rows, kv_cols, kv_cols],
        out_specs=[row, kv_cols, kv_cols],
        out_shape=[jax.ShapeDtypeStruct((ATT_HEADS, seq, HEAD_DIM), F32),
                   jax.ShapeDtypeStruct((KV_HEADS, nk, HEAD_DIM, bk), F32),
                   jax.ShapeDtypeStruct((KV_HEADS, nk, HEAD_DIM, bk), F32)],
        scratch=[pltpu.VMEM((2, HEADS_PER_STEP, bq, bk), F32), pltpu.VMEM((2, HEADS_PER_STEP, bq, bk), BF16)],
        sem=("parallel", "arbitrary", "arbitrary"), vmem_mb=ATTN_BWD_VMEM_MB)(
            qs, qs_t, do_t, o_t, lse, k, k_t, v_t)


def _halo_specs(t, col, n_tiles):
    per = t // HALO
    last = n_tiles * per - 1
    before = pl.BlockSpec((HALO, CONV_W), lambda i: (jnp.maximum(i * per - 1, 0), col))
    after = pl.BlockSpec((HALO, CONV_W), lambda i: (jnp.minimum((i + 1) * per, last), col))
    return before, after


def _glu(a, b):
    return a * _sigmoid(b)


SUBLANES = 8
SHIFT_ROWS = 2 * HALO - SUBLANES


def _shifted(sh_ref, off, t):
    return sh_ref[off % SUBLANES, pl.ds(off - off % SUBLANES, t), :]


def _conv_taps(sh_ref, w_ref, t, flip):
    acc = jnp.zeros((t, CONV_W), F32)
    for k in range(CONV_K):
        off = (HALO + CONV_PAD - k) if flip else (HALO - CONV_PAD + k)
        acc = acc + w_ref[k:k + 1, :] * _shifted(sh_ref, off, t)
    return acc


def _fill_ext(ext_ref, sh_ref, before, tile, after, t, i, n_tiles):
    ext_ref[pl.ds(0, HALO), :] = jnp.where(i > 0, before, 0.0)
    ext_ref[pl.ds(HALO, t), :] = tile
    ext_ref[pl.ds(HALO + t, HALO), :] = jnp.where(i < n_tiles - 1, after, 0.0)
    for b in range(SUBLANES):
        sh_ref[b] = ext_ref[pl.ds(b, t + SHIFT_ROWS), :]


def _conv_scratch(t):
    return [pltpu.VMEM((t + 2 * HALO, CONV_W), F32), pltpu.VMEM((SUBLANES, t + SHIFT_ROWS, CONV_W), F32)]


def _conv_fwd(proj, w, bias, ln_g, ln_b, name):
    seq = proj.shape[0]
    t = min(T_GROUP, seq)
    n_tiles = seq // t

    def body(a_ref, b_ref, ap_ref, bp_ref, an_ref, bn_ref, gate_ref, w_ref, bias_ref, g_ref, beta_ref, o_ref, y_ref,
             ext_ref, sh_ref):
        i = pl.program_id(0)
        _fill_ext(ext_ref, sh_ref, _glu(ap_ref[...], bp_ref[...]), _glu(a_ref[...], b_ref[...]),
                  _glu(an_ref[...], bn_ref[...]), t, i, n_tiles)
        y = _conv_taps(sh_ref, w_ref, t, False) + bias_ref[...]
        y_ref[...] = y
        mu = jnp.mean(y, axis=-1, keepdims=True)
        yc = y - mu
        rs = lax.rsqrt(jnp.mean(yc * yc, axis=-1, keepdims=True) + EPS)
        z = yc * rs * g_ref[...] + beta_ref[...]
        o_ref[...] = _silu_and_grad(z)[0] * _silu_and_grad(gate_ref[...])[0]

    tile = lambda c: pl.BlockSpec((t, CONV_W), lambda i: (i, c))
    ab, aa = _halo_specs(t, COL_A, n_tiles)
    bb, ba = _halo_specs(t, COL_B, n_tiles)
    vec = pl.BlockSpec((1, CONV_W), lambda i: (0, 0))
    return _pcall(
        body, name=name, grid=(n_tiles,),
        in_specs=[tile(COL_A), tile(COL_B), ab, bb, aa, ba, tile(COL_GCONV),
                  pl.BlockSpec((CONV_K, CONV_W), lambda i: (0, 0)), vec, vec, vec],
        out_specs=[pl.BlockSpec((t, CONV_W), lambda i: (i, 0))] * 2,
        out_shape=[jax.ShapeDtypeStruct((seq, CONV_W), F32)] * 2,
        scratch=_conv_scratch(t), sem=("parallel",))(
            proj, proj, proj, proj, proj, proj, proj, w, bias, ln_g, ln_b)


def _conv_bwd_a(proj, y_conv, dcnv, ln_g, ln_b, name):
    seq = proj.shape[0]
    t = min(T_GROUP, seq)
    n_tiles = seq // t

    def body(a_ref, b_ref, ap_ref, bp_ref, an_ref, bn_ref, gate_ref, y_ref, d_ref, g_ref, beta_ref,
             dy_ref, dgate_ref, dw_ref, dbias_ref, dg_ref, dbeta_ref, ext_ref, sh_ref):
        i = pl.program_id(0)
        _fill_ext(ext_ref, sh_ref, _glu(ap_ref[...], bp_ref[...]), _glu(a_ref[...], b_ref[...]),
                  _glu(an_ref[...], bn_ref[...]), t, i, n_tiles)
        y = y_ref[...]
        mu = jnp.mean(y, axis=-1, keepdims=True)
        yc = y - mu
        rs = lax.rsqrt(jnp.mean(yc * yc, axis=-1, keepdims=True) + EPS)
        n = yc * rs
        z = n * g_ref[...] + beta_ref[...]
        act, dact = _silu_and_grad(z)
        gate, dgate = _silu_and_grad(gate_ref[...])
        d = d_ref[...]
        dgate_ref[...] = (d * act * dgate).astype(BF16)
        dz = d * gate * dact
        dn = dz * g_ref[...]
        dy = rs * (dn - jnp.mean(dn, axis=-1, keepdims=True) - n * jnp.mean(dn * n, axis=-1, keepdims=True))
        dy_ref[...] = dy

        @pl.when(i == 0)
        def _():
            dw_ref[...] = jnp.zeros_like(dw_ref)
            dbias_ref[...] = jnp.zeros_like(dbias_ref)
            dg_ref[...] = jnp.zeros_like(dg_ref)
            dbeta_ref[...] = jnp.zeros_like(dbeta_ref)

        dg_ref[...] += jnp.sum(dz * n, axis=0, keepdims=True)
        dbeta_ref[...] += jnp.sum(dz, axis=0, keepdims=True)
        dbias_ref[...] += jnp.sum(dy, axis=0, keepdims=True)
        for k in range(CONV_K):
            dw_ref[k:k + 1, :] += jnp.sum(dy * _shifted(sh_ref, HALO - CONV_PAD + k, t), axis=0, keepdims=True)

    tile = lambda c: pl.BlockSpec((t, CONV_W), lambda i: (i, c))
    own = pl.BlockSpec((t, CONV_W), lambda i: (i, 0))
    ab, aa = _halo_specs(t, COL_A, n_tiles)
    bb, ba = _halo_specs(t, COL_B, n_tiles)
    vec = pl.BlockSpec((1, CONV_W), lambda i: (0, 0))
    taps = pl.BlockSpec((CONV_K, CONV_W), lambda i: (0, 0))
    vshape = jax.ShapeDtypeStruct((1, CONV_W), F32)
    return _pcall(
        body, name=name, grid=(n_tiles,),
        in_specs=[tile(COL_A), tile(COL_B), ab, bb, aa, ba, tile(COL_GCONV), own, own, vec, vec],
        out_specs=[own, own, taps, vec, vec, vec],
        out_shape=[jax.ShapeDtypeStruct((seq, CONV_W), F32), jax.ShapeDtypeStruct((seq, CONV_W), BF16),
                   jax.ShapeDtypeStruct((CONV_K, CONV_W), F32), vshape, vshape, vshape],
        scratch=_conv_scratch(t), sem=("arbitrary",))(
            proj, proj, proj, proj, proj, proj, proj, y_conv, dcnv, ln_g, ln_b)


def _conv_bwd_b(proj, dy, w, name):
    seq = proj.shape[0]
    t = min(T_GROUP, seq)
    n_tiles = seq // t

    def body(a_ref, b_ref, dy_ref, dyp_ref, dyn_ref, w_ref, da_ref, db_ref, ext_ref, sh_ref):
        i = pl.program_id(0)
        _fill_ext(ext_ref, sh_ref, dyp_ref[...], dy_ref[...], dyn_ref[...], t, i, n_tiles)
        dh = _conv_taps(sh_ref, w_ref, t, True)
        sig = _sigmoid(b_ref[...])
        da_ref[...] = (dh * sig).astype(BF16)
        db_ref[...] = (dh * a_ref[...] * sig * (1.0 - sig)).astype(BF16)

    tile = lambda c: pl.BlockSpec((t, CONV_W), lambda i: (i, c))
    own = pl.BlockSpec((t, CONV_W), lambda i: (i, 0))
    before, after = _halo_specs(t, 0, n_tiles)
    return _pcall(
        body, name=name, grid=(n_tiles,),
        in_specs=[tile(COL_A), tile(COL_B), own, before, after, pl.BlockSpec((CONV_K, CONV_W), lambda i: (0, 0))],
        out_specs=[own, own], out_shape=[jax.ShapeDtypeStruct((seq, CONV_W), BF16)] * 2,
        scratch=_conv_scratch(t), sem=("parallel",))(proj, proj, dy, dy, dy, w)


def _head_masks():
    lane_head = lax.broadcasted_iota(jnp.int32, (SG_CHUNK, SG_W), 1) // HEAD_DIM
    return [lane_head == h for h in range(SG_HEADS)]


def _sg_mix(mats_ref, rhs, masks):
    out = jnp.zeros((SG_CHUNK, SG_W), F32)
    for h in range(SG_HEADS):
        out = out + jnp.where(masks[h], _dot(mats_ref[h], rhs), 0.0)
    return out


def _sg_specs(seq):
    t = min(T_GROUP, seq)
    tile = lambda c: pl.BlockSpec((t, SG_W), lambda i: (i, c))
    own = pl.BlockSpec((t, SG_W), lambda i: (i, 0))
    vec = pl.BlockSpec((1, SG_W), lambda i: (0, 0))
    mats = pl.BlockSpec((SG_HEADS, SG_CHUNK, SG_CHUNK), lambda i: (0, 0, 0))
    full = pl.BlockSpec((SG_CHUNK, SG_W), lambda i: (0, 0))
    return t, tile, own, vec, mats, full


def _sg_fwd(proj, ln_g, ln_b, w_b, bias_full, name):
    seq = proj.shape[0]
    t, tile, own, vec, mats, full = _sg_specs(seq)

    def body(u_ref, v_ref, gate_ref, g_ref, beta_ref, w_ref, bias_ref, o_ref):
        masks = _head_masks()
        for c in range(t // SG_CHUNK):
            rows = pl.ds(c * SG_CHUNK, SG_CHUNK)
            vg = _gelu_and_grad(v_ref[rows, :])[0]
            mu = jnp.mean(vg, axis=-1, keepdims=True)
            vc = vg - mu
            rs = lax.rsqrt(jnp.mean(vc * vc, axis=-1, keepdims=True) + EPS)
            vln = vc * rs * g_ref[...] + beta_ref[...]
            mixed = _sg_mix(w_ref, vln.astype(BF16), masks) + bias_ref[...]
            o_ref[rows, :] = _gelu_and_grad(u_ref[rows, :])[0] * mixed * _silu_and_grad(gate_ref[rows, :])[0]

    return _pcall(body, name=name, grid=(seq // t,),
                  in_specs=[tile(COL_U), tile(COL_VSG), tile(COL_GSG), vec, vec, mats, full], out_specs=own,
                  out_shape=jax.ShapeDtypeStruct((seq, SG_W), F32), sem=("parallel",))(
                      proj, proj, proj, ln_g, ln_b, w_b, bias_full)


def _sg_bwd(proj, dsg, ln_g, ln_b, w_b, w_t_b, bias_full, fold, name):
    seq = proj.shape[0]
    t, tile, own, vec, mats, full = _sg_specs(seq)
    n_tiles = seq // t

    def body(u_ref, v_ref, gate_ref, d_ref, g_ref, beta_ref, w_ref, wt_ref, bias_ref, fold_ref,
             du_ref, dv_ref, dgate_ref, dg_ref, dbeta_ref, dw_ref, db_ref, dbias_acc):
        i = pl.program_id(0)

        @pl.when(i == 0)
        def _():
            dg_ref[...] = jnp.zeros_like(dg_ref)
            dbeta_ref[...] = jnp.zeros_like(dbeta_ref)
            dw_ref[...] = jnp.zeros_like(dw_ref)
            dbias_acc[...] = jnp.zeros_like(dbias_acc)

        masks = _head_masks()
        for c in range(t // SG_CHUNK):
            rows = pl.ds(c * SG_CHUNK, SG_CHUNK)
            ug, dug = _gelu_and_grad(u_ref[rows, :])
            vg, dvg = _gelu_and_grad(v_ref[rows, :])
            mu = jnp.mean(vg, axis=-1, keepdims=True)
            vc = vg - mu
            rs = lax.rsqrt(jnp.mean(vc * vc, axis=-1, keepdims=True) + EPS)
            vn = vc * rs
            vln_b = (vn * g_ref[...] + beta_ref[...]).astype(BF16)
            mixed = _sg_mix(w_ref, vln_b, masks) + bias_ref[...]
            gate, dgate = _silu_and_grad(gate_ref[rows, :])
            d = d_ref[rows, :]
            dgate_ref[rows, :] = (d * ug * mixed * dgate).astype(BF16)
            du_ref[rows, :] = (d * mixed * gate * dug).astype(BF16)
            dmixed = d * ug * gate
            dbias_acc[...] += dmixed
            dmixed_b = dmixed.astype(BF16)
            for h in range(SG_HEADS):
                dm_h = jnp.where(masks[h], dmixed_b, jnp.zeros_like(dmixed_b))
                dw_ref[h] += lax.dot_general(dm_h, vln_b, (((1,), (1,)), ((), ())), preferred_element_type=F32)
            dvln = _sg_mix(wt_ref, dmixed_b, masks)
            dg_ref[...] += jnp.sum(dvln * vn, axis=0, keepdims=True)
            dbeta_ref[...] += jnp.sum(dvln, axis=0, keepdims=True)
            dvn = dvln * g_ref[...]
            dvgelu = rs * (dvn - jnp.mean(dvn, axis=-1, keepdims=True) - vn * jnp.mean(dvn * vn, axis=-1, keepdims=True))
            dv_ref[rows, :] = (dvgelu * dvg).astype(BF16)

        @pl.when(i == n_tiles - 1)
        def _():
            db_ref[...] = _split_dot(dbias_acc[...], fold_ref[...])

    sq = pl.BlockSpec((SG_CHUNK, SG_CHUNK), lambda i: (0, 0))
    vshape = jax.ShapeDtypeStruct((1, SG_W), F32)
    return _pcall(
        body, name=name, grid=(n_tiles,),
        in_specs=[tile(COL_U), tile(COL_VSG), tile(COL_GSG), own, vec, vec, mats, mats, full,
                  pl.BlockSpec((SG_W, SG_CHUNK), lambda i: (0, 0))],
        out_specs=[own, own, own, vec, vec, mats, sq],
        out_shape=[jax.ShapeDtypeStruct((seq, SG_W), BF16)] * 3 + [
            vshape, vshape, jax.ShapeDtypeStruct((SG_HEADS, SG_CHUNK, SG_CHUNK), F32),
            jax.ShapeDtypeStruct((SG_CHUNK, SG_CHUNK), F32)],
        scratch=[pltpu.VMEM((SG_CHUNK, SG_W), F32)], sem=("arbitrary",))(
            proj, proj, proj, dsg, ln_g, ln_b, w_b, w_t_b, bias_full, fold)


def _out_fwd(att_t, proj, cnv, sgu, x, w, gain, name, target=None):
    seq = x.shape[0]
    with_loss = target is not None

    def body(att_ref, g0_ref, g1_ref, cnv_ref, sgu_ref, x_ref, w_ref, gain_ref, *rest):
        mix_ref, cat_ref = rest[-2:]
        gate = jnp.concatenate([_silu_and_grad(g0_ref[...])[0], _silu_and_grad(g1_ref[...])[0]], axis=-1)
        cat_ref[:, 0:ATT_W] = (att_ref[...].T * gate).astype(BF16)
        cat_ref[:, ATT_W:ATT_W + CONV_W] = cnv_ref[...].astype(BF16)
        cat_ref[:, ATT_W + CONV_W:] = sgu_ref[...].astype(BF16)
        mix = _dot(cat_ref[...], w_ref[...])
        mix_ref[...] = mix
        r = lax.rsqrt(jnp.mean(mix * mix, axis=-1, keepdims=True) + EPS)
        x_new = x_ref[...] + mix * r * gain_ref[...]
        if not with_loss:
            rest[0][...] = x_new
            return
        t_ref, sse_ref, dy_ref = rest[:3]
        err = x_new - t_ref[...]
        dy_ref[...] = err * (1.0 / D_MODEL)

        @pl.when(pl.program_id(0) == 0)
        def _():
            sse_ref[...] = jnp.zeros_like(sse_ref)

        part = jnp.sum(jnp.sum(err * err, axis=0, keepdims=True), axis=-1, keepdims=True)
        sse_ref[...] += jnp.broadcast_to(part, (1, LANES))

    row = lambda w_: pl.BlockSpec((T_ROW, w_), lambda i: (i, 0))
    gate_blk = lambda c: pl.BlockSpec((T_ROW, 256), lambda i: (i, c))
    in_specs = [pl.BlockSpec((ATT_W, T_ROW), lambda i: (0, i)), gate_blk(COL_GATT), gate_blk(COL_GATT + 1),
                row(CONV_W), row(SG_W), row(D_MODEL),
                pl.BlockSpec((D_MODEL, D_MODEL), lambda i: (0, 0)), pl.BlockSpec((1, D_MODEL), lambda i: (0, 0))]
    tail_specs = [row(D_MODEL), row(D_MODEL)]
    tail_shape = [jax.ShapeDtypeStruct((seq, D_MODEL), F32), jax.ShapeDtypeStruct((seq, D_MODEL), BF16)]
    full = jax.ShapeDtypeStruct((seq, D_MODEL), F32)
    if with_loss:
        return _pcall(
            body, name=name, grid=(seq // T_ROW,), in_specs=in_specs + [row(D_MODEL)],
            out_specs=[pl.BlockSpec((1, LANES), lambda i: (0, 0)), row(D_MODEL)] + tail_specs,
            out_shape=[jax.ShapeDtypeStruct((1, LANES), F32), full] + tail_shape,
            sem=("arbitrary",), vmem_mb=VMEM_MB)(att_t, proj, proj, cnv, sgu, x, w, gain, target)
    return _pcall(
        body, name=name, grid=(seq // T_ROW,), in_specs=in_specs, out_specs=[row(D_MODEL)] + tail_specs,
        out_shape=[full] + tail_shape, sem=("parallel",), vmem_mb=VMEM_MB)(att_t, proj, proj, cnv, sgu, x, w, gain)


def _out_bwd(dxo, mix, gain, w_t, att_t, proj, name):
    seq = dxo.shape[0]

    def body(dxo_ref, mix_ref, gain_ref, w_ref, att_ref, g0_ref, g1_ref,
             dmix_ref, datt_ref, dgatt_ref, dcnv_ref, dsgu_ref, dgain_ref):
        mix = mix_ref[...]
        r = lax.rsqrt(jnp.mean(mix * mix, axis=-1, keepdims=True) + EPS)
        n = mix * r
        dout = dxo_ref[...]
        dn = dout * gain_ref[...]
        dmix = (r * (dn - n * jnp.mean(dn * n, axis=-1, keepdims=True))).astype(BF16)
        dmix_ref[...] = dmix

        @pl.when(pl.program_id(0) == 0)
        def _():
            dgain_ref[...] = jnp.zeros_like(dgain_ref)

        dgain_ref[...] += jnp.sum(dout * n, axis=0, keepdims=True)
        dcat = _dot(dmix, w_ref[...])
        g0, dg0 = _silu_and_grad(g0_ref[...])
        g1, dg1 = _silu_and_grad(g1_ref[...])
        gate = jnp.concatenate([g0, g1], axis=-1)
        dgate = jnp.concatenate([dg0, dg1], axis=-1)
        dca = dcat[:, 0:ATT_W]
        datt_ref[...] = (dca * gate).T
        dgatt_ref[...] = (dca * att_ref[...].T * dgate).astype(BF16)
        dcnv_ref[...] = dcat[:, ATT_W:ATT_W + CONV_W]
        dsgu_ref[...] = dcat[:, ATT_W + CONV_W:]

    row = lambda w_: pl.BlockSpec((T_ROW, w_), lambda i: (i, 0))
    gate_blk = lambda c: pl.BlockSpec((T_ROW, 256), lambda i: (i, c))
    vec = pl.BlockSpec((1, D_MODEL), lambda i: (0, 0))
    heads_t = pl.BlockSpec((ATT_W, T_ROW), lambda i: (0, i))
    return _pcall(
        body, name=name, grid=(seq // T_ROW,),
        in_specs=[row(D_MODEL), row(D_MODEL), vec, pl.BlockSpec((D_MODEL, D_MODEL), lambda i: (0, 0)), heads_t,
                  gate_blk(COL_GATT), gate_blk(COL_GATT + 1)],
        out_specs=[row(D_MODEL), heads_t, row(ATT_W), row(CONV_W), row(SG_W), vec],
        out_shape=[jax.ShapeDtypeStruct((seq, D_MODEL), BF16), jax.ShapeDtypeStruct((ATT_W, seq), F32),
                   jax.ShapeDtypeStruct((seq, ATT_W), BF16), jax.ShapeDtypeStruct((seq, CONV_W), F32),
                   jax.ShapeDtypeStruct((seq, SG_W), F32), jax.ShapeDtypeStruct((1, D_MODEL), F32)],
        sem=("arbitrary",), vmem_mb=VMEM_MB)(dxo, mix, gain, w_t, att_t, proj, proj)


def _row_blocks(a, bk):
    return a.reshape(a.shape[0], a.shape[1] // bk, bk, HEAD_DIM)


def _lane_blocks(a, bk):
    return a.reshape(a.shape[0], HEAD_DIM, a.shape[2] // bk, bk).transpose(0, 2, 1, 3)


def _from_lane_blocks(a):
    return a.transpose(0, 2, 1, 3).reshape(a.shape[0], HEAD_DIM, a.shape[1] * a.shape[3])


def _flat_rows(a, rows):
    flat = a.reshape(-1)
    return jnp.pad(flat, (0, rows * LANES - flat.shape[0])).reshape(rows, LANES)


SHARD_ROWS = {"w_in": 2 * D_MODEL * (D_IN // N_DEV) // LANES, "w_out": 2 * (D_MODEL // N_DEV) * D_MODEL // LANES,
              "conv_dw": 16}
REPL_SHAPES = [("pre_norm", (2, D_MODEL)), ("post_norm", (2, D_MODEL)), ("q_norm", (2, HEAD_DIM)),
               ("k_norm", (2, HEAD_DIM)), ("conv_dw_b", (2, CONV_W)), ("conv_ln_g", (2, CONV_W)),
               ("conv_ln_b", (2, CONV_W)), ("sg_ln_g", (2, SG_W)), ("sg_ln_b", (2, SG_W)),
               ("sg_w", (2, SG_HEADS, SG_CHUNK, SG_CHUNK)), ("sg_b", (2, SG_HEADS, SG_CHUNK))]
REPL_ROWS = 1088
WEIGHT_ORDER = ["pre_norm", "post_norm", "w_in", "w_out", "q_norm", "k_norm", "conv_dw", "conv_dw_b", "conv_ln_g",
                "conv_ln_b", "sg_ln_g", "sg_ln_b", "sg_w", "sg_b"]


def _pack_shard(parts):
    return jnp.concatenate([_flat_rows(parts[k], SHARD_ROWS[k]) for k in ("w_in", "w_out", "conv_dw")], axis=0)


def _unpack_shard(flat, shapes):
    out, at = {}, 0
    for k in ("w_in", "w_out", "conv_dw"):
        size = math.prod(shapes[k])
        out[k] = flat[at:at + SHARD_ROWS[k]].reshape(-1)[:size].reshape(shapes[k])
        at += SHARD_ROWS[k]
    return out


REPL_USED = sum(math.prod(shape) for _, shape in REPL_SHAPES)


def _pack_repl(parts, extra=None):
    tail = [] if extra is None else [extra.reshape(1)]
    flat = jnp.concatenate([parts[k].reshape(-1) for k, _ in REPL_SHAPES] + tail)
    return jnp.pad(flat, (0, REPL_ROWS * LANES - flat.shape[0])).reshape(REPL_ROWS, LANES)


def _unpack_repl(flat):
    out, at, flat = {}, 0, flat.reshape(-1)
    for k, shape in REPL_SHAPES:
        size = math.prod(shape)
        out[k] = flat[at:at + size].reshape(shape)
        at += size
    return out


def kernel(x, pre_norm, post_norm, w_in, w_out, q_norm, k_norm, conv_dw, conv_dw_b, conv_ln_g, conv_ln_b, sg_ln_g, sg_ln_b, sg_w, sg_b, loss_target, m_pre_norm, m_post_norm, m_w_in, m_w_out, m_q_norm, m_k_norm, m_conv_dw, m_conv_dw_b, m_conv_ln_g, m_conv_ln_b, m_sg_ln_g, m_sg_ln_b, m_sg_w, m_sg_b, v_pre_norm, v_post_norm, v_w_in, v_w_out, v_q_norm, v_k_norm, v_conv_dw, v_conv_dw_b, v_conv_ln_g, v_conv_ln_b, v_sg_ln_g, v_sg_ln_b, v_sg_w, v_sg_b):
    weights = dict(pre_norm=pre_norm, post_norm=post_norm, w_in=w_in, w_out=w_out, q_norm=q_norm, k_norm=k_norm,
                   conv_dw=conv_dw, conv_dw_b=conv_dw_b, conv_ln_g=conv_ln_g, conv_ln_b=conv_ln_b, sg_ln_g=sg_ln_g,
                   sg_ln_b=sg_ln_b, sg_w=sg_w, sg_b=sg_b)
    mom_m = dict(pre_norm=m_pre_norm, post_norm=m_post_norm, w_in=m_w_in, w_out=m_w_out, q_norm=m_q_norm,
                 k_norm=m_k_norm, conv_dw=m_conv_dw, conv_dw_b=m_conv_dw_b, conv_ln_g=m_conv_ln_g,
                 conv_ln_b=m_conv_ln_b, sg_ln_g=m_sg_ln_g, sg_ln_b=m_sg_ln_b, sg_w=m_sg_w, sg_b=m_sg_b)
    mom_v = dict(pre_norm=v_pre_norm, post_norm=v_post_norm, w_in=v_w_in, w_out=v_w_out, q_norm=v_q_norm,
                 k_norm=v_k_norm, conv_dw=v_conv_dw, conv_dw_b=v_conv_dw_b, conv_ln_g=v_conv_ln_g,
                 conv_ln_b=v_conv_ln_b, sg_ln_g=v_sg_ln_g, sg_ln_b=v_sg_ln_b, sg_w=v_sg_w, sg_b=v_sg_b)
    depth = pre_norm.shape[0]
    seq = x.shape[1]
    bk = min(BK, seq)
    x0 = x.reshape(seq, D_MODEL)
    target = loss_target.reshape(seq, D_MODEL)

    w_in_all, w_out_all, dw_all = _exchange(
        [], [w_in.astype(BF16), w_out.astype(BF16), jnp.pad(conv_dw, ((0, 0), (0, 1), (0, 0)))], "gather_weights")
    w_in_full = w_in_all.transpose(1, 2, 0, 3).reshape(depth, D_MODEL, D_IN)
    w_out_full = w_out_all.transpose(1, 0, 2, 3).reshape(depth, D_MODEL, D_MODEL)
    dw_full = dw_all[:, :, :CONV_K, :].transpose(1, 2, 0, 3).reshape(depth, CONV_K, CONV_W)

    cos, sin = _rope_tables(seq)
    lane = jnp.arange(SG_W)
    fold = (lane[:, None] // HEAD_DIM == jnp.arange(SG_CHUNK)[None, :]).astype(BF16)

    def layer_consts(l):
        return dict(
            q_gain=q_norm[l].reshape(HEAD_DIM, 1), k_gain=k_norm[l].reshape(HEAD_DIM, 1), sg_w_b=sg_w[l].astype(BF16), sg_wt_b=sg_w[l].transpose(0, 2, 1).astype(BF16),
            sg_bias=jnp.repeat(sg_b[l].T, HEAD_DIM, axis=1),
            vec=lambda a: a[l].reshape(1, -1))

    saved = []
    xc = x0
    for l in range(depth):
        c = layer_consts(l)
        proj, hb, qkv_t = _proj_fwd(xc, c["vec"](pre_norm), w_in_full[l], f"proj_fwd_{l}")
        qkv_t = qkv_t.reshape(QKV_HEADS, HEAD_DIM, seq)
        qs_t, qs = _qk_prep_fwd(qkv_t, 0, ATT_HEADS, c["q_gain"], Q_SCALE, cos, sin, f"q_prep_fwd_{l}")
        kr_t, kr = _qk_prep_fwd(qkv_t, ATT_HEADS, KV_HEADS, c["k_gain"], 1.0, cos, sin, f"k_prep_fwd_{l}")
        v_t = qkv_t[ATT_HEADS + KV_HEADS:].astype(BF16)
        k_rows, k_cols, v_cols = _row_blocks(kr, bk), _lane_blocks(kr_t, bk), _lane_blocks(v_t, bk)
        v_ext = jnp.concatenate([v_cols, jnp.ones_like(v_cols[:, :, :1]), jnp.zeros_like(v_cols[:, :, :7])], axis=2)
        o_t, lse = _attn_fwd(qs_t, k_rows, v_ext, f"attn_fwd_{l}")
        att_t = o_t.reshape(ATT_W, seq)
        cnv, y_conv = _conv_fwd(proj, dw_full[l], c["vec"](conv_dw_b), c["vec"](conv_ln_g), c["vec"](conv_ln_b),
                                f"conv_fwd_{l}")
        sgu = _sg_fwd(proj, c["vec"](sg_ln_g), c["vec"](sg_ln_b), c["sg_w_b"], c["sg_bias"], f"sg_fwd_{l}")
        if l < depth - 1:
            x_new, mix, cat_b = _out_fwd(att_t, proj, cnv, sgu, xc, w_out_full[l], c["vec"](post_norm),
                                         f"out_fwd_{l}")
        else:
            sse, dx, mix, cat_b = _out_fwd(att_t, proj, cnv, sgu, xc, w_out_full[l], c["vec"](post_norm),
                                           f"out_fwd_{l}", target=target)
            x_new = None
        saved.append(dict(x=xc, proj=proj, hb=hb, qkv_t=qkv_t, qs=qs, qs_t=qs_t, k_rows=k_rows,
                          k_cols=k_cols, v_cols=v_cols, o_t=o_t, lse=lse, mix=mix, cat_b=cat_b, y_conv=y_conv))
        xc = x_new

    grads = {k: [None] * depth for k in WEIGHT_ORDER}
    for l in reversed(range(depth)):
        c, s = layer_consts(l), saved[l]
        dmix_b, datt, dgatt, dcnv, dsgu, g_post = _out_bwd(
            dx, s["mix"], c["vec"](post_norm), w_out_full[l].T, s["o_t"].reshape(ATT_W, seq), s["proj"],
            f"out_bwd_{l}")
        grads["post_norm"][l] = g_post.reshape(-1)
        grads["w_out"][l] = _matmul_acc(s["cat_b"], dmix_b, D_MODEL, f"grad_w_out_{l}")
        dqs, dkt, dvt = _attn_bwd(s["qs"], s["qs_t"], datt.reshape(ATT_HEADS, HEAD_DIM, seq), s["o_t"], s["lse"],
                                  s["k_rows"], s["k_cols"], s["v_cols"], f"attn_bwd_{l}")
        d_q_t, g_qgain = _qk_prep_bwd(s["qkv_t"], 0, dqs, False, c["q_gain"], Q_SCALE, cos, sin, f"q_prep_bwd_{l}")
        d_k_t, g_kgain = _qk_prep_bwd(s["qkv_t"], ATT_HEADS, _from_lane_blocks(dkt), True, c["k_gain"], 1.0, cos, sin,
                                      f"k_prep_bwd_{l}")
        grads["q_norm"][l] = jnp.sum(g_qgain[:, :, 0], axis=0)
        grads["k_norm"][l] = jnp.sum(g_kgain[:, :, 0], axis=0)
        d_qkv = jnp.concatenate([d_q_t, d_k_t, _from_lane_blocks(dvt)], axis=0).astype(BF16).reshape(QKV_W, seq).T
        dy_conv, dg_conv, g_dw, g_dwb, g_clg, g_clb = _conv_bwd_a(
            s["proj"], s["y_conv"], dcnv, c["vec"](conv_ln_g), c["vec"](conv_ln_b), f"conv_bwd_a_{l}")
        da, db = _conv_bwd_b(s["proj"], dy_conv, dw_full[l], f"conv_bwd_b_{l}")
        grads["conv_dw"][l], grads["conv_dw_b"][l] = g_dw, g_dwb.reshape(-1)
        grads["conv_ln_g"][l], grads["conv_ln_b"][l] = g_clg.reshape(-1), g_clb.reshape(-1)
        du, dv_sg, dg_sg, g_slg, g_slb, g_sw, g_sb = _sg_bwd(
            s["proj"], dsgu, c["vec"](sg_ln_g), c["vec"](sg_ln_b), c["sg_w_b"], c["sg_wt_b"], c["sg_bias"], fold,
            f"sg_bwd_{l}")
        grads["sg_ln_g"][l], grads["sg_ln_b"][l] = g_slg.reshape(-1), g_slb.reshape(-1)
        grads["sg_w"][l], grads["sg_b"][l] = g_sw, g_sb[:, :SG_HEADS].T
        dproj = jnp.concatenate([d_qkv, dgatt, da, db, dg_conv, du, dv_sg, dg_sg], axis=-1).astype(BF16)
        grads["w_in"][l] = _matmul_acc(s["hb"], dproj, D_IN // 2, f"grad_w_in_{l}")
        dx, g_pre = _proj_bwd(dproj, w_in_full[l].T, s["x"], c["vec"](pre_norm), dx, f"proj_bwd_{l}")
        grads["pre_norm"][l] = g_pre.reshape(-1)
    grad_x = dx.reshape(x.shape)
    grads = {k: jnp.stack(v) for k, v in grads.items()}

    shard_blocks = dict(
        w_in=grads["w_in"].reshape(depth, D_MODEL, N_DEV, D_IN // N_DEV).transpose(2, 0, 1, 3),
        w_out=grads["w_out"].reshape(depth, N_DEV, D_MODEL // N_DEV, D_MODEL).transpose(1, 0, 2, 3),
        conv_dw=grads["conv_dw"].reshape(depth, CONV_K, N_DEV, CONV_W // N_DEV).transpose(2, 0, 1, 3))
    scatter_src = jnp.stack([_pack_shard({k: a[d] for k, a in shard_blocks.items()})
                             for d in range(N_DEV)]).astype(BF16)
    shard_slots, repl_slots = _exchange([scatter_src], [_pack_repl(grads, sse[0, 0])], "exchange_grads")

    shard_shapes = {k: weights[k].shape for k in SHARD_ROWS}
    gs, ds_, ms, vs = _sum_adamw(shard_slots, _pack_shard(weights), _pack_shard(mom_m), _pack_shard(mom_v),
                                 "adamw_sharded")
    gr, dr, mr, vr = _sum_adamw(repl_slots, _pack_repl(weights), _pack_repl(mom_m), _pack_repl(mom_v),
                                "adamw_replicated")
    loss = gr.reshape(-1)[REPL_USED] * (0.5 / D_MODEL)
    results = []
    for shard_flat, repl_flat in ((gs, gr), (ds_, dr), (ms, mr), (vs, vr)):
        parts = {**_unpack_shard(shard_flat, shard_shapes), **_unpack_repl(repl_flat)}
        results.append([parts[k] for k in WEIGHT_ORDER])
    return (loss, grad_x, *results[0], *results[1], *results[2], *results[3])
```
